```python
import math
import jax, jax.numpy as jnp
from jax import lax
import numpy as np

D_MODEL = 1024
BATCH = 1
SEQ = 16384
DEPTH = 4

HEAD_DIM = 64
BLK = 128
NEG_INF = -1e30

SWA_WINDOW = 128
A_Q_HEADS = 4
A_KV_HEADS = 2
A_GROUP = A_Q_HEADS // A_KV_HEADS
A_COLS = (A_Q_HEADS + 2 * A_KV_HEADS) * HEAD_DIM

DILATED_PAIRS = ((128, 1), (512, 4), (2048, 16))
B_N_GROUPS = len(DILATED_PAIRS)
B_HEADS_PER_GROUP = 4
B_HEADS = B_N_GROUPS * B_HEADS_PER_GROUP
B_COLS = 3 * B_HEADS * HEAD_DIM
SEQ_ALIGN = BLK * max(d for _, d in DILATED_PAIRS)

C_HEADS = 8
C_NOPE = 64
C_ROPE = 32
C_V = 64
Q_LORA = 256
KV_LORA = 128
ROPE_THETA = 10000.0
C_COLS = Q_LORA + KV_LORA + C_ROPE

N_IN = A_COLS + B_COLS + C_COLS
D_MIX = A_Q_HEADS * HEAD_DIM + B_HEADS_PER_GROUP * HEAD_DIM + C_HEADS * C_V

N_BUCKETS = 32
MAX_DISTANCE = 2048
N_BIAS_HEADS = A_Q_HEADS + B_HEADS

N_GROUPS = 4
EXPERTS_PER_GROUP = 8
N_EXPERTS = N_GROUPS * EXPERTS_PER_GROUP
TOP_K = 2
D_EXPERT = 256
ROW_BLK = 256

DEEPNORM_ALPHA = (2 * DEPTH) ** 0.25
DEEPNORM_BETA = (8 * DEPTH) ** -0.25

kernel_name = "hymba_swa_dilated_mla_hmoe_deepnorm"


def layer_norm(x, g, b, eps=1e-5):
    xf = x.astype(jnp.float32)
    mu = xf.mean(-1, keepdims=True)
    var = jnp.square(xf - mu).mean(-1, keepdims=True)
    return ((xf - mu) * lax.rsqrt(var + eps) * g + b).astype(x.dtype)


def rms_norm(x, g, eps=1e-6):
    xf = x.astype(jnp.float32)
    return (xf * lax.rsqrt(jnp.mean(xf * xf, -1, keepdims=True) + eps) * g).astype(x.dtype)


def rotary(x, cos, sin):
    x1, x2 = jnp.split(x, 2, axis=-1)
    return jnp.concatenate([x1 * cos - x2 * sin, x1 * sin + x2 * cos], -1).astype(x.dtype)


def t5_bucket(dist):
    max_exact = N_BUCKETS // 2
    large = max_exact + (jnp.log(jnp.maximum(dist, 1).astype(jnp.float32) / max_exact)
                         / math.log(MAX_DISTANCE / max_exact) * (N_BUCKETS - max_exact)).astype(jnp.int32)
    return jnp.where(dist < max_exact, dist, jnp.minimum(large, N_BUCKETS - 1))


def head_bias(table_cols, dist):
    return table_cols[t5_bucket(jnp.maximum(dist, 0))].transpose(2, 0, 1).astype(jnp.float32)


def band_blocks(t, n_blocks):
    tb = t.reshape((t.shape[0], n_blocks, BLK) + t.shape[2:])
    prev = jnp.pad(tb[:, :-1], ((0, 0), (1, 0)) + ((0, 0),) * (tb.ndim - 2))
    return jnp.concatenate([prev, tb], axis=2)


def banded_logits(q, k, bias, max_dist):
    Bp, L = q.shape[:2]
    n = L // BLK
    qb = q.reshape((Bp, n, BLK) + q.shape[2:])
    kb = band_blocks(k, n)
    s = jnp.einsum('bnqhgd,bnkhd->bnhgqk', qb, kb).astype(jnp.float32) * (q.shape[-1] ** -0.5) + bias
    qi = jnp.arange(BLK)[:, None]
    kj = jnp.arange(2 * BLK)[None, :]
    dist = BLK + qi - kj
    kpos = jnp.arange(n)[:, None, None] * BLK + kj[None] - BLK
    mask = (dist >= 0) & (dist <= max_dist) & (kpos >= 0)
    return jnp.where(mask[None, :, None, None], s, NEG_INF)


def swa_sink_mixer(h_a, sink, bias_a):
    Bsz, T, _ = h_a.shape
    q, k, v = jnp.split(h_a, [A_Q_HEADS * HEAD_DIM, (A_Q_HEADS + A_KV_HEADS) * HEAD_DIM], axis=-1)
    q = q.reshape(Bsz, T, A_KV_HEADS, A_GROUP, HEAD_DIM)
    k = k.reshape(Bsz, T, A_KV_HEADS, HEAD_DIM)
    v = v.reshape(Bsz, T, A_KV_HEADS, HEAD_DIM)
    s = banded_logits(q, k, bias_a, SWA_WINDOW - 1)
    sink_b = sink.astype(jnp.float32).reshape(1, 1, A_KV_HEADS, A_GROUP, 1, 1)
    m = jnp.maximum(s.max(-1, keepdims=True), sink_b)
    e = jnp.exp(s - m)
    p = e / (e.sum(-1, keepdims=True) + jnp.exp(sink_b - m))
    o = jnp.einsum('bnhgqk,bnkhd->bnqhgd', p.astype(v.dtype), band_blocks(v, T // BLK))
    return o.reshape(Bsz, T, A_Q_HEADS * HEAD_DIM)


def dilated_group(q, k, v, bias, dil, steps):
    Bsz, Tp, H, hd = q.shape
    Ls = Tp // dil
    n = Ls // BLK

    def to_sub(t):
        return t.reshape(Bsz, Ls, dil, H, hd).transpose(0, 2, 1, 3, 4).reshape(Bsz * dil, Ls, H, hd)

    qs, ks, vs = to_sub(q), to_sub(k), to_sub(v)
    s = banded_logits(qs[:, :, :, None], ks, bias[:, None], steps)
    m = s.max(-1, keepdims=True)
    e = jnp.exp(s - m)
    l = e.sum(-1, keepdims=True)
    lse = (m + jnp.log(l))[..., 0]
    o = jnp.einsum('bnhgqk,bnkhd->bnqhgd', (e / l).astype(v.dtype), band_blocks(vs, n))
    o = o.reshape(Bsz, dil, Ls, H, hd).transpose(0, 2, 1, 3, 4).reshape(Bsz, Tp, H, hd)
    lse = lse.transpose(0, 1, 4, 2, 3).reshape(Bsz, dil, Ls, H).transpose(0, 2, 1, 3).reshape(Bsz, Tp, H)
    return o, lse


def dilated_mixer(h_b, bias_b):
    Bsz, T, _ = h_b.shape
    T_pad = -(-T // SEQ_ALIGN) * SEQ_ALIGN
    h_b = jnp.pad(h_b, ((0, 0), (0, T_pad - T), (0, 0)))
    shape5 = (Bsz, T_pad, B_N_GROUPS, B_HEADS_PER_GROUP, HEAD_DIM)
    q, k, v = [t.reshape(shape5) for t in jnp.split(h_b, 3, axis=-1)]
    outs, lses = [], []
    for g, (window, dil) in enumerate(DILATED_PAIRS):
        o_g, lse_g = dilated_group(q[:, :, g], k[:, :, g], v[:, :, g], bias_b[g], dil, window // dil)
        outs.append(o_g)
        lses.append(lse_g)
    w = jax.nn.softmax(jnp.stack(lses), axis=0)
    o = jnp.einsum('gbth,gbthd->bthd', w.astype(h_b.dtype), jnp.stack(outs))
    return o[:, :T].reshape(Bsz, T, B_HEADS_PER_GROUP * HEAD_DIM)


def mla_mixer(h_c, q_norm_g, kv_norm_g, w_uq, w_ukv, cos, sin):
    Bsz, T, _ = h_c.shape
    c_q, c_kv, k_rope = jnp.split(h_c, [Q_LORA, Q_LORA + KV_LORA], axis=-1)
    q = (rms_norm(c_q, q_norm_g) @ w_uq).reshape(Bsz, T, C_HEADS, C_NOPE + C_ROPE)
    kv = (rms_norm(c_kv, kv_norm_g) @ w_ukv).reshape(Bsz, T, C_HEADS, C_NOPE + C_V)
    q_nope, q_rope = jnp.split(q, [C_NOPE], axis=-1)
    k_nope, v = jnp.split(kv, [C_NOPE], axis=-1)
    q_rope = rotary(q_rope, cos[:, None], sin[:, None])
    k_rope = rotary(k_rope, cos, sin)
    n = T // BLK
    scale = (C_NOPE + C_ROPE) ** -0.5
    key_pos = jnp.arange(T)

    def to_blocks(t):
        return t.reshape((Bsz, n, BLK) + t.shape[2:]).swapaxes(0, 1)

    def one_block(args):
        qn, qr, blk = args
        s = (jnp.einsum('bqhd,bkhd->bhqk', qn, k_nope)
             + jnp.einsum('bqhd,bkd->bhqk', qr, k_rope)).astype(jnp.float32) * scale
        q_pos = blk * BLK + jnp.arange(BLK)
        s = jnp.where(key_pos[None, :] <= q_pos[:, None], s, NEG_INF)
        p = jax.nn.softmax(s, axis=-1)
        return jnp.einsum('bhqk,bkhd->bqhd', p.astype(v.dtype), v)

    o = lax.map(one_block, (to_blocks(q_nope), to_blocks(q_rope), jnp.arange(n)))
    return o.swapaxes(0, 1).reshape(Bsz, T, C_HEADS * C_V)


def hybrid_mixer(x, w_in, w_out, sink, bias_a, bias_b, q_norm_g, kv_norm_g, w_uq, w_ukv, cos, sin):
    h = x @ w_in
    h_a, h_b, h_c = jnp.split(h, [A_COLS, A_COLS + B_COLS], axis=-1)
    o_a = swa_sink_mixer(h_a, sink, bias_a)
    o_b = dilated_mixer(h_b, bias_b)
    o_c = mla_mixer(h_c, q_norm_g, kv_norm_g, w_uq, w_ukv, cos, sin)
    return jnp.concatenate([o_a, o_b, o_c], axis=-1) @ w_out


def hierarchical_moe(h, w_rg, b_rg, w_re, b_re, w_gate, w_up, w_down):
    Bsz, T, D = h.shape
    n_tok = Bsz * T
    x = h.reshape(n_tok, D)
    g_logits = (x @ w_rg).astype(jnp.float32) + b_rg
    g_prob = jax.nn.softmax(g_logits, axis=-1)
    grp = jnp.argmax(g_logits, axis=-1)
    p_grp = jnp.take_along_axis(g_prob, grp[:, None], axis=-1)
    e_all = jnp.einsum('nd,gde->nge', x, w_re).astype(jnp.float32) + b_re
    e_logits = jnp.take_along_axis(e_all, grp[:, None, None], axis=1)[:, 0]
    top_val, top_idx = lax.top_k(e_logits, TOP_K)
    gate = jax.nn.softmax(top_val, axis=-1) * p_grp
    expert_ids = grp[:, None] * EXPERTS_PER_GROUP + top_idx
    A = n_tok * TOP_K
    flat_e = expert_ids.reshape(A).astype(jnp.int32)
    flat_w = gate.reshape(A)
    flat_tok = jnp.arange(A, dtype=jnp.int32) // TOP_K
    order = jnp.argsort(flat_e)
    e_sorted = flat_e[order]
    counts = jnp.zeros((N_EXPERTS,), jnp.int32).at[flat_e].add(1)
    padded = ((counts + ROW_BLK - 1) // ROW_BLK) * ROW_BLK
    pad_end = jnp.cumsum(padded)
    pad_start = pad_end - padded
    start = jnp.cumsum(counts) - counts
    dest = pad_start[e_sorted] + (jnp.arange(A, dtype=jnp.int32) - start[e_sorted])
    n_blocks = A // ROW_BLK + N_EXPERTS
    P = n_blocks * ROW_BLK
    row_tok = jnp.full((P,), n_tok, jnp.int32).at[dest].set(flat_tok[order])
    row_w = jnp.zeros((P,), h.dtype).at[dest].set(flat_w[order].astype(h.dtype))
    block_expert = jnp.minimum(
        jnp.searchsorted(pad_end, jnp.arange(n_blocks) * ROW_BLK, side='right'), N_EXPERTS - 1)
    x_pad = jnp.concatenate([x, jnp.zeros((1, D), x.dtype)], axis=0)
    rows = x_pad[row_tok].reshape(n_blocks, ROW_BLK, D)

    def block_ffn(args):
        xb, e = args
        return (jax.nn.silu(xb @ w_gate[e]) * (xb @ w_up[e])) @ w_down[e]

    ys = lax.map(block_ffn, (rows, block_expert)).reshape(P, D) * row_w[:, None]
    out = jnp.zeros((n_tok + 1, D), ys.dtype).at[row_tok].add(ys)[:n_tok]
    return out.reshape(Bsz, T, D)


def setup_inputs(seed: int = 0) -> dict:
    key = jax.random.key(seed)
    ks = jax.random.split(key, 20)
    f32 = jnp.float32

    def nrm(k, shape, scale):
        return jax.random.normal(k, shape, f32) * scale

    return {
        "x": nrm(ks[0], (BATCH, SEQ, D_MODEL), 1.0),
        "w_in": nrm(ks[1], (DEPTH, D_MODEL, N_IN), D_MODEL ** -0.5),
        "w_out": nrm(ks[2], (DEPTH, D_MIX, D_MODEL), DEEPNORM_BETA * D_MIX ** -0.5),
        "sinks": nrm(ks[3], (DEPTH, A_Q_HEADS), 0.5),
        "rel_bias": nrm(ks[4], (N_BUCKETS, N_BIAS_HEADS), 0.1),
        "mla_q_norm": 1.0 + nrm(ks[5], (DEPTH, Q_LORA), 0.05),
        "mla_kv_norm": 1.0 + nrm(ks[6], (DEPTH, KV_LORA), 0.05),
        "w_uq": nrm(ks[7], (DEPTH, Q_LORA, C_HEADS * (C_NOPE + C_ROPE)), Q_LORA ** -0.5),
        "w_ukv": nrm(ks[8], (DEPTH, KV_LORA, C_HEADS * (C_NOPE + C_V)), KV_LORA ** -0.5),
        "ln1_g": 1.0 + nrm(ks[9], (DEPTH, D_MODEL), 0.05),
        "ln1_b": nrm(ks[10], (DEPTH, D_MODEL), 0.02),
        "w_route_group": nrm(ks[11], (DEPTH, D_MODEL, N_GROUPS), D_MODEL ** -0.5),
        "b_route_group": nrm(ks[12], (DEPTH, N_GROUPS), 0.01),
        "w_route_expert": nrm(ks[13], (DEPTH, N_GROUPS, D_MODEL, EXPERTS_PER_GROUP), D_MODEL ** -0.5),
        "b_route_expert": nrm(ks[14], (DEPTH, N_GROUPS, EXPERTS_PER_GROUP), 0.01),
        "w_expert_gate": nrm(ks[15], (DEPTH, N_EXPERTS, D_MODEL, D_EXPERT), D_MODEL ** -0.5),
        "w_expert_up": nrm(ks[16], (DEPTH, N_EXPERTS, D_MODEL, D_EXPERT), D_MODEL ** -0.5),
        "w_expert_down": nrm(ks[17], (DEPTH, N_EXPERTS, D_EXPERT, D_MODEL), DEEPNORM_BETA * D_EXPERT ** -0.5),
        "ln2_g": 1.0 + nrm(ks[18], (DEPTH, D_MODEL), 0.05),
        "ln2_b": nrm(ks[19], (DEPTH, D_MODEL), 0.02),
    }


def reference(x, w_in, w_out, sinks, rel_bias, mla_q_norm, mla_kv_norm, w_uq, w_ukv, ln1_g, ln1_b,
              w_route_group, b_route_group, w_route_expert, b_route_expert,
              w_expert_gate, w_expert_up, w_expert_down, ln2_g, ln2_b):
    T = x.shape[1]
    inv_freq = ROPE_THETA ** (-jnp.arange(0, C_ROPE, 2, dtype=jnp.float32) / C_ROPE)
    ang = jnp.arange(T, dtype=jnp.float32)[:, None] * inv_freq[None, :]
    cos, sin = jnp.cos(ang), jnp.sin(ang)
    dist = BLK + jnp.arange(BLK)[:, None] - jnp.arange(2 * BLK)[None, :]
    bias_a = head_bias(rel_bias[:, :A_Q_HEADS], dist).reshape(A_KV_HEADS, A_GROUP, BLK, 2 * BLK)
    bias_b = [head_bias(rel_bias[:, A_Q_HEADS + g * B_HEADS_PER_GROUP:A_Q_HEADS + (g + 1) * B_HEADS_PER_GROUP],
                        dist * dil)
              for g, (_, dil) in enumerate(DILATED_PAIRS)]
    for layer in range(DEPTH):
        y = hybrid_mixer(x, w_in[layer], w_out[layer], sinks[layer], bias_a, bias_b,
                         mla_q_norm[layer], mla_kv_norm[layer], w_uq[layer], w_ukv[layer], cos, sin)
        x = layer_norm(DEEPNORM_ALPHA * x + y, ln1_g[layer], ln1_b[layer])
        y = hierarchical_moe(x, w_route_group[layer], b_route_group[layer], w_route_expert[layer],
                             b_route_expert[layer], w_expert_gate[layer], w_expert_up[layer],
                             w_expert_down[layer])
        x = layer_norm(DEEPNORM_ALPHA * x + y, ln2_g[layer], ln2_b[layer])
    return x
```

```python
import functools
import math

import jax
import jax.numpy as jnp
from jax import lax
from jax.experimental import pallas as pl
from jax.experimental.pallas import tpu as pltpu

F32 = jnp.float32
BF16 = jnp.bfloat16

D_MODEL = 1024
DEPTH = 4
HEAD_DIM = 64
BLK = 128
NEG_INF = -1e30

SWA_WINDOW = 128
A_Q_HEADS = 4
A_KV_HEADS = 2
A_COLS = (A_Q_HEADS + 2 * A_KV_HEADS) * HEAD_DIM

DILATED_PAIRS = ((128, 1), (512, 4), (2048, 16))
B_HEADS_PER_GROUP = 4
B_HEADS = len(DILATED_PAIRS) * B_HEADS_PER_GROUP
B_COLS = 3 * B_HEADS * HEAD_DIM
AB_COLS = A_COLS + B_COLS

C_HEADS = 8
C_NOPE = 64
C_ROPE = 32
C_V = 64
Q_LORA = 256
KV_LORA = 128
ROPE_THETA = 10000.0
C_PAD = 128

N_BUCKETS = 32
MAX_DISTANCE = 2048

N_GROUPS = 4
EXPERTS_PER_GROUP = 8
N_EXPERTS = N_GROUPS * EXPERTS_PER_GROUP
TOP_K = 2
D_EXPERT = 256
ROW_BLK = 256

DEEPNORM_ALPHA = (2 * DEPTH) ** 0.25

VMEM_LIMIT = 56 * 1024 * 1024

CHUNK = 2048
TM_IN = 256
TM_OUT = 256
TQ = 512
TK = 512
LANES = 128


def _dot(a, b):
    return jnp.dot(a, b, preferred_element_type=F32)


def _dot_nt(a, b):
    return lax.dot_general(a, b, (((1,), (1,)), ((), ())), preferred_element_type=F32)


def _cparams(*sem):
    return pltpu.CompilerParams(dimension_semantics=sem, vmem_limit_bytes=VMEM_LIMIT)


def _in_proj_kernel(x_ref, wab_ref, wc_ref, wq_ref, wqs_ref, wk_ref, wv_ref, gq_ref, gkv_ref, cs_ref, sn_ref,
                    hab_ref, q_ref, k_ref, v_ref):
    xb = x_ref[...].astype(BF16)
    hab_ref[...] = _dot(xb, wab_ref[...])
    hc = _dot(xb, wc_ref[...])
    cq = hc[:, :Q_LORA]
    ckv = hc[:, Q_LORA:Q_LORA + KV_LORA]
    kr = hc[:, Q_LORA + KV_LORA:Q_LORA + KV_LORA + C_PAD]
    krs = hc[:, Q_LORA + KV_LORA + C_PAD:]
    cqn = (cq * lax.rsqrt(jnp.mean(cq * cq, -1, keepdims=True) + 1e-6) * gq_ref[...]).astype(BF16)
    ckvn = (ckv * lax.rsqrt(jnp.mean(ckv * ckv, -1, keepdims=True) + 1e-6) * gkv_ref[...]).astype(BF16)
    cs = cs_ref[...]
    sn = sn_ref[...]
    q = _dot(cqn, wq_ref[...])
    qs = _dot(cqn, wqs_ref[...])
    k = _dot(ckvn, wk_ref[...])
    v = _dot(ckvn, wv_ref[...])
    krot = kr * cs + krs * sn
    scale = (C_NOPE + C_ROPE) ** -0.5
    for h in range(C_HEADS):
        sl = slice(h * C_PAD, (h + 1) * C_PAD)
        q_ref[h] = ((q[:, sl] * cs + qs[:, sl] * sn) * scale).astype(BF16)
        k_ref[h] = (k[:, sl] + krot).astype(BF16)
        v_ref[h] = v[:, sl].astype(BF16)


def _in_proj(x, wab, wc, wq, wqs, wk, wv, gq, gkv, cs, sn):
    T = x.shape[0]
    tm = TM_IN
    full = lambda a: pl.BlockSpec(a.shape, lambda i: (0,) * a.ndim)
    row = lambda w: pl.BlockSpec((tm, w), lambda i: (i, 0))
    head = pl.BlockSpec((C_HEADS, tm, C_PAD), lambda i: (0, i, 0))
    return pl.pallas_call(
        _in_proj_kernel,
        grid=(T // tm,),
        in_specs=[row(D_MODEL), full(wab), full(wc), full(wq), full(wqs), full(wk), full(wv), full(gq), full(gkv),
                  row(C_PAD), row(C_PAD)],
        out_specs=[row(AB_COLS), head, head, head],
        out_shape=[jax.ShapeDtypeStruct((T, AB_COLS), F32)] + [jax.ShapeDtypeStruct((C_HEADS, T, C_PAD), BF16)] * 3,
        compiler_params=_cparams("arbitrary"),
        name="in_proj",
    )(x, wab, wc, wq, wqs, wk, wv, gq, gkv, cs, sn)


def _banded_kernel(sink_ref, *refs, dil, n_heads, group, has_sink, want_lse):
    nq = n_heads // 2
    nkv = n_heads // group // 2
    q_refs, refs = refs[:nq], refs[nq:]
    k_refs, v_refs, kp_refs, vp_refs = (refs[i * nkv:(i + 1) * nkv] for i in range(4))
    bias_ref = refs[4 * nkv]
    o_refs = refs[4 * nkv + 1:4 * nkv + 1 + nq]
    lse_refs = refs[4 * nkv + 1 + nq:]
    c = pl.program_id(0)
    span = BLK * dil
    nb = CHUNK // span
    col = lax.broadcasted_iota(jnp.int32, (BLK, 2 * BLK), 1)
    first_mask = jnp.logical_and(c == 0, col < BLK)
    scale = HEAD_DIM ** -0.5

    def rows(start):
        return pl.ds(start, BLK) if dil == 1 else pl.ds(start, BLK, stride=dil)

    for b in range(nb):
        for r in range(dil):
            start = b * span + r
            q = [ref[rows(start), :].astype(BF16) for ref in q_refs]
            kcat, vcat = [], []
            for j in range(nkv):
                if b == 0:
                    kp, vp = kp_refs[j][rows(r), :], vp_refs[j][rows(r), :]
                else:
                    kp, vp = k_refs[j][rows(start - span), :], v_refs[j][rows(start - span), :]
                kcat.append(jnp.concatenate([kp, k_refs[j][rows(start), :]], axis=0).astype(BF16))
                vcat.append(jnp.concatenate([vp, v_refs[j][rows(start), :]], axis=0).astype(BF16))
            outs, lses = [], []
            for h in range(n_heads):
                hk = h // group
                ql = slice((h % 2) * HEAD_DIM, (h % 2 + 1) * HEAD_DIM)
                kl = slice((hk % 2) * HEAD_DIM, (hk % 2 + 1) * HEAD_DIM)
                s = _dot_nt(q[h // 2][:, ql], kcat[hk // 2][:, kl]) * scale + bias_ref[h]
                if b == 0:
                    s = jnp.where(first_mask, NEG_INF, s)
                m = jnp.max(s, axis=-1, keepdims=True)
                if has_sink:
                    snk = sink_ref[h]
                    m = jnp.maximum(m, snk)
                e = jnp.exp(s - m)
                l = jnp.sum(e, axis=-1, keepdims=True)
                if has_sink:
                    l = l + jnp.exp(snk - m)
                p = (e / l).astype(BF16)
                outs.append(_dot(p, vcat[hk // 2][:, kl]))
                if want_lse:
                    lses.append(jnp.broadcast_to(m + jnp.log(l), (BLK, HEAD_DIM)))
            for j in range(nq):
                o_refs[j][rows(start), :] = jnp.concatenate(outs[2 * j:2 * j + 2], axis=1)
                if want_lse:
                    lse_refs[j][rows(start), :] = jnp.concatenate(lses[2 * j:2 * j + 2], axis=1)


def _banded(hab, sink, bias, *, q_col, k_col, v_col, dil, n_heads, group, has_sink, want_lse):
    T = hab.shape[0]
    span = BLK * dil
    nb = CHUNK // span
    nq = n_heads // 2
    nkv = n_heads // group // 2
    kernel = functools.partial(_banded_kernel, dil=dil, n_heads=n_heads, group=group,
                               has_sink=has_sink, want_lse=want_lse)
    prev = lambda cb: pl.BlockSpec((span, LANES), lambda c, s: (jnp.maximum(c * nb - 1, 0), cb))
    cur = lambda cb: pl.BlockSpec((CHUNK, LANES), lambda c, s: (c, cb))
    in_specs = ([cur(q_col + j) for j in range(nq)] + [cur(k_col + j) for j in range(nkv)]
                + [cur(v_col + j) for j in range(nkv)] + [prev(k_col + j) for j in range(nkv)]
                + [prev(v_col + j) for j in range(nkv)] + [pl.BlockSpec(bias.shape, lambda c, s: (0, 0, 0))])
    n_out = nq * (2 if want_lse else 1)
    grid_spec = pltpu.PrefetchScalarGridSpec(
        num_scalar_prefetch=1,
        grid=(T // CHUNK,),
        in_specs=in_specs,
        out_specs=[pl.BlockSpec((CHUNK, LANES), lambda c, s: (c, 0))] * n_out,
    )
    outs = pl.pallas_call(
        kernel,
        grid_spec=grid_spec,
        out_shape=[jax.ShapeDtypeStruct((T, LANES), F32)] * n_out,
        compiler_params=_cparams("arbitrary"),
        name=f"banded_d{dil}" + ("_sink" if has_sink else ""),
    )(sink, *([hab] * (nq + 4 * nkv)), bias)
    return outs[:nq], outs[nq:]


def _flash_kernel(q_ref, k_ref, v_ref, o_ref, m_ref, l_ref, acc_ref):
    qi = pl.program_id(1)
    q = q_ref[...]
    m_ref[...] = jnp.full(m_ref.shape, NEG_INF, F32)
    l_ref[...] = jnp.zeros(l_ref.shape, F32)
    acc_ref[...] = jnp.zeros(acc_ref.shape, F32)

    def step(j, masked):
        off = pl.multiple_of(j * TK, TK)
        k = k_ref[pl.ds(off, TK), :]
        v = v_ref[pl.ds(off, TK), :]
        s = _dot_nt(q, k)
        if masked is not None:
            s = jnp.where(masked, s, NEG_INF)
        m_old = m_ref[...]
        m_new = jnp.maximum(m_old, jnp.max(s, axis=-1, keepdims=True))
        alpha = jnp.exp(m_old - m_new)
        p = jnp.exp(s - m_new)
        l_ref[...] = alpha * l_ref[...] + jnp.sum(p, axis=-1, keepdims=True)
        acc_ref[...] = alpha * acc_ref[...] + _dot(p.astype(BF16), v)
        m_ref[...] = m_new

    n_full = qi * (TQ // TK)

    def body(j, carry):
        step(j, None)
        return carry

    lax.fori_loop(0, n_full, body, 0)
    row = lax.broadcasted_iota(jnp.int32, (TQ, TK), 0)
    col = lax.broadcasted_iota(jnp.int32, (TQ, TK), 1)
    for d in range(TQ // TK):
        step(n_full + d, col + d * TK <= row)
    o_ref[...] = (acc_ref[...] / l_ref[...]).astype(o_ref.dtype)


def _flash(q, k, v):
    H, T, _ = q.shape
    return pl.pallas_call(
        _flash_kernel,
        grid=(H, T // TQ),
        in_specs=[pl.BlockSpec((None, TQ, C_PAD), lambda h, i: (h, i, 0)),
                  pl.BlockSpec((None, T, C_PAD), lambda h, i: (h, 0, 0)),
                  pl.BlockSpec((None, T, C_PAD), lambda h, i: (h, 0, 0))],
        out_specs=pl.BlockSpec((None, TQ, C_PAD), lambda h, i: (h, i, 0)),
        out_shape=jax.ShapeDtypeStruct((H, T, C_PAD), BF16),
        scratch_shapes=[pltpu.VMEM((TQ, 1), F32), pltpu.VMEM((TQ, 1), F32), pltpu.VMEM((TQ, C_PAD), F32)],
        compiler_params=_cparams("arbitrary", "arbitrary"),
        name="mla_flash",
    )(q, k, v)


def _layer_norm(z, g, b):
    mu = jnp.mean(z, -1, keepdims=True)
    zc = z - mu
    var = jnp.mean(zc * zc, -1, keepdims=True)
    return zc * lax.rsqrt(var + 1e-5) * g + b


def _out_router_kernel(*refs):
    n_ab = 2 + 6 + 6
    oa_refs, ob_refs, lse_refs = refs[0:2], refs[2:8], refs[8:14]
    (oc_ref, x_ref, woa_ref, wob_ref, woc_ref, g_ref, b_ref, wr_ref, br_ref,
     x1_ref, slab_ref, cnt_ref, carry_ref) = refs[n_ab:]
    i = pl.program_id(0)
    tm = x_ref.shape[0]

    @pl.when(i == 0)
    def _():
        carry_ref[...] = jnp.zeros(carry_ref.shape, F32)

    y = None
    for j in range(2):
        l0, l1, l2 = (lse_refs[2 * g + j][...] for g in range(3))
        mx = jnp.maximum(jnp.maximum(l0, l1), l2)
        e0, e1, e2 = jnp.exp(l0 - mx), jnp.exp(l1 - mx), jnp.exp(l2 - mx)
        den = e0 + e1 + e2
        ob = ((e0 / den) * ob_refs[j][...] + (e1 / den) * ob_refs[2 + j][...] + (e2 / den) * ob_refs[4 + j][...])
        part = _dot(oa_refs[j][...].astype(BF16), woa_ref[j]) + _dot(ob.astype(BF16), wob_ref[j])
        y = part if y is None else y + part
    for h in range(C_HEADS):
        y = y + _dot(oc_ref[h], woc_ref[h])
    x1 = _layer_norm(DEEPNORM_ALPHA * x_ref[...] + y, g_ref[...], b_ref[...])
    x1_ref[...] = x1

    logits = jnp.dot(x1, wr_ref[...], precision=lax.Precision.HIGHEST, preferred_element_type=F32) + br_ref[...]
    lane = lax.broadcasted_iota(jnp.int32, (tm, LANES), 1)
    ninf = jnp.float32(-jnp.inf)
    gmask = lane < N_GROUPS
    gl = jnp.where(gmask, logits, ninf)
    gmax = jnp.max(gl, axis=-1, keepdims=True)
    grp = jnp.min(jnp.where(gl == gmax, lane, LANES), axis=-1, keepdims=True)
    p_grp = 1.0 / jnp.sum(jnp.where(gmask, jnp.exp(logits - gmax), 0.0), axis=-1, keepdims=True)
    lo = N_GROUPS + EXPERTS_PER_GROUP * grp
    emask = jnp.logical_and(lane >= lo, lane < lo + EXPERTS_PER_GROUP)
    el = jnp.where(emask, logits, ninf)
    v0 = jnp.max(el, axis=-1, keepdims=True)
    i0 = jnp.min(jnp.where(el == v0, lane, LANES), axis=-1, keepdims=True)
    el2 = jnp.where(lane == i0, ninf, el)
    v1 = jnp.max(el2, axis=-1, keepdims=True)
    i1 = jnp.min(jnp.where(el2 == v1, lane, LANES), axis=-1, keepdims=True)
    t = jnp.exp(v1 - v0)
    gate0 = p_grp / (1.0 + t)
    gate1 = p_grp * t / (1.0 + t)

    hit0 = lane == i0
    hit1 = lane == i1
    cmat = jnp.logical_or(hit0, hit1).astype(BF16)
    tr = lax.broadcasted_iota(jnp.int32, (tm, tm), 0)
    tc = lax.broadcasted_iota(jnp.int32, (tm, tm), 1)
    tril = (tc < tr).astype(BF16)
    prefix = _dot(tril, cmat) + carry_ref[...]
    r0 = jnp.sum(jnp.where(hit0, prefix, 0.0), axis=-1, keepdims=True)
    r1 = jnp.sum(jnp.where(hit1, prefix, 0.0), axis=-1, keepdims=True)
    carry_ref[...] = carry_ref[...] + jnp.sum(cmat.astype(F32), axis=0, keepdims=True)
    cnt_ref[...] = carry_ref[...]

    fields = [(i0 - N_GROUPS).astype(F32), (i1 - N_GROUPS).astype(F32), r0, r1, gate0, gate1]
    slab = jnp.zeros((tm, LANES), F32)
    for n, f in enumerate(fields):
        slab = jnp.where(lane == n, f, slab)
    slab_ref[...] = slab


def _out_router(oa, ob, lse, oc, x, woa, wob, woc, g, b, wr, br):
    T = x.shape[0]
    tm = TM_OUT
    full = lambda a: pl.BlockSpec(a.shape, lambda i: (0,) * a.ndim)
    row = lambda w: pl.BlockSpec((tm, w), lambda i: (i, 0))
    return pl.pallas_call(
        _out_router_kernel,
        grid=(T // tm,),
        in_specs=[row(LANES)] * 14 + [pl.BlockSpec((C_HEADS, tm, C_PAD), lambda i: (0, i, 0)), row(D_MODEL),
                                      full(woa), full(wob), full(woc), full(g), full(b), full(wr), full(br)],
        out_specs=[row(D_MODEL), row(LANES), pl.BlockSpec((1, LANES), lambda i: (0, 0))],
        out_shape=[jax.ShapeDtypeStruct((T, D_MODEL), F32), jax.ShapeDtypeStruct((T, LANES), F32),
                   jax.ShapeDtypeStruct((1, LANES), F32)],
        scratch_shapes=[pltpu.VMEM((1, LANES), F32)],
        compiler_params=_cparams("arbitrary"),
        name="out_router",
    )(*oa, *ob, *lse, oc, x, woa, wob, woc, g, b, wr, br)


def _dispatch_kernel(d0_ref, d1_ref, x_ref, zeros_ref, xs_ref, sem):
    del zeros_ref
    i = pl.program_id(0)
    tm = x_ref.shape[0]
    base = i * tm

    def copy(t, d_ref):
        return pltpu.make_async_copy(x_ref.at[pl.ds(t, 1), :], xs_ref.at[pl.ds(d_ref[base + t], 1), :], sem)

    def start(t, carry):
        copy(t, d0_ref).start()
        copy(t, d1_ref).start()
        return carry

    def wait(t, carry):
        copy(t, d0_ref).wait()
        copy(t, d1_ref).wait()
        return carry

    lax.fori_loop(0, tm, start, 0)
    lax.fori_loop(0, tm, wait, 0)


def _dispatch(d0, d1, x1, n_rows):
    T = x1.shape[0]
    tm = ROW_BLK
    zeros = jnp.zeros((n_rows, D_MODEL), F32)
    grid_spec = pltpu.PrefetchScalarGridSpec(
        num_scalar_prefetch=2,
        grid=(T // tm,),
        in_specs=[pl.BlockSpec((tm, D_MODEL), lambda i, a, b: (i, 0)), pl.BlockSpec(memory_space=pl.ANY)],
        out_specs=pl.BlockSpec(memory_space=pl.ANY),
        scratch_shapes=[pltpu.SemaphoreType.DMA],
    )
    return pl.pallas_call(
        _dispatch_kernel,
        grid_spec=grid_spec,
        out_shape=jax.ShapeDtypeStruct((n_rows, D_MODEL), F32),
        input_output_aliases={3: 0},
        compiler_params=_cparams("arbitrary"),
        name="moe_dispatch",
    )(d0, d1, x1, zeros)


def _ffn_kernel(be_ref, nu_ref, xs_ref, wg_ref, wu_ref, wd_ref, ys_ref):
    i = pl.program_id(0)

    @pl.when(i < nu_ref[0])
    def _():
        xb = xs_ref[...].astype(BF16)
        a = _dot(xb, wg_ref[...])
        u = _dot(xb, wu_ref[...])
        hmid = (a * jax.nn.sigmoid(a) * u).astype(BF16)
        ys_ref[...] = _dot(hmid, wd_ref[...])

    @pl.when(i >= nu_ref[0])
    def _():
        ys_ref[...] = jnp.zeros(ys_ref.shape, F32)


def _ffn(block_expert, n_used, xs, wg, wu, wd):
    n_rows = xs.shape[0]
    grid_spec = pltpu.PrefetchScalarGridSpec(
        num_scalar_prefetch=2,
        grid=(n_rows // ROW_BLK,),
        in_specs=[pl.BlockSpec((ROW_BLK, D_MODEL), lambda i, be, nu: (i, 0)),
                  pl.BlockSpec((None, D_MODEL, D_EXPERT), lambda i, be, nu: (be[i], 0, 0)),
                  pl.BlockSpec((None, D_MODEL, D_EXPERT), lambda i, be, nu: (be[i], 0, 0)),
                  pl.BlockSpec((None, D_EXPERT, D_MODEL), lambda i, be, nu: (be[i], 0, 0))],
        out_specs=pl.BlockSpec((ROW_BLK, D_MODEL), lambda i, be, nu: (i, 0)),
    )
    return pl.pallas_call(
        _ffn_kernel,
        grid_spec=grid_spec,
        out_shape=jax.ShapeDtypeStruct((n_rows, D_MODEL), F32),
        compiler_params=_cparams("arbitrary"),
        name="moe_ffn",
    )(block_expert, n_used, xs, wg, wu, wd)


def _combine_kernel(d0_ref, d1_ref, ys_ref, x1_ref, slab_ref, g_ref, b_ref, o_ref, buf_ref, sem):
    i = pl.program_id(0)
    tm = x1_ref.shape[0]
    base = i * tm

    def copy(t, d_ref, slot):
        return pltpu.make_async_copy(ys_ref.at[pl.ds(d_ref[base + t], 1), :], buf_ref.at[slot, pl.ds(t, 1), :], sem)

    def start(t, carry):
        copy(t, d0_ref, 0).start()
        copy(t, d1_ref, 1).start()
        return carry

    def wait(t, carry):
        copy(t, d0_ref, 0).wait()
        copy(t, d1_ref, 1).wait()
        return carry

    lax.fori_loop(0, tm, start, 0)
    lax.fori_loop(0, tm, wait, 0)
    slab = slab_ref[...]
    y = slab[:, 4:5] * buf_ref[0] + slab[:, 5:6] * buf_ref[1]
    o_ref[...] = _layer_norm(DEEPNORM_ALPHA * x1_ref[...] + y, g_ref[...], b_ref[...])


def _combine(d0, d1, ys, x1, slab, g, b):
    T = x1.shape[0]
    tm = ROW_BLK
    grid_spec = pltpu.PrefetchScalarGridSpec(
        num_scalar_prefetch=2,
        grid=(T // tm,),
        in_specs=[pl.BlockSpec(memory_space=pl.ANY),
                  pl.BlockSpec((tm, D_MODEL), lambda i, a, c: (i, 0)),
                  pl.BlockSpec((tm, LANES), lambda i, a, c: (i, 0)),
                  pl.BlockSpec((1, D_MODEL), lambda i, a, c: (0, 0)),
                  pl.BlockSpec((1, D_MODEL), lambda i, a, c: (0, 0))],
        out_specs=pl.BlockSpec((tm, D_MODEL), lambda i, a, c: (i, 0)),
        scratch_shapes=[pltpu.VMEM((2, tm, D_MODEL), F32), pltpu.SemaphoreType.DMA],
    )
    return pl.pallas_call(
        _combine_kernel,
        grid_spec=grid_spec,
        out_shape=jax.ShapeDtypeStruct((T, D_MODEL), F32),
        compiler_params=_cparams("arbitrary"),
        name="moe_combine",
    )(d0, d1, ys, x1, slab, g, b)


def _t5_bucket(dist):
    max_exact = N_BUCKETS // 2
    large = max_exact + (jnp.log(jnp.maximum(dist, 1).astype(F32) / max_exact)
                         / math.log(MAX_DISTANCE / max_exact) * (N_BUCKETS - max_exact)).astype(jnp.int32)
    return jnp.where(dist < max_exact, dist, jnp.minimum(large, N_BUCKETS - 1))


def _bias_table(table_cols, dil, max_dist):
    dist = BLK + jnp.arange(BLK)[:, None] - jnp.arange(2 * BLK)[None, :]
    bias = table_cols[_t5_bucket(jnp.maximum(dist * dil, 0))].transpose(2, 0, 1).astype(F32)
    mask = (dist >= 0) & (dist <= max_dist)
    return jnp.where(mask[None], bias, NEG_INF)


def _rope_tables(T):
    inv_freq = ROPE_THETA ** (-jnp.arange(0, C_ROPE, 2, dtype=F32) / C_ROPE)
    ang = jnp.arange(T, dtype=F32)[:, None] * inv_freq[None, :]
    cos, sin = jnp.cos(ang), jnp.sin(ang)
    ones = jnp.ones((T, C_NOPE), F32)
    zeros_n = jnp.zeros((T, C_NOPE), F32)
    zeros_p = jnp.zeros((T, C_PAD - C_NOPE - C_ROPE), F32)
    cs = jnp.concatenate([ones, cos, cos, zeros_p], axis=1)
    sn = jnp.concatenate([zeros_n, -sin, sin, zeros_p], axis=1)
    return cs, sn


def _swap_halves(w):
    half = w.shape[-1] // 2
    return jnp.concatenate([w[..., half:], w[..., :half]], axis=-1)


def _pad_cols(w, width):
    return jnp.pad(w, ((0, 0), (0, width - w.shape[1])))


def _prep_layer_weights(w_in, w_out, w_uq, w_ukv, w_rg, b_rg, w_re, b_re):
    wab = w_in[:, :AB_COLS].astype(BF16)
    c0 = AB_COLS
    w_cq = w_in[:, c0:c0 + Q_LORA]
    w_ckv = w_in[:, c0 + Q_LORA:c0 + Q_LORA + KV_LORA]
    w_kr = w_in[:, c0 + Q_LORA + KV_LORA:]
    lead = jnp.zeros((D_MODEL, C_NOPE), F32)
    wc = jnp.concatenate([w_cq, w_ckv,
                          _pad_cols(jnp.concatenate([lead, w_kr], 1), C_PAD),
                          _pad_cols(jnp.concatenate([lead, _swap_halves(w_kr)], 1), C_PAD)], axis=1).astype(BF16)
    uq = w_uq.reshape(Q_LORA, C_HEADS, C_NOPE + C_ROPE)
    wq = jnp.pad(uq, ((0, 0), (0, 0), (0, C_PAD - C_NOPE - C_ROPE))).reshape(Q_LORA, C_HEADS * C_PAD).astype(BF16)
    uqs = jnp.concatenate([jnp.zeros((Q_LORA, C_HEADS, C_NOPE), F32), _swap_halves(uq[..., C_NOPE:])], axis=-1)
    wqs = jnp.pad(uqs, ((0, 0), (0, 0), (0, C_PAD - C_NOPE - C_ROPE))).reshape(Q_LORA, C_HEADS * C_PAD).astype(BF16)
    ukv = w_ukv.reshape(KV_LORA, C_HEADS, C_NOPE + C_V)
    wk = jnp.pad(ukv[..., :C_NOPE], ((0, 0), (0, 0), (0, C_PAD - C_NOPE))).reshape(KV_LORA, C_HEADS * C_PAD)
    wv = jnp.pad(ukv[..., C_NOPE:], ((0, 0), (0, 0), (0, C_PAD - C_V))).reshape(KV_LORA, C_HEADS * C_PAD)
    woa = w_out[:256].reshape(2, LANES, D_MODEL).astype(BF16)
    wob = w_out[256:512].reshape(2, LANES, D_MODEL).astype(BF16)
    woc = jnp.pad(w_out[512:].reshape(C_HEADS, C_V, D_MODEL), ((0, 0), (0, C_PAD - C_V), (0, 0))).astype(BF16)
    wr = _pad_cols(jnp.concatenate([w_rg, w_re.transpose(1, 0, 2).reshape(D_MODEL, N_EXPERTS)], axis=1), LANES)
    br = _pad_cols(jnp.concatenate([b_rg, b_re.reshape(N_EXPERTS)])[None, :], LANES)
    return dict(wab=wab, wc=wc, wq=wq, wqs=wqs, wk=wk.astype(BF16), wv=wv.astype(BF16),
                woa=woa, wob=wob, woc=woc, wr=wr, br=br)


def kernel(x, w_in, w_out, sinks, rel_bias, mla_q_norm, mla_kv_norm, w_uq, w_ukv, ln1_g, ln1_b,
           w_route_group, b_route_group, w_route_expert, b_route_expert,
           w_expert_gate, w_expert_up, w_expert_down, ln2_g, ln2_b):
    Bsz, T, D = x.shape
    assert Bsz == 1 and D == D_MODEL and T % CHUNK == 0
    xt = x.reshape(T, D)
    cs, sn = _rope_tables(T)
    bias_a = _bias_table(rel_bias[:, :A_Q_HEADS], 1, SWA_WINDOW - 1)
    bias_b = [_bias_table(rel_bias[:, A_Q_HEADS + g * B_HEADS_PER_GROUP:A_Q_HEADS + (g + 1) * B_HEADS_PER_GROUP],
                          dil, window // dil) for g, (window, dil) in enumerate(DILATED_PAIRS)]
    n_assign = T * TOP_K
    n_blocks = n_assign // ROW_BLK + N_EXPERTS
    n_rows = n_blocks * ROW_BLK
    no_sink = jnp.zeros((B_HEADS_PER_GROUP,), F32)

    for layer in range(w_in.shape[0]):
        w = _prep_layer_weights(w_in[layer], w_out[layer], w_uq[layer], w_ukv[layer],
                                w_route_group[layer], b_route_group[layer],
                                w_route_expert[layer], b_route_expert[layer])
        hab, q, k, v = _in_proj(xt, w["wab"], w["wc"], w["wq"], w["wqs"], w["wk"], w["wv"],
                                mla_q_norm[layer][None, :], mla_kv_norm[layer][None, :], cs, sn)
        oa, _ = _banded(hab, sinks[layer], bias_a, q_col=0, k_col=2, v_col=3, dil=1,
                        n_heads=A_Q_HEADS, group=A_Q_HEADS // A_KV_HEADS, has_sink=True, want_lse=False)
        ob, lse = [], []
        for g, (_, dil) in enumerate(DILATED_PAIRS):
            o_g, lse_g = _banded(hab, no_sink, bias_b[g], q_col=4 + 2 * g, k_col=10 + 2 * g, v_col=16 + 2 * g,
                                 dil=dil, n_heads=B_HEADS_PER_GROUP, group=1, has_sink=False, want_lse=True)
            ob += o_g
            lse += lse_g
        oc = _flash(q, k, v)
        x1, slab, cnt = _out_router(oa, ob, lse, oc, xt, w["woa"], w["wob"], w["woc"],
                                    ln1_g[layer][None, :], ln1_b[layer][None, :], w["wr"], w["br"])

        counts = cnt[0, N_GROUPS:N_GROUPS + N_EXPERTS].astype(jnp.int32)
        padded = ((counts + ROW_BLK - 1) // ROW_BLK) * ROW_BLK
        pad_end = jnp.cumsum(padded)
        pad_start = pad_end - padded
        e0 = slab[:, 0].astype(jnp.int32)
        e1 = slab[:, 1].astype(jnp.int32)
        d0 = pad_start[e0] + slab[:, 2].astype(jnp.int32)
        d1 = pad_start[e1] + slab[:, 3].astype(jnp.int32)
        n_used = (pad_end[-1:] // ROW_BLK).astype(jnp.int32)
        block_expert = jnp.minimum(
            jnp.searchsorted(pad_end, jnp.arange(n_blocks, dtype=jnp.int32) * ROW_BLK, side='right'),
            N_EXPERTS - 1).astype(jnp.int32)

        xs = _dispatch(d0, d1, x1, n_rows)
        ys = _ffn(block_expert, n_used, xs, w_expert_gate[layer].astype(BF16), w_expert_up[layer].astype(BF16),
                  w_expert_down[layer].astype(BF16))
        xt = _combine(d0, d1, ys, x1, slab, ln2_g[layer][None, :], ln2_b[layer][None, :])
    return xt.reshape(Bsz, T, D)
```

```python
import functools
import math

import jax
import jax.numpy as jnp
from jax import lax
from jax.experimental import pallas as pl
from jax.experimental.pallas import tpu as pltpu

F32 = jnp.float32
BF16 = jnp.bfloat16

D_MODEL = 1024
DEPTH = 4
HEAD_DIM = 64
BLK = 128
NEG_INF = -1e30

SWA_WINDOW = 128
A_Q_HEADS = 4
A_KV_HEADS = 2
A_COLS = (A_Q_HEADS + 2 * A_KV_HEADS) * HEAD_DIM

DILATED_PAIRS = ((128, 1), (512, 4), (2048, 16))
B_HEADS_PER_GROUP = 4
B_HEADS = len(DILATED_PAIRS) * B_HEADS_PER_GROUP
B_COLS = 3 * B_HEADS * HEAD_DIM
AB_COLS = A_COLS + B_COLS

C_HEADS = 8
C_NOPE = 64
C_ROPE = 32
C_V = 64
Q_LORA = 256
KV_LORA = 128
ROPE_THETA = 10000.0
C_PAD = 128

N_BUCKETS = 32
MAX_DISTANCE = 2048

N_GROUPS = 4
EXPERTS_PER_GROUP = 8
N_EXPERTS = N_GROUPS * EXPERTS_PER_GROUP
TOP_K = 2
D_EXPERT = 256
ROW_BLK = 256

DEEPNORM_ALPHA = (2 * DEPTH) ** 0.25

VMEM_LIMIT = 56 * 1024 * 1024

CHUNK = 2048
TM_IN = 256
TM_OUT = 256
TQ = 512
TK = 512
FLASH_UNROLL = 4
LANES = 128


def _dot(a, b):
    return jnp.dot(a, b, preferred_element_type=F32)


def _dot_nt(a, b):
    return lax.dot_general(a, b, (((1,), (1,)), ((), ())), preferred_element_type=F32)


def _cparams(*sem):
    return pltpu.CompilerParams(dimension_semantics=sem, vmem_limit_bytes=VMEM_LIMIT)


def _in_proj_kernel(x_ref, wab_ref, wc_ref, wq_ref, wqs_ref, wk_ref, wv_ref, gq_ref, gkv_ref, cs_ref, sn_ref,
                    hab_ref, qt_ref, k_ref, vt_ref):
    xb = x_ref[...].astype(BF16)
    hab_ref[...] = _dot(xb, wab_ref[...])
    hc = _dot(xb, wc_ref[...])
    cq = hc[:, :Q_LORA]
    ckv = hc[:, Q_LORA:Q_LORA + KV_LORA]
    kr = hc[:, Q_LORA + KV_LORA:Q_LORA + KV_LORA + C_PAD]
    krs = hc[:, Q_LORA + KV_LORA + C_PAD:]
    cqn = (cq * lax.rsqrt(jnp.mean(cq * cq, -1, keepdims=True) + 1e-6) * gq_ref[...]).astype(BF16)
    ckvn = (ckv * lax.rsqrt(jnp.mean(ckv * ckv, -1, keepdims=True) + 1e-6) * gkv_ref[...]).astype(BF16)
    cs = cs_ref[...]
    sn = sn_ref[...]
    q = _dot(cqn, wq_ref[...])
    qs = _dot(cqn, wqs_ref[...])
    k = _dot(ckvn, wk_ref[...])
    v = _dot(ckvn, wv_ref[...])
    krot = kr * cs + krs * sn
    scale = (C_NOPE + C_ROPE) ** -0.5 * math.log2(math.e)
    for h in range(C_HEADS):
        sl = slice(h * C_PAD, (h + 1) * C_PAD)
        qh = (q[:, sl] * cs + qs[:, sl] * sn) * scale
        qt_ref[sl, :] = qh.T.astype(BF16)
        k_ref[h] = (k[:, sl] + krot).astype(BF16)
    vt_ref[...] = v.T.astype(BF16)


def _in_proj(x, wab, wc, wq, wqs, wk, wv, gq, gkv, cs, sn):
    T = x.shape[0]
    tm = TM_IN
    full = lambda a: pl.BlockSpec(a.shape, lambda i: (0,) * a.ndim)
    row = lambda w: pl.BlockSpec((tm, w), lambda i: (i, 0))
    colblk = lambda r: pl.BlockSpec((r, tm), lambda i: (0, i))
    return pl.pallas_call(
        _in_proj_kernel,
        grid=(T // tm,),
        in_specs=[row(D_MODEL), full(wab), full(wc), full(wq), full(wqs), full(wk), full(wv), full(gq), full(gkv),
                  row(C_PAD), row(C_PAD)],
        out_specs=[row(AB_COLS), colblk(C_HEADS * C_PAD), pl.BlockSpec((C_HEADS, tm, C_PAD), lambda i: (0, i, 0)),
                   colblk(C_HEADS * C_V)],
        out_shape=[jax.ShapeDtypeStruct((T, AB_COLS), F32), jax.ShapeDtypeStruct((C_HEADS * C_PAD, T), BF16),
                   jax.ShapeDtypeStruct((C_HEADS, T, C_PAD), BF16), jax.ShapeDtypeStruct((C_HEADS * C_V, T), BF16)],
        compiler_params=_cparams("arbitrary"),
        name="in_proj",
    )(x, wab, wc, wq, wqs, wk, wv, gq, gkv, cs, sn)


def _banded_kernel(sink_ref, *refs, dil, n_heads, group, has_sink, want_lse):
    nq = n_heads // 2
    nkv = n_heads // group // 2
    q_refs, refs = refs[:nq], refs[nq:]
    k_refs, v_refs, kp_refs, vp_refs = (refs[i * nkv:(i + 1) * nkv] for i in range(4))
    bias_ref = refs[4 * nkv]
    o_refs = refs[4 * nkv + 1:4 * nkv + 1 + nq]
    lse_refs = refs[4 * nkv + 1 + nq:]
    c = pl.program_id(0)
    span = BLK * dil
    nb = CHUNK // span
    col = lax.broadcasted_iota(jnp.int32, (BLK, 2 * BLK), 1)
    first_mask = jnp.logical_and(c == 0, col < BLK)
    scale = HEAD_DIM ** -0.5

    def rows(start):
        return pl.ds(start, BLK) if dil == 1 else pl.ds(start, BLK, stride=dil)

    for b in range(nb):
        for r in range(dil):
            start = b * span + r
            q = [ref[rows(start), :].astype(BF16) for ref in q_refs]
            kcat, vcat = [], []
            for j in range(nkv):
                if b == 0:
                    kp, vp = kp_refs[j][rows(r), :], vp_refs[j][rows(r), :]
                else:
                    kp, vp = k_refs[j][rows(start - span), :], v_refs[j][rows(start - span), :]
                kcat.append(jnp.concatenate([kp, k_refs[j][rows(start), :]], axis=0).astype(BF16))
                vcat.append(jnp.concatenate([vp, v_refs[j][rows(start), :]], axis=0).astype(BF16))
            outs, lses = [], []
            for h in range(n_heads):
                hk = h // group
                ql = slice((h % 2) * HEAD_DIM, (h % 2 + 1) * HEAD_DIM)
                kl = slice((hk % 2) * HEAD_DIM, (hk % 2 + 1) * HEAD_DIM)
                s = _dot_nt(q[h // 2][:, ql], kcat[hk // 2][:, kl]) * scale + bias_ref[h]
                if b == 0:
                    s = jnp.where(first_mask, NEG_INF, s)
                m = jnp.max(s, axis=-1, keepdims=True)
                if has_sink:
                    snk = sink_ref[h]
                    m = jnp.maximum(m, snk)
                e = jnp.exp(s - m)
                l = jnp.sum(e, axis=-1, keepdims=True)
                if has_sink:
                    l = l + jnp.exp(snk - m)
                p = (e / l).astype(BF16)
                outs.append(_dot(p, vcat[hk // 2][:, kl]))
                if want_lse:
                    lses.append(jnp.broadcast_to(m + jnp.log(l), (BLK, HEAD_DIM)))
            for j in range(nq):
                o_refs[j][rows(start), :] = jnp.concatenate(outs[2 * j:2 * j + 2], axis=1)
                if want_lse:
                    lse_refs[j][rows(start), :] = jnp.concatenate(lses[2 * j:2 * j + 2], axis=1)


def _banded(hab, sink, bias, *, q_col, k_col, v_col, dil, n_heads, group, has_sink, want_lse):
    T = hab.shape[0]
    span = BLK * dil
    nb = CHUNK // span
    nq = n_heads // 2
    nkv = n_heads // group // 2
    kernel = functools.partial(_banded_kernel, dil=dil, n_heads=n_heads, group=group,
                               has_sink=has_sink, want_lse=want_lse)
    prev = lambda cb: pl.BlockSpec((span, LANES), lambda c, s: (jnp.maximum(c * nb - 1, 0), cb))
    cur = lambda cb: pl.BlockSpec((CHUNK, LANES), lambda c, s: (c, cb))
    in_specs = ([cur(q_col + j) for j in range(nq)] + [cur(k_col + j) for j in range(nkv)]
                + [cur(v_col + j) for j in range(nkv)] + [prev(k_col + j) for j in range(nkv)]
                + [prev(v_col + j) for j in range(nkv)] + [pl.BlockSpec(bias.shape, lambda c, s: (0, 0, 0))])
    n_out = nq * (2 if want_lse else 1)
    grid_spec = pltpu.PrefetchScalarGridSpec(
        num_scalar_prefetch=1,
        grid=(T // CHUNK,),
        in_specs=in_specs,
        out_specs=[pl.BlockSpec((CHUNK, LANES), lambda c, s: (c, 0))] * n_out,
    )
    outs = pl.pallas_call(
        kernel,
        grid_spec=grid_spec,
        out_shape=[jax.ShapeDtypeStruct((T, LANES), F32)] * n_out,
        compiler_params=_cparams("arbitrary"),
        name=f"banded_d{dil}" + ("_sink" if has_sink else ""),
    )(sink, *([hab] * (nq + 4 * nkv)), bias)
    return outs[:nq], outs[nq:]


def _flash_kernel(qt_ref, k_ref, vt_ref, ot_ref, m_ref, l_ref, acc_ref, s_ref):
    qi = pl.program_id(1)
    qt = qt_ref[...]
    m_ref[...] = jnp.full(m_ref.shape, NEG_INF, F32)
    l_ref[...] = jnp.zeros(l_ref.shape, F32)
    acc_ref[...] = jnp.zeros(acc_ref.shape, F32)

    def scores(j):
        off = pl.multiple_of(j * TK, TK)
        return _dot(k_ref[pl.ds(off, TK), :], qt)

    def accumulate(j, s):
        off = pl.multiple_of(j * TK, TK)
        m_old = m_ref[...]
        m_new = jnp.maximum(m_old, jnp.max(s, axis=0, keepdims=True))
        alpha = jnp.exp2(m_old - m_new)
        p = jnp.exp2(s - m_new)
        l_ref[...] = alpha * l_ref[...] + jnp.sum(p, axis=0, keepdims=True)
        acc_ref[...] = alpha * acc_ref[...] + _dot(vt_ref[:, pl.ds(off, TK)], p.astype(BF16))
        m_ref[...] = m_new

    def diagonal(s):
        key = lax.broadcasted_iota(jnp.int32, (TK, TQ), 0)
        qry = lax.broadcasted_iota(jnp.int32, (TK, TQ), 1)
        return jnp.where(key <= qry, s, NEG_INF)

    s_ref[0] = scores(0)
    n_main = qi // FLASH_UNROLL

    def main_body(i, carry):
        j = FLASH_UNROLL * i
        for u in range(FLASH_UNROLL):
            s_ref[(u + 1) % 2] = scores(j + u + 1)
            accumulate(j + u, s_ref[u % 2])
        return carry

    lax.fori_loop(0, n_main, main_body, 0)

    base = FLASH_UNROLL * n_main
    for rem in range(FLASH_UNROLL):
        @pl.when(qi - base == rem)
        def _():
            for u in range(rem):
                s_ref[(u + 1) % 2] = scores(base + u + 1)
                accumulate(base + u, s_ref[u % 2])
            accumulate(qi, diagonal(s_ref[rem % 2]))

    ot_ref[...] = (acc_ref[...] / l_ref[...]).astype(ot_ref.dtype)


def _flash(qt, k, vt):
    H, T, _ = k.shape
    return pl.pallas_call(
        _flash_kernel,
        grid=(H, T // TQ),
        in_specs=[pl.BlockSpec((C_PAD, TQ), lambda h, i: (h, i)),
                  pl.BlockSpec((None, T, C_PAD), lambda h, i: (h, 0, 0)),
                  pl.BlockSpec((C_V, T), lambda h, i: (h, 0))],
        out_specs=pl.BlockSpec((C_V, TQ), lambda h, i: (h, i)),
        out_shape=jax.ShapeDtypeStruct((H * C_V, T), BF16),
        scratch_shapes=[pltpu.VMEM((1, TQ), F32), pltpu.VMEM((1, TQ), F32), pltpu.VMEM((C_V, TQ), F32),
                        pltpu.VMEM((2, TK, TQ), F32)],
        compiler_params=_cparams("arbitrary", "arbitrary"),
        name="mla_flash",
    )(qt, k, vt)


def _layer_norm(z, g, b):
    mu = jnp.mean(z, -1, keepdims=True)
    zc = z - mu
    var = jnp.mean(zc * zc, -1, keepdims=True)
    return zc * lax.rsqrt(var + 1e-5) * g + b


def _out_router_kernel(*refs):
    n_ab = 2 + 6 + 6
    oa_refs, ob_refs, lse_refs = refs[0:2], refs[2:8], refs[8:14]
    (oc_ref, x_ref, woa_ref, wob_ref, woc_ref, g_ref, b_ref, wr_ref, br_ref,
     x1_ref, slab_ref, cnt_ref, carry_ref) = refs[n_ab:]
    i = pl.program_id(0)
    tm = x_ref.shape[0]

    @pl.when(i == 0)
    def _():
        carry_ref[...] = jnp.zeros(carry_ref.shape, F32)

    y = None
    for j in range(2):
        l0, l1, l2 = (lse_refs[2 * g + j][...] for g in range(3))
        mx = jnp.maximum(jnp.maximum(l0, l1), l2)
        e0, e1, e2 = jnp.exp(l0 - mx), jnp.exp(l1 - mx), jnp.exp(l2 - mx)
        den = e0 + e1 + e2
        ob = ((e0 / den) * ob_refs[j][...] + (e1 / den) * ob_refs[2 + j][...] + (e2 / den) * ob_refs[4 + j][...])
        part = _dot(oa_refs[j][...].astype(BF16), woa_ref[j]) + _dot(ob.astype(BF16), wob_ref[j])
        y = part if y is None else y + part
    y = y + _dot(oc_ref[...].astype(F32).T.astype(BF16), woc_ref[...])
    x1 = _layer_norm(DEEPNORM_ALPHA * x_ref[...] + y, g_ref[...], b_ref[...])
    x1_ref[...] = x1

    logits = jnp.dot(x1, wr_ref[...], precision=lax.Precision.HIGHEST, preferred_element_type=F32) + br_ref[...]
    lane = lax.broadcasted_iota(jnp.int32, (tm, LANES), 1)
    ninf = jnp.float32(-jnp.inf)
    gmask = lane < N_GROUPS
    gl = jnp.where(gmask, logits, ninf)
    gmax = jnp.max(gl, axis=-1, keepdims=True)
    grp = jnp.min(jnp.where(gl == gmax, lane, LANES), axis=-1, keepdims=True)
    p_grp = 1.0 / jnp.sum(jnp.where(gmask, jnp.exp(logits - gmax), 0.0), axis=-1, keepdims=True)
    lo = N_GROUPS + EXPERTS_PER_GROUP * grp
    emask = jnp.logical_and(lane >= lo, lane < lo + EXPERTS_PER_GROUP)
    el = jnp.where(emask, logits, ninf)
    v0 = jnp.max(el, axis=-1, keepdims=True)
    i0 = jnp.min(jnp.where(el == v0, lane, LANES), axis=-1, keepdims=True)
    el2 = jnp.where(lane == i0, ninf, el)
    v1 = jnp.max(el2, axis=-1, keepdims=True)
    i1 = jnp.min(jnp.where(el2 == v1, lane, LANES), axis=-1, keepdims=True)
    t = jnp.exp(v1 - v0)
    gate0 = p_grp / (1.0 + t)
    gate1 = p_grp * t / (1.0 + t)

    hit0 = lane == i0
    hit1 = lane == i1
    cmat = jnp.logical_or(hit0, hit1).astype(BF16)
    tr = lax.broadcasted_iota(jnp.int32, (tm, tm), 0)
    tc = lax.broadcasted_iota(jnp.int32, (tm, tm), 1)
    tril = (tc < tr).astype(BF16)
    prefix = _dot(tril, cmat) + carry_ref[...]
    r0 = jnp.sum(jnp.where(hit0, prefix, 0.0), axis=-1, keepdims=True)
    r1 = jnp.sum(jnp.where(hit1, prefix, 0.0), axis=-1, keepdims=True)
    carry_ref[...] = carry_ref[...] + jnp.sum(cmat.astype(F32), axis=0, keepdims=True)
    cnt_ref[...] = carry_ref[...]

    fields = [(i0 - N_GROUPS).astype(F32), (i1 - N_GROUPS).astype(F32), r0, r1, gate0, gate1]
    slab = jnp.zeros((tm, LANES), F32)
    for n, f in enumerate(fields):
        slab = jnp.where(lane == n, f, slab)
    slab_ref[...] = slab


def _out_router(oa, ob, lse, oc, x, woa, wob, woc, g, b, wr, br):
    T = x.shape[0]
    tm = TM_OUT
    full = lambda a: pl.BlockSpec(a.shape, lambda i: (0,) * a.ndim)
    row = lambda w: pl.BlockSpec((tm, w), lambda i: (i, 0))
    return pl.pallas_call(
        _out_router_kernel,
        grid=(T // tm,),
        in_specs=[row(LANES)] * 14 + [pl.BlockSpec((C_HEADS * C_V, tm), lambda i: (0, i)), row(D_MODEL),
                                      full(woa), full(wob), full(woc), full(g), full(b), full(wr), full(br)],
        out_specs=[row(D_MODEL), row(LANES), pl.BlockSpec((1, LANES), lambda i: (0, 0))],
        out_shape=[jax.ShapeDtypeStruct((T, D_MODEL), F32), jax.ShapeDtypeStruct((T, LANES), F32),
                   jax.ShapeDtypeStruct((1, LANES), F32)],
        scratch_shapes=[pltpu.VMEM((1, LANES), F32)],
        compiler_params=_cparams("arbitrary"),
        name="out_router",
    )(*oa, *ob, *lse, oc, x, woa, wob, woc, g, b, wr, br)


def _dispatch_kernel(d0_ref, d1_ref, x_ref, zeros_ref, xs_ref, sem):
    del zeros_ref
    i = pl.program_id(0)
    tm = x_ref.shape[0]
    base = i * tm

    def copy(t, d_ref):
        return pltpu.make_async_copy(x_ref.at[pl.ds(t, 1), :], xs_ref.at[pl.ds(d_ref[base + t], 1), :], sem)

    def start(t, carry):
        copy(t, d0_ref).start()
        copy(t, d1_ref).start()
        return carry

    def wait(t, carry):
        copy(t, d0_ref).wait()
        copy(t, d1_ref).wait()
        return carry

    lax.fori_loop(0, tm, start, 0)
    lax.fori_loop(0, tm, wait, 0)


def _dispatch(d0, d1, x1, n_rows):
    T = x1.shape[0]
    tm = ROW_BLK
    zeros = jnp.zeros((n_rows, D_MODEL), F32)
    grid_spec = pltpu.PrefetchScalarGridSpec(
        num_scalar_prefetch=2,
        grid=(T // tm,),
        in_specs=[pl.BlockSpec((tm, D_MODEL), lambda i, a, b: (i, 0)), pl.BlockSpec(memory_space=pl.ANY)],
        out_specs=pl.BlockSpec(memory_space=pl.ANY),
        scratch_shapes=[pltpu.SemaphoreType.DMA],
    )
    return pl.pallas_call(
        _dispatch_kernel,
        grid_spec=grid_spec,
        out_shape=jax.ShapeDtypeStruct((n_rows, D_MODEL), F32),
        input_output_aliases={3: 0},
        compiler_params=_cparams("arbitrary"),
        name="moe_dispatch",
    )(d0, d1, x1, zeros)


def _ffn_kernel(be_ref, nu_ref, xs_ref, wg_ref, wu_ref, wd_ref, ys_ref):
    i = pl.program_id(0)

    @pl.when(i < nu_ref[0])
    def _():
        xb = xs_ref[...].astype(BF16)
        a = _dot(xb, wg_ref[...])
        u = _dot(xb, wu_ref[...])
        hmid = (a * jax.nn.sigmoid(a) * u).astype(BF16)
        ys_ref[...] = _dot(hmid, wd_ref[...])

    @pl.when(i >= nu_ref[0])
    def _():
        ys_ref[...] = jnp.zeros(ys_ref.shape, F32)


def _ffn(block_expert, n_used, xs, wg, wu, wd):
    n_rows = xs.shape[0]
    grid_spec = pltpu.PrefetchScalarGridSpec(
        num_scalar_prefetch=2,
        grid=(n_rows // ROW_BLK,),
        in_specs=[pl.BlockSpec((ROW_BLK, D_MODEL), lambda i, be, nu: (i, 0)),
                  pl.BlockSpec((None, D_MODEL, D_EXPERT), lambda i, be, nu: (be[i], 0, 0)),
                  pl.BlockSpec((None, D_MODEL, D_EXPERT), lambda i, be, nu: (be[i], 0, 0)),
                  pl.BlockSpec((None, D_EXPERT, D_MODEL), lambda i, be, nu: (be[i], 0, 0))],
        out_specs=pl.BlockSpec((ROW_BLK, D_MODEL), lambda i, be, nu: (i, 0)),
    )
    return pl.pallas_call(
        _ffn_kernel,
        grid_spec=grid_spec,
        out_shape=jax.ShapeDtypeStruct((n_rows, D_MODEL), F32),
        compiler_params=_cparams("arbitrary"),
        name="moe_ffn",
    )(block_expert, n_used, xs, wg, wu, wd)


def _combine_kernel(d0_ref, d1_ref, ys_ref, x1_ref, slab_ref, g_ref, b_ref, o_ref, buf_ref, sem):
    i = pl.program_id(0)
    tm = x1_ref.shape[0]
    base = i * tm

    def copy(t, d_ref, slot):
        return pltpu.make_async_copy(ys_ref.at[pl.ds(d_ref[base + t], 1), :], buf_ref.at[slot, pl.ds(t, 1), :], sem)

    def start(t, carry):
        copy(t, d0_ref, 0).start()
        copy(t, d1_ref, 1).start()
        return carry

    def wait(t, carry):
        copy(t, d0_ref, 0).wait()
        copy(t, d1_ref, 1).wait()
        return carry

    lax.fori_loop(0, tm, start, 0)
    lax.fori_loop(0, tm, wait, 0)
    slab = slab_ref[...]
    y = slab[:, 4:5] * buf_ref[0] + slab[:, 5:6] * buf_ref[1]
    o_ref[...] = _layer_norm(DEEPNORM_ALPHA * x1_ref[...] + y, g_ref[...], b_ref[...])


def _combine(d0, d1, ys, x1, slab, g, b):
    T = x1.shape[0]
    tm = ROW_BLK
    grid_spec = pltpu.PrefetchScalarGridSpec(
        num_scalar_prefetch=2,
        grid=(T // tm,),
        in_specs=[pl.BlockSpec(memory_space=pl.ANY),
                  pl.BlockSpec((tm, D_MODEL), lambda i, a, c: (i, 0)),
                  pl.BlockSpec((tm, LANES), lambda i, a, c: (i, 0)),
                  pl.BlockSpec((1, D_MODEL), lambda i, a, c: (0, 0)),
                  pl.BlockSpec((1, D_MODEL), lambda i, a, c: (0, 0))],
        out_specs=pl.BlockSpec((tm, D_MODEL), lambda i, a, c: (i, 0)),
        scratch_shapes=[pltpu.VMEM((2, tm, D_MODEL), F32), pltpu.SemaphoreType.DMA],
    )
    return pl.pallas_call(
        _combine_kernel,
        grid_spec=grid_spec,
        out_shape=jax.ShapeDtypeStruct((T, D_MODEL), F32),
        compiler_params=_cparams("arbitrary"),
        name="moe_combine",
    )(d0, d1, ys, x1, slab, g, b)


def _t5_bucket(dist):
    max_exact = N_BUCKETS // 2
    large = max_exact + (jnp.log(jnp.maximum(dist, 1).astype(F32) / max_exact)
                         / math.log(MAX_DISTANCE / max_exact) * (N_BUCKETS - max_exact)).astype(jnp.int32)
    return jnp.where(dist < max_exact, dist, jnp.minimum(large, N_BUCKETS - 1))


def _bias_table(table_cols, dil, max_dist):
    dist = BLK + jnp.arange(BLK)[:, None] - jnp.arange(2 * BLK)[None, :]
    bias = table_cols[_t5_bucket(jnp.maximum(dist * dil, 0))].transpose(2, 0, 1).astype(F32)
    mask = (dist >= 0) & (dist <= max_dist)
    return jnp.where(mask[None], bias, NEG_INF)


def _rope_tables(T):
    inv_freq = ROPE_THETA ** (-jnp.arange(0, C_ROPE, 2, dtype=F32) / C_ROPE)
    ang = jnp.arange(T, dtype=F32)[:, None] * inv_freq[None, :]
    cos, sin = jnp.cos(ang), jnp.sin(ang)
    ones = jnp.ones((T, C_NOPE), F32)
    zeros_n = jnp.zeros((T, C_NOPE), F32)
    zeros_p = jnp.zeros((T, C_PAD - C_NOPE - C_ROPE), F32)
    cs = jnp.concatenate([ones, cos, cos, zeros_p], axis=1)
    sn = jnp.concatenate([zeros_n, -sin, sin, zeros_p], axis=1)
    return cs, sn


def _swap_halves(w):
    half = w.shape[-1] // 2
    return jnp.concatenate([w[..., half:], w[..., :half]], axis=-1)


def _pad_cols(w, width):
    return jnp.pad(w, ((0, 0), (0, width - w.shape[1])))


def _prep_layer_weights(w_in, w_out, w_uq, w_ukv, w_rg, b_rg, w_re, b_re):
    wab = w_in[:, :AB_COLS].astype(BF16)
    c0 = AB_COLS
    w_cq = w_in[:, c0:c0 + Q_LORA]
    w_ckv = w_in[:, c0 + Q_LORA:c0 + Q_LORA + KV_LORA]
    w_kr = w_in[:, c0 + Q_LORA + KV_LORA:]
    lead = jnp.zeros((D_MODEL, C_NOPE), F32)
    wc = jnp.concatenate([w_cq, w_ckv,
                          _pad_cols(jnp.concatenate([lead, w_kr], 1), C_PAD),
                          _pad_cols(jnp.concatenate([lead, _swap_halves(w_kr)], 1), C_PAD)], axis=1).astype(BF16)
    uq = w_uq.reshape(Q_LORA, C_HEADS, C_NOPE + C_ROPE)
    wq = jnp.pad(uq, ((0, 0), (0, 0), (0, C_PAD - C_NOPE - C_ROPE))).reshape(Q_LORA, C_HEADS * C_PAD).astype(BF16)
    uqs = jnp.concatenate([jnp.zeros((Q_LORA, C_HEADS, C_NOPE), F32), _swap_halves(uq[..., C_NOPE:])], axis=-1)
    wqs = jnp.pad(uqs, ((0, 0), (0, 0), (0, C_PAD - C_NOPE - C_ROPE))).reshape(Q_LORA, C_HEADS * C_PAD).astype(BF16)
    ukv = w_ukv.reshape(KV_LORA, C_HEADS, C_NOPE + C_V)
    wk = jnp.pad(ukv[..., :C_NOPE], ((0, 0), (0, 0), (0, C_PAD - C_NOPE))).reshape(KV_LORA, C_HEADS * C_PAD)
    wv = ukv[..., C_NOPE:].reshape(KV_LORA, C_HEADS * C_V)
    woa = w_out[:256].reshape(2, LANES, D_MODEL).astype(BF16)
    wob = w_out[256:512].reshape(2, LANES, D_MODEL).astype(BF16)
    woc = w_out[512:].astype(BF16)
    wr = _pad_cols(jnp.concatenate([w_rg, w_re.transpose(1, 0, 2).reshape(D_MODEL, N_EXPERTS)], axis=1), LANES)
    br = _pad_cols(jnp.concatenate([b_rg, b_re.reshape(N_EXPERTS)])[None, :], LANES)
    return dict(wab=wab, wc=wc, wq=wq, wqs=wqs, wk=wk.astype(BF16), wv=wv.astype(BF16),
                woa=woa, wob=wob, woc=woc, wr=wr, br=br)


def kernel(x, w_in, w_out, sinks, rel_bias, mla_q_norm, mla_kv_norm, w_uq, w_ukv, ln1_g, ln1_b,
           w_route_group, b_route_group, w_route_expert, b_route_expert,
           w_expert_gate, w_expert_up, w_expert_down, ln2_g, ln2_b):
    Bsz, T, D = x.shape
    assert Bsz == 1 and D == D_MODEL and T % CHUNK == 0
    xt = x.reshape(T, D)
    cs, sn = _rope_tables(T)
    bias_a = _bias_table(rel_bias[:, :A_Q_HEADS], 1, SWA_WINDOW - 1)
    bias_b = [_bias_table(rel_bias[:, A_Q_HEADS + g * B_HEADS_PER_GROUP:A_Q_HEADS + (g + 1) * B_HEADS_PER_GROUP],
                          dil, window // dil) for g, (window, dil) in enumerate(DILATED_PAIRS)]
    n_assign = T * TOP_K
    n_blocks = n_assign // ROW_BLK + N_EXPERTS
    n_rows = n_blocks * ROW_BLK
    no_sink = jnp.zeros((B_HEADS_PER_GROUP,), F32)

    for layer in range(w_in.shape[0]):
        w = _prep_layer_weights(w_in[layer], w_out[layer], w_uq[layer], w_ukv[layer],
                                w_route_group[layer], b_route_group[layer],
                                w_route_expert[layer], b_route_expert[layer])
        hab, qt, k, vt = _in_proj(xt, w["wab"], w["wc"], w["wq"], w["wqs"], w["wk"], w["wv"],
                                mla_q_norm[layer][None, :], mla_kv_norm[layer][None, :], cs, sn)
        oa, _ = _banded(hab, sinks[layer], bias_a, q_col=0, k_col=2, v_col=3, dil=1,
                        n_heads=A_Q_HEADS, group=A_Q_HEADS // A_KV_HEADS, has_sink=True, want_lse=False)
        ob, lse = [], []
        for g, (_, dil) in enumerate(DILATED_PAIRS):
            o_g, lse_g = _banded(hab, no_sink, bias_b[g], q_col=4 + 2 * g, k_col=10 + 2 * g, v_col=16 + 2 * g,
                                 dil=dil, n_heads=B_HEADS_PER_GROUP, group=1, has_sink=False, want_lse=True)
            ob += o_g
            lse += lse_g
        oc = _flash(qt, k, vt)
        x1, slab, cnt = _out_router(oa, ob, lse, oc, xt, w["woa"], w["wob"], w["woc"],
                                    ln1_g[layer][None, :], ln1_b[layer][None, :], w["wr"], w["br"])

        counts = cnt[0, N_GROUPS:N_GROUPS + N_EXPERTS].astype(jnp.int32)
        padded = ((counts + ROW_BLK - 1) // ROW_BLK) * ROW_BLK
        pad_end = jnp.cumsum(padded)
        pad_start = pad_end - padded
        e0 = slab[:, 0].astype(jnp.int32)
        e1 = slab[:, 1].astype(jnp.int32)
        d0 = pad_start[e0] + slab[:, 2].astype(jnp.int32)
        d1 = pad_start[e1] + slab[:, 3].astype(jnp.int32)
        n_used = (pad_end[-1:] // ROW_BLK).astype(jnp.int32)
        block_expert = jnp.minimum(
            jnp.searchsorted(pad_end, jnp.arange(n_blocks, dtype=jnp.int32) * ROW_BLK, side='right'),
            N_EXPERTS - 1).astype(jnp.int32)

        xs = _dispatch(d0, d1, x1, n_rows)
        ys = _ffn(block_expert, n_used, xs, w_expert_gate[layer].astype(BF16), w_expert_up[layer].astype(BF16),
                  w_expert_down[layer].astype(BF16))
        xt = _combine(d0, d1, ys, x1, slab, ln2_g[layer][None, :], ln2_b[layer][None, :])
    return xt.reshape(Bsz, T, D)
```

```python
import functools
import math

import jax
import jax.numpy as jnp
from jax import lax
from jax.experimental import pallas as pl
from jax.experimental.pallas import tpu as pltpu

F32 = jnp.float32
BF16 = jnp.bfloat16

D_MODEL = 1024
DEPTH = 4
HEAD_DIM = 64
BLK = 128
NEG_INF = -1e30

SWA_WINDOW = 128
A_Q_HEADS = 4
A_KV_HEADS = 2
A_COLS = (A_Q_HEADS + 2 * A_KV_HEADS) * HEAD_DIM

DILATED_PAIRS = ((128, 1), (512, 4), (2048, 16))
B_HEADS_PER_GROUP = 4
B_HEADS = len(DILATED_PAIRS) * B_HEADS_PER_GROUP
B_COLS = 3 * B_HEADS * HEAD_DIM
AB_COLS = A_COLS + B_COLS

C_HEADS = 8
C_NOPE = 64
C_ROPE = 32
C_V = 64
Q_LORA = 256
KV_LORA = 128
ROPE_THETA = 10000.0
C_PAD = 128
C_VROWS = 80

N_BUCKETS = 32
MAX_DISTANCE = 2048

N_GROUPS = 4
EXPERTS_PER_GROUP = 8
N_EXPERTS = N_GROUPS * EXPERTS_PER_GROUP
TOP_K = 2
D_EXPERT = 256
ROW_BLK = 256

DEEPNORM_ALPHA = (2 * DEPTH) ** 0.25

VMEM_LIMIT = 56 * 1024 * 1024

CHUNK = 2048
TM_IN = 256
TM_OUT = 256
TQ = 512
TK = 512
FLASH_UNROLL = 4
LANES = 128


def _dot(a, b):
    return jnp.dot(a, b, preferred_element_type=F32)


def _dot_nt(a, b):
    return lax.dot_general(a, b, (((1,), (1,)), ((), ())), preferred_element_type=F32)


def _cparams(*sem):
    return pltpu.CompilerParams(dimension_semantics=sem, vmem_limit_bytes=VMEM_LIMIT)


def _in_proj_kernel(x_ref, wab_ref, wc_ref, wq_ref, wqs_ref, wk_ref, wv_ref, gq_ref, gkv_ref, cs_ref, sn_ref,
                    hab_ref, qt_ref, k_ref, vt_ref):
    xb = x_ref[...].astype(BF16)
    hab_ref[...] = _dot(xb, wab_ref[...])
    hc = _dot(xb, wc_ref[...])
    cq = hc[:, :Q_LORA]
    ckv = hc[:, Q_LORA:Q_LORA + KV_LORA]
    kr = hc[:, Q_LORA + KV_LORA:Q_LORA + KV_LORA + C_PAD]
    krs = hc[:, Q_LORA + KV_LORA + C_PAD:]
    cqn = (cq * lax.rsqrt(jnp.mean(cq * cq, -1, keepdims=True) + 1e-6) * gq_ref[...]).astype(BF16)
    ckvn = (ckv * lax.rsqrt(jnp.mean(ckv * ckv, -1, keepdims=True) + 1e-6) * gkv_ref[...]).astype(BF16)
    cs = cs_ref[...]
    sn = sn_ref[...]
    q = _dot(cqn, wq_ref[...])
    qs = _dot(cqn, wqs_ref[...])
    k = _dot(ckvn, wk_ref[...])
    v = _dot(ckvn, wv_ref[...])
    krot = kr * cs + krs * sn
    scale = (C_NOPE + C_ROPE) ** -0.5 * math.log2(math.e)
    for h in range(C_HEADS):
        sl = slice(h * C_PAD, (h + 1) * C_PAD)
        qh = (q[:, sl] * cs + qs[:, sl] * sn) * scale
        qt_ref[sl, :] = qh.T.astype(BF16)
        k_ref[h] = (k[:, sl] + krot).astype(BF16)
    one_row = (lax.broadcasted_iota(jnp.int32, (1, C_PAD), 1) == C_V).astype(F32)
    for h in range(C_HEADS):
        vh = v[:, h * C_PAD:(h + 1) * C_PAD] + one_row
        vt_ref[h * C_VROWS:(h + 1) * C_VROWS, :] = vh.T[:C_VROWS, :].astype(BF16)


def _in_proj(x, wab, wc, wq, wqs, wk, wv, gq, gkv, cs, sn):
    T = x.shape[0]
    tm = TM_IN
    full = lambda a: pl.BlockSpec(a.shape, lambda i: (0,) * a.ndim)
    row = lambda w: pl.BlockSpec((tm, w), lambda i: (i, 0))
    colblk = lambda r: pl.BlockSpec((r, tm), lambda i: (0, i))
    return pl.pallas_call(
        _in_proj_kernel,
        grid=(T // tm,),
        in_specs=[row(D_MODEL), full(wab), full(wc), full(wq), full(wqs), full(wk), full(wv), full(gq), full(gkv),
                  row(C_PAD), row(C_PAD)],
        out_specs=[row(AB_COLS), colblk(C_HEADS * C_PAD), pl.BlockSpec((C_HEADS, tm, C_PAD), lambda i: (0, i, 0)),
                   colblk(C_HEADS * C_VROWS)],
        out_shape=[jax.ShapeDtypeStruct((T, AB_COLS), F32), jax.ShapeDtypeStruct((C_HEADS * C_PAD, T), BF16),
                   jax.ShapeDtypeStruct((C_HEADS, T, C_PAD), BF16),
                   jax.ShapeDtypeStruct((C_HEADS * C_VROWS, T), BF16)],
        compiler_params=_cparams("arbitrary"),
        name="in_proj",
    )(x, wab, wc, wq, wqs, wk, wv, gq, gkv, cs, sn)


def _banded_kernel(sink_ref, *refs, dil, n_heads, group, has_sink, want_lse):
    nq = n_heads // 2
    nkv = n_heads // group // 2
    q_refs, refs = refs[:nq], refs[nq:]
    k_refs, v_refs, kp_refs, vp_refs = (refs[i * nkv:(i + 1) * nkv] for i in range(4))
    bias_ref = refs[4 * nkv]
    o_refs = refs[4 * nkv + 1:4 * nkv + 1 + nq]
    lse_refs = refs[4 * nkv + 1 + nq:]
    c = pl.program_id(0)
    span = BLK * dil
    nb = CHUNK // span
    col = lax.broadcasted_iota(jnp.int32, (BLK, 2 * BLK), 1)
    first_mask = jnp.logical_and(c == 0, col < BLK)
    scale = HEAD_DIM ** -0.5

    def rows(start):
        return pl.ds(start, BLK) if dil == 1 else pl.ds(start, BLK, stride=dil)

    for b in range(nb):
        for r in range(dil):
            start = b * span + r
            q = [ref[rows(start), :].astype(BF16) for ref in q_refs]
            kcat, vcat = [], []
            for j in range(nkv):
                if b == 0:
                    kp, vp = kp_refs[j][rows(r), :], vp_refs[j][rows(r), :]
                else:
                    kp, vp = k_refs[j][rows(start - span), :], v_refs[j][rows(start - span), :]
                kcat.append(jnp.concatenate([kp, k_refs[j][rows(start), :]], axis=0).astype(BF16))
                vcat.append(jnp.concatenate([vp, v_refs[j][rows(start), :]], axis=0).astype(BF16))
            outs, lses = [], []
            for h in range(n_heads):
                hk = h // group
                ql = slice((h % 2) * HEAD_DIM, (h % 2 + 1) * HEAD_DIM)
                kl = slice((hk % 2) * HEAD_DIM, (hk % 2 + 1) * HEAD_DIM)
                s = _dot_nt(q[h // 2][:, ql], kcat[hk // 2][:, kl]) * scale + bias_ref[h]
                if b == 0:
                    s = jnp.where(first_mask, NEG_INF, s)
                m = jnp.max(s, axis=-1, keepdims=True)
                if has_sink:
                    snk = sink_ref[h]
                    m = jnp.maximum(m, snk)
                e = jnp.exp(s - m)
                l = jnp.sum(e, axis=-1, keepdims=True)
                if has_sink:
                    l = l + jnp.exp(snk - m)
                p = (e / l).astype(BF16)
                outs.append(_dot(p, vcat[hk // 2][:, kl]))
                if want_lse:
                    lses.append(jnp.broadcast_to(m + jnp.log(l), (BLK, HEAD_DIM)))
            for j in range(nq):
                o_refs[j][rows(start), :] = jnp.concatenate(outs[2 * j:2 * j + 2], axis=1)
                if want_lse:
                    lse_refs[j][rows(start), :] = jnp.concatenate(lses[2 * j:2 * j + 2], axis=1)


def _banded(hab, sink, bias, *, q_col, k_col, v_col, dil, n_heads, group, has_sink, want_lse):
    T = hab.shape[0]
    span = BLK * dil
    nb = CHUNK // span
    nq = n_heads // 2
    nkv = n_heads // group // 2
    kernel = functools.partial(_banded_kernel, dil=dil, n_heads=n_heads, group=group,
                               has_sink=has_sink, want_lse=want_lse)
    prev = lambda cb: pl.BlockSpec((span, LANES), lambda c, s: (jnp.maximum(c * nb - 1, 0), cb))
    cur = lambda cb: pl.BlockSpec((CHUNK, LANES), lambda c, s: (c, cb))
    in_specs = ([cur(q_col + j) for j in range(nq)] + [cur(k_col + j) for j in range(nkv)]
                + [cur(v_col + j) for j in range(nkv)] + [prev(k_col + j) for j in range(nkv)]
                + [prev(v_col + j) for j in range(nkv)] + [pl.BlockSpec(bias.shape, lambda c, s: (0, 0, 0))])
    n_out = nq * (2 if want_lse else 1)
    grid_spec = pltpu.PrefetchScalarGridSpec(
        num_scalar_prefetch=1,
        grid=(T // CHUNK,),
        in_specs=in_specs,
        out_specs=[pl.BlockSpec((CHUNK, LANES), lambda c, s: (c, 0))] * n_out,
    )
    outs = pl.pallas_call(
        kernel,
        grid_spec=grid_spec,
        out_shape=[jax.ShapeDtypeStruct((T, LANES), F32)] * n_out,
        compiler_params=_cparams("arbitrary"),
        name=f"banded_d{dil}" + ("_sink" if has_sink else ""),
    )(sink, *([hab] * (nq + 4 * nkv)), bias)
    return outs[:nq], outs[nq:]


def _flash_kernel(qt_ref, k_ref, vt_ref, ot_ref, m_ref, acc_ref, s_ref):
    qi = pl.program_id(1)
    qt = qt_ref[...]
    m_ref[...] = jnp.full(m_ref.shape, NEG_INF, F32)
    acc_ref[...] = jnp.zeros(acc_ref.shape, F32)

    def scores(j):
        off = pl.multiple_of(j * TK, TK)
        return _dot(k_ref[pl.ds(off, TK), :], qt)

    def accumulate(j, s):
        off = pl.multiple_of(j * TK, TK)
        m_old = m_ref[...]
        m_new = jnp.maximum(m_old, jnp.max(s, axis=0, keepdims=True))
        alpha = jnp.exp2(m_old - m_new)
        p = jnp.exp2(s - m_new).astype(BF16)
        acc_ref[...] = alpha * acc_ref[...] + _dot(vt_ref[:, pl.ds(off, TK)], p)
        m_ref[...] = m_new

    def diagonal(s):
        key = lax.broadcasted_iota(jnp.int32, (TK, TQ), 0)
        qry = lax.broadcasted_iota(jnp.int32, (TK, TQ), 1)
        return jnp.where(key <= qry, s, NEG_INF)

    s_ref[0] = scores(0)
    n_main = qi // FLASH_UNROLL

    def main_body(i, carry):
        j = FLASH_UNROLL * i
        for u in range(FLASH_UNROLL):
            s_ref[(u + 1) % 2] = scores(j + u + 1)
            accumulate(j + u, s_ref[u % 2])
        return carry

    lax.fori_loop(0, n_main, main_body, 0)

    base = FLASH_UNROLL * n_main
    for rem in range(FLASH_UNROLL):
        @pl.when(qi - base == rem)
        def _():
            for u in range(rem):
                s_ref[(u + 1) % 2] = scores(base + u + 1)
                accumulate(base + u, s_ref[u % 2])
            accumulate(qi, diagonal(s_ref[rem % 2]))

    ot_ref[...] = (acc_ref[:C_V, :] / acc_ref[C_V:C_V + 1, :]).astype(ot_ref.dtype)


def _flash(qt, k, vt):
    H, T, _ = k.shape
    return pl.pallas_call(
        _flash_kernel,
        grid=(H, T // TQ),
        in_specs=[pl.BlockSpec((C_PAD, TQ), lambda h, i: (h, i)),
                  pl.BlockSpec((None, T, C_PAD), lambda h, i: (h, 0, 0)),
                  pl.BlockSpec((C_VROWS, T), lambda h, i: (h, 0))],
        out_specs=pl.BlockSpec((C_V, TQ), lambda h, i: (h, i)),
        out_shape=jax.ShapeDtypeStruct((H * C_V, T), BF16),
        scratch_shapes=[pltpu.VMEM((1, TQ), F32), pltpu.VMEM((C_VROWS, TQ), F32), pltpu.VMEM((2, TK, TQ), F32)],
        compiler_params=_cparams("arbitrary", "arbitrary"),
        name="mla_flash",
    )(qt, k, vt)


def _layer_norm(z, g, b):
    mu = jnp.mean(z, -1, keepdims=True)
    zc = z - mu
    var = jnp.mean(zc * zc, -1, keepdims=True)
    return zc * lax.rsqrt(var + 1e-5) * g + b


def _out_router_kernel(*refs):
    n_ab = 2 + 6 + 6
    oa_refs, ob_refs, lse_refs = refs[0:2], refs[2:8], refs[8:14]
    (oc_ref, x_ref, woa_ref, wob_ref, woc_ref, g_ref, b_ref, wr_ref, br_ref,
     x1_ref, slab_ref, cnt_ref, carry_ref) = refs[n_ab:]
    i = pl.program_id(0)
    tm = x_ref.shape[0]

    @pl.when(i == 0)
    def _():
        carry_ref[...] = jnp.zeros(carry_ref.shape, F32)

    y = None
    for j in range(2):
        l0, l1, l2 = (lse_refs[2 * g + j][...] for g in range(3))
        mx = jnp.maximum(jnp.maximum(l0, l1), l2)
        e0, e1, e2 = jnp.exp(l0 - mx), jnp.exp(l1 - mx), jnp.exp(l2 - mx)
        den = e0 + e1 + e2
        ob = ((e0 / den) * ob_refs[j][...] + (e1 / den) * ob_refs[2 + j][...] + (e2 / den) * ob_refs[4 + j][...])
        part = _dot(oa_refs[j][...].astype(BF16), woa_ref[j]) + _dot(ob.astype(BF16), wob_ref[j])
        y = part if y is None else y + part
    y = y + _dot(oc_ref[...].astype(F32).T.astype(BF16), woc_ref[...])
    x1 = _layer_norm(DEEPNORM_ALPHA * x_ref[...] + y, g_ref[...], b_ref[...])
    x1_ref[...] = x1

    logits = jnp.dot(x1, wr_ref[...], precision=lax.Precision.HIGHEST, preferred_element_type=F32) + br_ref[...]
    lane = lax.broadcasted_iota(jnp.int32, (tm, LANES), 1)
    ninf = jnp.float32(-jnp.inf)
    gmask = lane < N_GROUPS
    gl = jnp.where(gmask, logits, ninf)
    gmax = jnp.max(gl, axis=-1, keepdims=True)
    grp = jnp.min(jnp.where(gl == gmax, lane, LANES), axis=-1, keepdims=True)
    p_grp = 1.0 / jnp.sum(jnp.where(gmask, jnp.exp(logits - gmax), 0.0), axis=-1, keepdims=True)
    lo = N_GROUPS + EXPERTS_PER_GROUP * grp
    emask = jnp.logical_and(lane >= lo, lane < lo + EXPERTS_PER_GROUP)
    el = jnp.where(emask, logits, ninf)
    v0 = jnp.max(el, axis=-1, keepdims=True)
    i0 = jnp.min(jnp.where(el == v0, lane, LANES), axis=-1, keepdims=True)
    el2 = jnp.where(lane == i0, ninf, el)
    v1 = jnp.max(el2, axis=-1, keepdims=True)
    i1 = jnp.min(jnp.where(el2 == v1, lane, LANES), axis=-1, keepdims=True)
    t = jnp.exp(v1 - v0)
    gate0 = p_grp / (1.0 + t)
    gate1 = p_grp * t / (1.0 + t)

    hit0 = lane == i0
    hit1 = lane == i1
    cmat = jnp.logical_or(hit0, hit1).astype(BF16)
    tr = lax.broadcasted_iota(jnp.int32, (tm, tm), 0)
    tc = lax.broadcasted_iota(jnp.int32, (tm, tm), 1)
    tril = (tc < tr).astype(BF16)
    prefix = _dot(tril, cmat) + carry_ref[...]
    r0 = jnp.sum(jnp.where(hit0, prefix, 0.0), axis=-1, keepdims=True)
    r1 = jnp.sum(jnp.where(hit1, prefix, 0.0), axis=-1, keepdims=True)
    carry_ref[...] = carry_ref[...] + jnp.sum(cmat.astype(F32), axis=0, keepdims=True)
    cnt_ref[...] = carry_ref[...]

    fields = [(i0 - N_GROUPS).astype(F32), (i1 - N_GROUPS).astype(F32), r0, r1, gate0, gate1]
    slab = jnp.zeros((tm, LANES), F32)
    for n, f in enumerate(fields):
        slab = jnp.where(lane == n, f, slab)
    slab_ref[...] = slab


def _out_router(oa, ob, lse, oc, x, woa, wob, woc, g, b, wr, br):
    T = x.shape[0]
    tm = TM_OUT
    full = lambda a: pl.BlockSpec(a.shape, lambda i: (0,) * a.ndim)
    row = lambda w: pl.BlockSpec((tm, w), lambda i: (i, 0))
    return pl.pallas_call(
        _out_router_kernel,
        grid=(T // tm,),
        in_specs=[row(LANES)] * 14 + [pl.BlockSpec((C_HEADS * C_V, tm), lambda i: (0, i)), row(D_MODEL),
                                      full(woa), full(wob), full(woc), full(g), full(b), full(wr), full(br)],
        out_specs=[row(D_MODEL), row(LANES), pl.BlockSpec((1, LANES), lambda i: (0, 0))],
        out_shape=[jax.ShapeDtypeStruct((T, D_MODEL), F32), jax.ShapeDtypeStruct((T, LANES), F32),
                   jax.ShapeDtypeStruct((1, LANES), F32)],
        scratch_shapes=[pltpu.VMEM((1, LANES), F32)],
        compiler_params=_cparams("arbitrary"),
        name="out_router",
    )(*oa, *ob, *lse, oc, x, woa, wob, woc, g, b, wr, br)


def _dispatch_kernel(d0_ref, d1_ref, x_ref, zeros_ref, xs_ref, sem):
    del zeros_ref
    i = pl.program_id(0)
    tm = x_ref.shape[0]
    base = i * tm

    def copy(t, d_ref):
        return pltpu.make_async_copy(x_ref.at[pl.ds(t, 1), :], xs_ref.at[pl.ds(d_ref[base + t], 1), :], sem)

    def start(t, carry):
        copy(t, d0_ref).start()
        copy(t, d1_ref).start()
        return carry

    def wait(t, carry):
        copy(t, d0_ref).wait()
        copy(t, d1_ref).wait()
        return carry

    lax.fori_loop(0, tm, start, 0)
    lax.fori_loop(0, tm, wait, 0)


def _dispatch(d0, d1, x1, n_rows):
    T = x1.shape[0]
    tm = ROW_BLK
    zeros = jnp.zeros((n_rows, D_MODEL), F32)
    grid_spec = pltpu.PrefetchScalarGridSpec(
        num_scalar_prefetch=2,
        grid=(T // tm,),
        in_specs=[pl.BlockSpec((tm, D_MODEL), lambda i, a, b: (i, 0)), pl.BlockSpec(memory_space=pl.ANY)],
        out_specs=pl.BlockSpec(memory_space=pl.ANY),
        scratch_shapes=[pltpu.SemaphoreType.DMA],
    )
    return pl.pallas_call(
        _dispatch_kernel,
        grid_spec=grid_spec,
        out_shape=jax.ShapeDtypeStruct((n_rows, D_MODEL), F32),
        input_output_aliases={3: 0},
        compiler_params=_cparams("arbitrary"),
        name="moe_dispatch",
    )(d0, d1, x1, zeros)


def _ffn_kernel(be_ref, nu_ref, xs_ref, wg_ref, wu_ref, wd_ref, ys_ref):
    i = pl.program_id(0)

    @pl.when(i < nu_ref[0])
    def _():
        xb = xs_ref[...].astype(BF16)
        a = _dot(xb, wg_ref[...].astype(BF16))
        u = _dot(xb, wu_ref[...].astype(BF16))
        hmid = (a * jax.nn.sigmoid(a) * u).astype(BF16)
        ys_ref[...] = _dot(hmid, wd_ref[...].astype(BF16))

    @pl.when(i >= nu_ref[0])
    def _():
        ys_ref[...] = jnp.zeros(ys_ref.shape, F32)


def _ffn(block_expert, n_used, xs, wg, wu, wd):
    n_rows = xs.shape[0]
    grid_spec = pltpu.PrefetchScalarGridSpec(
        num_scalar_prefetch=2,
        grid=(n_rows // ROW_BLK,),
        in_specs=[pl.BlockSpec((ROW_BLK, D_MODEL), lambda i, be, nu: (i, 0)),
                  pl.BlockSpec((None, D_MODEL, D_EXPERT), lambda i, be, nu: (be[i], 0, 0)),
                  pl.BlockSpec((None, D_MODEL, D_EXPERT), lambda i, be, nu: (be[i], 0, 0)),
                  pl.BlockSpec((None, D_EXPERT, D_MODEL), lambda i, be, nu: (be[i], 0, 0))],
        out_specs=pl.BlockSpec((ROW_BLK, D_MODEL), lambda i, be, nu: (i, 0)),
    )
    return pl.pallas_call(
        _ffn_kernel,
        grid_spec=grid_spec,
        out_shape=jax.ShapeDtypeStruct((n_rows, D_MODEL), F32),
        compiler_params=_cparams("arbitrary"),
        name="moe_ffn",
    )(block_expert, n_used, xs, wg, wu, wd)


def _combine_kernel(d0_ref, d1_ref, ys_ref, x1_ref, slab_ref, g_ref, b_ref, o_ref, buf_ref, sem):
    i = pl.program_id(0)
    tm = x1_ref.shape[0]
    base = i * tm

    def copy(t, d_ref, slot):
        return pltpu.make_async_copy(ys_ref.at[pl.ds(d_ref[base + t], 1), :], buf_ref.at[slot, pl.ds(t, 1), :], sem)

    def start(t, carry):
        copy(t, d0_ref, 0).start()
        copy(t, d1_ref, 1).start()
        return carry

    def wait(t, carry):
        copy(t, d0_ref, 0).wait()
        copy(t, d1_ref, 1).wait()
        return carry

    lax.fori_loop(0, tm, start, 0)
    lax.fori_loop(0, tm, wait, 0)
    slab = slab_ref[...]
    y = slab[:, 4:5] * buf_ref[0] + slab[:, 5:6] * buf_ref[1]
    o_ref[...] = _layer_norm(DEEPNORM_ALPHA * x1_ref[...] + y, g_ref[...], b_ref[...])


def _combine(d0, d1, ys, x1, slab, g, b):
    T = x1.shape[0]
    tm = ROW_BLK
    grid_spec = pltpu.PrefetchScalarGridSpec(
        num_scalar_prefetch=2,
        grid=(T // tm,),
        in_specs=[pl.BlockSpec(memory_space=pl.ANY),
                  pl.BlockSpec((tm, D_MODEL), lambda i, a, c: (i, 0)),
                  pl.BlockSpec((tm, LANES), lambda i, a, c: (i, 0)),
                  pl.BlockSpec((1, D_MODEL), lambda i, a, c: (0, 0)),
                  pl.BlockSpec((1, D_MODEL), lambda i, a, c: (0, 0))],
        out_specs=pl.BlockSpec((tm, D_MODEL), lambda i, a, c: (i, 0)),
        scratch_shapes=[pltpu.VMEM((2, tm, D_MODEL), F32), pltpu.SemaphoreType.DMA],
    )
    return pl.pallas_call(
        _combine_kernel,
        grid_spec=grid_spec,
        out_shape=jax.ShapeDtypeStruct((T, D_MODEL), F32),
        compiler_params=_cparams("arbitrary"),
        name="moe_combine",
    )(d0, d1, ys, x1, slab, g, b)


def _t5_bucket(dist):
    max_exact = N_BUCKETS // 2
    large = max_exact + (jnp.log(jnp.maximum(dist, 1).astype(F32) / max_exact)
                         / math.log(MAX_DISTANCE / max_exact) * (N_BUCKETS - max_exact)).astype(jnp.int32)
    return jnp.where(dist < max_exact, dist, jnp.minimum(large, N_BUCKETS - 1))


def _bias_table(table_cols, dil, max_dist):
    dist = BLK + jnp.arange(BLK)[:, None] - jnp.arange(2 * BLK)[None, :]
    bias = table_cols[_t5_bucket(jnp.maximum(dist * dil, 0))].transpose(2, 0, 1).astype(F32)
    mask = (dist >= 0) & (dist <= max_dist)
    return jnp.where(mask[None], bias, NEG_INF)


def _rope_tables(T):
    inv_freq = ROPE_THETA ** (-jnp.arange(0, C_ROPE, 2, dtype=F32) / C_ROPE)
    ang = jnp.arange(T, dtype=F32)[:, None] * inv_freq[None, :]
    cos, sin = jnp.cos(ang), jnp.sin(ang)
    ones = jnp.ones((T, C_NOPE), F32)
    zeros_n = jnp.zeros((T, C_NOPE), F32)
    zeros_p = jnp.zeros((T, C_PAD - C_NOPE - C_ROPE), F32)
    cs = jnp.concatenate([ones, cos, cos, zeros_p], axis=1)
    sn = jnp.concatenate([zeros_n, -sin, sin, zeros_p], axis=1)
    return cs, sn


def _swap_halves(w):
    half = w.shape[-1] // 2
    return jnp.concatenate([w[..., half:], w[..., :half]], axis=-1)


def _pad_cols(w, width):
    return jnp.pad(w, ((0, 0), (0, width - w.shape[1])))


def _prep_layer_weights(w_in, w_out, w_uq, w_ukv, w_rg, b_rg, w_re, b_re):
    wab = w_in[:, :AB_COLS].astype(BF16)
    c0 = AB_COLS
    w_cq = w_in[:, c0:c0 + Q_LORA]
    w_ckv = w_in[:, c0 + Q_LORA:c0 + Q_LORA + KV_LORA]
    w_kr = w_in[:, c0 + Q_LORA + KV_LORA:]
    lead = jnp.zeros((D_MODEL, C_NOPE), F32)
    wc = jnp.concatenate([w_cq, w_ckv,
                          _pad_cols(jnp.concatenate([lead, w_kr], 1), C_PAD),
                          _pad_cols(jnp.concatenate([lead, _swap_halves(w_kr)], 1), C_PAD)], axis=1).astype(BF16)
    uq = w_uq.reshape(Q_LORA, C_HEADS, C_NOPE + C_ROPE)
    wq = jnp.pad(uq, ((0, 0), (0, 0), (0, C_PAD - C_NOPE - C_ROPE))).reshape(Q_LORA, C_HEADS * C_PAD).astype(BF16)
    uqs = jnp.concatenate([jnp.zeros((Q_LORA, C_HEADS, C_NOPE), F32), _swap_halves(uq[..., C_NOPE:])], axis=-1)
    wqs = jnp.pad(uqs, ((0, 0), (0, 0), (0, C_PAD - C_NOPE - C_ROPE))).reshape(Q_LORA, C_HEADS * C_PAD).astype(BF16)
    ukv = w_ukv.reshape(KV_LORA, C_HEADS, C_NOPE + C_V)
    wk = jnp.pad(ukv[..., :C_NOPE], ((0, 0), (0, 0), (0, C_PAD - C_NOPE))).reshape(KV_LORA, C_HEADS * C_PAD)
    wv = jnp.pad(ukv[..., C_NOPE:], ((0, 0), (0, 0), (0, C_PAD - C_V))).reshape(KV_LORA, C_HEADS * C_PAD)
    woa = w_out[:256].reshape(2, LANES, D_MODEL).astype(BF16)
    wob = w_out[256:512].reshape(2, LANES, D_MODEL).astype(BF16)
    woc = w_out[512:].astype(BF16)
    wr = _pad_cols(jnp.concatenate([w_rg, w_re.transpose(1, 0, 2).reshape(D_MODEL, N_EXPERTS)], axis=1), LANES)
    br = _pad_cols(jnp.concatenate([b_rg, b_re.reshape(N_EXPERTS)])[None, :], LANES)
    return dict(wab=wab, wc=wc, wq=wq, wqs=wqs, wk=wk.astype(BF16), wv=wv.astype(BF16),
                woa=woa, wob=wob, woc=woc, wr=wr, br=br)


def kernel(x, w_in, w_out, sinks, rel_bias, mla_q_norm, mla_kv_norm, w_uq, w_ukv, ln1_g, ln1_b,
           w_route_group, b_route_group, w_route_expert, b_route_expert,
           w_expert_gate, w_expert_up, w_expert_down, ln2_g, ln2_b):
    Bsz, T, D = x.shape
    assert Bsz == 1 and D == D_MODEL and T % CHUNK == 0
    xt = x.reshape(T, D)
    cs, sn = _rope_tables(T)
    bias_a = _bias_table(rel_bias[:, :A_Q_HEADS], 1, SWA_WINDOW - 1)
    bias_b = [_bias_table(rel_bias[:, A_Q_HEADS + g * B_HEADS_PER_GROUP:A_Q_HEADS + (g + 1) * B_HEADS_PER_GROUP],
                          dil, window // dil) for g, (window, dil) in enumerate(DILATED_PAIRS)]
    n_assign = T * TOP_K
    n_blocks = n_assign // ROW_BLK + N_EXPERTS
    n_rows = n_blocks * ROW_BLK
    no_sink = jnp.zeros((B_HEADS_PER_GROUP,), F32)

    for layer in range(w_in.shape[0]):
        w = _prep_layer_weights(w_in[layer], w_out[layer], w_uq[layer], w_ukv[layer],
                                w_route_group[layer], b_route_group[layer],
                                w_route_expert[layer], b_route_expert[layer])
        hab, qt, k, vt = _in_proj(xt, w["wab"], w["wc"], w["wq"], w["wqs"], w["wk"], w["wv"],
                                mla_q_norm[layer][None, :], mla_kv_norm[layer][None, :], cs, sn)
        oa, _ = _banded(hab, sinks[layer], bias_a, q_col=0, k_col=2, v_col=3, dil=1,
                        n_heads=A_Q_HEADS, group=A_Q_HEADS // A_KV_HEADS, has_sink=True, want_lse=False)
        ob, lse = [], []
        for g, (_, dil) in enumerate(DILATED_PAIRS):
            o_g, lse_g = _banded(hab, no_sink, bias_b[g], q_col=4 + 2 * g, k_col=10 + 2 * g, v_col=16 + 2 * g,
                                 dil=dil, n_heads=B_HEADS_PER_GROUP, group=1, has_sink=False, want_lse=True)
            ob += o_g
            lse += lse_g
        oc = _flash(qt, k, vt)
        x1, slab, cnt = _out_router(oa, ob, lse, oc, xt, w["woa"], w["wob"], w["woc"],
                                    ln1_g[layer][None, :], ln1_b[layer][None, :], w["wr"], w["br"])

        counts = cnt[0, N_GROUPS:N_GROUPS + N_EXPERTS].astype(jnp.int32)
        padded = ((counts + ROW_BLK - 1) // ROW_BLK) * ROW_BLK
        pad_end = jnp.cumsum(padded)
        pad_start = pad_end - padded
        eids = jnp.arange(N_EXPERTS, dtype=jnp.int32)
        pick = lambda e: jnp.sum(jnp.where(e.astype(jnp.int32)[:, None] == eids[None, :], pad_start[None, :], 0), axis=1)
        d0 = pick(slab[:, 0]) + slab[:, 2].astype(jnp.int32)
        d1 = pick(slab[:, 1]) + slab[:, 3].astype(jnp.int32)
        n_used = (pad_end[-1:] // ROW_BLK).astype(jnp.int32)
        block_start = jnp.arange(n_blocks, dtype=jnp.int32) * ROW_BLK
        block_expert = jnp.minimum(jnp.sum((pad_end[None, :] <= block_start[:, None]).astype(jnp.int32), axis=1),
                                   N_EXPERTS - 1)

        xs = _dispatch(d0, d1, x1, n_rows)
        ys = _ffn(block_expert, n_used, xs, w_expert_gate[layer], w_expert_up[layer], w_expert_down[layer])
        xt = _combine(d0, d1, ys, x1, slab, ln2_g[layer][None, :], ln2_b[layer][None, :])
    return xt.reshape(Bsz, T, D)
```

```python
import functools
import math

import jax
import jax.numpy as jnp
from jax import lax
from jax.experimental import pallas as pl
from jax.experimental.pallas import tpu as pltpu

F32 = jnp.float32
BF16 = jnp.bfloat16

D_MODEL = 1024
DEPTH = 4
HEAD_DIM = 64
BLK = 128
NEG_INF = -1e30

SWA_WINDOW = 128
A_Q_HEADS = 4
A_KV_HEADS = 2
A_COLS = (A_Q_HEADS + 2 * A_KV_HEADS) * HEAD_DIM
A_HEAD_ORDER = (0, 2, 1, 3)

DILATED_PAIRS = ((128, 1), (512, 4), (2048, 16))
B_HEADS_PER_GROUP = 4
B_HEADS = len(DILATED_PAIRS) * B_HEADS_PER_GROUP
B_COLS = 3 * B_HEADS * HEAD_DIM
AB_COLS = A_COLS + B_COLS

C_HEADS = 8
C_NOPE = 64
C_ROPE = 32
C_V = 64
Q_LORA = 256
KV_LORA = 128
ROPE_THETA = 10000.0
C_PAD = 128
C_VROWS = 80

N_BUCKETS = 32
MAX_DISTANCE = 2048

N_GROUPS = 4
EXPERTS_PER_GROUP = 8
N_EXPERTS = N_GROUPS * EXPERTS_PER_GROUP
TOP_K = 2
D_EXPERT = 256
ROW_BLK = 256

DEEPNORM_ALPHA = (2 * DEPTH) ** 0.25

VMEM_LIMIT = 56 * 1024 * 1024

CHUNK = 2048
TM_IN = 256
TM_OUT = 256
TQ = 512
TK = 512
FLASH_UNROLL = 4
LANES = 128


def _dot(a, b):
    return jnp.dot(a, b, preferred_element_type=F32)


def _dot_nt(a, b):
    return lax.dot_general(a, b, (((1,), (1,)), ((), ())), preferred_element_type=F32)


def _bdot(a, b):
    return lax.dot_general(a, b, (((2,), (1,)), ((0,), (0,))), preferred_element_type=F32)


def _bdot_nt(a, b):
    return lax.dot_general(a, b, (((2,), (2,)), ((0,), (0,))), preferred_element_type=F32)


def _cparams(*sem):
    return pltpu.CompilerParams(dimension_semantics=sem, vmem_limit_bytes=VMEM_LIMIT)


def _in_proj_kernel(x_ref, wab_ref, wc_ref, wq_ref, wqs_ref, wk_ref, wv_ref, gq_ref, gkv_ref, cs_ref, sn_ref,
                    hab_ref, qt_ref, k_ref, vt_ref):
    xb = x_ref[...].astype(BF16)
    hab_ref[...] = _dot(xb, wab_ref[...])
    hc = _dot(xb, wc_ref[...])
    cq = hc[:, :Q_LORA]
    ckv = hc[:, Q_LORA:Q_LORA + KV_LORA]
    kr = hc[:, Q_LORA + KV_LORA:Q_LORA + KV_LORA + C_PAD]
    krs = hc[:, Q_LORA + KV_LORA + C_PAD:]
    cqn = (cq * lax.rsqrt(jnp.mean(cq * cq, -1, keepdims=True) + 1e-6) * gq_ref[...]).astype(BF16)
    ckvn = (ckv * lax.rsqrt(jnp.mean(ckv * ckv, -1, keepdims=True) + 1e-6) * gkv_ref[...]).astype(BF16)
    cs = cs_ref[...]
    sn = sn_ref[...]
    q = _dot(cqn, wq_ref[...])
    qs = _dot(cqn, wqs_ref[...])
    k = _dot(ckvn, wk_ref[...])
    v = _dot(ckvn, wv_ref[...])
    krot = kr * cs + krs * sn
    scale = (C_NOPE + C_ROPE) ** -0.5 * math.log2(math.e)
    for h in range(C_HEADS):
        sl = slice(h * C_PAD, (h + 1) * C_PAD)
        qh = (q[:, sl] * cs + qs[:, sl] * sn) * scale
        qt_ref[sl, :] = qh.T.astype(BF16)
        k_ref[h] = (k[:, sl] + krot).astype(BF16)
    one_row = (lax.broadcasted_iota(jnp.int32, (1, C_PAD), 1) == C_V).astype(F32)
    for h in range(C_HEADS):
        vh = v[:, h * C_PAD:(h + 1) * C_PAD] + one_row
        vt_ref[h * C_VROWS:(h + 1) * C_VROWS, :] = vh.T[:C_VROWS, :].astype(BF16)


def _in_proj(x, wab, wc, wq, wqs, wk, wv, gq, gkv, cs, sn):
    T = x.shape[0]
    tm = TM_IN
    full = lambda a: pl.BlockSpec(a.shape, lambda i: (0,) * a.ndim)
    row = lambda w: pl.BlockSpec((tm, w), lambda i: (i, 0))
    colblk = lambda r: pl.BlockSpec((r, tm), lambda i: (0, i))
    return pl.pallas_call(
        _in_proj_kernel,
        grid=(T // tm,),
        in_specs=[row(D_MODEL), full(wab), full(wc), full(wq), full(wqs), full(wk), full(wv), full(gq), full(gkv),
                  row(C_PAD), row(C_PAD)],
        out_specs=[row(AB_COLS), colblk(C_HEADS * C_PAD), pl.BlockSpec((C_HEADS, tm, C_PAD), lambda i: (0, i, 0)),
                   colblk(C_HEADS * C_VROWS)],
        out_shape=[jax.ShapeDtypeStruct((T, AB_COLS), F32), jax.ShapeDtypeStruct((C_HEADS * C_PAD, T), BF16),
                   jax.ShapeDtypeStruct((C_HEADS, T, C_PAD), BF16),
                   jax.ShapeDtypeStruct((C_HEADS * C_VROWS, T), BF16)],
        compiler_params=_cparams("arbitrary"),
        name="in_proj",
    )(x, wab, wc, wq, wqs, wk, wv, gq, gkv, cs, sn)


def _banded_kernel(sink_ref, *refs, dil, n_heads, group, has_sink, want_lse):
    nq = n_heads // 2
    nkv = n_heads // group // 2
    q_refs, refs = refs[:nq], refs[nq:]
    k_refs, v_refs, kp_refs, vp_refs = (refs[i * nkv:(i + 1) * nkv] for i in range(4))
    bias_ref = refs[4 * nkv]
    o_refs = refs[4 * nkv + 1:4 * nkv + 1 + nq]
    lse_refs = refs[4 * nkv + 1 + nq:]
    c = pl.program_id(0)
    span = BLK * dil
    nb = CHUNK // span
    n_combo = nb * dil
    starts = [b * span + r for b in range(nb) for r in range(dil)]
    scale = HEAD_DIM ** -0.5

    def rows(start):
        return pl.ds(start, BLK) if dil == 1 else pl.ds(start, BLK, stride=dil)

    def gather(ref):
        return jnp.stack([ref[rows(s), :] for s in starts], axis=0)

    def with_prev(ref, prev_ref):
        cur = gather(ref)
        first = jnp.stack([prev_ref[rows(r), :] for r in range(dil)], axis=0)
        prev = jnp.concatenate([first, cur[:n_combo - dil]], axis=0) if dil < n_combo else first
        return jnp.concatenate([prev, cur], axis=1).astype(BF16)

    q = [gather(ref).astype(BF16) for ref in q_refs]
    kcat = [with_prev(k_refs[j], kp_refs[j]) for j in range(nkv)]
    vcat = [with_prev(v_refs[j], vp_refs[j]) for j in range(nkv)]

    lane = lax.broadcasted_iota(jnp.int32, (1, 1, LANES), 2)
    low = lane < HEAD_DIM
    combo = lax.broadcasted_iota(jnp.int32, (n_combo, 1, 2 * BLK), 0)
    col = lax.broadcasted_iota(jnp.int32, (n_combo, 1, 2 * BLK), 2)
    no_prev = jnp.logical_and(c == 0, jnp.logical_and(combo < dil, col < BLK))

    for j in range(nq):
        o_par, lse_par = [], []
        for par in range(2):
            h = 2 * j + par if group == 1 else group * par + j
            kvb = j if group == 1 else 0
            keep = low if par == 0 else jnp.logical_not(low)
            kk = jnp.where(keep, kcat[kvb], jnp.zeros((), BF16))
            s = _bdot_nt(q[j], kk)
            s = s * scale + bias_ref[h][None]
            s = jnp.where(no_prev, NEG_INF, s)
            m = jnp.max(s, axis=-1, keepdims=True)
            if has_sink:
                snk = sink_ref[h]
                m = jnp.maximum(m, snk)
            e = jnp.exp(s - m)
            l = jnp.sum(e, axis=-1, keepdims=True)
            if has_sink:
                l = l + jnp.exp(snk - m)
            p = (e * (1.0 / l)).astype(BF16)
            o_par.append(_bdot(p, vcat[kvb]))
            if want_lse:
                lse_par.append(m + jnp.log(l))
        o_pair = jnp.where(low, o_par[0], o_par[1])
        if want_lse:
            lse_pair = jnp.where(low, lse_par[0], lse_par[1])
        for ci, s0 in enumerate(starts):
            o_refs[j][rows(s0), :] = o_pair[ci]
            if want_lse:
                lse_refs[j][rows(s0), :] = lse_pair[ci]


def _banded(hab, sink, bias, *, q_col, k_col, v_col, dil, n_heads, group, has_sink, want_lse):
    T = hab.shape[0]
    span = BLK * dil
    nb = CHUNK // span
    nq = n_heads // 2
    nkv = n_heads // group // 2
    kernel = functools.partial(_banded_kernel, dil=dil, n_heads=n_heads, group=group,
                               has_sink=has_sink, want_lse=want_lse)
    prev = lambda cb: pl.BlockSpec((span, LANES), lambda c, s: (jnp.maximum(c * nb - 1, 0), cb))
    cur = lambda cb: pl.BlockSpec((CHUNK, LANES), lambda c, s: (c, cb))
    in_specs = ([cur(q_col + j) for j in range(nq)] + [cur(k_col + j) for j in range(nkv)]
                + [cur(v_col + j) for j in range(nkv)] + [prev(k_col + j) for j in range(nkv)]
                + [prev(v_col + j) for j in range(nkv)] + [pl.BlockSpec(bias.shape, lambda c, s: (0, 0, 0))])
    n_out = nq * (2 if want_lse else 1)
    grid_spec = pltpu.PrefetchScalarGridSpec(
        num_scalar_prefetch=1,
        grid=(T // CHUNK,),
        in_specs=in_specs,
        out_specs=[pl.BlockSpec((CHUNK, LANES), lambda c, s: (c, 0))] * n_out,
    )
    outs = pl.pallas_call(
        kernel,
        grid_spec=grid_spec,
        out_shape=[jax.ShapeDtypeStruct((T, LANES), F32)] * n_out,
        compiler_params=_cparams("arbitrary"),
        name=f"banded_d{dil}" + ("_sink" if has_sink else ""),
    )(sink, *([hab] * (nq + 4 * nkv)), bias)
    return outs[:nq], outs[nq:]


def _flash_kernel(qt_ref, k_ref, vt_ref, ot_ref, m_ref, acc_ref, s_ref):
    qi = pl.program_id(1)
    qt = qt_ref[...]
    m_ref[...] = jnp.full(m_ref.shape, NEG_INF, F32)
    acc_ref[...] = jnp.zeros(acc_ref.shape, F32)

    def scores(j):
        off = pl.multiple_of(j * TK, TK)
        return _dot(k_ref[pl.ds(off, TK), :], qt)

    def accumulate(j, s):
        off = pl.multiple_of(j * TK, TK)
        m_old = m_ref[...]
        m_new = jnp.maximum(m_old, jnp.max(s, axis=0, keepdims=True))
        alpha = jnp.exp2(m_old - m_new)
        p = jnp.exp2(s - m_new).astype(BF16)
        acc_ref[...] = alpha * acc_ref[...] + _dot(vt_ref[:, pl.ds(off, TK)], p)
        m_ref[...] = m_new

    def diagonal(s):
        key = lax.broadcasted_iota(jnp.int32, (TK, TQ), 0)
        qry = lax.broadcasted_iota(jnp.int32, (TK, TQ), 1)
        return jnp.where(key <= qry, s, NEG_INF)

    s_ref[0] = scores(0)
    n_main = qi // FLASH_UNROLL

    def main_body(i, carry):
        j = FLASH_UNROLL * i
        for u in range(FLASH_UNROLL):
            s_ref[(u + 1) % 2] = scores(j + u + 1)
            accumulate(j + u, s_ref[u % 2])
        return carry

    lax.fori_loop(0, n_main, main_body, 0)

    base = FLASH_UNROLL * n_main
    for rem in range(FLASH_UNROLL):
        @pl.when(qi - base == rem)
        def _():
            for u in range(rem):
                s_ref[(u + 1) % 2] = scores(base + u + 1)
                accumulate(base + u, s_ref[u % 2])
            accumulate(qi, diagonal(s_ref[rem % 2]))

    ot_ref[...] = (acc_ref[:C_V, :] / acc_ref[C_V:C_V + 1, :]).astype(ot_ref.dtype)


def _flash(qt, k, vt):
    H, T, _ = k.shape
    return pl.pallas_call(
        _flash_kernel,
        grid=(H, T // TQ),
        in_specs=[pl.BlockSpec((C_PAD, TQ), lambda h, i: (h, i)),
                  pl.BlockSpec((None, T, C_PAD), lambda h, i: (h, 0, 0)),
                  pl.BlockSpec((C_VROWS, T), lambda h, i: (h, 0))],
        out_specs=pl.BlockSpec((C_V, TQ), lambda h, i: (h, i)),
        out_shape=jax.ShapeDtypeStruct((H * C_V, T), BF16),
        scratch_shapes=[pltpu.VMEM((1, TQ), F32), pltpu.VMEM((C_VROWS, TQ), F32), pltpu.VMEM((2, TK, TQ), F32)],
        compiler_params=_cparams("arbitrary", "arbitrary"),
        name="mla_flash",
    )(qt, k, vt)


def _layer_norm(z, g, b):
    mu = jnp.mean(z, -1, keepdims=True)
    zc = z - mu
    var = jnp.mean(zc * zc, -1, keepdims=True)
    return zc * lax.rsqrt(var + 1e-5) * g + b


def _out_router_kernel(*refs):
    n_ab = 2 + 6 + 6
    oa_refs, ob_refs, lse_refs = refs[0:2], refs[2:8], refs[8:14]
    (oc_ref, x_ref, woa_ref, wob_ref, woc_ref, g_ref, b_ref, wr_ref, br_ref,
     x1_ref, slab_ref, cnt_ref, carry_ref) = refs[n_ab:]
    i = pl.program_id(0)
    tm = x_ref.shape[0]

    @pl.when(i == 0)
    def _():
        carry_ref[...] = jnp.zeros(carry_ref.shape, F32)

    y = None
    for j in range(2):
        l0, l1, l2 = (lse_refs[2 * g + j][...] for g in range(3))
        mx = jnp.maximum(jnp.maximum(l0, l1), l2)
        e0, e1, e2 = jnp.exp(l0 - mx), jnp.exp(l1 - mx), jnp.exp(l2 - mx)
        den = e0 + e1 + e2
        ob = ((e0 / den) * ob_refs[j][...] + (e1 / den) * ob_refs[2 + j][...] + (e2 / den) * ob_refs[4 + j][...])
        part = _dot(oa_refs[j][...].astype(BF16), woa_ref[j]) + _dot(ob.astype(BF16), wob_ref[j])
        y = part if y is None else y + part
    y = y + _dot(oc_ref[...].astype(F32).T.astype(BF16), woc_ref[...])
    x1 = _layer_norm(DEEPNORM_ALPHA * x_ref[...] + y, g_ref[...], b_ref[...])
    x1_ref[...] = x1

    logits = jnp.dot(x1, wr_ref[...], precision=lax.Precision.HIGHEST, preferred_element_type=F32) + br_ref[...]
    lane = lax.broadcasted_iota(jnp.int32, (tm, LANES), 1)
    ninf = jnp.float32(-jnp.inf)
    gmask = lane < N_GROUPS
    gl = jnp.where(gmask, logits, ninf)
    gmax = jnp.max(gl, axis=-1, keepdims=True)
    grp = jnp.min(jnp.where(gl == gmax, lane, LANES), axis=-1, keepdims=True)
    p_grp = 1.0 / jnp.sum(jnp.where(gmask, jnp.exp(logits - gmax), 0.0), axis=-1, keepdims=True)
    lo = N_GROUPS + EXPERTS_PER_GROUP * grp
    emask = jnp.logical_and(lane >= lo, lane < lo + EXPERTS_PER_GROUP)
    el = jnp.where(emask, logits, ninf)
    v0 = jnp.max(el, axis=-1, keepdims=True)
    i0 = jnp.min(jnp.where(el == v0, lane, LANES), axis=-1, keepdims=True)
    el2 = jnp.where(lane == i0, ninf, el)
    v1 = jnp.max(el2, axis=-1, keepdims=True)
    i1 = jnp.min(jnp.where(el2 == v1, lane, LANES), axis=-1, keepdims=True)
    t = jnp.exp(v1 - v0)
    gate0 = p_grp / (1.0 + t)
    gate1 = p_grp * t / (1.0 + t)

    hit0 = lane == i0
    hit1 = lane == i1
    cmat = jnp.logical_or(hit0, hit1).astype(BF16)
    tr = lax.broadcasted_iota(jnp.int32, (tm, tm), 0)
    tc = lax.broadcasted_iota(jnp.int32, (tm, tm), 1)
    tril = (tc < tr).astype(BF16)
    prefix = _dot(tril, cmat) + carry_ref[...]
    r0 = jnp.sum(jnp.where(hit0, prefix, 0.0), axis=-1, keepdims=True)
    r1 = jnp.sum(jnp.where(hit1, prefix, 0.0), axis=-1, keepdims=True)
    carry_ref[...] = carry_ref[...] + jnp.sum(cmat.astype(F32), axis=0, keepdims=True)
    cnt_ref[...] = carry_ref[...]

    fields = [(i0 - N_GROUPS).astype(F32), (i1 - N_GROUPS).astype(F32), r0, r1, gate0, gate1]
    slab = jnp.zeros((tm, LANES), F32)
    for n, f in enumerate(fields):
        slab = jnp.where(lane == n, f, slab)
    slab_ref[...] = slab


def _out_router(oa, ob, lse, oc, x, woa, wob, woc, g, b, wr, br):
    T = x.shape[0]
    tm = TM_OUT
    full = lambda a: pl.BlockSpec(a.shape, lambda i: (0,) * a.ndim)
    row = lambda w: pl.BlockSpec((tm, w), lambda i: (i, 0))
    return pl.pallas_call(
        _out_router_kernel,
        grid=(T // tm,),
        in_specs=[row(LANES)] * 14 + [pl.BlockSpec((C_HEADS * C_V, tm), lambda i: (0, i)), row(D_MODEL),
                                      full(woa), full(wob), full(woc), full(g), full(b), full(wr), full(br)],
        out_specs=[row(D_MODEL), row(LANES), pl.BlockSpec((1, LANES), lambda i: (0, 0))],
        out_shape=[jax.ShapeDtypeStruct((T, D_MODEL), F32), jax.ShapeDtypeStruct((T, LANES), F32),
                   jax.ShapeDtypeStruct((1, LANES), F32)],
        scratch_shapes=[pltpu.VMEM((1, LANES), F32)],
        compiler_params=_cparams("arbitrary"),
        name="out_router",
    )(*oa, *ob, *lse, oc, x, woa, wob, woc, g, b, wr, br)


def _dispatch_kernel(d0_ref, d1_ref, x_ref, zeros_ref, xs_ref, sem):
    del zeros_ref
    i = pl.program_id(0)
    tm = x_ref.shape[0]
    base = i * tm

    def copy(t, d_ref):
        return pltpu.make_async_copy(x_ref.at[pl.ds(t, 1), :], xs_ref.at[pl.ds(d_ref[base + t], 1), :], sem)

    def start(t, carry):
        copy(t, d0_ref).start()
        copy(t, d1_ref).start()
        return carry

    def wait(t, carry):
        copy(t, d0_ref).wait()
        copy(t, d1_ref).wait()
        return carry

    lax.fori_loop(0, tm, start, 0)
    lax.fori_loop(0, tm, wait, 0)


def _dispatch(d0, d1, x1, n_rows):
    T = x1.shape[0]
    tm = ROW_BLK
    zeros = jnp.zeros((n_rows, D_MODEL), F32)
    grid_spec = pltpu.PrefetchScalarGridSpec(
        num_scalar_prefetch=2,
        grid=(T // tm,),
        in_specs=[pl.BlockSpec((tm, D_MODEL), lambda i, a, b: (i, 0)), pl.BlockSpec(memory_space=pl.ANY)],
        out_specs=pl.BlockSpec(memory_space=pl.ANY),
        scratch_shapes=[pltpu.SemaphoreType.DMA],
    )
    return pl.pallas_call(
        _dispatch_kernel,
        grid_spec=grid_spec,
        out_shape=jax.ShapeDtypeStruct((n_rows, D_MODEL), F32),
        input_output_aliases={3: 0},
        compiler_params=_cparams("arbitrary"),
        name="moe_dispatch",
    )(d0, d1, x1, zeros)


def _ffn_kernel(be_ref, nu_ref, xs_ref, wg_ref, wu_ref, wd_ref, ys_ref):
    i = pl.program_id(0)

    @pl.when(i < nu_ref[0])
    def _():
        xb = xs_ref[...].astype(BF16)
        a = _dot(xb, wg_ref[...].astype(BF16))
        u = _dot(xb, wu_ref[...].astype(BF16))
        hmid = (a * jax.nn.sigmoid(a) * u).astype(BF16)
        ys_ref[...] = _dot(hmid, wd_ref[...].astype(BF16))

    @pl.when(i >= nu_ref[0])
    def _():
        ys_ref[...] = jnp.zeros(ys_ref.shape, F32)


def _ffn(block_expert, n_used, xs, wg, wu, wd, layer):
    n_rows = xs.shape[0]
    grid_spec = pltpu.PrefetchScalarGridSpec(
        num_scalar_prefetch=2,
        grid=(n_rows // ROW_BLK,),
        in_specs=[pl.BlockSpec((ROW_BLK, D_MODEL), lambda i, be, nu: (i, 0)),
                  pl.BlockSpec((None, None, D_MODEL, D_EXPERT), lambda i, be, nu: (layer, be[i], 0, 0)),
                  pl.BlockSpec((None, None, D_MODEL, D_EXPERT), lambda i, be, nu: (layer, be[i], 0, 0)),
                  pl.BlockSpec((None, None, D_EXPERT, D_MODEL), lambda i, be, nu: (layer, be[i], 0, 0))],
        out_specs=pl.BlockSpec((ROW_BLK, D_MODEL), lambda i, be, nu: (i, 0)),
    )
    return pl.pallas_call(
        _ffn_kernel,
        grid_spec=grid_spec,
        out_shape=jax.ShapeDtypeStruct((n_rows, D_MODEL), F32),
        compiler_params=_cparams("arbitrary"),
        name="moe_ffn",
    )(block_expert, n_used, xs, wg, wu, wd)


def _combine_kernel(d0_ref, d1_ref, ys_ref, x1_ref, slab_ref, g_ref, b_ref, o_ref, buf_ref, sem):
    i = pl.program_id(0)
    tm = x1_ref.shape[0]
    base = i * tm

    def copy(t, d_ref, slot):
        return pltpu.make_async_copy(ys_ref.at[pl.ds(d_ref[base + t], 1), :], buf_ref.at[slot, pl.ds(t, 1), :], sem)

    def start(t, carry):
        copy(t, d0_ref, 0).start()
        copy(t, d1_ref, 1).start()
        return carry

    def wait(t, carry):
        copy(t, d0_ref, 0).wait()
        copy(t, d1_ref, 1).wait()
        return carry

    lax.fori_loop(0, tm, start, 0)
    lax.fori_loop(0, tm, wait, 0)
    slab = slab_ref[...]
    y = slab[:, 4:5] * buf_ref[0] + slab[:, 5:6] * buf_ref[1]
    o_ref[...] = _layer_norm(DEEPNORM_ALPHA * x1_ref[...] + y, g_ref[...], b_ref[...])


def _combine(d0, d1, ys, x1, slab, g, b):
    T = x1.shape[0]
    tm = ROW_BLK
    grid_spec = pltpu.PrefetchScalarGridSpec(
        num_scalar_prefetch=2,
        grid=(T // tm,),
        in_specs=[pl.BlockSpec(memory_space=pl.ANY),
                  pl.BlockSpec((tm, D_MODEL), lambda i, a, c: (i, 0)),
                  pl.BlockSpec((tm, LANES), lambda i, a, c: (i, 0)),
                  pl.BlockSpec((1, D_MODEL), lambda i, a, c: (0, 0)),
                  pl.BlockSpec((1, D_MODEL), lambda i, a, c: (0, 0))],
        out_specs=pl.BlockSpec((tm, D_MODEL), lambda i, a, c: (i, 0)),
        scratch_shapes=[pltpu.VMEM((2, tm, D_MODEL), F32), pltpu.SemaphoreType.DMA],
    )
    return pl.pallas_call(
        _combine_kernel,
        grid_spec=grid_spec,
        out_shape=jax.ShapeDtypeStruct((T, D_MODEL), F32),
        compiler_params=_cparams("arbitrary"),
        name="moe_combine",
    )(d0, d1, ys, x1, slab, g, b)


def _t5_bucket(dist):
    max_exact = N_BUCKETS // 2
    large = max_exact + (jnp.log(jnp.maximum(dist, 1).astype(F32) / max_exact)
                         / math.log(MAX_DISTANCE / max_exact) * (N_BUCKETS - max_exact)).astype(jnp.int32)
    return jnp.where(dist < max_exact, dist, jnp.minimum(large, N_BUCKETS - 1))


def _bias_table(table_cols, dil, max_dist):
    dist = BLK + jnp.arange(BLK)[:, None] - jnp.arange(2 * BLK)[None, :]
    bucket = _t5_bucket(jnp.maximum(dist * dil, 0))
    hit = bucket[None, :, :, None] == jnp.arange(N_BUCKETS)[None, None, None, :]
    bias = jnp.sum(jnp.where(hit, table_cols.T.astype(F32)[:, None, None, :], 0.0), axis=-1)
    mask = (dist >= 0) & (dist <= max_dist)
    return jnp.where(mask[None], bias, NEG_INF)


def _rope_tables(T):
    inv_freq = ROPE_THETA ** (-jnp.arange(0, C_ROPE, 2, dtype=F32) / C_ROPE)
    ang = jnp.arange(T, dtype=F32)[:, None] * inv_freq[None, :]
    cos, sin = jnp.cos(ang), jnp.sin(ang)
    ones = jnp.ones((T, C_NOPE), F32)
    zeros_n = jnp.zeros((T, C_NOPE), F32)
    zeros_p = jnp.zeros((T, C_PAD - C_NOPE - C_ROPE), F32)
    cs = jnp.concatenate([ones, cos, cos, zeros_p], axis=1)
    sn = jnp.concatenate([zeros_n, -sin, sin, zeros_p], axis=1)
    return cs, sn


def _swap_halves(w):
    half = w.shape[-1] // 2
    return jnp.concatenate([w[..., half:], w[..., :half]], axis=-1)


def _pad_cols(w, width):
    return jnp.pad(w, ((0, 0), (0, width - w.shape[1])))


def _prep_layer_weights(w_in, w_out, w_uq, w_ukv, w_rg, b_rg, w_re, b_re):
    head_cols = lambda w, h: w[:, h * HEAD_DIM:(h + 1) * HEAD_DIM]
    wab = jnp.concatenate([head_cols(w_in, h) for h in A_HEAD_ORDER] + [w_in[:, A_Q_HEADS * HEAD_DIM:AB_COLS]],
                          axis=1).astype(BF16)
    c0 = AB_COLS
    w_cq = w_in[:, c0:c0 + Q_LORA]
    w_ckv = w_in[:, c0 + Q_LORA:c0 + Q_LORA + KV_LORA]
    w_kr = w_in[:, c0 + Q_LORA + KV_LORA:]
    lead = jnp.zeros((D_MODEL, C_NOPE), F32)
    wc = jnp.concatenate([w_cq, w_ckv,
                          _pad_cols(jnp.concatenate([lead, w_kr], 1), C_PAD),
                          _pad_cols(jnp.concatenate([lead, _swap_halves(w_kr)], 1), C_PAD)], axis=1).astype(BF16)
    uq = w_uq.reshape(Q_LORA, C_HEADS, C_NOPE + C_ROPE)
    wq = jnp.pad(uq, ((0, 0), (0, 0), (0, C_PAD - C_NOPE - C_ROPE))).reshape(Q_LORA, C_HEADS * C_PAD).astype(BF16)
    uqs = jnp.concatenate([jnp.zeros((Q_LORA, C_HEADS, C_NOPE), F32), _swap_halves(uq[..., C_NOPE:])], axis=-1)
    wqs = jnp.pad(uqs, ((0, 0), (0, 0), (0, C_PAD - C_NOPE - C_ROPE))).reshape(Q_LORA, C_HEADS * C_PAD).astype(BF16)
    ukv = w_ukv.reshape(KV_LORA, C_HEADS, C_NOPE + C_V)
    wk = jnp.pad(ukv[..., :C_NOPE], ((0, 0), (0, 0), (0, C_PAD - C_NOPE))).reshape(KV_LORA, C_HEADS * C_PAD)
    wv = jnp.pad(ukv[..., C_NOPE:], ((0, 0), (0, 0), (0, C_PAD - C_V))).reshape(KV_LORA, C_HEADS * C_PAD)
    woa = jnp.concatenate([w_out[h * HEAD_DIM:(h + 1) * HEAD_DIM] for h in A_HEAD_ORDER],
                          axis=0).reshape(2, LANES, D_MODEL).astype(BF16)
    wob = w_out[256:512].reshape(2, LANES, D_MODEL).astype(BF16)
    woc = w_out[512:].astype(BF16)
    wr = _pad_cols(jnp.concatenate([w_rg, w_re.transpose(1, 0, 2).reshape(D_MODEL, N_EXPERTS)], axis=1), LANES)
    br = _pad_cols(jnp.concatenate([b_rg, b_re.reshape(N_EXPERTS)])[None, :], LANES)
    return dict(wab=wab, wc=wc, wq=wq, wqs=wqs, wk=wk.astype(BF16), wv=wv.astype(BF16),
                woa=woa, wob=wob, woc=woc, wr=wr, br=br)


def kernel(x, w_in, w_out, sinks, rel_bias, mla_q_norm, mla_kv_norm, w_uq, w_ukv, ln1_g, ln1_b,
           w_route_group, b_route_group, w_route_expert, b_route_expert,
           w_expert_gate, w_expert_up, w_expert_down, ln2_g, ln2_b):
    Bsz, T, D = x.shape
    assert Bsz == 1 and D == D_MODEL and T % CHUNK == 0
    xt = x.reshape(T, D)
    cs, sn = _rope_tables(T)
    bias_a = _bias_table(rel_bias[:, :A_Q_HEADS], 1, SWA_WINDOW - 1)
    bias_b = [_bias_table(rel_bias[:, A_Q_HEADS + g * B_HEADS_PER_GROUP:A_Q_HEADS + (g + 1) * B_HEADS_PER_GROUP],
                          dil, window // dil) for g, (window, dil) in enumerate(DILATED_PAIRS)]
    n_assign = T * TOP_K
    n_blocks = n_assign // ROW_BLK + N_EXPERTS
    n_rows = n_blocks * ROW_BLK
    no_sink = jnp.zeros((B_HEADS_PER_GROUP,), F32)

    for layer in range(w_in.shape[0]):
        w = _prep_layer_weights(w_in[layer], w_out[layer], w_uq[layer], w_ukv[layer],
                                w_route_group[layer], b_route_group[layer],
                                w_route_expert[layer], b_route_expert[layer])
        hab, qt, k, vt = _in_proj(xt, w["wab"], w["wc"], w["wq"], w["wqs"], w["wk"], w["wv"],
                                mla_q_norm[layer][None, :], mla_kv_norm[layer][None, :], cs, sn)
        oa, _ = _banded(hab, sinks[layer], bias_a, q_col=0, k_col=2, v_col=3, dil=1,
                        n_heads=A_Q_HEADS, group=A_Q_HEADS // A_KV_HEADS, has_sink=True, want_lse=False)
        ob, lse = [], []
        for g, (_, dil) in enumerate(DILATED_PAIRS):
            o_g, lse_g = _banded(hab, no_sink, bias_b[g], q_col=4 + 2 * g, k_col=10 + 2 * g, v_col=16 + 2 * g,
                                 dil=dil, n_heads=B_HEADS_PER_GROUP, group=1, has_sink=False, want_lse=True)
            ob += o_g
            lse += lse_g
        oc = _flash(qt, k, vt)
        x1, slab, cnt = _out_router(oa, ob, lse, oc, xt, w["woa"], w["wob"], w["woc"],
                                    ln1_g[layer][None, :], ln1_b[layer][None, :], w["wr"], w["br"])

        counts = cnt[0, N_GROUPS:N_GROUPS + N_EXPERTS].astype(jnp.int32)
        padded = ((counts + ROW_BLK - 1) // ROW_BLK) * ROW_BLK
        pad_end = jnp.cumsum(padded)
        pad_start = pad_end - padded
        eids = jnp.arange(N_EXPERTS, dtype=jnp.int32)
        pick = lambda e: jnp.sum(jnp.where(e.astype(jnp.int32)[:, None] == eids[None, :], pad_start[None, :], 0), axis=1)
        d0 = pick(slab[:, 0]) + slab[:, 2].astype(jnp.int32)
        d1 = pick(slab[:, 1]) + slab[:, 3].astype(jnp.int32)
        n_used = (pad_end[-1:] // ROW_BLK).astype(jnp.int32)
        block_start = jnp.arange(n_blocks, dtype=jnp.int32) * ROW_BLK
        block_expert = jnp.minimum(jnp.sum((pad_end[None, :] <= block_start[:, None]).astype(jnp.int32), axis=1),
                                   N_EXPERTS - 1)

        xs = _dispatch(d0, d1, x1, n_rows)
        ys = _ffn(block_expert, n_used, xs, w_expert_gate, w_expert_up, w_expert_down, layer)
        xt = _combine(d0, d1, ys, x1, slab, ln2_g[layer][None, :], ln2_b[layer][None, :])
    return xt.reshape(Bsz, T, D)
```

```python
import functools
import math

import jax
import jax.numpy as jnp
from jax import lax
from jax.experimental import pallas as pl
from jax.experimental.pallas import tpu as pltpu

F32 = jnp.float32
BF16 = jnp.bfloat16

D_MODEL = 1024
DEPTH = 4
HEAD_DIM = 64
BLK = 128
NEG_INF = -1e30

SWA_WINDOW = 128
A_Q_HEADS = 4
A_KV_HEADS = 2
A_COLS = (A_Q_HEADS + 2 * A_KV_HEADS) * HEAD_DIM
A_HEAD_ORDER = (0, 2, 1, 3)

DILATED_PAIRS = ((128, 1), (512, 4), (2048, 16))
B_HEADS_PER_GROUP = 4
B_HEADS = len(DILATED_PAIRS) * B_HEADS_PER_GROUP
B_COLS = 3 * B_HEADS * HEAD_DIM
AB_COLS = A_COLS + B_COLS

C_HEADS = 8
C_NOPE = 64
C_ROPE = 32
C_V = 64
Q_LORA = 256
KV_LORA = 128
ROPE_THETA = 10000.0
C_PAD = 128
C_VROWS = 80

N_BUCKETS = 32
MAX_DISTANCE = 2048

N_GROUPS = 4
EXPERTS_PER_GROUP = 8
N_EXPERTS = N_GROUPS * EXPERTS_PER_GROUP
TOP_K = 2
D_EXPERT = 256
ROW_BLK = 256

DEEPNORM_ALPHA = (2 * DEPTH) ** 0.25

VMEM_LIMIT = 56 * 1024 * 1024

CHUNK = 2048
TM_IN = 256
TM_OUT = 256
TQ = 512
TK = 512
FLASH_UNROLL = 4
LANES = 128


def _dot(a, b):
    return jnp.dot(a, b, preferred_element_type=F32)


def _dot_nt(a, b):
    return lax.dot_general(a, b, (((1,), (1,)), ((), ())), preferred_element_type=F32)


def _bdot(a, b):
    return lax.dot_general(a, b, (((2,), (1,)), ((0,), (0,))), preferred_element_type=F32)


def _bdot_nt(a, b):
    return lax.dot_general(a, b, (((2,), (2,)), ((0,), (0,))), preferred_element_type=F32)


def _cparams(*sem):
    return pltpu.CompilerParams(dimension_semantics=sem, vmem_limit_bytes=VMEM_LIMIT)


def _in_proj_kernel(x_ref, wab_ref, wc_ref, wq_ref, wqs_ref, wk_ref, wv_ref, gq_ref, gkv_ref, cs_ref, sn_ref,
                    hab_ref, qt_ref, k_ref, vt_ref):
    xb = x_ref[...].astype(BF16)
    hab_ref[...] = _dot(xb, wab_ref[...])
    hc = _dot(xb, wc_ref[...])
    cq = hc[:, :Q_LORA]
    ckv = hc[:, Q_LORA:Q_LORA + KV_LORA]
    kr = hc[:, Q_LORA + KV_LORA:Q_LORA + KV_LORA + C_PAD]
    krs = hc[:, Q_LORA + KV_LORA + C_PAD:]
    cqn = (cq * lax.rsqrt(jnp.mean(cq * cq, -1, keepdims=True) + 1e-6) * gq_ref[...]).astype(BF16)
    ckvn = (ckv * lax.rsqrt(jnp.mean(ckv * ckv, -1, keepdims=True) + 1e-6) * gkv_ref[...]).astype(BF16)
    cs = cs_ref[...]
    sn = sn_ref[...]
    q = _dot(cqn, wq_ref[...])
    qs = _dot(cqn, wqs_ref[...])
    k = _dot(ckvn, wk_ref[...])
    v = _dot(ckvn, wv_ref[...])
    krot = kr * cs + krs * sn
    scale = (C_NOPE + C_ROPE) ** -0.5 * math.log2(math.e)
    for h in range(C_HEADS):
        sl = slice(h * C_PAD, (h + 1) * C_PAD)
        qh = (q[:, sl] * cs + qs[:, sl] * sn) * scale
        qt_ref[sl, :] = qh.T.astype(BF16)
        k_ref[h] = (k[:, sl] + krot).astype(BF16)
    one_row = (lax.broadcasted_iota(jnp.int32, (1, C_PAD), 1) == C_V).astype(F32)
    for h in range(C_HEADS):
        vh = v[:, h * C_PAD:(h + 1) * C_PAD] + one_row
        vt_ref[h * C_VROWS:(h + 1) * C_VROWS, :] = vh.T[:C_VROWS, :].astype(BF16)


def _in_proj(x, wab, wc, wq, wqs, wk, wv, gq, gkv, cs, sn):
    T = x.shape[0]
    tm = TM_IN
    full = lambda a: pl.BlockSpec(a.shape, lambda i: (0,) * a.ndim)
    row = lambda w: pl.BlockSpec((tm, w), lambda i: (i, 0))
    colblk = lambda r: pl.BlockSpec((r, tm), lambda i: (0, i))
    return pl.pallas_call(
        _in_proj_kernel,
        grid=(T // tm,),
        in_specs=[row(D_MODEL), full(wab), full(wc), full(wq), full(wqs), full(wk), full(wv), full(gq), full(gkv),
                  row(C_PAD), row(C_PAD)],
        out_specs=[row(AB_COLS), colblk(C_HEADS * C_PAD), pl.BlockSpec((C_HEADS, tm, C_PAD), lambda i: (0, i, 0)),
                   colblk(C_HEADS * C_VROWS)],
        out_shape=[jax.ShapeDtypeStruct((T, AB_COLS), F32), jax.ShapeDtypeStruct((C_HEADS * C_PAD, T), BF16),
                   jax.ShapeDtypeStruct((C_HEADS, T, C_PAD), BF16),
                   jax.ShapeDtypeStruct((C_HEADS * C_VROWS, T), BF16)],
        compiler_params=_cparams("arbitrary"),
        name="in_proj",
    )(x, wab, wc, wq, wqs, wk, wv, gq, gkv, cs, sn)


def _banded_kernel(sink_ref, *refs, dil, n_heads, group, has_sink, want_lse):
    nq = n_heads // 2
    nkv = n_heads // group // 2
    q_refs, refs = refs[:nq], refs[nq:]
    k_refs, v_refs, kp_refs, vp_refs = (refs[i * nkv:(i + 1) * nkv] for i in range(4))
    bias_ref = refs[4 * nkv]
    o_refs = refs[4 * nkv + 1:4 * nkv + 1 + nq]
    lse_refs = refs[4 * nkv + 1 + nq:]
    c = pl.program_id(0)
    span = BLK * dil
    nb = CHUNK // span
    n_combo = nb * dil
    starts = [b * span + r for b in range(nb) for r in range(dil)]
    scale = HEAD_DIM ** -0.5

    def rows(start):
        return pl.ds(start, BLK) if dil == 1 else pl.ds(start, BLK, stride=dil)

    def gather(ref):
        return jnp.stack([ref[rows(s), :] for s in starts], axis=0)

    def with_prev(ref, prev_ref):
        cur = gather(ref)
        first = jnp.stack([prev_ref[rows(r), :] for r in range(dil)], axis=0)
        prev = jnp.concatenate([first, cur[:n_combo - dil]], axis=0) if dil < n_combo else first
        return jnp.concatenate([prev, cur], axis=1).astype(BF16)

    q = [gather(ref).astype(BF16) for ref in q_refs]
    kcat = [with_prev(k_refs[j], kp_refs[j]) for j in range(nkv)]
    vcat = [with_prev(v_refs[j], vp_refs[j]) for j in range(nkv)]

    lane = lax.broadcasted_iota(jnp.int32, (1, 1, LANES), 2)
    low = lane < HEAD_DIM
    combo = lax.broadcasted_iota(jnp.int32, (n_combo, 1, 2 * BLK), 0)
    col = lax.broadcasted_iota(jnp.int32, (n_combo, 1, 2 * BLK), 2)
    no_prev = jnp.logical_and(c == 0, jnp.logical_and(combo < dil, col < BLK))

    for j in range(nq):
        o_par, lse_par = [], []
        for par in range(2):
            h = 2 * j + par if group == 1 else group * par + j
            kvb = j if group == 1 else 0
            keep = low if par == 0 else jnp.logical_not(low)
            kk = jnp.where(keep, kcat[kvb], jnp.zeros((), BF16))
            s = _bdot_nt(q[j], kk)
            s = s * scale + bias_ref[h][None]
            s = jnp.where(no_prev, NEG_INF, s)
            m = jnp.max(s, axis=-1, keepdims=True)
            if has_sink:
                snk = sink_ref[h]
                m = jnp.maximum(m, snk)
            e = jnp.exp(s - m)
            l = jnp.sum(e, axis=-1, keepdims=True)
            if has_sink:
                l = l + jnp.exp(snk - m)
            p = (e * (1.0 / l)).astype(BF16)
            o_par.append(_bdot(p, vcat[kvb]))
            if want_lse:
                lse_par.append(m + jnp.log(l))
        o_pair = jnp.where(low, o_par[0], o_par[1])
        if want_lse:
            lse_pair = jnp.where(low, lse_par[0], lse_par[1])
        for ci, s0 in enumerate(starts):
            o_refs[j][rows(s0), :] = o_pair[ci]
            if want_lse:
                lse_refs[j][rows(s0), :] = lse_pair[ci]


def _banded(hab, sink, bias, *, q_col, k_col, v_col, dil, n_heads, group, has_sink, want_lse):
    T = hab.shape[0]
    span = BLK * dil
    nb = CHUNK // span
    nq = n_heads // 2
    nkv = n_heads // group // 2
    kernel = functools.partial(_banded_kernel, dil=dil, n_heads=n_heads, group=group,
                               has_sink=has_sink, want_lse=want_lse)
    prev = lambda cb: pl.BlockSpec((span, LANES), lambda c, s: (jnp.maximum(c * nb - 1, 0), cb))
    cur = lambda cb: pl.BlockSpec((CHUNK, LANES), lambda c, s: (c, cb))
    in_specs = ([cur(q_col + j) for j in range(nq)] + [cur(k_col + j) for j in range(nkv)]
                + [cur(v_col + j) for j in range(nkv)] + [prev(k_col + j) for j in range(nkv)]
                + [prev(v_col + j) for j in range(nkv)] + [pl.BlockSpec(bias.shape, lambda c, s: (0, 0, 0))])
    n_out = nq * (2 if want_lse else 1)
    grid_spec = pltpu.PrefetchScalarGridSpec(
        num_scalar_prefetch=1,
        grid=(T // CHUNK,),
        in_specs=in_specs,
        out_specs=[pl.BlockSpec((CHUNK, LANES), lambda c, s: (c, 0))] * n_out,
    )
    outs = pl.pallas_call(
        kernel,
        grid_spec=grid_spec,
        out_shape=[jax.ShapeDtypeStruct((T, LANES), F32)] * n_out,
        compiler_params=_cparams("arbitrary"),
        name=f"banded_d{dil}" + ("_sink" if has_sink else ""),
    )(sink, *([hab] * (nq + 4 * nkv)), bias)
    return outs[:nq], outs[nq:]


def _flash_kernel(qt_ref, k_ref, vt_ref, ot_ref, m_ref, acc_ref, s_ref):
    qi = pl.program_id(1)
    qt = qt_ref[...]
    m_ref[...] = jnp.full(m_ref.shape, NEG_INF, F32)
    acc_ref[...] = jnp.zeros(acc_ref.shape, F32)

    def scores(j):
        off = pl.multiple_of(j * TK, TK)
        return _dot(k_ref[pl.ds(off, TK), :], qt)

    def accumulate(j, s):
        off = pl.multiple_of(j * TK, TK)
        m_old = m_ref[...]
        m_new = jnp.maximum(m_old, jnp.max(s, axis=0, keepdims=True))
        alpha = jnp.exp2(m_old - m_new)
        p = jnp.exp2(s - m_new).astype(BF16)
        acc_ref[...] = alpha * acc_ref[...] + _dot(vt_ref[:, pl.ds(off, TK)], p)
        m_ref[...] = m_new

    def diagonal(s):
        key = lax.broadcasted_iota(jnp.int32, (TK, TQ), 0)
        qry = lax.broadcasted_iota(jnp.int32, (TK, TQ), 1)
        return jnp.where(key <= qry, s, NEG_INF)

    s_ref[0] = scores(0)
    n_main = qi // FLASH_UNROLL

    def main_body(i, carry):
        j = FLASH_UNROLL * i
        for u in range(FLASH_UNROLL):
            s_ref[(u + 1) % 2] = scores(j + u + 1)
            accumulate(j + u, s_ref[u % 2])
        return carry

    lax.fori_loop(0, n_main, main_body, 0)

    base = FLASH_UNROLL * n_main
    for rem in range(FLASH_UNROLL):
        @pl.when(qi - base == rem)
        def _():
            for u in range(rem):
                s_ref[(u + 1) % 2] = scores(base + u + 1)
                accumulate(base + u, s_ref[u % 2])
            accumulate(qi, diagonal(s_ref[rem % 2]))

    ot_ref[...] = (acc_ref[:C_V, :] / acc_ref[C_V:C_V + 1, :]).astype(ot_ref.dtype)


def _flash(qt, k, vt):
    H, T, _ = k.shape
    return pl.pallas_call(
        _flash_kernel,
        grid=(H, T // TQ),
        in_specs=[pl.BlockSpec((C_PAD, TQ), lambda h, i: (h, i)),
                  pl.BlockSpec((None, T, C_PAD), lambda h, i: (h, 0, 0)),
                  pl.BlockSpec((C_VROWS, T), lambda h, i: (h, 0))],
        out_specs=pl.BlockSpec((C_V, TQ), lambda h, i: (h, i)),
        out_shape=jax.ShapeDtypeStruct((H * C_V, T), BF16),
        scratch_shapes=[pltpu.VMEM((1, TQ), F32), pltpu.VMEM((C_VROWS, TQ), F32), pltpu.VMEM((2, TK, TQ), F32)],
        compiler_params=_cparams("arbitrary", "arbitrary"),
        name="mla_flash",
    )(qt, k, vt)


def _layer_norm(z, g, b):
    mu = jnp.mean(z, -1, keepdims=True)
    zc = z - mu
    var = jnp.mean(zc * zc, -1, keepdims=True)
    return zc * lax.rsqrt(var + 1e-5) * g + b


def _out_router_kernel(*refs):
    n_ab = 2 + 6 + 6
    oa_refs, ob_refs, lse_refs = refs[0:2], refs[2:8], refs[8:14]
    (oc_ref, x_ref, woa_ref, wob_ref, woc_ref, g_ref, b_ref, wr_ref, br_ref,
     x1_ref, slab_ref, cnt_ref, carry_ref) = refs[n_ab:]
    i = pl.program_id(0)
    tm = x_ref.shape[0]

    @pl.when(i == 0)
    def _():
        carry_ref[...] = jnp.zeros(carry_ref.shape, F32)

    y = None
    for j in range(2):
        l0, l1, l2 = (lse_refs[2 * g + j][...] for g in range(3))
        mx = jnp.maximum(jnp.maximum(l0, l1), l2)
        e0, e1, e2 = jnp.exp(l0 - mx), jnp.exp(l1 - mx), jnp.exp(l2 - mx)
        den = e0 + e1 + e2
        ob = ((e0 / den) * ob_refs[j][...] + (e1 / den) * ob_refs[2 + j][...] + (e2 / den) * ob_refs[4 + j][...])
        part = _dot(oa_refs[j][...].astype(BF16), woa_ref[j]) + _dot(ob.astype(BF16), wob_ref[j])
        y = part if y is None else y + part
    y = y + _dot(oc_ref[...].astype(F32).T.astype(BF16), woc_ref[...])
    x1 = _layer_norm(DEEPNORM_ALPHA * x_ref[...] + y, g_ref[...], b_ref[...])
    x1_ref[...] = x1

    logits = jnp.dot(x1, wr_ref[...], precision=lax.Precision.HIGHEST, preferred_element_type=F32) + br_ref[...]
    lane = lax.broadcasted_iota(jnp.int32, (tm, LANES), 1)
    ninf = jnp.float32(-jnp.inf)
    gmask = lane < N_GROUPS
    gl = jnp.where(gmask, logits, ninf)
    gmax = jnp.max(gl, axis=-1, keepdims=True)
    grp = jnp.min(jnp.where(gl == gmax, lane, LANES), axis=-1, keepdims=True)
    p_grp = 1.0 / jnp.sum(jnp.where(gmask, jnp.exp(logits - gmax), 0.0), axis=-1, keepdims=True)
    lo = N_GROUPS + EXPERTS_PER_GROUP * grp
    emask = jnp.logical_and(lane >= lo, lane < lo + EXPERTS_PER_GROUP)
    el = jnp.where(emask, logits, ninf)
    v0 = jnp.max(el, axis=-1, keepdims=True)
    i0 = jnp.min(jnp.where(el == v0, lane, LANES), axis=-1, keepdims=True)
    el2 = jnp.where(lane == i0, ninf, el)
    v1 = jnp.max(el2, axis=-1, keepdims=True)
    i1 = jnp.min(jnp.where(el2 == v1, lane, LANES), axis=-1, keepdims=True)
    t = jnp.exp(v1 - v0)
    gate0 = p_grp / (1.0 + t)
    gate1 = p_grp * t / (1.0 + t)

    hit0 = lane == i0
    hit1 = lane == i1
    cmat = jnp.logical_or(hit0, hit1).astype(BF16)
    tr = lax.broadcasted_iota(jnp.int32, (tm, tm), 0)
    tc = lax.broadcasted_iota(jnp.int32, (tm, tm), 1)
    tril = (tc < tr).astype(BF16)
    prefix = _dot(tril, cmat) + carry_ref[...]
    r0 = jnp.sum(jnp.where(hit0, prefix, 0.0), axis=-1, keepdims=True)
    r1 = jnp.sum(jnp.where(hit1, prefix, 0.0), axis=-1, keepdims=True)
    carry_ref[...] = carry_ref[...] + jnp.sum(cmat.astype(F32), axis=0, keepdims=True)
    cnt_ref[...] = carry_ref[...]

    fields = [(i0 - N_GROUPS).astype(F32), (i1 - N_GROUPS).astype(F32), r0, r1, gate0, gate1]
    slab = jnp.zeros((tm, LANES), F32)
    for n, f in enumerate(fields):
        slab = jnp.where(lane == n, f, slab)
    slab_ref[...] = slab


def _out_router(oa, ob, lse, oc, x, woa, wob, woc, g, b, wr, br):
    T = x.shape[0]
    tm = TM_OUT
    full = lambda a: pl.BlockSpec(a.shape, lambda i: (0,) * a.ndim)
    row = lambda w: pl.BlockSpec((tm, w), lambda i: (i, 0))
    return pl.pallas_call(
        _out_router_kernel,
        grid=(T // tm,),
        in_specs=[row(LANES)] * 14 + [pl.BlockSpec((C_HEADS * C_V, tm), lambda i: (0, i)), row(D_MODEL),
                                      full(woa), full(wob), full(woc), full(g), full(b), full(wr), full(br)],
        out_specs=[row(D_MODEL), row(LANES), pl.BlockSpec((1, LANES), lambda i: (0, 0))],
        out_shape=[jax.ShapeDtypeStruct((T, D_MODEL), F32), jax.ShapeDtypeStruct((T, LANES), F32),
                   jax.ShapeDtypeStruct((1, LANES), F32)],
        scratch_shapes=[pltpu.VMEM((1, LANES), F32)],
        compiler_params=_cparams("arbitrary"),
        name="out_router",
    )(*oa, *ob, *lse, oc, x, woa, wob, woc, g, b, wr, br)


def _dispatch_kernel(d0_ref, d1_ref, x_ref, zeros_ref, xs_ref, sem):
    del zeros_ref
    i = pl.program_id(0)
    tm = x_ref.shape[0]
    base = i * tm

    def copy(t, d_ref):
        return pltpu.make_async_copy(x_ref.at[pl.ds(t, 1), :], xs_ref.at[pl.ds(d_ref[base + t], 1), :], sem)

    def start(t, carry):
        copy(t, d0_ref).start()
        copy(t, d1_ref).start()
        return carry

    def wait(t, carry):
        copy(t, d0_ref).wait()
        copy(t, d1_ref).wait()
        return carry

    lax.fori_loop(0, tm, start, 0)
    lax.fori_loop(0, tm, wait, 0)


def _dispatch(d0, d1, x1, n_rows):
    T = x1.shape[0]
    tm = ROW_BLK
    zeros = jnp.zeros((n_rows, D_MODEL), F32)
    grid_spec = pltpu.PrefetchScalarGridSpec(
        num_scalar_prefetch=2,
        grid=(T // tm,),
        in_specs=[pl.BlockSpec((tm, D_MODEL), lambda i, a, b: (i, 0)), pl.BlockSpec(memory_space=pl.ANY)],
        out_specs=pl.BlockSpec(memory_space=pl.ANY),
        scratch_shapes=[pltpu.SemaphoreType.DMA],
    )
    return pl.pallas_call(
        _dispatch_kernel,
        grid_spec=grid_spec,
        out_shape=jax.ShapeDtypeStruct((n_rows, D_MODEL), F32),
        input_output_aliases={3: 0},
        compiler_params=_cparams("arbitrary"),
        name="moe_dispatch",
    )(d0, d1, x1, zeros)


def _ffn_kernel(be_ref, nu_ref, xs_ref, wg_ref, wu_ref, wd_ref, ys_ref):
    i = pl.program_id(0)

    @pl.when(i < nu_ref[0])
    def _():
        xb = xs_ref[...].astype(BF16)
        a = _dot(xb, wg_ref[...].astype(BF16))
        u = _dot(xb, wu_ref[...].astype(BF16))
        hmid = (a * jax.nn.sigmoid(a) * u).astype(BF16)
        ys_ref[...] = _dot(hmid, wd_ref[...].astype(BF16))

    @pl.when(i >= nu_ref[0])
    def _():
        ys_ref[...] = jnp.zeros(ys_ref.shape, F32)


def _ffn(block_expert, n_used, xs, wg, wu, wd, layer):
    n_rows = xs.shape[0]
    grid_spec = pltpu.PrefetchScalarGridSpec(
        num_scalar_prefetch=2,
        grid=(n_rows // ROW_BLK,),
        in_specs=[pl.BlockSpec((ROW_BLK, D_MODEL), lambda i, be, nu: (i, 0)),
                  pl.BlockSpec((None, None, D_MODEL, D_EXPERT), lambda i, be, nu: (layer, be[i], 0, 0)),
                  pl.BlockSpec((None, None, D_MODEL, D_EXPERT), lambda i, be, nu: (layer, be[i], 0, 0)),
                  pl.BlockSpec((None, None, D_EXPERT, D_MODEL), lambda i, be, nu: (layer, be[i], 0, 0))],
        out_specs=pl.BlockSpec((ROW_BLK, D_MODEL), lambda i, be, nu: (i, 0)),
    )
    return pl.pallas_call(
        _ffn_kernel,
        grid_spec=grid_spec,
        out_shape=jax.ShapeDtypeStruct((n_rows, D_MODEL), F32),
        compiler_params=_cparams("arbitrary"),
        name="moe_ffn",
    )(block_expert, n_used, xs, wg, wu, wd)


def _combine_kernel(d0_ref, d1_ref, ys_ref, x1_ref, slab_ref, g_ref, b_ref, o_ref, buf_ref, sem):
    i = pl.program_id(0)
    tm = x1_ref.shape[0]
    base = i * tm

    def copy(t, d_ref, slot):
        return pltpu.make_async_copy(ys_ref.at[pl.ds(d_ref[base + t], 1), :], buf_ref.at[slot, pl.ds(t, 1), :], sem)

    def start(t, carry):
        copy(t, d0_ref, 0).start()
        copy(t, d1_ref, 1).start()
        return carry

    def wait(t, carry):
        copy(t, d0_ref, 0).wait()
        copy(t, d1_ref, 1).wait()
        return carry

    lax.fori_loop(0, tm, start, 0)
    lax.fori_loop(0, tm, wait, 0)
    slab = slab_ref[...]
    y = slab[:, 4:5] * buf_ref[0] + slab[:, 5:6] * buf_ref[1]
    o_ref[...] = _layer_norm(DEEPNORM_ALPHA * x1_ref[...] + y, g_ref[...], b_ref[...])


def _combine(d0, d1, ys, x1, slab, g, b):
    T = x1.shape[0]
    tm = ROW_BLK
    grid_spec = pltpu.PrefetchScalarGridSpec(
        num_scalar_prefetch=2,
        grid=(T // tm,),
        in_specs=[pl.BlockSpec(memory_space=pl.ANY),
                  pl.BlockSpec((tm, D_MODEL), lambda i, a, c: (i, 0)),
                  pl.BlockSpec((tm, LANES), lambda i, a, c: (i, 0)),
                  pl.BlockSpec((1, D_MODEL), lambda i, a, c: (0, 0)),
                  pl.BlockSpec((1, D_MODEL), lambda i, a, c: (0, 0))],
        out_specs=pl.BlockSpec((tm, D_MODEL), lambda i, a, c: (i, 0)),
        scratch_shapes=[pltpu.VMEM((2, tm, D_MODEL), F32), pltpu.SemaphoreType.DMA],
    )
    return pl.pallas_call(
        _combine_kernel,
        grid_spec=grid_spec,
        out_shape=jax.ShapeDtypeStruct((T, D_MODEL), F32),
        compiler_params=_cparams("arbitrary"),
        name="moe_combine",
    )(d0, d1, ys, x1, slab, g, b)


def _invert_kernel(d0_ref, d1_ref, ros_ref):
    n_tok = d0_ref.shape[0]
    plane = n_tok + ROW_BLK

    def fill(s, carry):
        odd_block = lax.shift_right_logical(s, ROW_BLK.bit_length() - 1) & 1
        spare = n_tok + (s & (ROW_BLK - 1)) + odd_block * plane
        ros_ref[s] = ((n_tok - 1) << 16) | spare
        return carry

    def put(t, carry):
        ros_ref[d0_ref[t]] = (t << 16) | t
        ros_ref[d1_ref[t]] = (t << 16) | (plane + t)
        return carry

    i = pl.program_id(0)
    half = pl.num_programs(0) // 2
    n_fill = ros_ref.shape[0] // half
    n_put = n_tok // half

    @pl.when(i < half)
    def _():
        lax.fori_loop(i * n_fill, (i + 1) * n_fill, fill, 0)

    @pl.when(i >= half)
    def _():
        lax.fori_loop((i - half) * n_put, (i - half + 1) * n_put, put, 0)


INVERT_STEPS = 64


def _invert(d0, d1, n_rows):
    assert n_rows % INVERT_STEPS == 0 and d0.shape[0] % INVERT_STEPS == 0
    smem = pl.BlockSpec(memory_space=pltpu.SMEM)
    return pl.pallas_call(
        _invert_kernel,
        grid=(2 * INVERT_STEPS,),
        in_specs=[smem, smem],
        out_specs=smem,
        out_shape=jax.ShapeDtypeStruct((n_rows,), jnp.int32),
        compiler_params=_cparams("arbitrary"),
        name="moe_invert",
    )(d0, d1)


def _ffn_fused_kernel(be_ref, ros_ref, x1_ref, wg0_ref, wu0_ref, wd0_ref, wg1_ref, wu1_ref, wd1_ref, out_ref,
                      xb0, xb1, yb0, yb1, gsem, ssem):
    del be_ref
    i = pl.program_id(0)
    last = pl.num_programs(0) - 1
    n_tok = x1_ref.shape[0]
    plane = n_tok + ROW_BLK

    def gather(blk, xbuf, sem):
        for r in range(ROW_BLK):
            tok = lax.shift_right_logical(ros_ref[blk * ROW_BLK + r], 16)
            pltpu.make_async_copy(x1_ref.at[pl.ds(tok, 1), :], xbuf.at[pl.ds(r, 1), :], sem).start()

    def scatter(blk, ybuf, sem):
        for r in range(ROW_BLK):
            row = ros_ref[blk * ROW_BLK + r] & 0xFFFF
            pltpu.make_async_copy(ybuf.at[pl.ds(r, 1), :], out_ref.at[pl.ds(row, 1), :], sem).start()

    def wait_rows(buf, sem):
        pltpu.make_async_copy(x1_ref.at[pl.ds(0, ROW_BLK), :], buf, sem).wait()

    def ffn(xbuf, ybuf, wg_ref, wu_ref, wd_ref):
        xb = xbuf[...].astype(BF16)
        a = _dot(xb, wg_ref[...].astype(BF16))
        u = _dot(xb, wu_ref[...].astype(BF16))
        hmid = (a * jax.nn.sigmoid(a) * u).astype(BF16)
        ybuf[...] = _dot(hmid, wd_ref[...].astype(BF16))

    b0 = 2 * i
    b1 = 2 * i + 1
    b2 = jnp.minimum(2 * i + 2, 2 * last + 1)

    @pl.when(i == 0)
    def _():
        yb0[...] = jnp.zeros(yb0.shape, F32)
        for base in (n_tok, plane + n_tok):
            spare = pltpu.make_async_copy(yb0, out_ref.at[pl.ds(base, ROW_BLK), :], ssem.at[0])
            spare.start()
            spare.wait()
        gather(0, xb0, gsem.at[0])

    wait_rows(xb0, gsem.at[0])
    gather(b1, xb1, gsem.at[1])

    @pl.when(i > 0)
    def _():
        wait_rows(yb0, ssem.at[0])

    ffn(xb0, yb0, wg0_ref, wu0_ref, wd0_ref)
    scatter(b0, yb0, ssem.at[0])
    wait_rows(xb1, gsem.at[1])
    gather(b2, xb0, gsem.at[0])

    @pl.when(i > 0)
    def _():
        wait_rows(yb1, ssem.at[1])

    ffn(xb1, yb1, wg1_ref, wu1_ref, wd1_ref)
    scatter(b1, yb1, ssem.at[1])

    @pl.when(i == last)
    def _():
        wait_rows(xb0, gsem.at[0])
        wait_rows(yb0, ssem.at[0])
        wait_rows(yb1, ssem.at[1])


def _ffn_fused(block_expert, ros, x1, wg, wu, wd, layer):
    T = x1.shape[0]
    n_blocks = ros.shape[0] // ROW_BLK
    assert n_blocks % 2 == 0
    w_spec = lambda shape, k: pl.BlockSpec((None, None) + shape, lambda i, be, ro: (layer, be[2 * i + k], 0, 0))
    grid_spec = pltpu.PrefetchScalarGridSpec(
        num_scalar_prefetch=2,
        grid=(n_blocks // 2,),
        in_specs=[pl.BlockSpec(memory_space=pl.ANY)]
        + [w_spec(s, k) for k in range(2) for s in ((D_MODEL, D_EXPERT), (D_MODEL, D_EXPERT), (D_EXPERT, D_MODEL))],
        out_specs=pl.BlockSpec(memory_space=pl.ANY),
        scratch_shapes=[pltpu.VMEM((ROW_BLK, D_MODEL), F32)] * 4 + [pltpu.SemaphoreType.DMA((2,))] * 2,
    )
    return pl.pallas_call(
        _ffn_fused_kernel,
        grid_spec=grid_spec,
        out_shape=jax.ShapeDtypeStruct((2 * (T + ROW_BLK), D_MODEL), F32),
        compiler_params=_cparams("arbitrary"),
        name="moe_ffn",
    )(block_expert, ros, x1, wg, wu, wd, wg, wu, wd)


def _combine2_kernel(y0_ref, y1_ref, x1_ref, slab_ref, g_ref, b_ref, o_ref):
    slab = slab_ref[...]
    y = slab[:, 4:5] * y0_ref[...] + slab[:, 5:6] * y1_ref[...]
    o_ref[...] = _layer_norm(DEEPNORM_ALPHA * x1_ref[...] + y, g_ref[...], b_ref[...])


def _combine2(out2, x1, slab, g, b):
    T = x1.shape[0]
    tm = ROW_BLK
    plane_blocks = (T + ROW_BLK) // tm
    row = lambda w: pl.BlockSpec((tm, w), lambda i: (i, 0))
    one = pl.BlockSpec((1, D_MODEL), lambda i: (0, 0))
    return pl.pallas_call(
        _combine2_kernel,
        grid=(T // tm,),
        in_specs=[row(D_MODEL), pl.BlockSpec((tm, D_MODEL), lambda i: (plane_blocks + i, 0)), row(D_MODEL), row(LANES),
                  one, one],
        out_specs=row(D_MODEL),
        out_shape=jax.ShapeDtypeStruct((T, D_MODEL), F32),
        compiler_params=_cparams("arbitrary"),
        name="moe_combine",
    )(out2, out2, x1, slab, g, b)


def _t5_bucket(dist):
    max_exact = N_BUCKETS // 2
    large = max_exact + (jnp.log(jnp.maximum(dist, 1).astype(F32) / max_exact)
                         / math.log(MAX_DISTANCE / max_exact) * (N_BUCKETS - max_exact)).astype(jnp.int32)
    return jnp.where(dist < max_exact, dist, jnp.minimum(large, N_BUCKETS - 1))


def _bias_table(table_cols, dil, max_dist):
    dist = BLK + jnp.arange(BLK)[:, None] - jnp.arange(2 * BLK)[None, :]
    bucket = _t5_bucket(jnp.maximum(dist * dil, 0))
    hit = bucket[None, :, :, None] == jnp.arange(N_BUCKETS)[None, None, None, :]
    bias = jnp.sum(jnp.where(hit, table_cols.T.astype(F32)[:, None, None, :], 0.0), axis=-1)
    mask = (dist >= 0) & (dist <= max_dist)
    return jnp.where(mask[None], bias, NEG_INF)


def _rope_tables(T):
    inv_freq = ROPE_THETA ** (-jnp.arange(0, C_ROPE, 2, dtype=F32) / C_ROPE)
    ang = jnp.arange(T, dtype=F32)[:, None] * inv_freq[None, :]
    cos, sin = jnp.cos(ang), jnp.sin(ang)
    ones = jnp.ones((T, C_NOPE), F32)
    zeros_n = jnp.zeros((T, C_NOPE), F32)
    zeros_p = jnp.zeros((T, C_PAD - C_NOPE - C_ROPE), F32)
    cs = jnp.concatenate([ones, cos, cos, zeros_p], axis=1)
    sn = jnp.concatenate([zeros_n, -sin, sin, zeros_p], axis=1)
    return cs, sn


def _swap_halves(w):
    half = w.shape[-1] // 2
    return jnp.concatenate([w[..., half:], w[..., :half]], axis=-1)


def _pad_cols(w, width):
    return jnp.pad(w, ((0, 0), (0, width - w.shape[1])))


def _prep_layer_weights(w_in, w_out, w_uq, w_ukv, w_rg, b_rg, w_re, b_re):
    head_cols = lambda w, h: w[:, h * HEAD_DIM:(h + 1) * HEAD_DIM]
    wab = jnp.concatenate([head_cols(w_in, h) for h in A_HEAD_ORDER] + [w_in[:, A_Q_HEADS * HEAD_DIM:AB_COLS]],
                          axis=1).astype(BF16)
    c0 = AB_COLS
    w_cq = w_in[:, c0:c0 + Q_LORA]
    w_ckv = w_in[:, c0 + Q_LORA:c0 + Q_LORA + KV_LORA]
    w_kr = w_in[:, c0 + Q_LORA + KV_LORA:]
    lead = jnp.zeros((D_MODEL, C_NOPE), F32)
    wc = jnp.concatenate([w_cq, w_ckv,
                          _pad_cols(jnp.concatenate([lead, w_kr], 1), C_PAD),
                          _pad_cols(jnp.concatenate([lead, _swap_halves(w_kr)], 1), C_PAD)], axis=1).astype(BF16)
    uq = w_uq.reshape(Q_LORA, C_HEADS, C_NOPE + C_ROPE)
    wq = jnp.pad(uq, ((0, 0), (0, 0), (0, C_PAD - C_NOPE - C_ROPE))).reshape(Q_LORA, C_HEADS * C_PAD).astype(BF16)
    uqs = jnp.concatenate([jnp.zeros((Q_LORA, C_HEADS, C_NOPE), F32), _swap_halves(uq[..., C_NOPE:])], axis=-1)
    wqs = jnp.pad(uqs, ((0, 0), (0, 0), (0, C_PAD - C_NOPE - C_ROPE))).reshape(Q_LORA, C_HEADS * C_PAD).astype(BF16)
    ukv = w_ukv.reshape(KV_LORA, C_HEADS, C_NOPE + C_V)
    wk = jnp.pad(ukv[..., :C_NOPE], ((0, 0), (0, 0), (0, C_PAD - C_NOPE))).reshape(KV_LORA, C_HEADS * C_PAD)
    wv = jnp.pad(ukv[..., C_NOPE:], ((0, 0), (0, 0), (0, C_PAD - C_V))).reshape(KV_LORA, C_HEADS * C_PAD)
    woa = jnp.concatenate([w_out[h * HEAD_DIM:(h + 1) * HEAD_DIM] for h in A_HEAD_ORDER],
                          axis=0).reshape(2, LANES, D_MODEL).astype(BF16)
    wob = w_out[256:512].reshape(2, LANES, D_MODEL).astype(BF16)
    woc = w_out[512:].astype(BF16)
    wr = _pad_cols(jnp.concatenate([w_rg, w_re.transpose(1, 0, 2).reshape(D_MODEL, N_EXPERTS)], axis=1), LANES)
    br = _pad_cols(jnp.concatenate([b_rg, b_re.reshape(N_EXPERTS)])[None, :], LANES)
    return dict(wab=wab, wc=wc, wq=wq, wqs=wqs, wk=wk.astype(BF16), wv=wv.astype(BF16),
                woa=woa, wob=wob, woc=woc, wr=wr, br=br)


def kernel(x, w_in, w_out, sinks, rel_bias, mla_q_norm, mla_kv_norm, w_uq, w_ukv, ln1_g, ln1_b,
           w_route_group, b_route_group, w_route_expert, b_route_expert,
           w_expert_gate, w_expert_up, w_expert_down, ln2_g, ln2_b):
    Bsz, T, D = x.shape
    assert Bsz == 1 and D == D_MODEL and T % CHUNK == 0
    xt = x.reshape(T, D)
    cs, sn = _rope_tables(T)
    bias_a = _bias_table(rel_bias[:, :A_Q_HEADS], 1, SWA_WINDOW - 1)
    bias_b = [_bias_table(rel_bias[:, A_Q_HEADS + g * B_HEADS_PER_GROUP:A_Q_HEADS + (g + 1) * B_HEADS_PER_GROUP],
                          dil, window // dil) for g, (window, dil) in enumerate(DILATED_PAIRS)]
    n_assign = T * TOP_K
    n_blocks = n_assign // ROW_BLK + N_EXPERTS
    n_rows = n_blocks * ROW_BLK
    no_sink = jnp.zeros((B_HEADS_PER_GROUP,), F32)

    for layer in range(w_in.shape[0]):
        w = _prep_layer_weights(w_in[layer], w_out[layer], w_uq[layer], w_ukv[layer],
                                w_route_group[layer], b_route_group[layer],
                                w_route_expert[layer], b_route_expert[layer])
        hab, qt, k, vt = _in_proj(xt, w["wab"], w["wc"], w["wq"], w["wqs"], w["wk"], w["wv"],
                                mla_q_norm[layer][None, :], mla_kv_norm[layer][None, :], cs, sn)
        oa, _ = _banded(hab, sinks[layer], bias_a, q_col=0, k_col=2, v_col=3, dil=1,
                        n_heads=A_Q_HEADS, group=A_Q_HEADS // A_KV_HEADS, has_sink=True, want_lse=False)
        ob, lse = [], []
        for g, (_, dil) in enumerate(DILATED_PAIRS):
            o_g, lse_g = _banded(hab, no_sink, bias_b[g], q_col=4 + 2 * g, k_col=10 + 2 * g, v_col=16 + 2 * g,
                                 dil=dil, n_heads=B_HEADS_PER_GROUP, group=1, has_sink=False, want_lse=True)
            ob += o_g
            lse += lse_g
        oc = _flash(qt, k, vt)
        x1, slab, cnt = _out_router(oa, ob, lse, oc, xt, w["woa"], w["wob"], w["woc"],
                                    ln1_g[layer][None, :], ln1_b[layer][None, :], w["wr"], w["br"])

        counts = cnt[0, N_GROUPS:N_GROUPS + N_EXPERTS].astype(jnp.int32)
        padded = ((counts + ROW_BLK - 1) // ROW_BLK) * ROW_BLK
        pad_end = jnp.cumsum(padded)
        pad_start = pad_end - padded
        eids = jnp.arange(N_EXPERTS, dtype=jnp.int32)
        pick = lambda e: jnp.sum(jnp.where(e.astype(jnp.int32)[:, None] == eids[None, :], pad_start[None, :], 0), axis=1)
        d0 = pick(slab[:, 0]) + slab[:, 2].astype(jnp.int32)
        d1 = pick(slab[:, 1]) + slab[:, 3].astype(jnp.int32)
        block_start = jnp.arange(n_blocks, dtype=jnp.int32) * ROW_BLK
        block_expert = jnp.minimum(jnp.sum((pad_end[None, :] <= block_start[:, None]).astype(jnp.int32), axis=1),
                                   N_EXPERTS - 1)

        ros = _invert(d0, d1, n_rows)
        out2 = _ffn_fused(block_expert, ros, x1, w_expert_gate, w_expert_up, w_expert_down, layer)
        xt = _combine2(out2, x1, slab, ln2_g[layer][None, :], ln2_b[layer][None, :])
    return xt.reshape(Bsz, T, D)
```

```python
import functools
import math

import jax
import jax.numpy as jnp
from jax import lax
from jax.experimental import pallas as pl
from jax.experimental.pallas import tpu as pltpu

F32 = jnp.float32
BF16 = jnp.bfloat16

D_MODEL = 1024
DEPTH = 4
HEAD_DIM = 64
BLK = 128
NEG_INF = -1e30

SWA_WINDOW = 128
A_Q_HEADS = 4
A_KV_HEADS = 2
A_COLS = (A_Q_HEADS + 2 * A_KV_HEADS) * HEAD_DIM
A_HEAD_ORDER = (0, 2, 1, 3)

DILATED_PAIRS = ((128, 1), (512, 4), (2048, 16))
B_HEADS_PER_GROUP = 4
B_HEADS = len(DILATED_PAIRS) * B_HEADS_PER_GROUP
B_COLS = 3 * B_HEADS * HEAD_DIM
AB_COLS = A_COLS + B_COLS

C_HEADS = 8
C_NOPE = 64
C_ROPE = 32
C_V = 64
Q_LORA = 256
KV_LORA = 128
ROPE_THETA = 10000.0
C_PAD = 128
C_VROWS = 80

N_BUCKETS = 32
MAX_DISTANCE = 2048

N_GROUPS = 4
EXPERTS_PER_GROUP = 8
N_EXPERTS = N_GROUPS * EXPERTS_PER_GROUP
TOP_K = 2
D_EXPERT = 256
ROW_BLK = 256

DEEPNORM_ALPHA = (2 * DEPTH) ** 0.25

VMEM_LIMIT = 56 * 1024 * 1024

CHUNK = 2048
TM_IN = 256
TM_OUT = 256
TQ = 512
TK = 512
FLASH_UNROLL = 4
LANES = 128


def _dot(a, b):
    return jnp.dot(a, b, preferred_element_type=F32)


def _dot_nt(a, b):
    return lax.dot_general(a, b, (((1,), (1,)), ((), ())), preferred_element_type=F32)


def _bdot(a, b):
    return lax.dot_general(a, b, (((2,), (1,)), ((0,), (0,))), preferred_element_type=F32)


def _bdot_nt(a, b):
    return lax.dot_general(a, b, (((2,), (2,)), ((0,), (0,))), preferred_element_type=F32)


def _cparams(*sem):
    return pltpu.CompilerParams(dimension_semantics=sem, vmem_limit_bytes=VMEM_LIMIT)


def _in_proj_kernel(x_ref, wab_ref, wc_ref, wq_ref, wqs_ref, wk_ref, wv_ref, gq_ref, gkv_ref, cs_ref, sn_ref,
                    hab_ref, qt_ref, k_ref, vt_ref):
    xb = x_ref[...].astype(BF16)
    hab_ref[...] = _dot(xb, wab_ref[...])
    hc = _dot(xb, wc_ref[...])
    cq = hc[:, :Q_LORA]
    ckv = hc[:, Q_LORA:Q_LORA + KV_LORA]
    kr = hc[:, Q_LORA + KV_LORA:Q_LORA + KV_LORA + C_PAD]
    krs = hc[:, Q_LORA + KV_LORA + C_PAD:]
    cqn = (cq * lax.rsqrt(jnp.mean(cq * cq, -1, keepdims=True) + 1e-6) * gq_ref[...]).astype(BF16)
    ckvn = (ckv * lax.rsqrt(jnp.mean(ckv * ckv, -1, keepdims=True) + 1e-6) * gkv_ref[...]).astype(BF16)
    cs = cs_ref[...]
    sn = sn_ref[...]
    q = _dot(cqn, wq_ref[...])
    qs = _dot(cqn, wqs_ref[...])
    k = _dot(ckvn, wk_ref[...])
    v = _dot(ckvn, wv_ref[...])
    krot = kr * cs + krs * sn
    scale = (C_NOPE + C_ROPE) ** -0.5 * math.log2(math.e)
    for h in range(C_HEADS):
        sl = slice(h * C_PAD, (h + 1) * C_PAD)
        qh = (q[:, sl] * cs + qs[:, sl] * sn) * scale
        qt_ref[sl, :] = qh.T.astype(BF16)
        k_ref[h] = (k[:, sl] + krot).astype(BF16)
    one_row = (lax.broadcasted_iota(jnp.int32, (1, C_PAD), 1) == C_V).astype(F32)
    for h in range(C_HEADS):
        vh = v[:, h * C_PAD:(h + 1) * C_PAD] + one_row
        vt_ref[h * C_VROWS:(h + 1) * C_VROWS, :] = vh.T[:C_VROWS, :].astype(BF16)


def _in_proj(x, wab, wc, wq, wqs, wk, wv, gq, gkv, cs, sn):
    T = x.shape[0]
    tm = TM_IN
    full = lambda a: pl.BlockSpec(a.shape, lambda i: (0,) * a.ndim)
    row = lambda w: pl.BlockSpec((tm, w), lambda i: (i, 0))
    colblk = lambda r: pl.BlockSpec((r, tm), lambda i: (0, i))
    return pl.pallas_call(
        _in_proj_kernel,
        grid=(T // tm,),
        in_specs=[row(D_MODEL), full(wab), full(wc), full(wq), full(wqs), full(wk), full(wv), full(gq), full(gkv),
                  row(C_PAD), row(C_PAD)],
        out_specs=[row(AB_COLS), colblk(C_HEADS * C_PAD), pl.BlockSpec((C_HEADS, tm, C_PAD), lambda i: (0, i, 0)),
                   colblk(C_HEADS * C_VROWS)],
        out_shape=[jax.ShapeDtypeStruct((T, AB_COLS), F32), jax.ShapeDtypeStruct((C_HEADS * C_PAD, T), BF16),
                   jax.ShapeDtypeStruct((C_HEADS, T, C_PAD), BF16),
                   jax.ShapeDtypeStruct((C_HEADS * C_VROWS, T), BF16)],
        compiler_params=_cparams("arbitrary"),
        name="in_proj",
    )(x, wab, wc, wq, wqs, wk, wv, gq, gkv, cs, sn)


def _banded_kernel(sink_ref, *refs, dil, n_heads, group, has_sink, want_lse):
    nq = n_heads // 2
    nkv = n_heads // group // 2
    q_refs, refs = refs[:nq], refs[nq:]
    k_refs, v_refs, kp_refs, vp_refs = (refs[i * nkv:(i + 1) * nkv] for i in range(4))
    bias_ref = refs[4 * nkv]
    o_refs = refs[4 * nkv + 1:4 * nkv + 1 + nq]
    lse_refs = refs[4 * nkv + 1 + nq:]
    c = pl.program_id(0)
    span = BLK * dil
    nb = CHUNK // span
    n_combo = nb * dil
    starts = [b * span + r for b in range(nb) for r in range(dil)]
    scale = HEAD_DIM ** -0.5

    def rows(start):
        return pl.ds(start, BLK) if dil == 1 else pl.ds(start, BLK, stride=dil)

    def gather(ref):
        return jnp.stack([ref[rows(s), :] for s in starts], axis=0)

    def with_prev(ref, prev_ref):
        cur = gather(ref)
        first = jnp.stack([prev_ref[rows(r), :] for r in range(dil)], axis=0)
        prev = jnp.concatenate([first, cur[:n_combo - dil]], axis=0) if dil < n_combo else first
        return jnp.concatenate([prev, cur], axis=1).astype(BF16)

    q = [gather(ref).astype(BF16) for ref in q_refs]
    kcat = [with_prev(k_refs[j], kp_refs[j]) for j in range(nkv)]
    vcat = [with_prev(v_refs[j], vp_refs[j]) for j in range(nkv)]

    lane = lax.broadcasted_iota(jnp.int32, (1, 1, LANES), 2)
    low = lane < HEAD_DIM
    combo = lax.broadcasted_iota(jnp.int32, (n_combo, 1, 2 * BLK), 0)
    col = lax.broadcasted_iota(jnp.int32, (n_combo, 1, 2 * BLK), 2)
    no_prev = jnp.logical_and(c == 0, jnp.logical_and(combo < dil, col < BLK))

    for j in range(nq):
        o_par, lse_par = [], []
        for par in range(2):
            h = 2 * j + par if group == 1 else group * par + j
            kvb = j if group == 1 else 0
            keep = low if par == 0 else jnp.logical_not(low)
            kk = jnp.where(keep, kcat[kvb], jnp.zeros((), BF16))
            s = _bdot_nt(q[j], kk)
            s = s * scale + bias_ref[h][None]
            s = jnp.where(no_prev, NEG_INF, s)
            m = jnp.max(s, axis=-1, keepdims=True)
            if has_sink:
                snk = sink_ref[h]
                m = jnp.maximum(m, snk)
            e = jnp.exp(s - m)
            l = jnp.sum(e, axis=-1, keepdims=True)
            if has_sink:
                l = l + jnp.exp(snk - m)
            p = (e * (1.0 / l)).astype(BF16)
            o_par.append(_bdot(p, vcat[kvb]))
            if want_lse:
                lse_par.append(m + jnp.log(l))
        o_pair = jnp.where(low, o_par[0], o_par[1])
        if want_lse:
            lse_pair = jnp.where(low, lse_par[0], lse_par[1])
        for ci, s0 in enumerate(starts):
            o_refs[j][rows(s0), :] = o_pair[ci]
            if want_lse:
                lse_refs[j][rows(s0), :] = lse_pair[ci]


def _banded(hab, sink, bias, *, q_col, k_col, v_col, dil, n_heads, group, has_sink, want_lse):
    T = hab.shape[0]
    span = BLK * dil
    nb = CHUNK // span
    nq = n_heads // 2
    nkv = n_heads // group // 2
    kernel = functools.partial(_banded_kernel, dil=dil, n_heads=n_heads, group=group,
                               has_sink=has_sink, want_lse=want_lse)
    prev = lambda cb: pl.BlockSpec((span, LANES), lambda c, s: (jnp.maximum(c * nb - 1, 0), cb))
    cur = lambda cb: pl.BlockSpec((CHUNK, LANES), lambda c, s: (c, cb))
    in_specs = ([cur(q_col + j) for j in range(nq)] + [cur(k_col + j) for j in range(nkv)]
                + [cur(v_col + j) for j in range(nkv)] + [prev(k_col + j) for j in range(nkv)]
                + [prev(v_col + j) for j in range(nkv)] + [pl.BlockSpec(bias.shape, lambda c, s: (0, 0, 0))])
    n_out = nq * (2 if want_lse else 1)
    grid_spec = pltpu.PrefetchScalarGridSpec(
        num_scalar_prefetch=1,
        grid=(T // CHUNK,),
        in_specs=in_specs,
        out_specs=[pl.BlockSpec((CHUNK, LANES), lambda c, s: (c, 0))] * n_out,
    )
    outs = pl.pallas_call(
        kernel,
        grid_spec=grid_spec,
        out_shape=[jax.ShapeDtypeStruct((T, LANES), F32)] * n_out,
        compiler_params=_cparams("arbitrary"),
        name=f"banded_d{dil}" + ("_sink" if has_sink else ""),
    )(sink, *([hab] * (nq + 4 * nkv)), bias)
    return outs[:nq], outs[nq:]


def _flash_kernel(qt_ref, k_ref, vt_ref, ot_ref, m_ref, acc_ref, s_ref):
    qi = pl.program_id(1)
    qt = qt_ref[...]
    m_ref[...] = jnp.full(m_ref.shape, NEG_INF, F32)
    acc_ref[...] = jnp.zeros(acc_ref.shape, F32)

    def scores(j):
        off = pl.multiple_of(j * TK, TK)
        return _dot(k_ref[pl.ds(off, TK), :], qt)

    def accumulate(j, s):
        off = pl.multiple_of(j * TK, TK)
        m_old = m_ref[...]
        m_new = jnp.maximum(m_old, jnp.max(s, axis=0, keepdims=True))
        alpha = jnp.exp2(m_old - m_new)
        p = jnp.exp2(s - m_new).astype(BF16)
        acc_ref[...] = alpha * acc_ref[...] + _dot(vt_ref[:, pl.ds(off, TK)], p)
        m_ref[...] = m_new

    def diagonal(s):
        key = lax.broadcasted_iota(jnp.int32, (TK, TQ), 0)
        qry = lax.broadcasted_iota(jnp.int32, (TK, TQ), 1)
        return jnp.where(key <= qry, s, NEG_INF)

    s_ref[0] = scores(0)
    n_main = qi // FLASH_UNROLL

    def main_body(i, carry):
        j = FLASH_UNROLL * i
        for u in range(FLASH_UNROLL):
            s_ref[(u + 1) % 2] = scores(j + u + 1)
            accumulate(j + u, s_ref[u % 2])
        return carry

    lax.fori_loop(0, n_main, main_body, 0)

    base = FLASH_UNROLL * n_main
    for rem in range(FLASH_UNROLL):
        @pl.when(qi - base == rem)
        def _():
            for u in range(rem):
                s_ref[(u + 1) % 2] = scores(base + u + 1)
                accumulate(base + u, s_ref[u % 2])
            accumulate(qi, diagonal(s_ref[rem % 2]))

    ot_ref[...] = (acc_ref[:C_V, :] / acc_ref[C_V:C_V + 1, :]).astype(ot_ref.dtype)


def _flash(qt, k, vt):
    H, T, _ = k.shape
    return pl.pallas_call(
        _flash_kernel,
        grid=(H, T // TQ),
        in_specs=[pl.BlockSpec((C_PAD, TQ), lambda h, i: (h, i)),
                  pl.BlockSpec((None, T, C_PAD), lambda h, i: (h, 0, 0)),
                  pl.BlockSpec((C_VROWS, T), lambda h, i: (h, 0))],
        out_specs=pl.BlockSpec((C_V, TQ), lambda h, i: (h, i)),
        out_shape=jax.ShapeDtypeStruct((H * C_V, T), BF16),
        scratch_shapes=[pltpu.VMEM((1, TQ), F32), pltpu.VMEM((C_VROWS, TQ), F32), pltpu.VMEM((2, TK, TQ), F32)],
        compiler_params=_cparams("arbitrary", "arbitrary"),
        name="mla_flash",
    )(qt, k, vt)


def _layer_norm(z, g, b):
    mu = jnp.mean(z, -1, keepdims=True)
    zc = z - mu
    var = jnp.mean(zc * zc, -1, keepdims=True)
    return zc * lax.rsqrt(var + 1e-5) * g + b


def _out_router_kernel(*refs):
    n_ab = 2 + 6 + 6
    oa_refs, ob_refs, lse_refs = refs[0:2], refs[2:8], refs[8:14]
    (oc_ref, x_ref, woa_ref, wob_ref, woc_ref, g_ref, b_ref, wr_ref, br_ref,
     x1_ref, slab_ref, cnt_ref, carry_ref, prev_ref) = refs[n_ab:]
    i = pl.program_id(0)
    tm = x_ref.shape[0]

    @pl.when(i == 0)
    def _():
        carry_ref[...] = jnp.zeros(carry_ref.shape, F32)
        prev_ref[...] = jnp.zeros(prev_ref.shape, F32)

    xr = prev_ref[...]

    y = None
    for j in range(2):
        l0, l1, l2 = (lse_refs[2 * g + j][...] for g in range(3))
        mx = jnp.maximum(jnp.maximum(l0, l1), l2)
        e0, e1, e2 = jnp.exp(l0 - mx), jnp.exp(l1 - mx), jnp.exp(l2 - mx)
        den = e0 + e1 + e2
        ob = ((e0 / den) * ob_refs[j][...] + (e1 / den) * ob_refs[2 + j][...] + (e2 / den) * ob_refs[4 + j][...])
        part = _dot(oa_refs[j][...].astype(BF16), woa_ref[j]) + _dot(ob.astype(BF16), wob_ref[j])
        y = part if y is None else y + part
    y = y + _dot(oc_ref[...].astype(F32).T.astype(BF16), woc_ref[...])
    x1 = _layer_norm(DEEPNORM_ALPHA * x_ref[...] + y, g_ref[...], b_ref[...])
    x1_ref[...] = x1
    prev_ref[...] = x1

    xr_hi = xr.astype(BF16)
    xr_lo = (xr - xr_hi.astype(F32)).astype(BF16)
    logits = _dot(jnp.concatenate([xr_hi, xr_lo, xr_hi], axis=1), wr_ref[...]) + br_ref[...]
    lane = lax.broadcasted_iota(jnp.int32, (tm, LANES), 1)
    ninf = jnp.float32(-jnp.inf)
    gmask = lane < N_GROUPS
    gl = jnp.where(gmask, logits, ninf)
    gmax = jnp.max(gl, axis=-1, keepdims=True)
    grp = jnp.min(jnp.where(gl == gmax, lane, LANES), axis=-1, keepdims=True)
    p_grp = 1.0 / jnp.sum(jnp.where(gmask, jnp.exp(logits - gmax), 0.0), axis=-1, keepdims=True)
    lo = N_GROUPS + EXPERTS_PER_GROUP * grp
    emask = jnp.logical_and(lane >= lo, lane < lo + EXPERTS_PER_GROUP)
    el = jnp.where(emask, logits, ninf)
    v0 = jnp.max(el, axis=-1, keepdims=True)
    i0 = jnp.min(jnp.where(el == v0, lane, LANES), axis=-1, keepdims=True)
    el2 = jnp.where(lane == i0, ninf, el)
    v1 = jnp.max(el2, axis=-1, keepdims=True)
    i1 = jnp.min(jnp.where(el2 == v1, lane, LANES), axis=-1, keepdims=True)
    t = jnp.exp(v1 - v0)
    gate0 = p_grp / (1.0 + t)
    gate1 = p_grp * t / (1.0 + t)

    hit0 = lane == i0
    hit1 = lane == i1
    cmat = jnp.logical_or(hit0, hit1).astype(BF16)
    tr = lax.broadcasted_iota(jnp.int32, (tm, tm), 0)
    tc = lax.broadcasted_iota(jnp.int32, (tm, tm), 1)
    tril = (tc < tr).astype(BF16)
    prefix = _dot(tril, cmat) + carry_ref[...]
    r0 = jnp.sum(jnp.where(hit0, prefix, 0.0), axis=-1, keepdims=True)
    r1 = jnp.sum(jnp.where(hit1, prefix, 0.0), axis=-1, keepdims=True)
    counted = (i > 0).astype(F32)
    carry_ref[...] = carry_ref[...] + counted * jnp.sum(cmat.astype(F32), axis=0, keepdims=True)
    cnt_ref[...] = carry_ref[...]

    fields = [(i0 - N_GROUPS).astype(F32), (i1 - N_GROUPS).astype(F32), r0, r1, gate0, gate1]
    slab = jnp.zeros((tm, LANES), F32)
    for n, f in enumerate(fields):
        slab = jnp.where(lane == n, f, slab)
    slab_ref[...] = slab


def _out_router(oa, ob, lse, oc, x, woa, wob, woc, g, b, wr, br):
    T = x.shape[0]
    tm = TM_OUT
    n = T // tm
    full = lambda a: pl.BlockSpec(a.shape, lambda i: (0,) * a.ndim)
    row = lambda w: pl.BlockSpec((tm, w), lambda i: (jnp.minimum(i, n - 1), 0))
    lag = pl.BlockSpec((tm, LANES), lambda i: (jnp.maximum(i - 1, 0), 0))
    return pl.pallas_call(
        _out_router_kernel,
        grid=(n + 1,),
        in_specs=[row(LANES)] * 14 + [pl.BlockSpec((C_HEADS * C_V, tm), lambda i: (0, jnp.minimum(i, n - 1))),
                                      row(D_MODEL),
                                      full(woa), full(wob), full(woc), full(g), full(b), full(wr), full(br)],
        out_specs=[row(D_MODEL), lag, pl.BlockSpec((1, LANES), lambda i: (0, 0))],
        out_shape=[jax.ShapeDtypeStruct((T, D_MODEL), F32), jax.ShapeDtypeStruct((T, LANES), F32),
                   jax.ShapeDtypeStruct((1, LANES), F32)],
        scratch_shapes=[pltpu.VMEM((1, LANES), F32), pltpu.VMEM((tm, D_MODEL), F32)],
        compiler_params=_cparams("arbitrary"),
        name="out_router",
    )(*oa, *ob, *lse, oc, x, woa, wob, woc, g, b, wr, br)


def _dispatch_kernel(d0_ref, d1_ref, x_ref, zeros_ref, xs_ref, sem):
    del zeros_ref
    i = pl.program_id(0)
    tm = x_ref.shape[0]
    base = i * tm

    def copy(t, d_ref):
        return pltpu.make_async_copy(x_ref.at[pl.ds(t, 1), :], xs_ref.at[pl.ds(d_ref[base + t], 1), :], sem)

    def start(t, carry):
        copy(t, d0_ref).start()
        copy(t, d1_ref).start()
        return carry

    lax.fori_loop(0, tm, start, 0, unroll=8)
    for _ in range(2):
        pltpu.make_async_copy(x_ref, xs_ref.at[pl.ds(0, tm), :], sem).wait()


def _dispatch(d0, d1, x1, n_rows):
    T = x1.shape[0]
    tm = ROW_BLK
    zeros = jnp.zeros((n_rows, D_MODEL), F32)
    grid_spec = pltpu.PrefetchScalarGridSpec(
        num_scalar_prefetch=2,
        grid=(T // tm,),
        in_specs=[pl.BlockSpec((tm, D_MODEL), lambda i, a, b: (i, 0)), pl.BlockSpec(memory_space=pl.ANY)],
        out_specs=pl.BlockSpec(memory_space=pl.ANY),
        scratch_shapes=[pltpu.SemaphoreType.DMA],
    )
    return pl.pallas_call(
        _dispatch_kernel,
        grid_spec=grid_spec,
        out_shape=jax.ShapeDtypeStruct((n_rows, D_MODEL), F32),
        input_output_aliases={3: 0},
        compiler_params=_cparams("arbitrary"),
        name="moe_dispatch",
    )(d0, d1, x1, zeros)


def _ffn_kernel(be_ref, nu_ref, xs_ref, wg_ref, wu_ref, wd_ref, ys_ref):
    i = pl.program_id(0)

    @pl.when(i < nu_ref[0])
    def _():
        xb = xs_ref[...].astype(BF16)
        a = _dot(xb, wg_ref[...].astype(BF16))
        u = _dot(xb, wu_ref[...].astype(BF16))
        hmid = (a * jax.nn.sigmoid(a) * u).astype(BF16)
        ys_ref[...] = _dot(hmid, wd_ref[...].astype(BF16))

    @pl.when(i >= nu_ref[0])
    def _():
        ys_ref[...] = jnp.zeros(ys_ref.shape, F32)


def _ffn(block_expert, n_used, xs, wg, wu, wd, layer):
    n_rows = xs.shape[0]
    grid_spec = pltpu.PrefetchScalarGridSpec(
        num_scalar_prefetch=2,
        grid=(n_rows // ROW_BLK,),
        in_specs=[pl.BlockSpec((ROW_BLK, D_MODEL), lambda i, be, nu: (i, 0)),
                  pl.BlockSpec((None, None, D_MODEL, D_EXPERT), lambda i, be, nu: (layer, be[i], 0, 0)),
                  pl.BlockSpec((None, None, D_MODEL, D_EXPERT), lambda i, be, nu: (layer, be[i], 0, 0)),
                  pl.BlockSpec((None, None, D_EXPERT, D_MODEL), lambda i, be, nu: (layer, be[i], 0, 0))],
        out_specs=pl.BlockSpec((ROW_BLK, D_MODEL), lambda i, be, nu: (i, 0)),
    )
    return pl.pallas_call(
        _ffn_kernel,
        grid_spec=grid_spec,
        out_shape=jax.ShapeDtypeStruct((n_rows, D_MODEL), F32),
        compiler_params=_cparams("arbitrary"),
        name="moe_ffn",
    )(block_expert, n_used, xs, wg, wu, wd)


def _combine_kernel(d0_ref, d1_ref, ys_ref, x1_ref, slab_ref, g_ref, b_ref, o_ref, buf_ref, sem):
    i = pl.program_id(0)
    tm = x1_ref.shape[0]
    base = i * tm

    def copy(t, d_ref, slot):
        return pltpu.make_async_copy(ys_ref.at[pl.ds(d_ref[base + t], 1), :], buf_ref.at[slot, pl.ds(t, 1), :], sem)

    def start(t, carry):
        copy(t, d0_ref, 0).start()
        copy(t, d1_ref, 1).start()
        return carry

    lax.fori_loop(0, tm, start, 0, unroll=8)
    for slot in range(2):
        pltpu.make_async_copy(ys_ref.at[pl.ds(0, tm), :], buf_ref.at[slot], sem).wait()
    slab = slab_ref[...]
    y = slab[:, 4:5] * buf_ref[0] + slab[:, 5:6] * buf_ref[1]
    o_ref[...] = _layer_norm(DEEPNORM_ALPHA * x1_ref[...] + y, g_ref[...], b_ref[...])


def _combine(d0, d1, ys, x1, slab, g, b):
    T = x1.shape[0]
    tm = ROW_BLK
    grid_spec = pltpu.PrefetchScalarGridSpec(
        num_scalar_prefetch=2,
        grid=(T // tm,),
        in_specs=[pl.BlockSpec(memory_space=pl.ANY),
                  pl.BlockSpec((tm, D_MODEL), lambda i, a, c: (i, 0)),
                  pl.BlockSpec((tm, LANES), lambda i, a, c: (i, 0)),
                  pl.BlockSpec((1, D_MODEL), lambda i, a, c: (0, 0)),
                  pl.BlockSpec((1, D_MODEL), lambda i, a, c: (0, 0))],
        out_specs=pl.BlockSpec((tm, D_MODEL), lambda i, a, c: (i, 0)),
        scratch_shapes=[pltpu.VMEM((2, tm, D_MODEL), F32), pltpu.SemaphoreType.DMA],
    )
    return pl.pallas_call(
        _combine_kernel,
        grid_spec=grid_spec,
        out_shape=jax.ShapeDtypeStruct((T, D_MODEL), F32),
        compiler_params=_cparams("arbitrary"),
        name="moe_combine",
    )(d0, d1, ys, x1, slab, g, b)


def _invert_kernel(d0_ref, d1_ref, ros_ref):
    n_tok = d0_ref.shape[0]
    plane = n_tok + ROW_BLK

    def fill(s, carry):
        odd_block = lax.shift_right_logical(s, ROW_BLK.bit_length() - 1) & 1
        spare = n_tok + (s & (ROW_BLK - 1)) + odd_block * plane
        ros_ref[s] = ((n_tok - 1) << 16) | spare
        return carry

    def put(t, carry):
        ros_ref[d0_ref[t]] = (t << 16) | t
        ros_ref[d1_ref[t]] = (t << 16) | (plane + t)
        return carry

    i = pl.program_id(0)
    half = pl.num_programs(0) // 2
    n_fill = ros_ref.shape[0] // half
    n_put = n_tok // half

    @pl.when(i < half)
    def _():
        lax.fori_loop(i * n_fill, (i + 1) * n_fill, fill, 0)

    @pl.when(i >= half)
    def _():
        lax.fori_loop((i - half) * n_put, (i - half + 1) * n_put, put, 0)


INVERT_STEPS = 64


def _invert(d0, d1, n_rows):
    assert n_rows % INVERT_STEPS == 0 and d0.shape[0] % INVERT_STEPS == 0
    smem = pl.BlockSpec(memory_space=pltpu.SMEM)
    return pl.pallas_call(
        _invert_kernel,
        grid=(2 * INVERT_STEPS,),
        in_specs=[smem, smem],
        out_specs=smem,
        out_shape=jax.ShapeDtypeStruct((n_rows,), jnp.int32),
        compiler_params=_cparams("arbitrary"),
        name="moe_invert",
    )(d0, d1)


def _ffn_fused_kernel(be_ref, ros_ref, x1_ref, wg0_ref, wu0_ref, wd0_ref, wg1_ref, wu1_ref, wd1_ref, out_ref,
                      xb0, xb1, yb0, yb1, gsem, ssem):
    del be_ref
    i = pl.program_id(0)
    last = pl.num_programs(0) - 1
    n_tok = x1_ref.shape[0]
    plane = n_tok + ROW_BLK

    def gather(blk, xbuf, sem):
        for r in range(ROW_BLK):
            tok = lax.shift_right_logical(ros_ref[blk * ROW_BLK + r], 16)
            pltpu.make_async_copy(x1_ref.at[pl.ds(tok, 1), :], xbuf.at[pl.ds(r, 1), :], sem).start()

    def scatter(blk, ybuf, sem):
        for r in range(ROW_BLK):
            row = ros_ref[blk * ROW_BLK + r] & 0xFFFF
            pltpu.make_async_copy(ybuf.at[pl.ds(r, 1), :], out_ref.at[pl.ds(row, 1), :], sem).start()

    def wait_rows(buf, sem):
        pltpu.make_async_copy(x1_ref.at[pl.ds(0, ROW_BLK), :], buf, sem).wait()

    def ffn(xbuf, ybuf, wg_ref, wu_ref, wd_ref):
        xb = xbuf[...].astype(BF16)
        a = _dot(xb, wg_ref[...].astype(BF16))
        u = _dot(xb, wu_ref[...].astype(BF16))
        hmid = (a * jax.nn.sigmoid(a) * u).astype(BF16)
        ybuf[...] = _dot(hmid, wd_ref[...].astype(BF16))

    b0 = 2 * i
    b1 = 2 * i + 1
    b2 = jnp.minimum(2 * i + 2, 2 * last + 1)

    @pl.when(i == 0)
    def _():
        yb0[...] = jnp.zeros(yb0.shape, F32)
        for base in (n_tok, plane + n_tok):
            spare = pltpu.make_async_copy(yb0, out_ref.at[pl.ds(base, ROW_BLK), :], ssem.at[0])
            spare.start()
            spare.wait()
        gather(0, xb0, gsem.at[0])

    wait_rows(xb0, gsem.at[0])
    gather(b1, xb1, gsem.at[1])

    @pl.when(i > 0)
    def _():
        wait_rows(yb0, ssem.at[0])

    ffn(xb0, yb0, wg0_ref, wu0_ref, wd0_ref)
    scatter(b0, yb0, ssem.at[0])
    wait_rows(xb1, gsem.at[1])
    gather(b2, xb0, gsem.at[0])

    @pl.when(i > 0)
    def _():
        wait_rows(yb1, ssem.at[1])

    ffn(xb1, yb1, wg1_ref, wu1_ref, wd1_ref)
    scatter(b1, yb1, ssem.at[1])

    @pl.when(i == last)
    def _():
        wait_rows(xb0, gsem.at[0])
        wait_rows(yb0, ssem.at[0])
        wait_rows(yb1, ssem.at[1])


def _ffn_fused(block_expert, ros, x1, wg, wu, wd, layer):
    T = x1.shape[0]
    n_blocks = ros.shape[0] // ROW_BLK
    assert n_blocks % 2 == 0
    w_spec = lambda shape, k: pl.BlockSpec((None, None) + shape, lambda i, be, ro: (layer, be[2 * i + k], 0, 0))
    grid_spec = pltpu.PrefetchScalarGridSpec(
        num_scalar_prefetch=2,
        grid=(n_blocks // 2,),
        in_specs=[pl.BlockSpec(memory_space=pl.ANY)]
        + [w_spec(s, k) for k in range(2) for s in ((D_MODEL, D_EXPERT), (D_MODEL, D_EXPERT), (D_EXPERT, D_MODEL))],
        out_specs=pl.BlockSpec(memory_space=pl.ANY),
        scratch_shapes=[pltpu.VMEM((ROW_BLK, D_MODEL), F32)] * 4 + [pltpu.SemaphoreType.DMA((2,))] * 2,
    )
    return pl.pallas_call(
        _ffn_fused_kernel,
        grid_spec=grid_spec,
        out_shape=jax.ShapeDtypeStruct((2 * (T + ROW_BLK), D_MODEL), F32),
        compiler_params=_cparams("arbitrary"),
        name="moe_ffn",
    )(block_expert, ros, x1, wg, wu, wd, wg, wu, wd)


def _combine2_kernel(y0_ref, y1_ref, x1_ref, slab_ref, g_ref, b_ref, o_ref):
    slab = slab_ref[...]
    y = slab[:, 4:5] * y0_ref[...] + slab[:, 5:6] * y1_ref[...]
    o_ref[...] = _layer_norm(DEEPNORM_ALPHA * x1_ref[...] + y, g_ref[...], b_ref[...])


def _combine2(out2, x1, slab, g, b):
    T = x1.shape[0]
    tm = ROW_BLK
    plane_blocks = (T + ROW_BLK) // tm
    row = lambda w: pl.BlockSpec((tm, w), lambda i: (i, 0))
    one = pl.BlockSpec((1, D_MODEL), lambda i: (0, 0))
    return pl.pallas_call(
        _combine2_kernel,
        grid=(T // tm,),
        in_specs=[row(D_MODEL), pl.BlockSpec((tm, D_MODEL), lambda i: (plane_blocks + i, 0)), row(D_MODEL), row(LANES),
                  one, one],
        out_specs=row(D_MODEL),
        out_shape=jax.ShapeDtypeStruct((T, D_MODEL), F32),
        compiler_params=_cparams("arbitrary"),
        name="moe_combine",
    )(out2, out2, x1, slab, g, b)


def _t5_bucket(dist):
    max_exact = N_BUCKETS // 2
    large = max_exact + (jnp.log(jnp.maximum(dist, 1).astype(F32) / max_exact)
                         / math.log(MAX_DISTANCE / max_exact) * (N_BUCKETS - max_exact)).astype(jnp.int32)
    return jnp.where(dist < max_exact, dist, jnp.minimum(large, N_BUCKETS - 1))


def _bias_table(table_cols, dil, max_dist):
    dist = BLK + jnp.arange(BLK)[:, None] - jnp.arange(2 * BLK)[None, :]
    bucket = _t5_bucket(jnp.maximum(dist * dil, 0))
    hit = bucket[None, :, :, None] == jnp.arange(N_BUCKETS)[None, None, None, :]
    bias = jnp.sum(jnp.where(hit, table_cols.T.astype(F32)[:, None, None, :], 0.0), axis=-1)
    mask = (dist >= 0) & (dist <= max_dist)
    return jnp.where(mask[None], bias, NEG_INF)


def _rope_tables(T):
    inv_freq = ROPE_THETA ** (-jnp.arange(0, C_ROPE, 2, dtype=F32) / C_ROPE)
    ang = jnp.arange(T, dtype=F32)[:, None] * inv_freq[None, :]
    cos, sin = jnp.cos(ang), jnp.sin(ang)
    ones = jnp.ones((T, C_NOPE), F32)
    zeros_n = jnp.zeros((T, C_NOPE), F32)
    zeros_p = jnp.zeros((T, C_PAD - C_NOPE - C_ROPE), F32)
    cs = jnp.concatenate([ones, cos, cos, zeros_p], axis=1)
    sn = jnp.concatenate([zeros_n, -sin, sin, zeros_p], axis=1)
    return cs, sn


def _swap_halves(w):
    half = w.shape[-1] // 2
    return jnp.concatenate([w[..., half:], w[..., :half]], axis=-1)


def _pad_cols(w, width):
    return jnp.pad(w, ((0, 0), (0, width - w.shape[1])))


def _prep_layer_weights(w_in, w_out, w_uq, w_ukv, w_rg, b_rg, w_re, b_re):
    head_cols = lambda w, h: w[:, h * HEAD_DIM:(h + 1) * HEAD_DIM]
    wab = jnp.concatenate([head_cols(w_in, h) for h in A_HEAD_ORDER] + [w_in[:, A_Q_HEADS * HEAD_DIM:AB_COLS]],
                          axis=1).astype(BF16)
    c0 = AB_COLS
    w_cq = w_in[:, c0:c0 + Q_LORA]
    w_ckv = w_in[:, c0 + Q_LORA:c0 + Q_LORA + KV_LORA]
    w_kr = w_in[:, c0 + Q_LORA + KV_LORA:]
    lead = jnp.zeros((D_MODEL, C_NOPE), F32)
    wc = jnp.concatenate([w_cq, w_ckv,
                          _pad_cols(jnp.concatenate([lead, w_kr], 1), C_PAD),
                          _pad_cols(jnp.concatenate([lead, _swap_halves(w_kr)], 1), C_PAD)], axis=1).astype(BF16)
    uq = w_uq.reshape(Q_LORA, C_HEADS, C_NOPE + C_ROPE)
    wq = jnp.pad(uq, ((0, 0), (0, 0), (0, C_PAD - C_NOPE - C_ROPE))).reshape(Q_LORA, C_HEADS * C_PAD).astype(BF16)
    uqs = jnp.concatenate([jnp.zeros((Q_LORA, C_HEADS, C_NOPE), F32), _swap_halves(uq[..., C_NOPE:])], axis=-1)
    wqs = jnp.pad(uqs, ((0, 0), (0, 0), (0, C_PAD - C_NOPE - C_ROPE))).reshape(Q_LORA, C_HEADS * C_PAD).astype(BF16)
    ukv = w_ukv.reshape(KV_LORA, C_HEADS, C_NOPE + C_V)
    wk = jnp.pad(ukv[..., :C_NOPE], ((0, 0), (0, 0), (0, C_PAD - C_NOPE))).reshape(KV_LORA, C_HEADS * C_PAD)
    wv = jnp.pad(ukv[..., C_NOPE:], ((0, 0), (0, 0), (0, C_PAD - C_V))).reshape(KV_LORA, C_HEADS * C_PAD)
    woa = jnp.concatenate([w_out[h * HEAD_DIM:(h + 1) * HEAD_DIM] for h in A_HEAD_ORDER],
                          axis=0).reshape(2, LANES, D_MODEL).astype(BF16)
    wob = w_out[256:512].reshape(2, LANES, D_MODEL).astype(BF16)
    woc = w_out[512:].astype(BF16)
    wr = _pad_cols(jnp.concatenate([w_rg, w_re.transpose(1, 0, 2).reshape(D_MODEL, N_EXPERTS)], axis=1), LANES)
    wr_hi = wr.astype(BF16)
    wr_lo = (wr - wr_hi.astype(F32)).astype(BF16)
    wr = jnp.concatenate([wr_hi, wr_hi, wr_lo], axis=0)
    br = _pad_cols(jnp.concatenate([b_rg, b_re.reshape(N_EXPERTS)])[None, :], LANES)
    return dict(wab=wab, wc=wc, wq=wq, wqs=wqs, wk=wk.astype(BF16), wv=wv.astype(BF16),
                woa=woa, wob=wob, woc=woc, wr=wr, br=br)


def kernel(x, w_in, w_out, sinks, rel_bias, mla_q_norm, mla_kv_norm, w_uq, w_ukv, ln1_g, ln1_b,
           w_route_group, b_route_group, w_route_expert, b_route_expert,
           w_expert_gate, w_expert_up, w_expert_down, ln2_g, ln2_b):
    Bsz, T, D = x.shape
    assert Bsz == 1 and D == D_MODEL and T % CHUNK == 0
    xt = x.reshape(T, D)
    cs, sn = _rope_tables(T)
    bias_a = _bias_table(rel_bias[:, :A_Q_HEADS], 1, SWA_WINDOW - 1)
    bias_b = [_bias_table(rel_bias[:, A_Q_HEADS + g * B_HEADS_PER_GROUP:A_Q_HEADS + (g + 1) * B_HEADS_PER_GROUP],
                          dil, window // dil) for g, (window, dil) in enumerate(DILATED_PAIRS)]
    n_assign = T * TOP_K
    n_blocks = n_assign // ROW_BLK + N_EXPERTS
    n_rows = n_blocks * ROW_BLK
    no_sink = jnp.zeros((B_HEADS_PER_GROUP,), F32)

    for layer in range(w_in.shape[0]):
        w = _prep_layer_weights(w_in[layer], w_out[layer], w_uq[layer], w_ukv[layer],
                                w_route_group[layer], b_route_group[layer],
                                w_route_expert[layer], b_route_expert[layer])
        hab, qt, k, vt = _in_proj(xt, w["wab"], w["wc"], w["wq"], w["wqs"], w["wk"], w["wv"],
                                mla_q_norm[layer][None, :], mla_kv_norm[layer][None, :], cs, sn)
        oa, _ = _banded(hab, sinks[layer], bias_a, q_col=0, k_col=2, v_col=3, dil=1,
                        n_heads=A_Q_HEADS, group=A_Q_HEADS // A_KV_HEADS, has_sink=True, want_lse=False)
        ob, lse = [], []
        for g, (_, dil) in enumerate(DILATED_PAIRS):
            o_g, lse_g = _banded(hab, no_sink, bias_b[g], q_col=4 + 2 * g, k_col=10 + 2 * g, v_col=16 + 2 * g,
                                 dil=dil, n_heads=B_HEADS_PER_GROUP, group=1, has_sink=False, want_lse=True)
            ob += o_g
            lse += lse_g
        oc = _flash(qt, k, vt)
        x1, slab, cnt = _out_router(oa, ob, lse, oc, xt, w["woa"], w["wob"], w["woc"],
                                    ln1_g[layer][None, :], ln1_b[layer][None, :], w["wr"], w["br"])

        counts = cnt[0, N_GROUPS:N_GROUPS + N_EXPERTS].astype(jnp.int32)
        padded = ((counts + ROW_BLK - 1) // ROW_BLK) * ROW_BLK
        pad_end = jnp.cumsum(padded)
        pad_start = pad_end - padded
        eids = jnp.arange(N_EXPERTS, dtype=jnp.int32)
        pick = lambda e: jnp.sum(jnp.where(e.astype(jnp.int32)[:, None] == eids[None, :], pad_start[None, :], 0), axis=1)
        d0 = pick(slab[:, 0]) + slab[:, 2].astype(jnp.int32)
        d1 = pick(slab[:, 1]) + slab[:, 3].astype(jnp.int32)
        block_start = jnp.arange(n_blocks, dtype=jnp.int32) * ROW_BLK
        block_expert = jnp.minimum(jnp.sum((pad_end[None, :] <= block_start[:, None]).astype(jnp.int32), axis=1),
                                   N_EXPERTS - 1)

        n_used = (pad_end[-1:] // ROW_BLK).astype(jnp.int32)
        xs = _dispatch(d0, d1, x1, n_rows)
        ys = _ffn(block_expert, n_used, xs, w_expert_gate, w_expert_up, w_expert_down, layer)
        xt = _combine(d0, d1, ys, x1, slab, ln2_g[layer][None, :], ln2_b[layer][None, :])
    return xt.reshape(Bsz, T, D)
```

```python
import functools
import math

import jax
import jax.numpy as jnp
from jax import lax
from jax.experimental import pallas as pl
from jax.experimental.pallas import tpu as pltpu

F32 = jnp.float32
BF16 = jnp.bfloat16

D_MODEL = 1024
DEPTH = 4
HEAD_DIM = 64
BLK = 128
NEG_INF = -1e30

SWA_WINDOW = 128
A_Q_HEADS = 4
A_KV_HEADS = 2
A_COLS = (A_Q_HEADS + 2 * A_KV_HEADS) * HEAD_DIM
A_HEAD_ORDER = (0, 2, 1, 3)

DILATED_PAIRS = ((128, 1), (512, 4), (2048, 16))
B_HEADS_PER_GROUP = 4
B_HEADS = len(DILATED_PAIRS) * B_HEADS_PER_GROUP
B_COLS = 3 * B_HEADS * HEAD_DIM
AB_COLS = A_COLS + B_COLS

C_HEADS = 8
C_NOPE = 64
C_ROPE = 32
C_V = 64
Q_LORA = 256
KV_LORA = 128
ROPE_THETA = 10000.0
C_PAD = 128
C_VROWS = 80

N_BUCKETS = 32
MAX_DISTANCE = 2048

N_GROUPS = 4
EXPERTS_PER_GROUP = 8
N_EXPERTS = N_GROUPS * EXPERTS_PER_GROUP
TOP_K = 2
D_EXPERT = 256
ROW_BLK = 256

DEEPNORM_ALPHA = (2 * DEPTH) ** 0.25

VMEM_LIMIT = 56 * 1024 * 1024

CHUNK = 2048
TM_IN = 256
TM_OUT = 256
TQ = 512
TK = 512
FLASH_UNROLL = 4
LANES = 128


def _dot(a, b):
    return jnp.dot(a, b, preferred_element_type=F32)


def _dot_nt(a, b):
    return lax.dot_general(a, b, (((1,), (1,)), ((), ())), preferred_element_type=F32)


def _bdot(a, b):
    return lax.dot_general(a, b, (((2,), (1,)), ((0,), (0,))), preferred_element_type=F32)


def _bdot_nt(a, b):
    return lax.dot_general(a, b, (((2,), (2,)), ((0,), (0,))), preferred_element_type=F32)


def _cparams(*sem):
    return pltpu.CompilerParams(dimension_semantics=sem, vmem_limit_bytes=VMEM_LIMIT)


def _in_proj_kernel(x_ref, wab_ref, wc_ref, wq_ref, wqs_ref, wk_ref, wv_ref, gq_ref, gkv_ref, cs_ref, sn_ref,
                    hab_ref, qt_ref, k_ref, vt_ref):
    xb = x_ref[...].astype(BF16)
    hab_ref[...] = _dot(xb, wab_ref[...])
    hc = _dot(xb, wc_ref[...])
    cq = hc[:, :Q_LORA]
    ckv = hc[:, Q_LORA:Q_LORA + KV_LORA]
    kr = hc[:, Q_LORA + KV_LORA:Q_LORA + KV_LORA + C_PAD]
    krs = hc[:, Q_LORA + KV_LORA + C_PAD:]
    cqn = (cq * lax.rsqrt(jnp.mean(cq * cq, -1, keepdims=True) + 1e-6) * gq_ref[...]).astype(BF16)
    ckvn = (ckv * lax.rsqrt(jnp.mean(ckv * ckv, -1, keepdims=True) + 1e-6) * gkv_ref[...]).astype(BF16)
    cs = cs_ref[...]
    sn = sn_ref[...]
    q = _dot(cqn, wq_ref[...])
    qs = _dot(cqn, wqs_ref[...])
    k = _dot(ckvn, wk_ref[...])
    v = _dot(ckvn, wv_ref[...])
    krot = kr * cs + krs * sn
    scale = (C_NOPE + C_ROPE) ** -0.5 * math.log2(math.e)
    for h in range(C_HEADS):
        sl = slice(h * C_PAD, (h + 1) * C_PAD)
        qh = (q[:, sl] * cs + qs[:, sl] * sn) * scale
        qt_ref[sl, :] = qh.T.astype(BF16)
        k_ref[h] = (k[:, sl] + krot).astype(BF16)
    one_row = (lax.broadcasted_iota(jnp.int32, (1, C_PAD), 1) == C_V).astype(F32)
    for h in range(C_HEADS):
        vh = v[:, h * C_PAD:(h + 1) * C_PAD] + one_row
        vt_ref[h * C_VROWS:(h + 1) * C_VROWS, :] = vh.T[:C_VROWS, :].astype(BF16)


def _in_proj(x, wab, wc, wq, wqs, wk, wv, gq, gkv, cs, sn):
    T = x.shape[0]
    tm = TM_IN
    full = lambda a: pl.BlockSpec(a.shape, lambda i: (0,) * a.ndim)
    row = lambda w: pl.BlockSpec((tm, w), lambda i: (i, 0))
    colblk = lambda r: pl.BlockSpec((r, tm), lambda i: (0, i))
    return pl.pallas_call(
        _in_proj_kernel,
        grid=(T // tm,),
        in_specs=[row(D_MODEL), full(wab), full(wc), full(wq), full(wqs), full(wk), full(wv), full(gq), full(gkv),
                  row(C_PAD), row(C_PAD)],
        out_specs=[row(AB_COLS), colblk(C_HEADS * C_PAD), pl.BlockSpec((C_HEADS, tm, C_PAD), lambda i: (0, i, 0)),
                   colblk(C_HEADS * C_VROWS)],
        out_shape=[jax.ShapeDtypeStruct((T, AB_COLS), F32), jax.ShapeDtypeStruct((C_HEADS * C_PAD, T), BF16),
                   jax.ShapeDtypeStruct((C_HEADS, T, C_PAD), BF16),
                   jax.ShapeDtypeStruct((C_HEADS * C_VROWS, T), BF16)],
        compiler_params=_cparams("arbitrary"),
        name="in_proj",
    )(x, wab, wc, wq, wqs, wk, wv, gq, gkv, cs, sn)


def _banded_kernel(sink_ref, *refs, dil, n_heads, group, has_sink, want_lse):
    nq = n_heads // 2
    nkv = n_heads // group // 2
    q_refs, refs = refs[:nq], refs[nq:]
    k_refs, v_refs, kp_refs, vp_refs = (refs[i * nkv:(i + 1) * nkv] for i in range(4))
    bias_ref = refs[4 * nkv]
    o_refs = refs[4 * nkv + 1:4 * nkv + 1 + nq]
    lse_refs = refs[4 * nkv + 1 + nq:]
    c = pl.program_id(0)
    span = BLK * dil
    nb = CHUNK // span
    n_combo = nb * dil
    starts = [b * span + r for b in range(nb) for r in range(dil)]
    scale = HEAD_DIM ** -0.5

    def rows(start):
        return pl.ds(start, BLK) if dil == 1 else pl.ds(start, BLK, stride=dil)

    def gather(ref):
        return jnp.stack([ref[rows(s), :] for s in starts], axis=0)

    def with_prev(ref, prev_ref):
        cur = gather(ref)
        first = jnp.stack([prev_ref[rows(r), :] for r in range(dil)], axis=0)
        prev = jnp.concatenate([first, cur[:n_combo - dil]], axis=0) if dil < n_combo else first
        return jnp.concatenate([prev, cur], axis=1).astype(BF16)

    q = [gather(ref).astype(BF16) for ref in q_refs]
    kcat = [with_prev(k_refs[j], kp_refs[j]) for j in range(nkv)]
    vcat = [with_prev(v_refs[j], vp_refs[j]) for j in range(nkv)]

    lane = lax.broadcasted_iota(jnp.int32, (1, 1, LANES), 2)
    low = lane < HEAD_DIM
    combo = lax.broadcasted_iota(jnp.int32, (n_combo, 1, 2 * BLK), 0)
    col = lax.broadcasted_iota(jnp.int32, (n_combo, 1, 2 * BLK), 2)
    no_prev = jnp.logical_and(c == 0, jnp.logical_and(combo < dil, col < BLK))

    for j in range(nq):
        o_par, lse_par = [], []
        for par in range(2):
            h = 2 * j + par if group == 1 else group * par + j
            kvb = j if group == 1 else 0
            keep = low if par == 0 else jnp.logical_not(low)
            kk = jnp.where(keep, kcat[kvb], jnp.zeros((), BF16))
            s = _bdot_nt(q[j], kk)
            s = s * scale + bias_ref[h][None]
            s = jnp.where(no_prev, NEG_INF, s)
            m = jnp.max(s, axis=-1, keepdims=True)
            if has_sink:
                snk = sink_ref[h]
                m = jnp.maximum(m, snk)
            e = jnp.exp(s - m)
            l = jnp.sum(e, axis=-1, keepdims=True)
            if has_sink:
                l = l + jnp.exp(snk - m)
            p = (e * (1.0 / l)).astype(BF16)
            o_par.append(_bdot(p, vcat[kvb]))
            if want_lse:
                lse_par.append(m + jnp.log(l))
        o_pair = jnp.where(low, o_par[0], o_par[1])
        if want_lse:
            lse_pair = jnp.where(low, lse_par[0], lse_par[1])
        for ci, s0 in enumerate(starts):
            o_refs[j][rows(s0), :] = o_pair[ci]
            if want_lse:
                lse_refs[j][rows(s0), :] = lse_pair[ci]


def _banded(hab, sink, bias, *, q_col, k_col, v_col, dil, n_heads, group, has_sink, want_lse):
    T = hab.shape[0]
    span = BLK * dil
    nb = CHUNK // span
    nq = n_heads // 2
    nkv = n_heads // group // 2
    kernel = functools.partial(_banded_kernel, dil=dil, n_heads=n_heads, group=group,
                               has_sink=has_sink, want_lse=want_lse)
    prev = lambda cb: pl.BlockSpec((span, LANES), lambda c, s: (jnp.maximum(c * nb - 1, 0), cb))
    cur = lambda cb: pl.BlockSpec((CHUNK, LANES), lambda c, s: (c, cb))
    in_specs = ([cur(q_col + j) for j in range(nq)] + [cur(k_col + j) for j in range(nkv)]
                + [cur(v_col + j) for j in range(nkv)] + [prev(k_col + j) for j in range(nkv)]
                + [prev(v_col + j) for j in range(nkv)] + [pl.BlockSpec(bias.shape, lambda c, s: (0, 0, 0))])
    n_out = nq * (2 if want_lse else 1)
    grid_spec = pltpu.PrefetchScalarGridSpec(
        num_scalar_prefetch=1,
        grid=(T // CHUNK,),
        in_specs=in_specs,
        out_specs=[pl.BlockSpec((CHUNK, LANES), lambda c, s: (c, 0))] * n_out,
    )
    outs = pl.pallas_call(
        kernel,
        grid_spec=grid_spec,
        out_shape=[jax.ShapeDtypeStruct((T, LANES), F32)] * n_out,
        compiler_params=_cparams("arbitrary"),
        name=f"banded_d{dil}" + ("_sink" if has_sink else ""),
    )(sink, *([hab] * (nq + 4 * nkv)), bias)
    return outs[:nq], outs[nq:]


def _flash_kernel(qt_ref, k_ref, vt_ref, ot_ref, m_ref, acc_ref, s_ref):
    qi = pl.program_id(1)
    qt = qt_ref[...]
    m_ref[...] = jnp.full(m_ref.shape, NEG_INF, F32)
    acc_ref[...] = jnp.zeros(acc_ref.shape, F32)

    def scores(j):
        off = pl.multiple_of(j * TK, TK)
        return _dot(k_ref[pl.ds(off, TK), :], qt)

    def accumulate(j, s):
        off = pl.multiple_of(j * TK, TK)
        m_old = m_ref[...]
        m_new = jnp.maximum(m_old, jnp.max(s, axis=0, keepdims=True))
        alpha = jnp.exp2(m_old - m_new)
        p = jnp.exp2(s - m_new).astype(BF16)
        acc_ref[...] = alpha * acc_ref[...] + _dot(vt_ref[:, pl.ds(off, TK)], p)
        m_ref[...] = m_new

    def diagonal(s):
        key = lax.broadcasted_iota(jnp.int32, (TK, TQ), 0)
        qry = lax.broadcasted_iota(jnp.int32, (TK, TQ), 1)
        return jnp.where(key <= qry, s, NEG_INF)

    s_ref[0] = scores(0)
    n_main = qi // FLASH_UNROLL

    def main_body(i, carry):
        j = FLASH_UNROLL * i
        for u in range(FLASH_UNROLL):
            s_ref[(u + 1) % 2] = scores(j + u + 1)
            accumulate(j + u, s_ref[u % 2])
        return carry

    lax.fori_loop(0, n_main, main_body, 0)

    base = FLASH_UNROLL * n_main
    for rem in range(FLASH_UNROLL):
        @pl.when(qi - base == rem)
        def _():
            for u in range(rem):
                s_ref[(u + 1) % 2] = scores(base + u + 1)
                accumulate(base + u, s_ref[u % 2])
            accumulate(qi, diagonal(s_ref[rem % 2]))

    ot_ref[...] = (acc_ref[:C_V, :] / acc_ref[C_V:C_V + 1, :]).astype(ot_ref.dtype)


def _flash(qt, k, vt):
    H, T, _ = k.shape
    return pl.pallas_call(
        _flash_kernel,
        grid=(H, T // TQ),
        in_specs=[pl.BlockSpec((C_PAD, TQ), lambda h, i: (h, i)),
                  pl.BlockSpec((None, T, C_PAD), lambda h, i: (h, 0, 0)),
                  pl.BlockSpec((C_VROWS, T), lambda h, i: (h, 0))],
        out_specs=pl.BlockSpec((C_V, TQ), lambda h, i: (h, i)),
        out_shape=jax.ShapeDtypeStruct((H * C_V, T), BF16),
        scratch_shapes=[pltpu.VMEM((1, TQ), F32), pltpu.VMEM((C_VROWS, TQ), F32), pltpu.VMEM((2, TK, TQ), F32)],
        compiler_params=_cparams("arbitrary", "arbitrary"),
        name="mla_flash",
    )(qt, k, vt)


def _layer_norm(z, g, b):
    mu = jnp.mean(z, -1, keepdims=True)
    zc = z - mu
    var = jnp.mean(zc * zc, -1, keepdims=True)
    return zc * lax.rsqrt(var + 1e-5) * g + b


def _out_router_kernel(*refs):
    n_ab = 2 + 6 + 6
    oa_refs, ob_refs, lse_refs = refs[0:2], refs[2:8], refs[8:14]
    (oc_ref, x_ref, woa_ref, wob_ref, woc_ref, g_ref, b_ref, wr_ref, br_ref,
     x1_ref, slab_ref, cnt_ref, carry_ref, prev_ref) = refs[n_ab:]
    i = pl.program_id(0)
    tm = x_ref.shape[0]

    @pl.when(i == 0)
    def _():
        carry_ref[...] = jnp.zeros(carry_ref.shape, F32)
        prev_ref[...] = jnp.zeros(prev_ref.shape, F32)

    xr = prev_ref[...]

    y = None
    for j in range(2):
        l0, l1, l2 = (lse_refs[2 * g + j][...] for g in range(3))
        mx = jnp.maximum(jnp.maximum(l0, l1), l2)
        e0, e1, e2 = jnp.exp(l0 - mx), jnp.exp(l1 - mx), jnp.exp(l2 - mx)
        den = e0 + e1 + e2
        ob = ((e0 / den) * ob_refs[j][...] + (e1 / den) * ob_refs[2 + j][...] + (e2 / den) * ob_refs[4 + j][...])
        part = _dot(oa_refs[j][...].astype(BF16), woa_ref[j]) + _dot(ob.astype(BF16), wob_ref[j])
        y = part if y is None else y + part
    y = y + _dot(oc_ref[...].astype(F32).T.astype(BF16), woc_ref[...])
    x1 = _layer_norm(DEEPNORM_ALPHA * x_ref[...] + y, g_ref[...], b_ref[...])
    x1_ref[...] = x1
    prev_ref[...] = x1

    xr_hi = xr.astype(BF16)
    xr_lo = (xr - xr_hi.astype(F32)).astype(BF16)
    logits = _dot(jnp.concatenate([xr_hi, xr_lo, xr_hi], axis=1), wr_ref[...]) + br_ref[...]
    lane = lax.broadcasted_iota(jnp.int32, (tm, LANES), 1)
    ninf = jnp.float32(-jnp.inf)
    gmask = lane < N_GROUPS
    gl = jnp.where(gmask, logits, ninf)
    gmax = jnp.max(gl, axis=-1, keepdims=True)
    grp = jnp.min(jnp.where(gl == gmax, lane, LANES), axis=-1, keepdims=True)
    p_grp = 1.0 / jnp.sum(jnp.where(gmask, jnp.exp(logits - gmax), 0.0), axis=-1, keepdims=True)
    lo = N_GROUPS + EXPERTS_PER_GROUP * grp
    emask = jnp.logical_and(lane >= lo, lane < lo + EXPERTS_PER_GROUP)
    el = jnp.where(emask, logits, ninf)
    v0 = jnp.max(el, axis=-1, keepdims=True)
    i0 = jnp.min(jnp.where(el == v0, lane, LANES), axis=-1, keepdims=True)
    el2 = jnp.where(lane == i0, ninf, el)
    v1 = jnp.max(el2, axis=-1, keepdims=True)
    i1 = jnp.min(jnp.where(el2 == v1, lane, LANES), axis=-1, keepdims=True)
    t = jnp.exp(v1 - v0)
    gate0 = p_grp / (1.0 + t)
    gate1 = p_grp * t / (1.0 + t)

    hit0 = lane == i0
    hit1 = lane == i1
    cmat = jnp.logical_or(hit0, hit1).astype(BF16)
    tr = lax.broadcasted_iota(jnp.int32, (tm, tm), 0)
    tc = lax.broadcasted_iota(jnp.int32, (tm, tm), 1)
    tril = (tc < tr).astype(BF16)
    prefix = _dot(tril, cmat) + carry_ref[...]
    r0 = jnp.sum(jnp.where(hit0, prefix, 0.0), axis=-1, keepdims=True)
    r1 = jnp.sum(jnp.where(hit1, prefix, 0.0), axis=-1, keepdims=True)
    counted = (i > 0).astype(F32)
    carry_ref[...] = carry_ref[...] + counted * jnp.sum(cmat.astype(F32), axis=0, keepdims=True)
    cnt_ref[...] = carry_ref[...]

    fields = [(i0 - N_GROUPS).astype(F32), (i1 - N_GROUPS).astype(F32), r0, r1, gate0, gate1]
    slab = jnp.zeros((tm, LANES), F32)
    for n, f in enumerate(fields):
        slab = jnp.where(lane == n, f, slab)
    slab_ref[...] = slab


def _out_router(oa, ob, lse, oc, x, woa, wob, woc, g, b, wr, br):
    T = x.shape[0]
    tm = TM_OUT
    n = T // tm
    full = lambda a: pl.BlockSpec(a.shape, lambda i: (0,) * a.ndim)
    row = lambda w: pl.BlockSpec((tm, w), lambda i: (jnp.minimum(i, n - 1), 0))
    lag = pl.BlockSpec((tm, LANES), lambda i: (jnp.maximum(i - 1, 0), 0))
    return pl.pallas_call(
        _out_router_kernel,
        grid=(n + 1,),
        in_specs=[row(LANES)] * 14 + [pl.BlockSpec((C_HEADS * C_V, tm), lambda i: (0, jnp.minimum(i, n - 1))),
                                      row(D_MODEL),
                                      full(woa), full(wob), full(woc), full(g), full(b), full(wr), full(br)],
        out_specs=[row(D_MODEL), lag, pl.BlockSpec((1, LANES), lambda i: (0, 0))],
        out_shape=[jax.ShapeDtypeStruct((T, D_MODEL), F32), jax.ShapeDtypeStruct((T, LANES), F32),
                   jax.ShapeDtypeStruct((1, LANES), F32)],
        scratch_shapes=[pltpu.VMEM((1, LANES), F32), pltpu.VMEM((tm, D_MODEL), F32)],
        compiler_params=_cparams("arbitrary"),
        name="out_router",
    )(*oa, *ob, *lse, oc, x, woa, wob, woc, g, b, wr, br)


SUBLANES = 8
PAD_BITS = tuple(1 << k for k in reversed(range(SUBLANES.bit_length() - 1, ROW_BLK.bit_length() - 1)))


def _dispatch_kernel(d0_ref, d1_ref, pad_at_ref, pad_len_ref, nu_ref, x_ref, xs_ref, zero_ref, sem, zsem):
    i = pl.program_id(0)
    tm = x_ref.shape[0]
    base = i * tm

    @pl.when(i == 0)
    def _():
        zero_ref[...] = jnp.zeros(zero_ref.shape, F32)

        def pad_copies(e):
            n = pad_len_ref[e]
            at = pad_at_ref[e]
            head = n & (SUBLANES - 1)
            out = [(j < head, pltpu.make_async_copy(zero_ref.at[pl.ds(0, 1), :], xs_ref.at[pl.ds(at + j, 1), :], zsem))
                   for j in range(SUBLANES - 1)]
            body = n - head
            for bit in PAD_BITS:
                start = pl.multiple_of(at + head + (body & ~(2 * bit - 1)), SUBLANES)
                out.append(((body & bit) != 0,
                            pltpu.make_async_copy(zero_ref.at[pl.ds(0, bit), :], xs_ref.at[pl.ds(start, bit), :], zsem)))
            return out

        def spare_copies():
            n_blocks = xs_ref.shape[0] // ROW_BLK
            first = d0_ref.shape[0] * TOP_K // ROW_BLK
            return [(b >= nu_ref[0], pltpu.make_async_copy(zero_ref, xs_ref.at[pl.ds(b * ROW_BLK, ROW_BLK), :], zsem))
                    for b in range(first, n_blocks)]

        for e in range(N_EXPERTS):
            for on, cp in pad_copies(e):
                pl.when(on)(cp.start)
        for on, cp in spare_copies():
            pl.when(on)(cp.start)
        for e in range(N_EXPERTS):
            for on, cp in pad_copies(e):
                pl.when(on)(cp.wait)
        for on, cp in spare_copies():
            pl.when(on)(cp.wait)

    def copy(t, d_ref):
        return pltpu.make_async_copy(x_ref.at[pl.ds(t, 1), :], xs_ref.at[pl.ds(d_ref[base + t], 1), :], sem)

    def start(t, carry):
        copy(t, d0_ref).start()
        copy(t, d1_ref).start()
        return carry

    lax.fori_loop(0, tm, start, 0, unroll=8)
    for _ in range(2):
        pltpu.make_async_copy(x_ref, xs_ref.at[pl.ds(0, tm), :], sem).wait()


def _dispatch(d0, d1, pad_at, pad_len, n_used, x1, n_rows):
    T = x1.shape[0]
    tm = ROW_BLK
    grid_spec = pltpu.PrefetchScalarGridSpec(
        num_scalar_prefetch=5,
        grid=(T // tm,),
        in_specs=[pl.BlockSpec((tm, D_MODEL), lambda i, *_: (i, 0))],
        out_specs=pl.BlockSpec(memory_space=pl.ANY),
        scratch_shapes=[pltpu.VMEM((ROW_BLK, D_MODEL), F32), pltpu.SemaphoreType.DMA, pltpu.SemaphoreType.DMA],
    )
    return pl.pallas_call(
        _dispatch_kernel,
        grid_spec=grid_spec,
        out_shape=jax.ShapeDtypeStruct((n_rows, D_MODEL), F32),
        compiler_params=_cparams("arbitrary"),
        name="moe_dispatch",
    )(d0, d1, pad_at, pad_len, n_used, x1)


FFN_BLOCKS = 2


def _ffn_kernel(be_ref, nu_ref, xs_ref, *refs):
    del be_ref
    w_refs, ys_ref = refs[:-1], refs[-1]
    i = pl.program_id(0)
    for k in range(FFN_BLOCKS):
        wg_ref, wu_ref, wd_ref = w_refs[3 * k:3 * k + 3]
        rows = pl.ds(k * ROW_BLK, ROW_BLK)
        used = FFN_BLOCKS * i + k < nu_ref[0]

        @pl.when(used)
        def _():
            xb = xs_ref[rows, :].astype(BF16)
            a = _dot(xb, wg_ref[...].astype(BF16))
            u = _dot(xb, wu_ref[...].astype(BF16))
            hmid = (a * jax.nn.sigmoid(a) * u).astype(BF16)
            ys_ref[rows, :] = _dot(hmid, wd_ref[...].astype(BF16))

        @pl.when(jnp.logical_not(used))
        def _():
            ys_ref[rows, :] = jnp.zeros((ROW_BLK, D_MODEL), F32)


def _ffn(block_expert, n_used, xs, wg, wu, wd, layer):
    n_rows = xs.shape[0]
    step_rows = FFN_BLOCKS * ROW_BLK
    assert n_rows % step_rows == 0
    w_spec = lambda shape, k: pl.BlockSpec((None, None) + shape,
                                           lambda i, be, nu: (layer, be[FFN_BLOCKS * i + k], 0, 0))
    w_specs = [w_spec(s, k) for k in range(FFN_BLOCKS)
               for s in ((D_MODEL, D_EXPERT), (D_MODEL, D_EXPERT), (D_EXPERT, D_MODEL))]
    grid_spec = pltpu.PrefetchScalarGridSpec(
        num_scalar_prefetch=2,
        grid=(n_rows // step_rows,),
        in_specs=[pl.BlockSpec((step_rows, D_MODEL), lambda i, be, nu: (i, 0))] + w_specs,
        out_specs=pl.BlockSpec((step_rows, D_MODEL), lambda i, be, nu: (i, 0)),
    )
    return pl.pallas_call(
        _ffn_kernel,
        grid_spec=grid_spec,
        out_shape=jax.ShapeDtypeStruct((n_rows, D_MODEL), F32),
        compiler_params=_cparams("arbitrary"),
        name="moe_ffn",
    )(block_expert, n_used, xs, *([wg, wu, wd] * FFN_BLOCKS))


def _combine_kernel(d0_ref, d1_ref, ys_ref, x1_ref, slab_ref, g_ref, b_ref, o_ref, buf_ref, sem):
    i = pl.program_id(0)
    n = pl.num_programs(0)
    tm = x1_ref.shape[0]

    def gather(tile):
        slot = tile % 2
        base = tile * tm

        def start(t, carry):
            for k, d_ref in enumerate((d0_ref, d1_ref)):
                pltpu.make_async_copy(ys_ref.at[pl.ds(d_ref[base + t], 1), :],
                                      buf_ref.at[slot, k, pl.ds(t, 1), :], sem.at[slot]).start()
            return carry

        lax.fori_loop(0, tm, start, 0, unroll=8)

    @pl.when(i == 0)
    def _():
        gather(0)

    @pl.when(i + 1 < n)
    def _():
        gather(i + 1)

    slot = i % 2
    for k in range(2):
        pltpu.make_async_copy(ys_ref.at[pl.ds(0, tm), :], buf_ref.at[slot, k], sem.at[slot]).wait()
    slab = slab_ref[...]
    y = slab[:, 4:5] * buf_ref[slot, 0] + slab[:, 5:6] * buf_ref[slot, 1]
    o_ref[...] = _layer_norm(DEEPNORM_ALPHA * x1_ref[...] + y, g_ref[...], b_ref[...])


def _combine(d0, d1, ys, x1, slab, g, b):
    T = x1.shape[0]
    tm = ROW_BLK
    grid_spec = pltpu.PrefetchScalarGridSpec(
        num_scalar_prefetch=2,
        grid=(T // tm,),
        in_specs=[pl.BlockSpec(memory_space=pl.ANY),
                  pl.BlockSpec((tm, D_MODEL), lambda i, a, c: (i, 0)),
                  pl.BlockSpec((tm, LANES), lambda i, a, c: (i, 0)),
                  pl.BlockSpec((1, D_MODEL), lambda i, a, c: (0, 0)),
                  pl.BlockSpec((1, D_MODEL), lambda i, a, c: (0, 0))],
        out_specs=pl.BlockSpec((tm, D_MODEL), lambda i, a, c: (i, 0)),
        scratch_shapes=[pltpu.VMEM((2, 2, tm, D_MODEL), F32), pltpu.SemaphoreType.DMA((2,))],
    )
    return pl.pallas_call(
        _combine_kernel,
        grid_spec=grid_spec,
        out_shape=jax.ShapeDtypeStruct((T, D_MODEL), F32),
        compiler_params=_cparams("arbitrary"),
        name="moe_combine",
    )(d0, d1, ys, x1, slab, g, b)


def _invert_kernel(d0_ref, d1_ref, ros_ref):
    n_tok = d0_ref.shape[0]
    plane = n_tok + ROW_BLK

    def fill(s, carry):
        odd_block = lax.shift_right_logical(s, ROW_BLK.bit_length() - 1) & 1
        spare = n_tok + (s & (ROW_BLK - 1)) + odd_block * plane
        ros_ref[s] = ((n_tok - 1) << 16) | spare
        return carry

    def put(t, carry):
        ros_ref[d0_ref[t]] = (t << 16) | t
        ros_ref[d1_ref[t]] = (t << 16) | (plane + t)
        return carry

    i = pl.program_id(0)
    half = pl.num_programs(0) // 2
    n_fill = ros_ref.shape[0] // half
    n_put = n_tok // half

    @pl.when(i < half)
    def _():
        lax.fori_loop(i * n_fill, (i + 1) * n_fill, fill, 0)

    @pl.when(i >= half)
    def _():
        lax.fori_loop((i - half) * n_put, (i - half + 1) * n_put, put, 0)


INVERT_STEPS = 64


def _invert(d0, d1, n_rows):
    assert n_rows % INVERT_STEPS == 0 and d0.shape[0] % INVERT_STEPS == 0
    smem = pl.BlockSpec(memory_space=pltpu.SMEM)
    return pl.pallas_call(
        _invert_kernel,
        grid=(2 * INVERT_STEPS,),
        in_specs=[smem, smem],
        out_specs=smem,
        out_shape=jax.ShapeDtypeStruct((n_rows,), jnp.int32),
        compiler_params=_cparams("arbitrary"),
        name="moe_invert",
    )(d0, d1)


def _ffn_fused_kernel(be_ref, ros_ref, x1_ref, wg0_ref, wu0_ref, wd0_ref, wg1_ref, wu1_ref, wd1_ref, out_ref,
                      xb0, xb1, yb0, yb1, gsem, ssem):
    del be_ref
    i = pl.program_id(0)
    last = pl.num_programs(0) - 1
    n_tok = x1_ref.shape[0]
    plane = n_tok + ROW_BLK

    def gather(blk, xbuf, sem):
        for r in range(ROW_BLK):
            tok = lax.shift_right_logical(ros_ref[blk * ROW_BLK + r], 16)
            pltpu.make_async_copy(x1_ref.at[pl.ds(tok, 1), :], xbuf.at[pl.ds(r, 1), :], sem).start()

    def scatter(blk, ybuf, sem):
        for r in range(ROW_BLK):
            row = ros_ref[blk * ROW_BLK + r] & 0xFFFF
            pltpu.make_async_copy(ybuf.at[pl.ds(r, 1), :], out_ref.at[pl.ds(row, 1), :], sem).start()

    def wait_rows(buf, sem):
        pltpu.make_async_copy(x1_ref.at[pl.ds(0, ROW_BLK), :], buf, sem).wait()

    def ffn(xbuf, ybuf, wg_ref, wu_ref, wd_ref):
        xb = xbuf[...].astype(BF16)
        a = _dot(xb, wg_ref[...].astype(BF16))
        u = _dot(xb, wu_ref[...].astype(BF16))
        hmid = (a * jax.nn.sigmoid(a) * u).astype(BF16)
        ybuf[...] = _dot(hmid, wd_ref[...].astype(BF16))

    b0 = 2 * i
    b1 = 2 * i + 1
    b2 = jnp.minimum(2 * i + 2, 2 * last + 1)

    @pl.when(i == 0)
    def _():
        yb0[...] = jnp.zeros(yb0.shape, F32)
        for base in (n_tok, plane + n_tok):
            spare = pltpu.make_async_copy(yb0, out_ref.at[pl.ds(base, ROW_BLK), :], ssem.at[0])
            spare.start()
            spare.wait()
        gather(0, xb0, gsem.at[0])

    wait_rows(xb0, gsem.at[0])
    gather(b1, xb1, gsem.at[1])

    @pl.when(i > 0)
    def _():
        wait_rows(yb0, ssem.at[0])

    ffn(xb0, yb0, wg0_ref, wu0_ref, wd0_ref)
    scatter(b0, yb0, ssem.at[0])
    wait_rows(xb1, gsem.at[1])
    gather(b2, xb0, gsem.at[0])

    @pl.when(i > 0)
    def _():
        wait_rows(yb1, ssem.at[1])

    ffn(xb1, yb1, wg1_ref, wu1_ref, wd1_ref)
    scatter(b1, yb1, ssem.at[1])

    @pl.when(i == last)
    def _():
        wait_rows(xb0, gsem.at[0])
        wait_rows(yb0, ssem.at[0])
        wait_rows(yb1, ssem.at[1])


def _ffn_fused(block_expert, ros, x1, wg, wu, wd, layer):
    T = x1.shape[0]
    n_blocks = ros.shape[0] // ROW_BLK
    assert n_blocks % 2 == 0
    w_spec = lambda shape, k: pl.BlockSpec((None, None) + shape, lambda i, be, ro: (layer, be[2 * i + k], 0, 0))
    grid_spec = pltpu.PrefetchScalarGridSpec(
        num_scalar_prefetch=2,
        grid=(n_blocks // 2,),
        in_specs=[pl.BlockSpec(memory_space=pl.ANY)]
        + [w_spec(s, k) for k in range(2) for s in ((D_MODEL, D_EXPERT), (D_MODEL, D_EXPERT), (D_EXPERT, D_MODEL))],
        out_specs=pl.BlockSpec(memory_space=pl.ANY),
        scratch_shapes=[pltpu.VMEM((ROW_BLK, D_MODEL), F32)] * 4 + [pltpu.SemaphoreType.DMA((2,))] * 2,
    )
    return pl.pallas_call(
        _ffn_fused_kernel,
        grid_spec=grid_spec,
        out_shape=jax.ShapeDtypeStruct((2 * (T + ROW_BLK), D_MODEL), F32),
        compiler_params=_cparams("arbitrary"),
        name="moe_ffn",
    )(block_expert, ros, x1, wg, wu, wd, wg, wu, wd)


def _combine2_kernel(y0_ref, y1_ref, x1_ref, slab_ref, g_ref, b_ref, o_ref):
    slab = slab_ref[...]
    y = slab[:, 4:5] * y0_ref[...] + slab[:, 5:6] * y1_ref[...]
    o_ref[...] = _layer_norm(DEEPNORM_ALPHA * x1_ref[...] + y, g_ref[...], b_ref[...])


def _combine2(out2, x1, slab, g, b):
    T = x1.shape[0]
    tm = ROW_BLK
    plane_blocks = (T + ROW_BLK) // tm
    row = lambda w: pl.BlockSpec((tm, w), lambda i: (i, 0))
    one = pl.BlockSpec((1, D_MODEL), lambda i: (0, 0))
    return pl.pallas_call(
        _combine2_kernel,
        grid=(T // tm,),
        in_specs=[row(D_MODEL), pl.BlockSpec((tm, D_MODEL), lambda i: (plane_blocks + i, 0)), row(D_MODEL), row(LANES),
                  one, one],
        out_specs=row(D_MODEL),
        out_shape=jax.ShapeDtypeStruct((T, D_MODEL), F32),
        compiler_params=_cparams("arbitrary"),
        name="moe_combine",
    )(out2, out2, x1, slab, g, b)


def _t5_bucket(dist):
    max_exact = N_BUCKETS // 2
    large = max_exact + (jnp.log(jnp.maximum(dist, 1).astype(F32) / max_exact)
                         / math.log(MAX_DISTANCE / max_exact) * (N_BUCKETS - max_exact)).astype(jnp.int32)
    return jnp.where(dist < max_exact, dist, jnp.minimum(large, N_BUCKETS - 1))


def _bias_table(table_cols, dil, max_dist):
    dist = BLK + jnp.arange(BLK)[:, None] - jnp.arange(2 * BLK)[None, :]
    bucket = _t5_bucket(jnp.maximum(dist * dil, 0))
    hit = bucket[None, :, :, None] == jnp.arange(N_BUCKETS)[None, None, None, :]
    bias = jnp.sum(jnp.where(hit, table_cols.T.astype(F32)[:, None, None, :], 0.0), axis=-1)
    mask = (dist >= 0) & (dist <= max_dist)
    return jnp.where(mask[None], bias, NEG_INF)


def _rope_tables(T):
    inv_freq = ROPE_THETA ** (-jnp.arange(0, C_ROPE, 2, dtype=F32) / C_ROPE)
    ang = jnp.arange(T, dtype=F32)[:, None] * inv_freq[None, :]
    cos, sin = jnp.cos(ang), jnp.sin(ang)
    ones = jnp.ones((T, C_NOPE), F32)
    zeros_n = jnp.zeros((T, C_NOPE), F32)
    zeros_p = jnp.zeros((T, C_PAD - C_NOPE - C_ROPE), F32)
    cs = jnp.concatenate([ones, cos, cos, zeros_p], axis=1)
    sn = jnp.concatenate([zeros_n, -sin, sin, zeros_p], axis=1)
    return cs, sn


def _swap_halves(w):
    half = w.shape[-1] // 2
    return jnp.concatenate([w[..., half:], w[..., :half]], axis=-1)


def _pad_cols(w, width):
    return jnp.pad(w, ((0, 0), (0, width - w.shape[1])))


def _prep_layer_weights(w_in, w_out, w_uq, w_ukv, w_rg, b_rg, w_re, b_re):
    head_cols = lambda w, h: w[:, h * HEAD_DIM:(h + 1) * HEAD_DIM]
    wab = jnp.concatenate([head_cols(w_in, h) for h in A_HEAD_ORDER] + [w_in[:, A_Q_HEADS * HEAD_DIM:AB_COLS]],
                          axis=1).astype(BF16)
    c0 = AB_COLS
    w_cq = w_in[:, c0:c0 + Q_LORA]
    w_ckv = w_in[:, c0 + Q_LORA:c0 + Q_LORA + KV_LORA]
    w_kr = w_in[:, c0 + Q_LORA + KV_LORA:]
    lead = jnp.zeros((D_MODEL, C_NOPE), F32)
    wc = jnp.concatenate([w_cq, w_ckv,
                          _pad_cols(jnp.concatenate([lead, w_kr], 1), C_PAD),
                          _pad_cols(jnp.concatenate([lead, _swap_halves(w_kr)], 1), C_PAD)], axis=1).astype(BF16)
    uq = w_uq.reshape(Q_LORA, C_HEADS, C_NOPE + C_ROPE)
    wq = jnp.pad(uq, ((0, 0), (0, 0), (0, C_PAD - C_NOPE - C_ROPE))).reshape(Q_LORA, C_HEADS * C_PAD).astype(BF16)
    uqs = jnp.concatenate([jnp.zeros((Q_LORA, C_HEADS, C_NOPE), F32), _swap_halves(uq[..., C_NOPE:])], axis=-1)
    wqs = jnp.pad(uqs, ((0, 0), (0, 0), (0, C_PAD - C_NOPE - C_ROPE))).reshape(Q_LORA, C_HEADS * C_PAD).astype(BF16)
    ukv = w_ukv.reshape(KV_LORA, C_HEADS, C_NOPE + C_V)
    wk = jnp.pad(ukv[..., :C_NOPE], ((0, 0), (0, 0), (0, C_PAD - C_NOPE))).reshape(KV_LORA, C_HEADS * C_PAD)
    wv = jnp.pad(ukv[..., C_NOPE:], ((0, 0), (0, 0), (0, C_PAD - C_V))).reshape(KV_LORA, C_HEADS * C_PAD)
    woa = jnp.concatenate([w_out[h * HEAD_DIM:(h + 1) * HEAD_DIM] for h in A_HEAD_ORDER],
                          axis=0).reshape(2, LANES, D_MODEL).astype(BF16)
    wob = w_out[256:512].reshape(2, LANES, D_MODEL).astype(BF16)
    woc = w_out[512:].astype(BF16)
    wr = _pad_cols(jnp.concatenate([w_rg, w_re.transpose(1, 0, 2).reshape(D_MODEL, N_EXPERTS)], axis=1), LANES)
    wr_hi = wr.astype(BF16)
    wr_lo = (wr - wr_hi.astype(F32)).astype(BF16)
    wr = jnp.concatenate([wr_hi, wr_hi, wr_lo], axis=0)
    br = _pad_cols(jnp.concatenate([b_rg, b_re.reshape(N_EXPERTS)])[None, :], LANES)
    return dict(wab=wab, wc=wc, wq=wq, wqs=wqs, wk=wk.astype(BF16), wv=wv.astype(BF16),
                woa=woa, wob=wob, woc=woc, wr=wr, br=br)


def kernel(x, w_in, w_out, sinks, rel_bias, mla_q_norm, mla_kv_norm, w_uq, w_ukv, ln1_g, ln1_b,
           w_route_group, b_route_group, w_route_expert, b_route_expert,
           w_expert_gate, w_expert_up, w_expert_down, ln2_g, ln2_b):
    Bsz, T, D = x.shape
    assert Bsz == 1 and D == D_MODEL and T % CHUNK == 0
    xt = x.reshape(T, D)
    cs, sn = _rope_tables(T)
    bias_a = _bias_table(rel_bias[:, :A_Q_HEADS], 1, SWA_WINDOW - 1)
    bias_b = [_bias_table(rel_bias[:, A_Q_HEADS + g * B_HEADS_PER_GROUP:A_Q_HEADS + (g + 1) * B_HEADS_PER_GROUP],
                          dil, window // dil) for g, (window, dil) in enumerate(DILATED_PAIRS)]
    n_assign = T * TOP_K
    n_blocks = n_assign // ROW_BLK + N_EXPERTS
    n_rows = n_blocks * ROW_BLK
    no_sink = jnp.zeros((B_HEADS_PER_GROUP,), F32)

    for layer in range(w_in.shape[0]):
        w = _prep_layer_weights(w_in[layer], w_out[layer], w_uq[layer], w_ukv[layer],
                                w_route_group[layer], b_route_group[layer],
                                w_route_expert[layer], b_route_expert[layer])
        hab, qt, k, vt = _in_proj(xt, w["wab"], w["wc"], w["wq"], w["wqs"], w["wk"], w["wv"],
                                mla_q_norm[layer][None, :], mla_kv_norm[layer][None, :], cs, sn)
        oa, _ = _banded(hab, sinks[layer], bias_a, q_col=0, k_col=2, v_col=3, dil=1,
                        n_heads=A_Q_HEADS, group=A_Q_HEADS // A_KV_HEADS, has_sink=True, want_lse=False)
        ob, lse = [], []
        for g, (_, dil) in enumerate(DILATED_PAIRS):
            o_g, lse_g = _banded(hab, no_sink, bias_b[g], q_col=4 + 2 * g, k_col=10 + 2 * g, v_col=16 + 2 * g,
                                 dil=dil, n_heads=B_HEADS_PER_GROUP, group=1, has_sink=False, want_lse=True)
            ob += o_g
            lse += lse_g
        oc = _flash(qt, k, vt)
        x1, slab, cnt = _out_router(oa, ob, lse, oc, xt, w["woa"], w["wob"], w["woc"],
                                    ln1_g[layer][None, :], ln1_b[layer][None, :], w["wr"], w["br"])

        counts = cnt[0, N_GROUPS:N_GROUPS + N_EXPERTS].astype(jnp.int32)
        padded = ((counts + ROW_BLK - 1) // ROW_BLK) * ROW_BLK
        pad_end = jnp.cumsum(padded)
        pad_start = pad_end - padded
        eids = jnp.arange(N_EXPERTS, dtype=jnp.int32)
        pick = lambda e: jnp.sum(jnp.where(e.astype(jnp.int32)[:, None] == eids[None, :], pad_start[None, :], 0), axis=1)
        d0 = pick(slab[:, 0]) + slab[:, 2].astype(jnp.int32)
        d1 = pick(slab[:, 1]) + slab[:, 3].astype(jnp.int32)
        block_start = jnp.arange(n_blocks, dtype=jnp.int32) * ROW_BLK
        block_expert = jnp.minimum(jnp.sum((pad_end[None, :] <= block_start[:, None]).astype(jnp.int32), axis=1),
                                   N_EXPERTS - 1)

        n_used = (pad_end[-1:] // ROW_BLK).astype(jnp.int32)
        xs = _dispatch(d0, d1, pad_start + counts, padded - counts, n_used, x1, n_rows)
        ys = _ffn(block_expert, n_used, xs, w_expert_gate, w_expert_up, w_expert_down, layer)
        xt = _combine(d0, d1, ys, x1, slab, ln2_g[layer][None, :], ln2_b[layer][None, :])
    return xt.reshape(Bsz, T, D)
```

```python
import functools
import math

import jax
import jax.numpy as jnp
from jax import lax
from jax.experimental import pallas as pl
from jax.experimental.pallas import tpu as pltpu

F32 = jnp.float32
BF16 = jnp.bfloat16

D_MODEL = 1024
DEPTH = 4
HEAD_DIM = 64
BLK = 128
NEG_INF = -1e30

SWA_WINDOW = 128
A_Q_HEADS = 4
A_KV_HEADS = 2
A_COLS = (A_Q_HEADS + 2 * A_KV_HEADS) * HEAD_DIM
A_HEAD_ORDER = (0, 2, 1, 3)

DILATED_PAIRS = ((128, 1), (512, 4), (2048, 16))
B_HEADS_PER_GROUP = 4
B_HEADS = len(DILATED_PAIRS) * B_HEADS_PER_GROUP
B_COLS = 3 * B_HEADS * HEAD_DIM
AB_COLS = A_COLS + B_COLS

C_HEADS = 8
C_NOPE = 64
C_ROPE = 32
C_V = 64
Q_LORA = 256
KV_LORA = 128
ROPE_THETA = 10000.0
C_PAD = 128
C_VROWS = 80

N_BUCKETS = 32
MAX_DISTANCE = 2048

N_GROUPS = 4
EXPERTS_PER_GROUP = 8
N_EXPERTS = N_GROUPS * EXPERTS_PER_GROUP
TOP_K = 2
D_EXPERT = 256
ROW_BLK = 512

DEEPNORM_ALPHA = (2 * DEPTH) ** 0.25

VMEM_LIMIT = 56 * 1024 * 1024

CHUNK = 2048
TM_IN = 256
TM_OUT = 256
TQ = 1024
TK = 512
N_DIAG = TQ // TK
FLASH_UNROLL = 4
LANES = 128


def _dot(a, b):
    return jnp.dot(a, b, preferred_element_type=F32)


def _dot_nt(a, b):
    return lax.dot_general(a, b, (((1,), (1,)), ((), ())), preferred_element_type=F32)


def _bdot(a, b):
    return lax.dot_general(a, b, (((2,), (1,)), ((0,), (0,))), preferred_element_type=F32)


def _bdot_nt(a, b):
    return lax.dot_general(a, b, (((2,), (2,)), ((0,), (0,))), preferred_element_type=F32)


def _cparams(*sem):
    return pltpu.CompilerParams(dimension_semantics=sem, vmem_limit_bytes=VMEM_LIMIT)


def _in_proj_kernel(x_ref, wab_ref, wc_ref, wq_ref, wqs_ref, wk_ref, wv_ref, gq_ref, gkv_ref, cs_ref, sn_ref,
                    hab_ref, qt_ref, k_ref, vt_ref):
    xb = x_ref[...].astype(BF16)
    hab_ref[...] = _dot(xb, wab_ref[...])
    hc = _dot(xb, wc_ref[...])
    cq = hc[:, :Q_LORA]
    ckv = hc[:, Q_LORA:Q_LORA + KV_LORA]
    kr = hc[:, Q_LORA + KV_LORA:Q_LORA + KV_LORA + C_PAD]
    krs = hc[:, Q_LORA + KV_LORA + C_PAD:]
    cqn = (cq * lax.rsqrt(jnp.mean(cq * cq, -1, keepdims=True) + 1e-6) * gq_ref[...]).astype(BF16)
    ckvn = (ckv * lax.rsqrt(jnp.mean(ckv * ckv, -1, keepdims=True) + 1e-6) * gkv_ref[...]).astype(BF16)
    cs = cs_ref[...]
    sn = sn_ref[...]
    q = _dot(cqn, wq_ref[...])
    qs = _dot(cqn, wqs_ref[...])
    k = _dot(ckvn, wk_ref[...])
    v = _dot(ckvn, wv_ref[...])
    krot = kr * cs + krs * sn
    scale = (C_NOPE + C_ROPE) ** -0.5 * math.log2(math.e)
    for h in range(C_HEADS):
        sl = slice(h * C_PAD, (h + 1) * C_PAD)
        qh = (q[:, sl] * cs + qs[:, sl] * sn) * scale
        qt_ref[sl, :] = qh.T.astype(BF16)
        k_ref[h] = (k[:, sl] + krot).astype(BF16)
    one_row = (lax.broadcasted_iota(jnp.int32, (1, C_PAD), 1) == C_V).astype(F32)
    for h in range(C_HEADS):
        vh = v[:, h * C_PAD:(h + 1) * C_PAD] + one_row
        vt_ref[h * C_VROWS:(h + 1) * C_VROWS, :] = vh.T[:C_VROWS, :].astype(BF16)


def _in_proj(x, wab, wc, wq, wqs, wk, wv, gq, gkv, cs, sn):
    T = x.shape[0]
    tm = TM_IN
    full = lambda a: pl.BlockSpec(a.shape, lambda i: (0,) * a.ndim)
    row = lambda w: pl.BlockSpec((tm, w), lambda i: (i, 0))
    colblk = lambda r: pl.BlockSpec((r, tm), lambda i: (0, i))
    return pl.pallas_call(
        _in_proj_kernel,
        grid=(T // tm,),
        in_specs=[row(D_MODEL), full(wab), full(wc), full(wq), full(wqs), full(wk), full(wv), full(gq), full(gkv),
                  row(C_PAD), row(C_PAD)],
        out_specs=[row(AB_COLS), colblk(C_HEADS * C_PAD), pl.BlockSpec((C_HEADS, tm, C_PAD), lambda i: (0, i, 0)),
                   colblk(C_HEADS * C_VROWS)],
        out_shape=[jax.ShapeDtypeStruct((T, AB_COLS), F32), jax.ShapeDtypeStruct((C_HEADS * C_PAD, T), BF16),
                   jax.ShapeDtypeStruct((C_HEADS, T, C_PAD), BF16),
                   jax.ShapeDtypeStruct((C_HEADS * C_VROWS, T), BF16)],
        compiler_params=_cparams("arbitrary"),
        name="in_proj",
    )(x, wab, wc, wq, wqs, wk, wv, gq, gkv, cs, sn)


def _banded_kernel(sink_ref, *refs, dil, n_heads, group, has_sink, want_lse):
    nq = n_heads // 2
    nkv = n_heads // group // 2
    q_refs, refs = refs[:nq], refs[nq:]
    k_refs, v_refs, kp_refs, vp_refs = (refs[i * nkv:(i + 1) * nkv] for i in range(4))
    bias_ref = refs[4 * nkv]
    o_refs = refs[4 * nkv + 1:4 * nkv + 1 + nq]
    lse_refs = refs[4 * nkv + 1 + nq:]
    c = pl.program_id(0)
    span = BLK * dil
    nb = CHUNK // span
    n_combo = nb * dil
    starts = [b * span + r for b in range(nb) for r in range(dil)]
    scale = HEAD_DIM ** -0.5

    def rows(start):
        return pl.ds(start, BLK) if dil == 1 else pl.ds(start, BLK, stride=dil)

    def gather(ref):
        return jnp.stack([ref[rows(s), :] for s in starts], axis=0)

    def with_prev(ref, prev_ref):
        cur = gather(ref)
        first = jnp.stack([prev_ref[rows(r), :] for r in range(dil)], axis=0)
        prev = jnp.concatenate([first, cur[:n_combo - dil]], axis=0) if dil < n_combo else first
        return jnp.concatenate([prev, cur], axis=1).astype(BF16)

    q = [gather(ref).astype(BF16) for ref in q_refs]
    kcat = [with_prev(k_refs[j], kp_refs[j]) for j in range(nkv)]
    vcat = [with_prev(v_refs[j], vp_refs[j]) for j in range(nkv)]

    lane = lax.broadcasted_iota(jnp.int32, (1, 1, LANES), 2)
    low = lane < HEAD_DIM
    combo = lax.broadcasted_iota(jnp.int32, (n_combo, 1, 2 * BLK), 0)
    col = lax.broadcasted_iota(jnp.int32, (n_combo, 1, 2 * BLK), 2)
    no_prev = jnp.logical_and(c == 0, jnp.logical_and(combo < dil, col < BLK))

    for j in range(nq):
        o_par, lse_par = [], []
        for par in range(2):
            h = 2 * j + par if group == 1 else group * par + j
            kvb = j if group == 1 else 0
            keep = low if par == 0 else jnp.logical_not(low)
            kk = jnp.where(keep, kcat[kvb], jnp.zeros((), BF16))
            s = _bdot_nt(q[j], kk)
            s = s * scale + bias_ref[h][None]
            s = jnp.where(no_prev, NEG_INF, s)
            m = jnp.max(s, axis=-1, keepdims=True)
            if has_sink:
                snk = sink_ref[h]
                m = jnp.maximum(m, snk)
            e = jnp.exp(s - m)
            l = jnp.sum(e, axis=-1, keepdims=True)
            if has_sink:
                l = l + jnp.exp(snk - m)
            p = (e * (1.0 / l)).astype(BF16)
            o_par.append(_bdot(p, vcat[kvb]))
            if want_lse:
                lse_par.append(m + jnp.log(l))
        o_pair = jnp.where(low, o_par[0], o_par[1])
        if want_lse:
            lse_pair = jnp.where(low, lse_par[0], lse_par[1])
        for ci, s0 in enumerate(starts):
            o_refs[j][rows(s0), :] = o_pair[ci]
            if want_lse:
                lse_refs[j][rows(s0), :] = lse_pair[ci]


def _banded(hab, sink, bias, *, q_col, k_col, v_col, dil, n_heads, group, has_sink, want_lse):
    T = hab.shape[0]
    span = BLK * dil
    nb = CHUNK // span
    nq = n_heads // 2
    nkv = n_heads // group // 2
    kernel = functools.partial(_banded_kernel, dil=dil, n_heads=n_heads, group=group,
                               has_sink=has_sink, want_lse=want_lse)
    prev = lambda cb: pl.BlockSpec((span, LANES), lambda c, s: (jnp.maximum(c * nb - 1, 0), cb))
    cur = lambda cb: pl.BlockSpec((CHUNK, LANES), lambda c, s: (c, cb))
    in_specs = ([cur(q_col + j) for j in range(nq)] + [cur(k_col + j) for j in range(nkv)]
                + [cur(v_col + j) for j in range(nkv)] + [prev(k_col + j) for j in range(nkv)]
                + [prev(v_col + j) for j in range(nkv)] + [pl.BlockSpec(bias.shape, lambda c, s: (0, 0, 0))])
    n_out = nq * (2 if want_lse else 1)
    grid_spec = pltpu.PrefetchScalarGridSpec(
        num_scalar_prefetch=1,
        grid=(T // CHUNK,),
        in_specs=in_specs,
        out_specs=[pl.BlockSpec((CHUNK, LANES), lambda c, s: (c, 0))] * n_out,
    )
    outs = pl.pallas_call(
        kernel,
        grid_spec=grid_spec,
        out_shape=[jax.ShapeDtypeStruct((T, LANES), F32)] * n_out,
        compiler_params=_cparams("arbitrary"),
        name=f"banded_d{dil}" + ("_sink" if has_sink else ""),
    )(sink, *([hab] * (nq + 4 * nkv)), bias)
    return outs[:nq], outs[nq:]


def _flash_kernel(qt_ref, k_ref, vt_ref, ot_ref, m_ref, acc_ref, s_ref):
    qi = pl.program_id(1)
    qt = qt_ref[...]
    m_ref[...] = jnp.full(m_ref.shape, NEG_INF, F32)
    acc_ref[...] = jnp.zeros(acc_ref.shape, F32)

    def scores(j):
        off = pl.multiple_of(j * TK, TK)
        return _dot(k_ref[pl.ds(off, TK), :], qt)

    def accumulate(j, s):
        off = pl.multiple_of(j * TK, TK)
        m_old = m_ref[...]
        m_new = jnp.maximum(m_old, jnp.max(s, axis=0, keepdims=True))
        alpha = jnp.exp2(m_old - m_new)
        p = jnp.exp2(s - m_new).astype(BF16)
        acc_ref[...] = alpha * acc_ref[...] + _dot(vt_ref[:, pl.ds(off, TK)], p)
        m_ref[...] = m_new

    def diagonal(s, d):
        key = lax.broadcasted_iota(jnp.int32, (TK, TQ), 0)
        qry = lax.broadcasted_iota(jnp.int32, (TK, TQ), 1)
        return jnp.where(key + d * TK <= qry, s, NEG_INF)

    n_full = qi * N_DIAG
    s_ref[0] = scores(0)
    n_main = n_full // FLASH_UNROLL

    def main_body(i, carry):
        j = FLASH_UNROLL * i
        for u in range(FLASH_UNROLL):
            s_ref[(u + 1) % 2] = scores(j + u + 1)
            accumulate(j + u, s_ref[u % 2])
        return carry

    lax.fori_loop(0, n_main, main_body, 0)

    base = FLASH_UNROLL * n_main
    for rem in range(0, FLASH_UNROLL, math.gcd(FLASH_UNROLL, N_DIAG)):
        @pl.when(n_full - base == rem)
        def _():
            n_tail = rem + N_DIAG
            for u in range(n_tail):
                if u + 1 < n_tail:
                    s_ref[(u + 1) % 2] = scores(base + u + 1)
                s = s_ref[u % 2]
                accumulate(base + u, s if u < rem else diagonal(s, u - rem))

    ot_ref[...] = (acc_ref[:C_V, :] / acc_ref[C_V:C_V + 1, :]).astype(ot_ref.dtype)


def _flash(qt, k, vt):
    H, T, _ = k.shape
    return pl.pallas_call(
        _flash_kernel,
        grid=(H, T // TQ),
        in_specs=[pl.BlockSpec((C_PAD, TQ), lambda h, i: (h, i)),
                  pl.BlockSpec((None, T, C_PAD), lambda h, i: (h, 0, 0)),
                  pl.BlockSpec((C_VROWS, T), lambda h, i: (h, 0))],
        out_specs=pl.BlockSpec((C_V, TQ), lambda h, i: (h, i)),
        out_shape=jax.ShapeDtypeStruct((H * C_V, T), BF16),
        scratch_shapes=[pltpu.VMEM((1, TQ), F32), pltpu.VMEM((C_VROWS, TQ), F32), pltpu.VMEM((2, TK, TQ), F32)],
        compiler_params=_cparams("arbitrary", "arbitrary"),
        name="mla_flash",
    )(qt, k, vt)


def _layer_norm(z, g, b):
    mu = jnp.mean(z, -1, keepdims=True)
    zc = z - mu
    var = jnp.mean(zc * zc, -1, keepdims=True)
    return zc * lax.rsqrt(var + 1e-5) * g + b


def _out_router_kernel(*refs):
    n_ab = 2 + 6 + 6
    oa_refs, ob_refs, lse_refs = refs[0:2], refs[2:8], refs[8:14]
    (oc_ref, x_ref, woa_ref, wob_ref, woc_ref, g_ref, b_ref, wr_ref, br_ref,
     x1_ref, slab_ref, cnt_ref, carry_ref, prev_ref) = refs[n_ab:]
    i = pl.program_id(0)
    tm = x_ref.shape[0]

    @pl.when(i == 0)
    def _():
        carry_ref[...] = jnp.zeros(carry_ref.shape, F32)
        prev_ref[...] = jnp.zeros(prev_ref.shape, F32)

    xr = prev_ref[...]

    y = None
    for j in range(2):
        l0, l1, l2 = (lse_refs[2 * g + j][...] for g in range(3))
        mx = jnp.maximum(jnp.maximum(l0, l1), l2)
        e0, e1, e2 = jnp.exp(l0 - mx), jnp.exp(l1 - mx), jnp.exp(l2 - mx)
        den = e0 + e1 + e2
        ob = ((e0 / den) * ob_refs[j][...] + (e1 / den) * ob_refs[2 + j][...] + (e2 / den) * ob_refs[4 + j][...])
        part = _dot(oa_refs[j][...].astype(BF16), woa_ref[j]) + _dot(ob.astype(BF16), wob_ref[j])
        y = part if y is None else y + part
    y = y + _dot(oc_ref[...].astype(F32).T.astype(BF16), woc_ref[...])
    x1 = _layer_norm(DEEPNORM_ALPHA * x_ref[...] + y, g_ref[...], b_ref[...])
    x1_ref[...] = x1
    prev_ref[...] = x1

    xr_hi = xr.astype(BF16)
    xr_lo = (xr - xr_hi.astype(F32)).astype(BF16)
    logits = _dot(jnp.concatenate([xr_hi, xr_lo, xr_hi], axis=1), wr_ref[...]) + br_ref[...]
    lane = lax.broadcasted_iota(jnp.int32, (tm, LANES), 1)
    ninf = jnp.float32(-jnp.inf)
    gmask = lane < N_GROUPS
    gl = jnp.where(gmask, logits, ninf)
    gmax = jnp.max(gl, axis=-1, keepdims=True)
    grp = jnp.min(jnp.where(gl == gmax, lane, LANES), axis=-1, keepdims=True)
    p_grp = 1.0 / jnp.sum(jnp.where(gmask, jnp.exp(logits - gmax), 0.0), axis=-1, keepdims=True)
    lo = N_GROUPS + EXPERTS_PER_GROUP * grp
    emask = jnp.logical_and(lane >= lo, lane < lo + EXPERTS_PER_GROUP)
    el = jnp.where(emask, logits, ninf)
    v0 = jnp.max(el, axis=-1, keepdims=True)
    i0 = jnp.min(jnp.where(el == v0, lane, LANES), axis=-1, keepdims=True)
    el2 = jnp.where(lane == i0, ninf, el)
    v1 = jnp.max(el2, axis=-1, keepdims=True)
    i1 = jnp.min(jnp.where(el2 == v1, lane, LANES), axis=-1, keepdims=True)
    t = jnp.exp(v1 - v0)
    gate0 = p_grp / (1.0 + t)
    gate1 = p_grp * t / (1.0 + t)

    hit0 = lane == i0
    hit1 = lane == i1
    cmat = jnp.logical_or(hit0, hit1).astype(BF16)
    tr = lax.broadcasted_iota(jnp.int32, (tm, tm), 0)
    tc = lax.broadcasted_iota(jnp.int32, (tm, tm), 1)
    tril = (tc < tr).astype(BF16)
    prefix = _dot(tril, cmat) + carry_ref[...]
    r0 = jnp.sum(jnp.where(hit0, prefix, 0.0), axis=-1, keepdims=True)
    r1 = jnp.sum(jnp.where(hit1, prefix, 0.0), axis=-1, keepdims=True)
    counted = (i > 0).astype(F32)
    carry_ref[...] = carry_ref[...] + counted * jnp.sum(cmat.astype(F32), axis=0, keepdims=True)
    cnt_ref[...] = carry_ref[...]

    fields = [(i0 - N_GROUPS).astype(F32), (i1 - N_GROUPS).astype(F32), r0, r1, gate0, gate1]
    slab = jnp.zeros((tm, LANES), F32)
    for n, f in enumerate(fields):
        slab = jnp.where(lane == n, f, slab)
    slab_ref[...] = slab


def _out_router(oa, ob, lse, oc, x, woa, wob, woc, g, b, wr, br):
    T = x.shape[0]
    tm = TM_OUT
    n = T // tm
    full = lambda a: pl.BlockSpec(a.shape, lambda i: (0,) * a.ndim)
    row = lambda w: pl.BlockSpec((tm, w), lambda i: (jnp.minimum(i, n - 1), 0))
    lag = pl.BlockSpec((tm, LANES), lambda i: (jnp.maximum(i - 1, 0), 0))
    return pl.pallas_call(
        _out_router_kernel,
        grid=(n + 1,),
        in_specs=[row(LANES)] * 14 + [pl.BlockSpec((C_HEADS * C_V, tm), lambda i: (0, jnp.minimum(i, n - 1))),
                                      row(D_MODEL),
                                      full(woa), full(wob), full(woc), full(g), full(b), full(wr), full(br)],
        out_specs=[row(D_MODEL), lag, pl.BlockSpec((1, LANES), lambda i: (0, 0))],
        out_shape=[jax.ShapeDtypeStruct((T, D_MODEL), F32), jax.ShapeDtypeStruct((T, LANES), F32),
                   jax.ShapeDtypeStruct((1, LANES), F32)],
        scratch_shapes=[pltpu.VMEM((1, LANES), F32), pltpu.VMEM((tm, D_MODEL), F32)],
        compiler_params=_cparams("arbitrary"),
        name="out_router",
    )(*oa, *ob, *lse, oc, x, woa, wob, woc, g, b, wr, br)


SUBLANES = 8
PAD_BITS = tuple(1 << k for k in reversed(range(SUBLANES.bit_length() - 1, ROW_BLK.bit_length() - 1)))


def _dispatch_kernel(d0_ref, d1_ref, pad_at_ref, pad_len_ref, nu_ref, x_ref, xs_ref, zero_ref, sem, zsem):
    i = pl.program_id(0)
    tm = x_ref.shape[0]
    base = i * tm

    @pl.when(i == 0)
    def _():
        zero_ref[...] = jnp.zeros(zero_ref.shape, F32)

        def pad_copies(e):
            n = pad_len_ref[e]
            at = pad_at_ref[e]
            head = n & (SUBLANES - 1)
            out = [(j < head, pltpu.make_async_copy(zero_ref.at[pl.ds(0, 1), :], xs_ref.at[pl.ds(at + j, 1), :], zsem))
                   for j in range(SUBLANES - 1)]
            body = n - head
            for bit in PAD_BITS:
                start = pl.multiple_of(at + head + (body & ~(2 * bit - 1)), SUBLANES)
                out.append(((body & bit) != 0,
                            pltpu.make_async_copy(zero_ref.at[pl.ds(0, bit), :], xs_ref.at[pl.ds(start, bit), :], zsem)))
            return out

        def spare_copies():
            n_blocks = xs_ref.shape[0] // ROW_BLK
            first = d0_ref.shape[0] * TOP_K // ROW_BLK
            return [(b >= nu_ref[0], pltpu.make_async_copy(zero_ref, xs_ref.at[pl.ds(b * ROW_BLK, ROW_BLK), :], zsem))
                    for b in range(first, n_blocks)]

        for e in range(N_EXPERTS):
            for on, cp in pad_copies(e):
                pl.when(on)(cp.start)
        for on, cp in spare_copies():
            pl.when(on)(cp.start)
        for e in range(N_EXPERTS):
            for on, cp in pad_copies(e):
                pl.when(on)(cp.wait)
        for on, cp in spare_copies():
            pl.when(on)(cp.wait)

    def copy(t, d_ref):
        return pltpu.make_async_copy(x_ref.at[pl.ds(t, 1), :], xs_ref.at[pl.ds(d_ref[base + t], 1), :], sem)

    def start(t, carry):
        copy(t, d0_ref).start()
        copy(t, d1_ref).start()
        return carry

    lax.fori_loop(0, tm, start, 0, unroll=8)
    for _ in range(2):
        pltpu.make_async_copy(x_ref, xs_ref.at[pl.ds(0, tm), :], sem).wait()


def _dispatch(d0, d1, pad_at, pad_len, n_used, x1, n_rows):
    T = x1.shape[0]
    tm = ROW_BLK
    grid_spec = pltpu.PrefetchScalarGridSpec(
        num_scalar_prefetch=5,
        grid=(T // tm,),
        in_specs=[pl.BlockSpec((tm, D_MODEL), lambda i, *_: (i, 0))],
        out_specs=pl.BlockSpec(memory_space=pl.ANY),
        scratch_shapes=[pltpu.VMEM((ROW_BLK, D_MODEL), F32), pltpu.SemaphoreType.DMA, pltpu.SemaphoreType.DMA],
    )
    return pl.pallas_call(
        _dispatch_kernel,
        grid_spec=grid_spec,
        out_shape=jax.ShapeDtypeStruct((n_rows, D_MODEL), F32),
        compiler_params=_cparams("arbitrary"),
        name="moe_dispatch",
    )(d0, d1, pad_at, pad_len, n_used, x1)


FFN_BLOCKS = 1


def _ffn_kernel(be_ref, nu_ref, xs_ref, *refs):
    del be_ref
    w_refs, ys_ref = refs[:-1], refs[-1]
    i = pl.program_id(0)
    for k in range(FFN_BLOCKS):
        wg_ref, wu_ref, wd_ref = w_refs[3 * k:3 * k + 3]
        rows = pl.ds(k * ROW_BLK, ROW_BLK)
        used = FFN_BLOCKS * i + k < nu_ref[0]

        @pl.when(used)
        def _():
            xb = xs_ref[rows, :].astype(BF16)
            a = _dot(xb, wg_ref[...].astype(BF16))
            u = _dot(xb, wu_ref[...].astype(BF16))
            hmid = (a * jax.nn.sigmoid(a) * u).astype(BF16)
            ys_ref[rows, :] = _dot(hmid, wd_ref[...].astype(BF16))

        @pl.when(jnp.logical_not(used))
        def _():
            ys_ref[rows, :] = jnp.zeros((ROW_BLK, D_MODEL), F32)


def _ffn(block_expert, n_used, xs, wg, wu, wd, layer):
    n_rows = xs.shape[0]
    step_rows = FFN_BLOCKS * ROW_BLK
    assert n_rows % step_rows == 0
    w_spec = lambda shape, k: pl.BlockSpec((None, None) + shape,
                                           lambda i, be, nu: (layer, be[FFN_BLOCKS * i + k], 0, 0))
    w_specs = [w_spec(s, k) for k in range(FFN_BLOCKS)
               for s in ((D_MODEL, D_EXPERT), (D_MODEL, D_EXPERT), (D_EXPERT, D_MODEL))]
    grid_spec = pltpu.PrefetchScalarGridSpec(
        num_scalar_prefetch=2,
        grid=(n_rows // step_rows,),
        in_specs=[pl.BlockSpec((step_rows, D_MODEL), lambda i, be, nu: (i, 0))] + w_specs,
        out_specs=pl.BlockSpec((step_rows, D_MODEL), lambda i, be, nu: (i, 0)),
    )
    return pl.pallas_call(
        _ffn_kernel,
        grid_spec=grid_spec,
        out_shape=jax.ShapeDtypeStruct((n_rows, D_MODEL), F32),
        compiler_params=_cparams("arbitrary"),
        name="moe_ffn",
    )(block_expert, n_used, xs, *([wg, wu, wd] * FFN_BLOCKS))


def _combine_kernel(d0_ref, d1_ref, ys_ref, x1_ref, slab_ref, g_ref, b_ref, o_ref, buf_ref, sem):
    i = pl.program_id(0)
    n = pl.num_programs(0)
    tm = x1_ref.shape[0]

    def gather(tile):
        slot = tile % 2
        base = tile * tm

        def start(t, carry):
            for k, d_ref in enumerate((d0_ref, d1_ref)):
                pltpu.make_async_copy(ys_ref.at[pl.ds(d_ref[base + t], 1), :],
                                      buf_ref.at[slot, k, pl.ds(t, 1), :], sem.at[slot]).start()
            return carry

        lax.fori_loop(0, tm, start, 0, unroll=8)

    @pl.when(i == 0)
    def _():
        gather(0)

    @pl.when(i + 1 < n)
    def _():
        gather(i + 1)

    slot = i % 2
    for k in range(2):
        pltpu.make_async_copy(ys_ref.at[pl.ds(0, tm), :], buf_ref.at[slot, k], sem.at[slot]).wait()
    slab = slab_ref[...]
    y = slab[:, 4:5] * buf_ref[slot, 0] + slab[:, 5:6] * buf_ref[slot, 1]
    o_ref[...] = _layer_norm(DEEPNORM_ALPHA * x1_ref[...] + y, g_ref[...], b_ref[...])


def _combine(d0, d1, ys, x1, slab, g, b):
    T = x1.shape[0]
    tm = ROW_BLK
    grid_spec = pltpu.PrefetchScalarGridSpec(
        num_scalar_prefetch=2,
        grid=(T // tm,),
        in_specs=[pl.BlockSpec(memory_space=pl.ANY),
                  pl.BlockSpec((tm, D_MODEL), lambda i, a, c: (i, 0)),
                  pl.BlockSpec((tm, LANES), lambda i, a, c: (i, 0)),
                  pl.BlockSpec((1, D_MODEL), lambda i, a, c: (0, 0)),
                  pl.BlockSpec((1, D_MODEL), lambda i, a, c: (0, 0))],
        out_specs=pl.BlockSpec((tm, D_MODEL), lambda i, a, c: (i, 0)),
        scratch_shapes=[pltpu.VMEM((2, 2, tm, D_MODEL), F32), pltpu.SemaphoreType.DMA((2,))],
    )
    return pl.pallas_call(
        _combine_kernel,
        grid_spec=grid_spec,
        out_shape=jax.ShapeDtypeStruct((T, D_MODEL), F32),
        compiler_params=_cparams("arbitrary"),
        name="moe_combine",
    )(d0, d1, ys, x1, slab, g, b)


def _invert_kernel(d0_ref, d1_ref, ros_ref):
    n_tok = d0_ref.shape[0]
    plane = n_tok + ROW_BLK

    def fill(s, carry):
        odd_block = lax.shift_right_logical(s, ROW_BLK.bit_length() - 1) & 1
        spare = n_tok + (s & (ROW_BLK - 1)) + odd_block * plane
        ros_ref[s] = ((n_tok - 1) << 16) | spare
        return carry

    def put(t, carry):
        ros_ref[d0_ref[t]] = (t << 16) | t
        ros_ref[d1_ref[t]] = (t << 16) | (plane + t)
        return carry

    i = pl.program_id(0)
    half = pl.num_programs(0) // 2
    n_fill = ros_ref.shape[0] // half
    n_put = n_tok // half

    @pl.when(i < half)
    def _():
        lax.fori_loop(i * n_fill, (i + 1) * n_fill, fill, 0)

    @pl.when(i >= half)
    def _():
        lax.fori_loop((i - half) * n_put, (i - half + 1) * n_put, put, 0)


INVERT_STEPS = 64


def _invert(d0, d1, n_rows):
    assert n_rows % INVERT_STEPS == 0 and d0.shape[0] % INVERT_STEPS == 0
    smem = pl.BlockSpec(memory_space=pltpu.SMEM)
    return pl.pallas_call(
        _invert_kernel,
        grid=(2 * INVERT_STEPS,),
        in_specs=[smem, smem],
        out_specs=smem,
        out_shape=jax.ShapeDtypeStruct((n_rows,), jnp.int32),
        compiler_params=_cparams("arbitrary"),
        name="moe_invert",
    )(d0, d1)


def _ffn_fused_kernel(be_ref, ros_ref, x1_ref, wg0_ref, wu0_ref, wd0_ref, wg1_ref, wu1_ref, wd1_ref, out_ref,
                      xb0, xb1, yb0, yb1, gsem, ssem):
    del be_ref
    i = pl.program_id(0)
    last = pl.num_programs(0) - 1
    n_tok = x1_ref.shape[0]
    plane = n_tok + ROW_BLK

    def gather(blk, xbuf, sem):
        for r in range(ROW_BLK):
            tok = lax.shift_right_logical(ros_ref[blk * ROW_BLK + r], 16)
            pltpu.make_async_copy(x1_ref.at[pl.ds(tok, 1), :], xbuf.at[pl.ds(r, 1), :], sem).start()

    def scatter(blk, ybuf, sem):
        for r in range(ROW_BLK):
            row = ros_ref[blk * ROW_BLK + r] & 0xFFFF
            pltpu.make_async_copy(ybuf.at[pl.ds(r, 1), :], out_ref.at[pl.ds(row, 1), :], sem).start()

    def wait_rows(buf, sem):
        pltpu.make_async_copy(x1_ref.at[pl.ds(0, ROW_BLK), :], buf, sem).wait()

    def ffn(xbuf, ybuf, wg_ref, wu_ref, wd_ref):
        xb = xbuf[...].astype(BF16)
        a = _dot(xb, wg_ref[...].astype(BF16))
        u = _dot(xb, wu_ref[...].astype(BF16))
        hmid = (a * jax.nn.sigmoid(a) * u).astype(BF16)
        ybuf[...] = _dot(hmid, wd_ref[...].astype(BF16))

    b0 = 2 * i
    b1 = 2 * i + 1
    b2 = jnp.minimum(2 * i + 2, 2 * last + 1)

    @pl.when(i == 0)
    def _():
        yb0[...] = jnp.zeros(yb0.shape, F32)
        for base in (n_tok, plane + n_tok):
            spare = pltpu.make_async_copy(yb0, out_ref.at[pl.ds(base, ROW_BLK), :], ssem.at[0])
            spare.start()
            spare.wait()
        gather(0, xb0, gsem.at[0])

    wait_rows(xb0, gsem.at[0])
    gather(b1, xb1, gsem.at[1])

    @pl.when(i > 0)
    def _():
        wait_rows(yb0, ssem.at[0])

    ffn(xb0, yb0, wg0_ref, wu0_ref, wd0_ref)
    scatter(b0, yb0, ssem.at[0])
    wait_rows(xb1, gsem.at[1])
    gather(b2, xb0, gsem.at[0])

    @pl.when(i > 0)
    def _():
        wait_rows(yb1, ssem.at[1])

    ffn(xb1, yb1, wg1_ref, wu1_ref, wd1_ref)
    scatter(b1, yb1, ssem.at[1])

    @pl.when(i == last)
    def _():
        wait_rows(xb0, gsem.at[0])
        wait_rows(yb0, ssem.at[0])
        wait_rows(yb1, ssem.at[1])


def _ffn_fused(block_expert, ros, x1, wg, wu, wd, layer):
    T = x1.shape[0]
    n_blocks = ros.shape[0] // ROW_BLK
    assert n_blocks % 2 == 0
    w_spec = lambda shape, k: pl.BlockSpec((None, None) + shape, lambda i, be, ro: (layer, be[2 * i + k], 0, 0))
    grid_spec = pltpu.PrefetchScalarGridSpec(
        num_scalar_prefetch=2,
        grid=(n_blocks // 2,),
        in_specs=[pl.BlockSpec(memory_space=pl.ANY)]
        + [w_spec(s, k) for k in range(2) for s in ((D_MODEL, D_EXPERT), (D_MODEL, D_EXPERT), (D_EXPERT, D_MODEL))],
        out_specs=pl.BlockSpec(memory_space=pl.ANY),
        scratch_shapes=[pltpu.VMEM((ROW_BLK, D_MODEL), F32)] * 4 + [pltpu.SemaphoreType.DMA((2,))] * 2,
    )
    return pl.pallas_call(
        _ffn_fused_kernel,
        grid_spec=grid_spec,
        out_shape=jax.ShapeDtypeStruct((2 * (T + ROW_BLK), D_MODEL), F32),
        compiler_params=_cparams("arbitrary"),
        name="moe_ffn",
    )(block_expert, ros, x1, wg, wu, wd, wg, wu, wd)


def _combine2_kernel(y0_ref, y1_ref, x1_ref, slab_ref, g_ref, b_ref, o_ref):
    slab = slab_ref[...]
    y = slab[:, 4:5] * y0_ref[...] + slab[:, 5:6] * y1_ref[...]
    o_ref[...] = _layer_norm(DEEPNORM_ALPHA * x1_ref[...] + y, g_ref[...], b_ref[...])


def _combine2(out2, x1, slab, g, b):
    T = x1.shape[0]
    tm = ROW_BLK
    plane_blocks = (T + ROW_BLK) // tm
    row = lambda w: pl.BlockSpec((tm, w), lambda i: (i, 0))
    one = pl.BlockSpec((1, D_MODEL), lambda i: (0, 0))
    return pl.pallas_call(
        _combine2_kernel,
        grid=(T // tm,),
        in_specs=[row(D_MODEL), pl.BlockSpec((tm, D_MODEL), lambda i: (plane_blocks + i, 0)), row(D_MODEL), row(LANES),
                  one, one],
        out_specs=row(D_MODEL),
        out_shape=jax.ShapeDtypeStruct((T, D_MODEL), F32),
        compiler_params=_cparams("arbitrary"),
        name="moe_combine",
    )(out2, out2, x1, slab, g, b)


def _t5_bucket(dist):
    max_exact = N_BUCKETS // 2
    large = max_exact + (jnp.log(jnp.maximum(dist, 1).astype(F32) / max_exact)
                         / math.log(MAX_DISTANCE / max_exact) * (N_BUCKETS - max_exact)).astype(jnp.int32)
    return jnp.where(dist < max_exact, dist, jnp.minimum(large, N_BUCKETS - 1))


def _bias_table(table_cols, dil, max_dist):
    dist = BLK + jnp.arange(BLK)[:, None] - jnp.arange(2 * BLK)[None, :]
    bucket = _t5_bucket(jnp.maximum(dist * dil, 0))
    hit = bucket[None, :, :, None] == jnp.arange(N_BUCKETS)[None, None, None, :]
    bias = jnp.sum(jnp.where(hit, table_cols.T.astype(F32)[:, None, None, :], 0.0), axis=-1)
    mask = (dist >= 0) & (dist <= max_dist)
    return jnp.where(mask[None], bias, NEG_INF)


def _rope_tables(T):
    inv_freq = ROPE_THETA ** (-jnp.arange(0, C_ROPE, 2, dtype=F32) / C_ROPE)
    ang = jnp.arange(T, dtype=F32)[:, None] * inv_freq[None, :]
    cos, sin = jnp.cos(ang), jnp.sin(ang)
    ones = jnp.ones((T, C_NOPE), F32)
    zeros_n = jnp.zeros((T, C_NOPE), F32)
    zeros_p = jnp.zeros((T, C_PAD - C_NOPE - C_ROPE), F32)
    cs = jnp.concatenate([ones, cos, cos, zeros_p], axis=1)
    sn = jnp.concatenate([zeros_n, -sin, sin, zeros_p], axis=1)
    return cs, sn


def _swap_halves(w):
    half = w.shape[-1] // 2
    return jnp.concatenate([w[..., half:], w[..., :half]], axis=-1)


def _pad_cols(w, width):
    return jnp.pad(w, ((0, 0), (0, width - w.shape[1])))


def _prep_layer_weights(w_in, w_out, w_uq, w_ukv, w_rg, b_rg, w_re, b_re):
    head_cols = lambda w, h: w[:, h * HEAD_DIM:(h + 1) * HEAD_DIM]
    wab = jnp.concatenate([head_cols(w_in, h) for h in A_HEAD_ORDER] + [w_in[:, A_Q_HEADS * HEAD_DIM:AB_COLS]],
                          axis=1).astype(BF16)
    c0 = AB_COLS
    w_cq = w_in[:, c0:c0 + Q_LORA]
    w_ckv = w_in[:, c0 + Q_LORA:c0 + Q_LORA + KV_LORA]
    w_kr = w_in[:, c0 + Q_LORA + KV_LORA:]
    lead = jnp.zeros((D_MODEL, C_NOPE), F32)
    wc = jnp.concatenate([w_cq, w_ckv,
                          _pad_cols(jnp.concatenate([lead, w_kr], 1), C_PAD),
                          _pad_cols(jnp.concatenate([lead, _swap_halves(w_kr)], 1), C_PAD)], axis=1).astype(BF16)
    uq = w_uq.reshape(Q_LORA, C_HEADS, C_NOPE + C_ROPE)
    wq = jnp.pad(uq, ((0, 0), (0, 0), (0, C_PAD - C_NOPE - C_ROPE))).reshape(Q_LORA, C_HEADS * C_PAD).astype(BF16)
    uqs = jnp.concatenate([jnp.zeros((Q_LORA, C_HEADS, C_NOPE), F32), _swap_halves(uq[..., C_NOPE:])], axis=-1)
    wqs = jnp.pad(uqs, ((0, 0), (0, 0), (0, C_PAD - C_NOPE - C_ROPE))).reshape(Q_LORA, C_HEADS * C_PAD).astype(BF16)
    ukv = w_ukv.reshape(KV_LORA, C_HEADS, C_NOPE + C_V)
    wk = jnp.pad(ukv[..., :C_NOPE], ((0, 0), (0, 0), (0, C_PAD - C_NOPE))).reshape(KV_LORA, C_HEADS * C_PAD)
    wv = jnp.pad(ukv[..., C_NOPE:], ((0, 0), (0, 0), (0, C_PAD - C_V))).reshape(KV_LORA, C_HEADS * C_PAD)
    woa = jnp.concatenate([w_out[h * HEAD_DIM:(h + 1) * HEAD_DIM] for h in A_HEAD_ORDER],
                          axis=0).reshape(2, LANES, D_MODEL).astype(BF16)
    wob = w_out[256:512].reshape(2, LANES, D_MODEL).astype(BF16)
    woc = w_out[512:].astype(BF16)
    wr = _pad_cols(jnp.concatenate([w_rg, w_re.transpose(1, 0, 2).reshape(D_MODEL, N_EXPERTS)], axis=1), LANES)
    wr_hi = wr.astype(BF16)
    wr_lo = (wr - wr_hi.astype(F32)).astype(BF16)
    wr = jnp.concatenate([wr_hi, wr_hi, wr_lo], axis=0)
    br = _pad_cols(jnp.concatenate([b_rg, b_re.reshape(N_EXPERTS)])[None, :], LANES)
    return dict(wab=wab, wc=wc, wq=wq, wqs=wqs, wk=wk.astype(BF16), wv=wv.astype(BF16),
                woa=woa, wob=wob, woc=woc, wr=wr, br=br)


def kernel(x, w_in, w_out, sinks, rel_bias, mla_q_norm, mla_kv_norm, w_uq, w_ukv, ln1_g, ln1_b,
           w_route_group, b_route_group, w_route_expert, b_route_expert,
           w_expert_gate, w_expert_up, w_expert_down, ln2_g, ln2_b):
    Bsz, T, D = x.shape
    assert Bsz == 1 and D == D_MODEL and T % CHUNK == 0
    xt = x.reshape(T, D)
    cs, sn = _rope_tables(T)
    bias_a = _bias_table(rel_bias[:, :A_Q_HEADS], 1, SWA_WINDOW - 1)
    bias_b = [_bias_table(rel_bias[:, A_Q_HEADS + g * B_HEADS_PER_GROUP:A_Q_HEADS + (g + 1) * B_HEADS_PER_GROUP],
                          dil, window // dil) for g, (window, dil) in enumerate(DILATED_PAIRS)]
    n_assign = T * TOP_K
    n_blocks = n_assign // ROW_BLK + N_EXPERTS
    n_rows = n_blocks * ROW_BLK
    no_sink = jnp.zeros((B_HEADS_PER_GROUP,), F32)

    for layer in range(w_in.shape[0]):
        w = _prep_layer_weights(w_in[layer], w_out[layer], w_uq[layer], w_ukv[layer],
                                w_route_group[layer], b_route_group[layer],
                                w_route_expert[layer], b_route_expert[layer])
        hab, qt, k, vt = _in_proj(xt, w["wab"], w["wc"], w["wq"], w["wqs"], w["wk"], w["wv"],
                                mla_q_norm[layer][None, :], mla_kv_norm[layer][None, :], cs, sn)
        oa, _ = _banded(hab, sinks[layer], bias_a, q_col=0, k_col=2, v_col=3, dil=1,
                        n_heads=A_Q_HEADS, group=A_Q_HEADS // A_KV_HEADS, has_sink=True, want_lse=False)
        ob, lse = [], []
        for g, (_, dil) in enumerate(DILATED_PAIRS):
            o_g, lse_g = _banded(hab, no_sink, bias_b[g], q_col=4 + 2 * g, k_col=10 + 2 * g, v_col=16 + 2 * g,
                                 dil=dil, n_heads=B_HEADS_PER_GROUP, group=1, has_sink=False, want_lse=True)
            ob += o_g
            lse += lse_g
        oc = _flash(qt, k, vt)
        x1, slab, cnt = _out_router(oa, ob, lse, oc, xt, w["woa"], w["wob"], w["woc"],
                                    ln1_g[layer][None, :], ln1_b[layer][None, :], w["wr"], w["br"])

        counts = cnt[0, N_GROUPS:N_GROUPS + N_EXPERTS].astype(jnp.int32)
        padded = ((counts + ROW_BLK - 1) // ROW_BLK) * ROW_BLK
        pad_end = jnp.cumsum(padded)
        pad_start = pad_end - padded
        eids = jnp.arange(N_EXPERTS, dtype=jnp.int32)
        pick = lambda e: jnp.sum(jnp.where(e.astype(jnp.int32)[:, None] == eids[None, :], pad_start[None, :], 0), axis=1)
        d0 = pick(slab[:, 0]) + slab[:, 2].astype(jnp.int32)
        d1 = pick(slab[:, 1]) + slab[:, 3].astype(jnp.int32)
        block_start = jnp.arange(n_blocks, dtype=jnp.int32) * ROW_BLK
        block_expert = jnp.minimum(jnp.sum((pad_end[None, :] <= block_start[:, None]).astype(jnp.int32), axis=1),
                                   N_EXPERTS - 1)

        n_used = (pad_end[-1:] // ROW_BLK).astype(jnp.int32)
        xs = _dispatch(d0, d1, pad_start + counts, padded - counts, n_used, x1, n_rows)
        ys = _ffn(block_expert, n_used, xs, w_expert_gate, w_expert_up, w_expert_down, layer)
        xt = _combine(d0, d1, ys, x1, slab, ln2_g[layer][None, :], ln2_b[layer][None, :])
    return xt.reshape(Bsz, T, D)
```

```python
import functools
import math

import jax
import jax.numpy as jnp
from jax import lax
from jax.experimental import pallas as pl
from jax.experimental.pallas import tpu as pltpu

F32 = jnp.float32
BF16 = jnp.bfloat16

D_MODEL = 1024
DEPTH = 4
HEAD_DIM = 64
BLK = 128
NEG_INF = -1e30

SWA_WINDOW = 128
A_Q_HEADS = 4
A_KV_HEADS = 2
A_COLS = (A_Q_HEADS + 2 * A_KV_HEADS) * HEAD_DIM
A_HEAD_ORDER = (0, 2, 1, 3)

DILATED_PAIRS = ((128, 1), (512, 4), (2048, 16))
B_HEADS_PER_GROUP = 4
B_HEADS = len(DILATED_PAIRS) * B_HEADS_PER_GROUP
B_COLS = 3 * B_HEADS * HEAD_DIM
AB_COLS = A_COLS + B_COLS

C_HEADS = 8
C_NOPE = 64
C_ROPE = 32
C_V = 64
Q_LORA = 256
KV_LORA = 128
ROPE_THETA = 10000.0
C_PAD = 128
C_VROWS = 80

N_BUCKETS = 32
MAX_DISTANCE = 2048

N_GROUPS = 4
EXPERTS_PER_GROUP = 8
N_EXPERTS = N_GROUPS * EXPERTS_PER_GROUP
TOP_K = 2
D_EXPERT = 256
ROW_BLK = 512

DEEPNORM_ALPHA = (2 * DEPTH) ** 0.25

VMEM_LIMIT = 56 * 1024 * 1024

CHUNK = 2048
TM_IN = 512
TM_OUT = 512
TQ = 1024
TK = 512
N_DIAG = TQ // TK
FLASH_UNROLL = 4
LANES = 128


def _dot(a, b):
    return jnp.dot(a, b, preferred_element_type=F32)


def _dot_nt(a, b):
    return lax.dot_general(a, b, (((1,), (1,)), ((), ())), preferred_element_type=F32)


def _bdot(a, b):
    return lax.dot_general(a, b, (((2,), (1,)), ((0,), (0,))), preferred_element_type=F32)


def _bdot_nt(a, b):
    return lax.dot_general(a, b, (((2,), (2,)), ((0,), (0,))), preferred_element_type=F32)


def _cparams(*sem):
    return pltpu.CompilerParams(dimension_semantics=sem, vmem_limit_bytes=VMEM_LIMIT)


def _in_proj_kernel(x_ref, wab_ref, wc_ref, wq_ref, wqs_ref, wk_ref, wv_ref, gq_ref, gkv_ref, cs_ref, sn_ref,
                    hab_ref, qt_ref, k_ref, vt_ref):
    xb = x_ref[...].astype(BF16)
    hab_ref[...] = _dot(xb, wab_ref[...])
    hc = _dot(xb, wc_ref[...])
    cq = hc[:, :Q_LORA]
    ckv = hc[:, Q_LORA:Q_LORA + KV_LORA]
    kr = hc[:, Q_LORA + KV_LORA:Q_LORA + KV_LORA + C_PAD]
    krs = hc[:, Q_LORA + KV_LORA + C_PAD:]
    cqn = (cq * lax.rsqrt(jnp.mean(cq * cq, -1, keepdims=True) + 1e-6) * gq_ref[...]).astype(BF16)
    ckvn = (ckv * lax.rsqrt(jnp.mean(ckv * ckv, -1, keepdims=True) + 1e-6) * gkv_ref[...]).astype(BF16)
    cs = cs_ref[...]
    sn = sn_ref[...]
    q = _dot(cqn, wq_ref[...])
    qs = _dot(cqn, wqs_ref[...])
    k = _dot(ckvn, wk_ref[...])
    v = _dot(ckvn, wv_ref[...])
    krot = kr * cs + krs * sn
    scale = (C_NOPE + C_ROPE) ** -0.5 * math.log2(math.e)
    for h in range(C_HEADS):
        sl = slice(h * C_PAD, (h + 1) * C_PAD)
        qh = (q[:, sl] * cs + qs[:, sl] * sn) * scale
        qt_ref[sl, :] = qh.T.astype(BF16)
        k_ref[h] = (k[:, sl] + krot).astype(BF16)
    one_row = (lax.broadcasted_iota(jnp.int32, (1, C_PAD), 1) == C_V).astype(F32)
    for h in range(C_HEADS):
        vh = v[:, h * C_PAD:(h + 1) * C_PAD] + one_row
        vt_ref[h * C_VROWS:(h + 1) * C_VROWS, :] = vh.T[:C_VROWS, :].astype(BF16)


def _in_proj(x, wab, wc, wq, wqs, wk, wv, gq, gkv, cs, sn):
    T = x.shape[0]
    tm = TM_IN
    full = lambda a: pl.BlockSpec(a.shape, lambda i: (0,) * a.ndim)
    row = lambda w: pl.BlockSpec((tm, w), lambda i: (i, 0))
    colblk = lambda r: pl.BlockSpec((r, tm), lambda i: (0, i))
    return pl.pallas_call(
        _in_proj_kernel,
        grid=(T // tm,),
        in_specs=[row(D_MODEL), full(wab), full(wc), full(wq), full(wqs), full(wk), full(wv), full(gq), full(gkv),
                  row(C_PAD), row(C_PAD)],
        out_specs=[row(AB_COLS), colblk(C_HEADS * C_PAD), pl.BlockSpec((C_HEADS, tm, C_PAD), lambda i: (0, i, 0)),
                   colblk(C_HEADS * C_VROWS)],
        out_shape=[jax.ShapeDtypeStruct((T, AB_COLS), F32), jax.ShapeDtypeStruct((C_HEADS * C_PAD, T), BF16),
                   jax.ShapeDtypeStruct((C_HEADS, T, C_PAD), BF16),
                   jax.ShapeDtypeStruct((C_HEADS * C_VROWS, T), BF16)],
        compiler_params=_cparams("arbitrary"),
        name="in_proj",
    )(x, wab, wc, wq, wqs, wk, wv, gq, gkv, cs, sn)


def _banded_kernel(sink_ref, *refs, dil, n_heads, group, has_sink, want_lse):
    nq = n_heads // 2
    nkv = n_heads // group // 2
    q_refs, refs = refs[:nq], refs[nq:]
    k_refs, v_refs, kp_refs, vp_refs = (refs[i * nkv:(i + 1) * nkv] for i in range(4))
    bias_ref = refs[4 * nkv]
    o_refs = refs[4 * nkv + 1:4 * nkv + 1 + nq]
    lse_refs = refs[4 * nkv + 1 + nq:]
    c = pl.program_id(0)
    span = BLK * dil
    nb = CHUNK // span
    n_combo = nb * dil
    starts = [b * span + r for b in range(nb) for r in range(dil)]
    scale = HEAD_DIM ** -0.5

    def rows(start):
        return pl.ds(start, BLK) if dil == 1 else pl.ds(start, BLK, stride=dil)

    def gather(ref):
        return jnp.stack([ref[rows(s), :] for s in starts], axis=0)

    def with_prev(ref, prev_ref):
        cur = gather(ref)
        first = jnp.stack([prev_ref[rows(r), :] for r in range(dil)], axis=0)
        prev = jnp.concatenate([first, cur[:n_combo - dil]], axis=0) if dil < n_combo else first
        return jnp.concatenate([prev, cur], axis=1).astype(BF16)

    q = [gather(ref).astype(BF16) for ref in q_refs]
    kcat = [with_prev(k_refs[j], kp_refs[j]) for j in range(nkv)]
    vcat = [with_prev(v_refs[j], vp_refs[j]) for j in range(nkv)]

    lane = lax.broadcasted_iota(jnp.int32, (1, 1, LANES), 2)
    low = lane < HEAD_DIM
    combo = lax.broadcasted_iota(jnp.int32, (n_combo, 1, 2 * BLK), 0)
    col = lax.broadcasted_iota(jnp.int32, (n_combo, 1, 2 * BLK), 2)
    no_prev = jnp.logical_and(c == 0, jnp.logical_and(combo < dil, col < BLK))

    for j in range(nq):
        o_par, lse_par = [], []
        for par in range(2):
            h = 2 * j + par if group == 1 else group * par + j
            kvb = j if group == 1 else 0
            keep = low if par == 0 else jnp.logical_not(low)
            kk = jnp.where(keep, kcat[kvb], jnp.zeros((), BF16))
            s = _bdot_nt(q[j], kk)
            s = s * scale + bias_ref[h][None]
            s = jnp.where(no_prev, NEG_INF, s)
            m = jnp.max(s, axis=-1, keepdims=True)
            if has_sink:
                snk = sink_ref[h]
                m = jnp.maximum(m, snk)
            e = jnp.exp(s - m)
            l = jnp.sum(e, axis=-1, keepdims=True)
            if has_sink:
                l = l + jnp.exp(snk - m)
            p = (e * (1.0 / l)).astype(BF16)
            o_par.append(_bdot(p, vcat[kvb]))
            if want_lse:
                lse_par.append(m + jnp.log(l))
        o_pair = jnp.where(low, o_par[0], o_par[1])
        if want_lse:
            lse_pair = jnp.where(low, lse_par[0], lse_par[1])
        for ci, s0 in enumerate(starts):
            o_refs[j][rows(s0), :] = o_pair[ci]
            if want_lse:
                lse_refs[j][rows(s0), :] = lse_pair[ci]


def _banded(hab, sink, bias, *, q_col, k_col, v_col, dil, n_heads, group, has_sink, want_lse):
    T = hab.shape[0]
    span = BLK * dil
    nb = CHUNK // span
    nq = n_heads // 2
    nkv = n_heads // group // 2
    kernel = functools.partial(_banded_kernel, dil=dil, n_heads=n_heads, group=group,
                               has_sink=has_sink, want_lse=want_lse)
    prev = lambda cb: pl.BlockSpec((span, LANES), lambda c, s: (jnp.maximum(c * nb - 1, 0), cb))
    cur = lambda cb: pl.BlockSpec((CHUNK, LANES), lambda c, s: (c, cb))
    in_specs = ([cur(q_col + j) for j in range(nq)] + [cur(k_col + j) for j in range(nkv)]
                + [cur(v_col + j) for j in range(nkv)] + [prev(k_col + j) for j in range(nkv)]
                + [prev(v_col + j) for j in range(nkv)] + [pl.BlockSpec(bias.shape, lambda c, s: (0, 0, 0))])
    n_out = nq * (2 if want_lse else 1)
    grid_spec = pltpu.PrefetchScalarGridSpec(
        num_scalar_prefetch=1,
        grid=(T // CHUNK,),
        in_specs=in_specs,
        out_specs=[pl.BlockSpec((CHUNK, LANES), lambda c, s: (c, 0))] * n_out,
    )
    outs = pl.pallas_call(
        kernel,
        grid_spec=grid_spec,
        out_shape=[jax.ShapeDtypeStruct((T, LANES), F32)] * n_out,
        compiler_params=_cparams("arbitrary"),
        name=f"banded_d{dil}" + ("_sink" if has_sink else ""),
    )(sink, *([hab] * (nq + 4 * nkv)), bias)
    return outs[:nq], outs[nq:]


def _flash_kernel(qt_ref, k_ref, vt_ref, ot_ref, m_ref, acc_ref, s_ref):
    qi = pl.program_id(1)
    qt = qt_ref[...]
    m_ref[...] = jnp.full(m_ref.shape, NEG_INF, F32)
    acc_ref[...] = jnp.zeros(acc_ref.shape, F32)

    def scores(j):
        off = pl.multiple_of(j * TK, TK)
        return _dot(k_ref[pl.ds(off, TK), :], qt)

    def accumulate(j, s):
        off = pl.multiple_of(j * TK, TK)
        m_old = m_ref[...]
        m_new = jnp.maximum(m_old, jnp.max(s, axis=0, keepdims=True))
        alpha = jnp.exp2(m_old - m_new)
        p = jnp.exp2(s - m_new).astype(BF16)
        acc_ref[...] = alpha * acc_ref[...] + _dot(vt_ref[:, pl.ds(off, TK)], p)
        m_ref[...] = m_new

    def diagonal(s, d):
        key = lax.broadcasted_iota(jnp.int32, (TK, TQ), 0)
        qry = lax.broadcasted_iota(jnp.int32, (TK, TQ), 1)
        return jnp.where(key + d * TK <= qry, s, NEG_INF)

    n_full = qi * N_DIAG
    s_ref[0] = scores(0)
    n_main = n_full // FLASH_UNROLL

    def main_body(i, carry):
        j = FLASH_UNROLL * i
        for u in range(FLASH_UNROLL):
            s_ref[(u + 1) % 2] = scores(j + u + 1)
            accumulate(j + u, s_ref[u % 2])
        return carry

    lax.fori_loop(0, n_main, main_body, 0)

    base = FLASH_UNROLL * n_main
    for rem in range(0, FLASH_UNROLL, math.gcd(FLASH_UNROLL, N_DIAG)):
        @pl.when(n_full - base == rem)
        def _():
            n_tail = rem + N_DIAG
            for u in range(n_tail):
                if u + 1 < n_tail:
                    s_ref[(u + 1) % 2] = scores(base + u + 1)
                s = s_ref[u % 2]
                accumulate(base + u, s if u < rem else diagonal(s, u - rem))

    ot_ref[...] = (acc_ref[:C_V, :] / acc_ref[C_V:C_V + 1, :]).astype(ot_ref.dtype)


def _flash(qt, k, vt):
    H, T, _ = k.shape
    return pl.pallas_call(
        _flash_kernel,
        grid=(H, T // TQ),
        in_specs=[pl.BlockSpec((C_PAD, TQ), lambda h, i: (h, i)),
                  pl.BlockSpec((None, T, C_PAD), lambda h, i: (h, 0, 0)),
                  pl.BlockSpec((C_VROWS, T), lambda h, i: (h, 0))],
        out_specs=pl.BlockSpec((C_V, TQ), lambda h, i: (h, i)),
        out_shape=jax.ShapeDtypeStruct((H * C_V, T), BF16),
        scratch_shapes=[pltpu.VMEM((1, TQ), F32), pltpu.VMEM((C_VROWS, TQ), F32), pltpu.VMEM((2, TK, TQ), F32)],
        compiler_params=_cparams("arbitrary", "arbitrary"),
        name="mla_flash",
    )(qt, k, vt)


def _layer_norm(z, g, b):
    mu = jnp.mean(z, -1, keepdims=True)
    zc = z - mu
    var = jnp.mean(zc * zc, -1, keepdims=True)
    return zc * lax.rsqrt(var + 1e-5) * g + b


def _out_router_kernel(*refs):
    n_ab = 2 + 6 + 6
    oa_refs, ob_refs, lse_refs = refs[0:2], refs[2:8], refs[8:14]
    (oc_ref, x_ref, woa_ref, wob_ref, woc_ref, g_ref, b_ref, wr_ref, br_ref,
     x1_ref, slab_ref, cnt_ref, carry_ref, prev_ref) = refs[n_ab:]
    i = pl.program_id(0)
    tm = x_ref.shape[0]

    @pl.when(i == 0)
    def _():
        carry_ref[...] = jnp.zeros(carry_ref.shape, F32)
        prev_ref[...] = jnp.zeros(prev_ref.shape, F32)

    xr = prev_ref[...]

    y = None
    for j in range(2):
        l0, l1, l2 = (lse_refs[2 * g + j][...] for g in range(3))
        mx = jnp.maximum(jnp.maximum(l0, l1), l2)
        e0, e1, e2 = jnp.exp(l0 - mx), jnp.exp(l1 - mx), jnp.exp(l2 - mx)
        den = e0 + e1 + e2
        ob = ((e0 / den) * ob_refs[j][...] + (e1 / den) * ob_refs[2 + j][...] + (e2 / den) * ob_refs[4 + j][...])
        part = _dot(oa_refs[j][...].astype(BF16), woa_ref[j]) + _dot(ob.astype(BF16), wob_ref[j])
        y = part if y is None else y + part
    y = y + _dot(oc_ref[...].astype(F32).T.astype(BF16), woc_ref[...])
    x1 = _layer_norm(DEEPNORM_ALPHA * x_ref[...] + y, g_ref[...], b_ref[...])
    x1_ref[...] = x1
    prev_ref[...] = x1

    xr_hi = xr.astype(BF16)
    xr_lo = (xr - xr_hi.astype(F32)).astype(BF16)
    logits = _dot(jnp.concatenate([xr_hi, xr_lo, xr_hi], axis=1), wr_ref[...]) + br_ref[...]
    lane = lax.broadcasted_iota(jnp.int32, (tm, LANES), 1)
    ninf = jnp.float32(-jnp.inf)
    gmask = lane < N_GROUPS
    gl = jnp.where(gmask, logits, ninf)
    gmax = jnp.max(gl, axis=-1, keepdims=True)
    grp = jnp.min(jnp.where(gl == gmax, lane, LANES), axis=-1, keepdims=True)
    p_grp = 1.0 / jnp.sum(jnp.where(gmask, jnp.exp(logits - gmax), 0.0), axis=-1, keepdims=True)
    lo = N_GROUPS + EXPERTS_PER_GROUP * grp
    emask = jnp.logical_and(lane >= lo, lane < lo + EXPERTS_PER_GROUP)
    el = jnp.where(emask, logits, ninf)
    v0 = jnp.max(el, axis=-1, keepdims=True)
    i0 = jnp.min(jnp.where(el == v0, lane, LANES), axis=-1, keepdims=True)
    el2 = jnp.where(lane == i0, ninf, el)
    v1 = jnp.max(el2, axis=-1, keepdims=True)
    i1 = jnp.min(jnp.where(el2 == v1, lane, LANES), axis=-1, keepdims=True)
    t = jnp.exp(v1 - v0)
    gate0 = p_grp / (1.0 + t)
    gate1 = p_grp * t / (1.0 + t)

    hit0 = lane == i0
    hit1 = lane == i1
    cmat = jnp.logical_or(hit0, hit1).astype(BF16)
    tr = lax.broadcasted_iota(jnp.int32, (tm, tm), 0)
    tc = lax.broadcasted_iota(jnp.int32, (tm, tm), 1)
    tril = (tc < tr).astype(BF16)
    prefix = _dot(tril, cmat) + carry_ref[...]
    r0 = jnp.sum(jnp.where(hit0, prefix, 0.0), axis=-1, keepdims=True)
    r1 = jnp.sum(jnp.where(hit1, prefix, 0.0), axis=-1, keepdims=True)
    counted = (i > 0).astype(F32)
    carry_ref[...] = carry_ref[...] + counted * jnp.sum(cmat.astype(F32), axis=0, keepdims=True)
    cnt_ref[...] = carry_ref[...]

    fields = [(i0 - N_GROUPS).astype(F32), (i1 - N_GROUPS).astype(F32), r0, r1, gate0, gate1]
    slab = jnp.zeros((tm, LANES), F32)
    for n, f in enumerate(fields):
        slab = jnp.where(lane == n, f, slab)
    slab_ref[...] = slab


def _out_router(oa, ob, lse, oc, x, woa, wob, woc, g, b, wr, br):
    T = x.shape[0]
    tm = TM_OUT
    n = T // tm
    full = lambda a: pl.BlockSpec(a.shape, lambda i: (0,) * a.ndim)
    row = lambda w: pl.BlockSpec((tm, w), lambda i: (jnp.minimum(i, n - 1), 0))
    lag = pl.BlockSpec((tm, LANES), lambda i: (jnp.maximum(i - 1, 0), 0))
    return pl.pallas_call(
        _out_router_kernel,
        grid=(n + 1,),
        in_specs=[row(LANES)] * 14 + [pl.BlockSpec((C_HEADS * C_V, tm), lambda i: (0, jnp.minimum(i, n - 1))),
                                      row(D_MODEL),
                                      full(woa), full(wob), full(woc), full(g), full(b), full(wr), full(br)],
        out_specs=[row(D_MODEL), lag, pl.BlockSpec((1, LANES), lambda i: (0, 0))],
        out_shape=[jax.ShapeDtypeStruct((T, D_MODEL), F32), jax.ShapeDtypeStruct((T, LANES), F32),
                   jax.ShapeDtypeStruct((1, LANES), F32)],
        scratch_shapes=[pltpu.VMEM((1, LANES), F32), pltpu.VMEM((tm, D_MODEL), F32)],
        compiler_params=_cparams("arbitrary"),
        name="out_router",
    )(*oa, *ob, *lse, oc, x, woa, wob, woc, g, b, wr, br)


SUBLANES = 8
PAD_BITS = tuple(1 << k for k in reversed(range(SUBLANES.bit_length() - 1, ROW_BLK.bit_length() - 1)))


def _dispatch_kernel(d0_ref, d1_ref, pad_at_ref, pad_len_ref, nu_ref, x_ref, xs_ref, zero_ref, sem, zsem):
    i = pl.program_id(0)
    tm = x_ref.shape[0]
    base = i * tm

    @pl.when(i == 0)
    def _():
        zero_ref[...] = jnp.zeros(zero_ref.shape, F32)

        def pad_copies(e):
            n = pad_len_ref[e]
            at = pad_at_ref[e]
            head = n & (SUBLANES - 1)
            out = [(j < head, pltpu.make_async_copy(zero_ref.at[pl.ds(0, 1), :], xs_ref.at[pl.ds(at + j, 1), :], zsem))
                   for j in range(SUBLANES - 1)]
            body = n - head
            for bit in PAD_BITS:
                start = pl.multiple_of(at + head + (body & ~(2 * bit - 1)), SUBLANES)
                out.append(((body & bit) != 0,
                            pltpu.make_async_copy(zero_ref.at[pl.ds(0, bit), :], xs_ref.at[pl.ds(start, bit), :], zsem)))
            return out

        def spare_copies():
            n_blocks = xs_ref.shape[0] // ROW_BLK
            first = d0_ref.shape[0] * TOP_K // ROW_BLK
            return [(b >= nu_ref[0], pltpu.make_async_copy(zero_ref, xs_ref.at[pl.ds(b * ROW_BLK, ROW_BLK), :], zsem))
                    for b in range(first, n_blocks)]

        for e in range(N_EXPERTS):
            for on, cp in pad_copies(e):
                pl.when(on)(cp.start)
        for on, cp in spare_copies():
            pl.when(on)(cp.start)
        for e in range(N_EXPERTS):
            for on, cp in pad_copies(e):
                pl.when(on)(cp.wait)
        for on, cp in spare_copies():
            pl.when(on)(cp.wait)

    def copy(t, d_ref):
        return pltpu.make_async_copy(x_ref.at[pl.ds(t, 1), :], xs_ref.at[pl.ds(d_ref[base + t], 1), :], sem)

    def start(t, carry):
        copy(t, d0_ref).start()
        copy(t, d1_ref).start()
        return carry

    lax.fori_loop(0, tm, start, 0, unroll=8)
    for _ in range(2):
        pltpu.make_async_copy(x_ref, xs_ref.at[pl.ds(0, tm), :], sem).wait()


def _dispatch(d0, d1, pad_at, pad_len, n_used, x1, n_rows):
    T = x1.shape[0]
    tm = ROW_BLK
    grid_spec = pltpu.PrefetchScalarGridSpec(
        num_scalar_prefetch=5,
        grid=(T // tm,),
        in_specs=[pl.BlockSpec((tm, D_MODEL), lambda i, *_: (i, 0))],
        out_specs=pl.BlockSpec(memory_space=pl.ANY),
        scratch_shapes=[pltpu.VMEM((ROW_BLK, D_MODEL), F32), pltpu.SemaphoreType.DMA, pltpu.SemaphoreType.DMA],
    )
    return pl.pallas_call(
        _dispatch_kernel,
        grid_spec=grid_spec,
        out_shape=jax.ShapeDtypeStruct((n_rows, D_MODEL), F32),
        compiler_params=_cparams("arbitrary"),
        name="moe_dispatch",
    )(d0, d1, pad_at, pad_len, n_used, x1)


FFN_BLOCKS = 1


def _ffn_kernel(be_ref, nu_ref, xs_ref, *refs):
    del be_ref
    w_refs, ys_ref = refs[:-1], refs[-1]
    i = pl.program_id(0)
    for k in range(FFN_BLOCKS):
        wg_ref, wu_ref, wd_ref = w_refs[3 * k:3 * k + 3]
        rows = pl.ds(k * ROW_BLK, ROW_BLK)
        used = FFN_BLOCKS * i + k < nu_ref[0]

        @pl.when(used)
        def _():
            xb = xs_ref[rows, :].astype(BF16)
            a = _dot(xb, wg_ref[...].astype(BF16))
            u = _dot(xb, wu_ref[...].astype(BF16))
            hmid = (a * jax.nn.sigmoid(a) * u).astype(BF16)
            ys_ref[rows, :] = _dot(hmid, wd_ref[...].astype(BF16))

        @pl.when(jnp.logical_not(used))
        def _():
            ys_ref[rows, :] = jnp.zeros((ROW_BLK, D_MODEL), F32)


def _ffn(block_expert, n_used, xs, wg, wu, wd, layer):
    n_rows = xs.shape[0]
    step_rows = FFN_BLOCKS * ROW_BLK
    assert n_rows % step_rows == 0
    w_spec = lambda shape, k: pl.BlockSpec((None, None) + shape,
                                           lambda i, be, nu: (layer, be[FFN_BLOCKS * i + k], 0, 0))
    w_specs = [w_spec(s, k) for k in range(FFN_BLOCKS)
               for s in ((D_MODEL, D_EXPERT), (D_MODEL, D_EXPERT), (D_EXPERT, D_MODEL))]
    grid_spec = pltpu.PrefetchScalarGridSpec(
        num_scalar_prefetch=2,
        grid=(n_rows // step_rows,),
        in_specs=[pl.BlockSpec((step_rows, D_MODEL),
                               lambda i, be, nu: (jnp.minimum(i, (nu[0] - 1) // FFN_BLOCKS), 0))] + w_specs,
        out_specs=pl.BlockSpec((step_rows, D_MODEL), lambda i, be, nu: (i, 0)),
    )
    return pl.pallas_call(
        _ffn_kernel,
        grid_spec=grid_spec,
        out_shape=jax.ShapeDtypeStruct((n_rows, D_MODEL), F32),
        compiler_params=_cparams("arbitrary"),
        name="moe_ffn",
    )(block_expert, n_used, xs, *([wg, wu, wd] * FFN_BLOCKS))


def _combine_kernel(d0_ref, d1_ref, ys_ref, x1_ref, slab_ref, g_ref, b_ref, o_ref, buf_ref, sem):
    i = pl.program_id(0)
    n = pl.num_programs(0)
    tm = x1_ref.shape[0]

    def gather(tile):
        slot = tile % 2
        base = tile * tm

        def start(t, carry):
            for k, d_ref in enumerate((d0_ref, d1_ref)):
                pltpu.make_async_copy(ys_ref.at[pl.ds(d_ref[base + t], 1), :],
                                      buf_ref.at[slot, k, pl.ds(t, 1), :], sem.at[slot]).start()
            return carry

        lax.fori_loop(0, tm, start, 0, unroll=8)

    @pl.when(i == 0)
    def _():
        gather(0)

    @pl.when(i + 1 < n)
    def _():
        gather(i + 1)

    slot = i % 2
    for k in range(2):
        pltpu.make_async_copy(ys_ref.at[pl.ds(0, tm), :], buf_ref.at[slot, k], sem.at[slot]).wait()
    slab = slab_ref[...]
    y = slab[:, 4:5] * buf_ref[slot, 0] + slab[:, 5:6] * buf_ref[slot, 1]
    o_ref[...] = _layer_norm(DEEPNORM_ALPHA * x1_ref[...] + y, g_ref[...], b_ref[...])


def _combine(d0, d1, ys, x1, slab, g, b):
    T = x1.shape[0]
    tm = ROW_BLK
    grid_spec = pltpu.PrefetchScalarGridSpec(
        num_scalar_prefetch=2,
        grid=(T // tm,),
        in_specs=[pl.BlockSpec(memory_space=pl.ANY),
                  pl.BlockSpec((tm, D_MODEL), lambda i, a, c: (i, 0)),
                  pl.BlockSpec((tm, LANES), lambda i, a, c: (i, 0)),
                  pl.BlockSpec((1, D_MODEL), lambda i, a, c: (0, 0)),
                  pl.BlockSpec((1, D_MODEL), lambda i, a, c: (0, 0))],
        out_specs=pl.BlockSpec((tm, D_MODEL), lambda i, a, c: (i, 0)),
        scratch_shapes=[pltpu.VMEM((2, 2, tm, D_MODEL), F32), pltpu.SemaphoreType.DMA((2,))],
    )
    return pl.pallas_call(
        _combine_kernel,
        grid_spec=grid_spec,
        out_shape=jax.ShapeDtypeStruct((T, D_MODEL), F32),
        compiler_params=_cparams("arbitrary"),
        name="moe_combine",
    )(d0, d1, ys, x1, slab, g, b)


def _t5_bucket(dist):
    max_exact = N_BUCKETS // 2
    large = max_exact + (jnp.log(jnp.maximum(dist, 1).astype(F32) / max_exact)
                         / math.log(MAX_DISTANCE / max_exact) * (N_BUCKETS - max_exact)).astype(jnp.int32)
    return jnp.where(dist < max_exact, dist, jnp.minimum(large, N_BUCKETS - 1))


def _bias_table(table_cols, dil, max_dist):
    dist = BLK + jnp.arange(BLK)[:, None] - jnp.arange(2 * BLK)[None, :]
    bucket = _t5_bucket(jnp.maximum(dist * dil, 0))
    hit = bucket[None, :, :, None] == jnp.arange(N_BUCKETS)[None, None, None, :]
    bias = jnp.sum(jnp.where(hit, table_cols.T.astype(F32)[:, None, None, :], 0.0), axis=-1)
    mask = (dist >= 0) & (dist <= max_dist)
    return jnp.where(mask[None], bias, NEG_INF)


def _rope_tables(T):
    inv_freq = ROPE_THETA ** (-jnp.arange(0, C_ROPE, 2, dtype=F32) / C_ROPE)
    ang = jnp.arange(T, dtype=F32)[:, None] * inv_freq[None, :]
    cos, sin = jnp.cos(ang), jnp.sin(ang)
    ones = jnp.ones((T, C_NOPE), F32)
    zeros_n = jnp.zeros((T, C_NOPE), F32)
    zeros_p = jnp.zeros((T, C_PAD - C_NOPE - C_ROPE), F32)
    cs = jnp.concatenate([ones, cos, cos, zeros_p], axis=1)
    sn = jnp.concatenate([zeros_n, -sin, sin, zeros_p], axis=1)
    return cs, sn


def _swap_halves(w):
    half = w.shape[-1] // 2
    return jnp.concatenate([w[..., half:], w[..., :half]], axis=-1)


def _pad_cols(w, width):
    return jnp.pad(w, ((0, 0), (0, width - w.shape[1])))


def _prep_layer_weights(w_in, w_out, w_uq, w_ukv, w_rg, b_rg, w_re, b_re):
    head_cols = lambda w, h: w[:, h * HEAD_DIM:(h + 1) * HEAD_DIM]
    wab = jnp.concatenate([head_cols(w_in, h) for h in A_HEAD_ORDER] + [w_in[:, A_Q_HEADS * HEAD_DIM:AB_COLS]],
                          axis=1).astype(BF16)
    c0 = AB_COLS
    w_cq = w_in[:, c0:c0 + Q_LORA]
    w_ckv = w_in[:, c0 + Q_LORA:c0 + Q_LORA + KV_LORA]
    w_kr = w_in[:, c0 + Q_LORA + KV_LORA:]
    lead = jnp.zeros((D_MODEL, C_NOPE), F32)
    wc = jnp.concatenate([w_cq, w_ckv,
                          _pad_cols(jnp.concatenate([lead, w_kr], 1), C_PAD),
                          _pad_cols(jnp.concatenate([lead, _swap_halves(w_kr)], 1), C_PAD)], axis=1).astype(BF16)
    uq = w_uq.reshape(Q_LORA, C_HEADS, C_NOPE + C_ROPE)
    wq = jnp.pad(uq, ((0, 0), (0, 0), (0, C_PAD - C_NOPE - C_ROPE))).reshape(Q_LORA, C_HEADS * C_PAD).astype(BF16)
    uqs = jnp.concatenate([jnp.zeros((Q_LORA, C_HEADS, C_NOPE), F32), _swap_halves(uq[..., C_NOPE:])], axis=-1)
    wqs = jnp.pad(uqs, ((0, 0), (0, 0), (0, C_PAD - C_NOPE - C_ROPE))).reshape(Q_LORA, C_HEADS * C_PAD).astype(BF16)
    ukv = w_ukv.reshape(KV_LORA, C_HEADS, C_NOPE + C_V)
    wk = jnp.pad(ukv[..., :C_NOPE], ((0, 0), (0, 0), (0, C_PAD - C_NOPE))).reshape(KV_LORA, C_HEADS * C_PAD)
    wv = jnp.pad(ukv[..., C_NOPE:], ((0, 0), (0, 0), (0, C_PAD - C_V))).reshape(KV_LORA, C_HEADS * C_PAD)
    woa = jnp.concatenate([w_out[h * HEAD_DIM:(h + 1) * HEAD_DIM] for h in A_HEAD_ORDER],
                          axis=0).reshape(2, LANES, D_MODEL).astype(BF16)
    wob = w_out[256:512].reshape(2, LANES, D_MODEL).astype(BF16)
    woc = w_out[512:].astype(BF16)
    wr = _pad_cols(jnp.concatenate([w_rg, w_re.transpose(1, 0, 2).reshape(D_MODEL, N_EXPERTS)], axis=1), LANES)
    wr_hi = wr.astype(BF16)
    wr_lo = (wr - wr_hi.astype(F32)).astype(BF16)
    wr = jnp.concatenate([wr_hi, wr_hi, wr_lo], axis=0)
    br = _pad_cols(jnp.concatenate([b_rg, b_re.reshape(N_EXPERTS)])[None, :], LANES)
    return dict(wab=wab, wc=wc, wq=wq, wqs=wqs, wk=wk.astype(BF16), wv=wv.astype(BF16),
                woa=woa, wob=wob, woc=woc, wr=wr, br=br)


def kernel(x, w_in, w_out, sinks, rel_bias, mla_q_norm, mla_kv_norm, w_uq, w_ukv, ln1_g, ln1_b,
           w_route_group, b_route_group, w_route_expert, b_route_expert,
           w_expert_gate, w_expert_up, w_expert_down, ln2_g, ln2_b):
    Bsz, T, D = x.shape
    assert Bsz == 1 and D == D_MODEL and T % CHUNK == 0
    xt = x.reshape(T, D)
    cs, sn = _rope_tables(T)
    bias_a = _bias_table(rel_bias[:, :A_Q_HEADS], 1, SWA_WINDOW - 1)
    bias_b = [_bias_table(rel_bias[:, A_Q_HEADS + g * B_HEADS_PER_GROUP:A_Q_HEADS + (g + 1) * B_HEADS_PER_GROUP],
                          dil, window // dil) for g, (window, dil) in enumerate(DILATED_PAIRS)]
    n_assign = T * TOP_K
    n_blocks = n_assign // ROW_BLK + N_EXPERTS
    n_rows = n_blocks * ROW_BLK
    no_sink = jnp.zeros((B_HEADS_PER_GROUP,), F32)

    for layer in range(w_in.shape[0]):
        w = _prep_layer_weights(w_in[layer], w_out[layer], w_uq[layer], w_ukv[layer],
                                w_route_group[layer], b_route_group[layer],
                                w_route_expert[layer], b_route_expert[layer])
        hab, qt, k, vt = _in_proj(xt, w["wab"], w["wc"], w["wq"], w["wqs"], w["wk"], w["wv"],
                                mla_q_norm[layer][None, :], mla_kv_norm[layer][None, :], cs, sn)
        oa, _ = _banded(hab, sinks[layer], bias_a, q_col=0, k_col=2, v_col=3, dil=1,
                        n_heads=A_Q_HEADS, group=A_Q_HEADS // A_KV_HEADS, has_sink=True, want_lse=False)
        ob, lse = [], []
        for g, (_, dil) in enumerate(DILATED_PAIRS):
            o_g, lse_g = _banded(hab, no_sink, bias_b[g], q_col=4 + 2 * g, k_col=10 + 2 * g, v_col=16 + 2 * g,
                                 dil=dil, n_heads=B_HEADS_PER_GROUP, group=1, has_sink=False, want_lse=True)
            ob += o_g
            lse += lse_g
        oc = _flash(qt, k, vt)
        x1, slab, cnt = _out_router(oa, ob, lse, oc, xt, w["woa"], w["wob"], w["woc"],
                                    ln1_g[layer][None, :], ln1_b[layer][None, :], w["wr"], w["br"])

        counts = cnt[0, N_GROUPS:N_GROUPS + N_EXPERTS].astype(jnp.int32)
        padded = ((counts + ROW_BLK - 1) // ROW_BLK) * ROW_BLK
        pad_end = jnp.cumsum(padded)
        pad_start = pad_end - padded
        eids = jnp.arange(N_EXPERTS, dtype=jnp.int32)
        pick = lambda e: jnp.sum(jnp.where(e.astype(jnp.int32)[:, None] == eids[None, :], pad_start[None, :], 0), axis=1)
        d0 = pick(slab[:, 0]) + slab[:, 2].astype(jnp.int32)
        d1 = pick(slab[:, 1]) + slab[:, 3].astype(jnp.int32)
        block_start = jnp.arange(n_blocks, dtype=jnp.int32) * ROW_BLK
        block_expert = jnp.minimum(jnp.sum((pad_end[None, :] <= block_start[:, None]).astype(jnp.int32), axis=1),
                                   N_EXPERTS - 1)

        n_used = (pad_end[-1:] // ROW_BLK).astype(jnp.int32)
        xs = _dispatch(d0, d1, pad_start + counts, padded - counts, n_used, x1, n_rows)
        ys = _ffn(block_expert, n_used, xs, w_expert_gate, w_expert_up, w_expert_down, layer)
        xt = _combine(d0, d1, ys, x1, slab, ln2_g[layer][None, :], ln2_b[layer][None, :])
    return xt.reshape(Bsz, T, D)
```

```python
import functools
import math

import jax
import jax.numpy as jnp
from jax import lax
from jax.experimental import pallas as pl
from jax.experimental.pallas import tpu as pltpu

F32 = jnp.float32
BF16 = jnp.bfloat16

D_MODEL = 1024
DEPTH = 4
HEAD_DIM = 64
BLK = 128
NEG_INF = -1e30

SWA_WINDOW = 128
A_Q_HEADS = 4
A_KV_HEADS = 2
A_COLS = (A_Q_HEADS + 2 * A_KV_HEADS) * HEAD_DIM
A_HEAD_ORDER = (0, 2, 1, 3)

DILATED_PAIRS = ((128, 1), (512, 4), (2048, 16))
B_HEADS_PER_GROUP = 4
B_HEADS = len(DILATED_PAIRS) * B_HEADS_PER_GROUP
B_COLS = 3 * B_HEADS * HEAD_DIM
AB_COLS = A_COLS + B_COLS

C_HEADS = 8
C_NOPE = 64
C_ROPE = 32
C_V = 64
Q_LORA = 256
KV_LORA = 128
ROPE_THETA = 10000.0
C_PAD = 128
C_VROWS = 80

N_BUCKETS = 32
MAX_DISTANCE = 2048

N_GROUPS = 4
EXPERTS_PER_GROUP = 8
N_EXPERTS = N_GROUPS * EXPERTS_PER_GROUP
TOP_K = 2
D_EXPERT = 256
ROW_BLK = 512

DEEPNORM_ALPHA = (2 * DEPTH) ** 0.25

VMEM_LIMIT = 56 * 1024 * 1024

CHUNK = 2048
TM_IN = 512
TM_OUT = 512
TQ = 1024
TK = 512
N_DIAG = TQ // TK
FLASH_UNROLL = 4
LANES = 128


def _dot(a, b):
    return jnp.dot(a, b, preferred_element_type=F32)


def _dot_nt(a, b):
    return lax.dot_general(a, b, (((1,), (1,)), ((), ())), preferred_element_type=F32)


def _bdot(a, b):
    return lax.dot_general(a, b, (((2,), (1,)), ((0,), (0,))), preferred_element_type=F32)


def _bdot_nt(a, b):
    return lax.dot_general(a, b, (((2,), (2,)), ((0,), (0,))), preferred_element_type=F32)


def _cparams(*sem):
    return pltpu.CompilerParams(dimension_semantics=sem, vmem_limit_bytes=VMEM_LIMIT)


def _in_proj_kernel(x_ref, wab_ref, wc_ref, wq_ref, wqs_ref, wk_ref, wv_ref, gq_ref, gkv_ref, cs_ref, sn_ref,
                    hab_ref, qt_ref, k_ref, vt_ref):
    xb = x_ref[...].astype(BF16)
    hab_ref[...] = _dot(xb, wab_ref[...])
    hc = _dot(xb, wc_ref[...])
    cq = hc[:, :Q_LORA]
    ckv = hc[:, Q_LORA:Q_LORA + KV_LORA]
    kr = hc[:, Q_LORA + KV_LORA:Q_LORA + KV_LORA + C_PAD]
    krs = hc[:, Q_LORA + KV_LORA + C_PAD:]
    cqn = (cq * lax.rsqrt(jnp.mean(cq * cq, -1, keepdims=True) + 1e-6) * gq_ref[...]).astype(BF16)
    ckvn = (ckv * lax.rsqrt(jnp.mean(ckv * ckv, -1, keepdims=True) + 1e-6) * gkv_ref[...]).astype(BF16)
    cs = cs_ref[...]
    sn = sn_ref[...]
    q = _dot(cqn, wq_ref[...])
    qs = _dot(cqn, wqs_ref[...])
    k = _dot(ckvn, wk_ref[...])
    v = _dot(ckvn, wv_ref[...])
    krot = kr * cs + krs * sn
    scale = (C_NOPE + C_ROPE) ** -0.5 * math.log2(math.e)
    for h in range(C_HEADS):
        sl = slice(h * C_PAD, (h + 1) * C_PAD)
        qh = (q[:, sl] * cs + qs[:, sl] * sn) * scale
        qt_ref[sl, :] = qh.T.astype(BF16)
        k_ref[h] = (k[:, sl] + krot).astype(BF16)
    one_row = (lax.broadcasted_iota(jnp.int32, (1, C_PAD), 1) == C_V).astype(F32)
    for h in range(C_HEADS):
        vh = v[:, h * C_PAD:(h + 1) * C_PAD] + one_row
        vt_ref[h * C_VROWS:(h + 1) * C_VROWS, :] = vh.T[:C_VROWS, :].astype(BF16)


def _in_proj(x, wab, wc, wq, wqs, wk, wv, gq, gkv, cs, sn):
    T = x.shape[0]
    tm = TM_IN
    full = lambda a: pl.BlockSpec(a.shape, lambda i: (0,) * a.ndim)
    row = lambda w: pl.BlockSpec((tm, w), lambda i: (i, 0))
    colblk = lambda r: pl.BlockSpec((r, tm), lambda i: (0, i))
    return pl.pallas_call(
        _in_proj_kernel,
        grid=(T // tm,),
        in_specs=[row(D_MODEL), full(wab), full(wc), full(wq), full(wqs), full(wk), full(wv), full(gq), full(gkv),
                  row(C_PAD), row(C_PAD)],
        out_specs=[row(AB_COLS), colblk(C_HEADS * C_PAD), pl.BlockSpec((C_HEADS, tm, C_PAD), lambda i: (0, i, 0)),
                   colblk(C_HEADS * C_VROWS)],
        out_shape=[jax.ShapeDtypeStruct((T, AB_COLS), F32), jax.ShapeDtypeStruct((C_HEADS * C_PAD, T), BF16),
                   jax.ShapeDtypeStruct((C_HEADS, T, C_PAD), BF16),
                   jax.ShapeDtypeStruct((C_HEADS * C_VROWS, T), BF16)],
        compiler_params=_cparams("arbitrary"),
        name="in_proj",
    )(x, wab, wc, wq, wqs, wk, wv, gq, gkv, cs, sn)


def _banded_kernel(sink_ref, *refs, dil, n_heads, group, has_sink, want_lse):
    nq = n_heads // 2
    nkv = n_heads // group // 2
    q_refs, refs = refs[:nq], refs[nq:]
    k_refs, v_refs, kp_refs, vp_refs = (refs[i * nkv:(i + 1) * nkv] for i in range(4))
    bias_ref = refs[4 * nkv]
    o_refs = refs[4 * nkv + 1:4 * nkv + 1 + nq]
    lse_refs = refs[4 * nkv + 1 + nq:]
    c = pl.program_id(0)
    span = BLK * dil
    nb = CHUNK // span
    n_combo = nb * dil
    starts = [b * span + r for b in range(nb) for r in range(dil)]
    scale = HEAD_DIM ** -0.5

    def rows(start):
        return pl.ds(start, BLK) if dil == 1 else pl.ds(start, BLK, stride=dil)

    def gather(ref):
        return jnp.stack([ref[rows(s), :] for s in starts], axis=0)

    def with_prev(ref, prev_ref):
        cur = gather(ref)
        first = jnp.stack([prev_ref[rows(r), :] for r in range(dil)], axis=0)
        prev = jnp.concatenate([first, cur[:n_combo - dil]], axis=0) if dil < n_combo else first
        return jnp.concatenate([prev, cur], axis=1).astype(BF16)

    q = [gather(ref).astype(BF16) for ref in q_refs]
    kcat = [with_prev(k_refs[j], kp_refs[j]) for j in range(nkv)]
    vcat = [with_prev(v_refs[j], vp_refs[j]) for j in range(nkv)]

    lane = lax.broadcasted_iota(jnp.int32, (1, 1, LANES), 2)
    low = lane < HEAD_DIM
    combo = lax.broadcasted_iota(jnp.int32, (n_combo, 1, 2 * BLK), 0)
    col = lax.broadcasted_iota(jnp.int32, (n_combo, 1, 2 * BLK), 2)
    no_prev = jnp.logical_and(c == 0, jnp.logical_and(combo < dil, col < BLK))

    for j in range(nq):
        o_par, lse_par = [], []
        for par in range(2):
            h = 2 * j + par if group == 1 else group * par + j
            kvb = j if group == 1 else 0
            keep = low if par == 0 else jnp.logical_not(low)
            kk = jnp.where(keep, kcat[kvb], jnp.zeros((), BF16))
            s = _bdot_nt(q[j], kk)
            s = s * scale + bias_ref[h][None]
            s = jnp.where(no_prev, NEG_INF, s)
            m = jnp.max(s, axis=-1, keepdims=True)
            if has_sink:
                snk = sink_ref[h]
                m = jnp.maximum(m, snk)
            e = jnp.exp(s - m)
            l = jnp.sum(e, axis=-1, keepdims=True)
            if has_sink:
                l = l + jnp.exp(snk - m)
            p = (e * (1.0 / l)).astype(BF16)
            o_par.append(_bdot(p, vcat[kvb]))
            if want_lse:
                lse_par.append(m + jnp.log(l))
        o_pair = jnp.where(low, o_par[0], o_par[1])
        if want_lse:
            lse_pair = jnp.where(low, lse_par[0], lse_par[1])
        for ci, s0 in enumerate(starts):
            o_refs[j][rows(s0), :] = o_pair[ci]
            if want_lse:
                lse_refs[j][rows(s0), :] = lse_pair[ci]


def _banded(hab, sink, bias, *, q_col, k_col, v_col, dil, n_heads, group, has_sink, want_lse):
    T = hab.shape[0]
    span = BLK * dil
    nb = CHUNK // span
    nq = n_heads // 2
    nkv = n_heads // group // 2
    kernel = functools.partial(_banded_kernel, dil=dil, n_heads=n_heads, group=group,
                               has_sink=has_sink, want_lse=want_lse)
    prev = lambda cb: pl.BlockSpec((span, LANES), lambda c, s: (jnp.maximum(c * nb - 1, 0), cb))
    cur = lambda cb: pl.BlockSpec((CHUNK, LANES), lambda c, s: (c, cb))
    in_specs = ([cur(q_col + j) for j in range(nq)] + [cur(k_col + j) for j in range(nkv)]
                + [cur(v_col + j) for j in range(nkv)] + [prev(k_col + j) for j in range(nkv)]
                + [prev(v_col + j) for j in range(nkv)] + [pl.BlockSpec(bias.shape, lambda c, s: (0, 0, 0))])
    n_out = nq * (2 if want_lse else 1)
    grid_spec = pltpu.PrefetchScalarGridSpec(
        num_scalar_prefetch=1,
        grid=(T // CHUNK,),
        in_specs=in_specs,
        out_specs=[pl.BlockSpec((CHUNK, LANES), lambda c, s: (c, 0))] * n_out,
    )
    outs = pl.pallas_call(
        kernel,
        grid_spec=grid_spec,
        out_shape=[jax.ShapeDtypeStruct((T, LANES), F32)] * n_out,
        compiler_params=_cparams("arbitrary"),
        name=f"banded_d{dil}" + ("_sink" if has_sink else ""),
    )(sink, *([hab] * (nq + 4 * nkv)), bias)
    return outs[:nq], outs[nq:]


def _flash_kernel(qt_ref, k_ref, vt_ref, ot_ref, m_ref, acc_ref, s_ref):
    qi = pl.program_id(1)
    qt = qt_ref[...]
    m_ref[...] = jnp.full(m_ref.shape, NEG_INF, F32)
    acc_ref[...] = jnp.zeros(acc_ref.shape, F32)

    def scores(j):
        off = pl.multiple_of(j * TK, TK)
        return _dot(k_ref[pl.ds(off, TK), :], qt)

    def accumulate(j, s):
        off = pl.multiple_of(j * TK, TK)
        m_old = m_ref[...]
        m_new = jnp.maximum(m_old, jnp.max(s, axis=0, keepdims=True))
        alpha = jnp.exp2(m_old - m_new)
        p = jnp.exp2(s - m_new).astype(BF16)
        acc_ref[...] = alpha * acc_ref[...] + _dot(vt_ref[:, pl.ds(off, TK)], p)
        m_ref[...] = m_new

    def diagonal(s, d):
        key = lax.broadcasted_iota(jnp.int32, (TK, TQ), 0)
        qry = lax.broadcasted_iota(jnp.int32, (TK, TQ), 1)
        return jnp.where(key + d * TK <= qry, s, NEG_INF)

    n_full = qi * N_DIAG
    s_ref[0] = scores(0)
    n_main = n_full // FLASH_UNROLL

    def main_body(i, carry):
        j = FLASH_UNROLL * i
        for u in range(FLASH_UNROLL):
            s_ref[(u + 1) % 2] = scores(j + u + 1)
            accumulate(j + u, s_ref[u % 2])
        return carry

    lax.fori_loop(0, n_main, main_body, 0)

    base = FLASH_UNROLL * n_main
    for rem in range(0, FLASH_UNROLL, math.gcd(FLASH_UNROLL, N_DIAG)):
        @pl.when(n_full - base == rem)
        def _():
            n_tail = rem + N_DIAG
            for u in range(n_tail):
                if u + 1 < n_tail:
                    s_ref[(u + 1) % 2] = scores(base + u + 1)
                s = s_ref[u % 2]
                accumulate(base + u, s if u < rem else diagonal(s, u - rem))

    ot_ref[...] = (acc_ref[:C_V, :] / acc_ref[C_V:C_V + 1, :]).astype(ot_ref.dtype)


def _flash(qt, k, vt):
    H, T, _ = k.shape
    return pl.pallas_call(
        _flash_kernel,
        grid=(H, T // TQ),
        in_specs=[pl.BlockSpec((C_PAD, TQ), lambda h, i: (h, i)),
                  pl.BlockSpec((None, T, C_PAD), lambda h, i: (h, 0, 0)),
                  pl.BlockSpec((C_VROWS, T), lambda h, i: (h, 0))],
        out_specs=pl.BlockSpec((C_V, TQ), lambda h, i: (h, i)),
        out_shape=jax.ShapeDtypeStruct((H * C_V, T), BF16),
        scratch_shapes=[pltpu.VMEM((1, TQ), F32), pltpu.VMEM((C_VROWS, TQ), F32), pltpu.VMEM((2, TK, TQ), F32)],
        compiler_params=_cparams("arbitrary", "arbitrary"),
        name="mla_flash",
    )(qt, k, vt)


def _layer_norm(z, g, b):
    mu = jnp.mean(z, -1, keepdims=True)
    zc = z - mu
    var = jnp.mean(zc * zc, -1, keepdims=True)
    return zc * lax.rsqrt(var + 1e-5) * g + b


def _out_router_kernel(*refs):
    n_ab = 2 + 6 + 6
    oa_refs, ob_refs, lse_refs = refs[0:2], refs[2:8], refs[8:14]
    (oc_ref, x_ref, woa_ref, wob_ref, woc_ref, g_ref, b_ref, wr_ref, br_ref,
     x1_ref, slab_ref, cnt_ref, carry_ref, prev_ref) = refs[n_ab:]
    i = pl.program_id(0)
    tm = x_ref.shape[0]

    @pl.when(i == 0)
    def _():
        carry_ref[...] = jnp.zeros(carry_ref.shape, F32)
        prev_ref[...] = jnp.zeros(prev_ref.shape, F32)

    xr = prev_ref[...]

    y = None
    for j in range(2):
        l0, l1, l2 = (lse_refs[2 * g + j][...] for g in range(3))
        mx = jnp.maximum(jnp.maximum(l0, l1), l2)
        e0, e1, e2 = jnp.exp(l0 - mx), jnp.exp(l1 - mx), jnp.exp(l2 - mx)
        den = e0 + e1 + e2
        ob = ((e0 / den) * ob_refs[j][...] + (e1 / den) * ob_refs[2 + j][...] + (e2 / den) * ob_refs[4 + j][...])
        part = _dot(oa_refs[j][...].astype(BF16), woa_ref[j]) + _dot(ob.astype(BF16), wob_ref[j])
        y = part if y is None else y + part
    y = y + _dot(oc_ref[...].astype(F32).T.astype(BF16), woc_ref[...])
    x1 = _layer_norm(DEEPNORM_ALPHA * x_ref[...] + y, g_ref[...], b_ref[...])
    x1_ref[...] = x1
    prev_ref[...] = x1

    xr_hi = xr.astype(BF16)
    xr_lo = (xr - xr_hi.astype(F32)).astype(BF16)
    logits = _dot(jnp.concatenate([xr_hi, xr_lo, xr_hi], axis=1), wr_ref[...]) + br_ref[...]
    lane = lax.broadcasted_iota(jnp.int32, (tm, LANES), 1)
    ninf = jnp.float32(-jnp.inf)
    gmask = lane < N_GROUPS
    gl = jnp.where(gmask, logits, ninf)
    gmax = jnp.max(gl, axis=-1, keepdims=True)
    grp = jnp.min(jnp.where(gl == gmax, lane, LANES), axis=-1, keepdims=True)
    p_grp = 1.0 / jnp.sum(jnp.where(gmask, jnp.exp(logits - gmax), 0.0), axis=-1, keepdims=True)
    lo = N_GROUPS + EXPERTS_PER_GROUP * grp
    emask = jnp.logical_and(lane >= lo, lane < lo + EXPERTS_PER_GROUP)
    el = jnp.where(emask, logits, ninf)
    v0 = jnp.max(el, axis=-1, keepdims=True)
    i0 = jnp.min(jnp.where(el == v0, lane, LANES), axis=-1, keepdims=True)
    el2 = jnp.where(lane == i0, ninf, el)
    v1 = jnp.max(el2, axis=-1, keepdims=True)
    i1 = jnp.min(jnp.where(el2 == v1, lane, LANES), axis=-1, keepdims=True)
    t = jnp.exp(v1 - v0)
    gate0 = p_grp / (1.0 + t)
    gate1 = p_grp * t / (1.0 + t)

    hit0 = lane == i0
    hit1 = lane == i1
    cmat = jnp.logical_or(hit0, hit1).astype(BF16)
    tr = lax.broadcasted_iota(jnp.int32, (tm, tm), 0)
    tc = lax.broadcasted_iota(jnp.int32, (tm, tm), 1)
    tril = (tc < tr).astype(BF16)
    prefix = _dot(tril, cmat) + carry_ref[...]
    r0 = jnp.sum(jnp.where(hit0, prefix, 0.0), axis=-1, keepdims=True)
    r1 = jnp.sum(jnp.where(hit1, prefix, 0.0), axis=-1, keepdims=True)
    counted = (i > 0).astype(F32)
    carry_ref[...] = carry_ref[...] + counted * jnp.sum(cmat.astype(F32), axis=0, keepdims=True)
    cnt_ref[...] = carry_ref[...]

    fields = [(i0 - N_GROUPS).astype(F32), (i1 - N_GROUPS).astype(F32), r0, r1, gate0, gate1]
    slab = jnp.zeros((tm, LANES), F32)
    for n, f in enumerate(fields):
        slab = jnp.where(lane == n, f, slab)
    slab_ref[...] = slab


def _out_router(oa, ob, lse, oc, x, woa, wob, woc, g, b, wr, br):
    T = x.shape[0]
    tm = TM_OUT
    n = T // tm
    full = lambda a: pl.BlockSpec(a.shape, lambda i: (0,) * a.ndim)
    row = lambda w: pl.BlockSpec((tm, w), lambda i: (jnp.minimum(i, n - 1), 0))
    lag = pl.BlockSpec((tm, LANES), lambda i: (jnp.maximum(i - 1, 0), 0))
    return pl.pallas_call(
        _out_router_kernel,
        grid=(n + 1,),
        in_specs=[row(LANES)] * 14 + [pl.BlockSpec((C_HEADS * C_V, tm), lambda i: (0, jnp.minimum(i, n - 1))),
                                      row(D_MODEL),
                                      full(woa), full(wob), full(woc), full(g), full(b), full(wr), full(br)],
        out_specs=[row(D_MODEL), lag, pl.BlockSpec((1, LANES), lambda i: (0, 0))],
        out_shape=[jax.ShapeDtypeStruct((T, D_MODEL), F32), jax.ShapeDtypeStruct((T, LANES), F32),
                   jax.ShapeDtypeStruct((1, LANES), F32)],
        scratch_shapes=[pltpu.VMEM((1, LANES), F32), pltpu.VMEM((tm, D_MODEL), F32)],
        compiler_params=_cparams("arbitrary"),
        name="out_router",
    )(*oa, *ob, *lse, oc, x, woa, wob, woc, g, b, wr, br)


SUBLANES = 8
PAD_BITS = tuple(1 << k for k in reversed(range(SUBLANES.bit_length() - 1, ROW_BLK.bit_length() - 1)))


def _dispatch_kernel(d0_ref, d1_ref, pad_at_ref, pad_len_ref, nu_ref, x_ref, xs_ref, zero_ref, sem, zsem):
    i = pl.program_id(0)
    tm = x_ref.shape[0]
    base = i * tm

    @pl.when(i == 0)
    def _():
        zero_ref[...] = jnp.zeros(zero_ref.shape, F32)

        def pad_copies(e):
            n = pad_len_ref[e]
            at = pad_at_ref[e]
            head = n & (SUBLANES - 1)
            out = [(j < head, pltpu.make_async_copy(zero_ref.at[pl.ds(0, 1), :], xs_ref.at[pl.ds(at + j, 1), :], zsem))
                   for j in range(SUBLANES - 1)]
            body = n - head
            for bit in PAD_BITS:
                start = pl.multiple_of(at + head + (body & ~(2 * bit - 1)), SUBLANES)
                out.append(((body & bit) != 0,
                            pltpu.make_async_copy(zero_ref.at[pl.ds(0, bit), :], xs_ref.at[pl.ds(start, bit), :], zsem)))
            return out

        def spare_copies():
            n_blocks = xs_ref.shape[0] // ROW_BLK
            first = d0_ref.shape[0] * TOP_K // ROW_BLK
            return [(b >= nu_ref[0], pltpu.make_async_copy(zero_ref, xs_ref.at[pl.ds(b * ROW_BLK, ROW_BLK), :], zsem))
                    for b in range(first, n_blocks)]

        for e in range(N_EXPERTS):
            for on, cp in pad_copies(e):
                pl.when(on)(cp.start)
        for on, cp in spare_copies():
            pl.when(on)(cp.start)
        for e in range(N_EXPERTS):
            for on, cp in pad_copies(e):
                pl.when(on)(cp.wait)
        for on, cp in spare_copies():
            pl.when(on)(cp.wait)

    def copy(t, d_ref):
        return pltpu.make_async_copy(x_ref.at[pl.ds(t, 1), :], xs_ref.at[pl.ds(d_ref[base + t], 1), :], sem)

    def start(t, carry):
        copy(t, d0_ref).start(priority=0)
        copy(t, d1_ref).start(priority=1)
        return carry

    lax.fori_loop(0, tm, start, 0, unroll=8)
    for _ in range(2):
        pltpu.make_async_copy(x_ref, xs_ref.at[pl.ds(0, tm), :], sem).wait()


def _dispatch(d0, d1, pad_at, pad_len, n_used, x1, n_rows):
    T = x1.shape[0]
    tm = ROW_BLK
    grid_spec = pltpu.PrefetchScalarGridSpec(
        num_scalar_prefetch=5,
        grid=(T // tm,),
        in_specs=[pl.BlockSpec((tm, D_MODEL), lambda i, *_: (i, 0))],
        out_specs=pl.BlockSpec(memory_space=pl.ANY),
        scratch_shapes=[pltpu.VMEM((ROW_BLK, D_MODEL), F32), pltpu.SemaphoreType.DMA, pltpu.SemaphoreType.DMA],
    )
    return pl.pallas_call(
        _dispatch_kernel,
        grid_spec=grid_spec,
        out_shape=jax.ShapeDtypeStruct((n_rows, D_MODEL), F32),
        compiler_params=_cparams("arbitrary"),
        name="moe_dispatch",
    )(d0, d1, pad_at, pad_len, n_used, x1)


FFN_BLOCKS = 1


def _ffn_kernel(be_ref, nu_ref, xs_ref, *refs):
    del be_ref
    w_refs, ys_ref = refs[:-1], refs[-1]
    i = pl.program_id(0)
    for k in range(FFN_BLOCKS):
        wg_ref, wu_ref, wd_ref = w_refs[3 * k:3 * k + 3]
        rows = pl.ds(k * ROW_BLK, ROW_BLK)
        used = FFN_BLOCKS * i + k < nu_ref[0]

        @pl.when(used)
        def _():
            xb = xs_ref[rows, :].astype(BF16)
            a = _dot(xb, wg_ref[...].astype(BF16))
            u = _dot(xb, wu_ref[...].astype(BF16))
            hmid = (a * jax.nn.sigmoid(a) * u).astype(BF16)
            ys_ref[rows, :] = _dot(hmid, wd_ref[...].astype(BF16))

        @pl.when(jnp.logical_not(used))
        def _():
            ys_ref[rows, :] = jnp.zeros((ROW_BLK, D_MODEL), F32)


def _ffn(block_expert, n_used, xs, wg, wu, wd, layer):
    n_rows = xs.shape[0]
    step_rows = FFN_BLOCKS * ROW_BLK
    assert n_rows % step_rows == 0
    w_spec = lambda shape, k: pl.BlockSpec((None, None) + shape,
                                           lambda i, be, nu: (layer, be[FFN_BLOCKS * i + k], 0, 0))
    w_specs = [w_spec(s, k) for k in range(FFN_BLOCKS)
               for s in ((D_MODEL, D_EXPERT), (D_MODEL, D_EXPERT), (D_EXPERT, D_MODEL))]
    grid_spec = pltpu.PrefetchScalarGridSpec(
        num_scalar_prefetch=2,
        grid=(n_rows // step_rows,),
        in_specs=[pl.BlockSpec((step_rows, D_MODEL),
                               lambda i, be, nu: (jnp.minimum(i, (nu[0] - 1) // FFN_BLOCKS), 0))] + w_specs,
        out_specs=pl.BlockSpec((step_rows, D_MODEL), lambda i, be, nu: (i, 0)),
    )
    return pl.pallas_call(
        _ffn_kernel,
        grid_spec=grid_spec,
        out_shape=jax.ShapeDtypeStruct((n_rows, D_MODEL), F32),
        compiler_params=_cparams("arbitrary"),
        name="moe_ffn",
    )(block_expert, n_used, xs, *([wg, wu, wd] * FFN_BLOCKS))


def _combine_kernel(d0_ref, d1_ref, ys_ref, x1_ref, slab_ref, g_ref, b_ref, o_ref, buf_ref, sem):
    i = pl.program_id(0)
    n = pl.num_programs(0)
    tm = x1_ref.shape[0]

    def gather(tile):
        slot = tile % 2
        base = tile * tm

        def start(t, carry):
            for k, d_ref in enumerate((d0_ref, d1_ref)):
                pltpu.make_async_copy(ys_ref.at[pl.ds(d_ref[base + t], 1), :],
                                      buf_ref.at[slot, k, pl.ds(t, 1), :], sem.at[slot]).start(priority=k)
            return carry

        lax.fori_loop(0, tm, start, 0, unroll=8)

    @pl.when(i == 0)
    def _():
        gather(0)

    @pl.when(i + 1 < n)
    def _():
        gather(i + 1)

    slot = i % 2
    for k in range(2):
        pltpu.make_async_copy(ys_ref.at[pl.ds(0, tm), :], buf_ref.at[slot, k], sem.at[slot]).wait()
    slab = slab_ref[...]
    y = slab[:, 4:5] * buf_ref[slot, 0] + slab[:, 5:6] * buf_ref[slot, 1]
    o_ref[...] = _layer_norm(DEEPNORM_ALPHA * x1_ref[...] + y, g_ref[...], b_ref[...])


def _combine(d0, d1, ys, x1, slab, g, b):
    T = x1.shape[0]
    tm = ROW_BLK
    grid_spec = pltpu.PrefetchScalarGridSpec(
        num_scalar_prefetch=2,
        grid=(T // tm,),
        in_specs=[pl.BlockSpec(memory_space=pl.ANY),
                  pl.BlockSpec((tm, D_MODEL), lambda i, a, c: (i, 0)),
                  pl.BlockSpec((tm, LANES), lambda i, a, c: (i, 0)),
                  pl.BlockSpec((1, D_MODEL), lambda i, a, c: (0, 0)),
                  pl.BlockSpec((1, D_MODEL), lambda i, a, c: (0, 0))],
        out_specs=pl.BlockSpec((tm, D_MODEL), lambda i, a, c: (i, 0)),
        scratch_shapes=[pltpu.VMEM((2, 2, tm, D_MODEL), F32), pltpu.SemaphoreType.DMA((2,))],
    )
    return pl.pallas_call(
        _combine_kernel,
        grid_spec=grid_spec,
        out_shape=jax.ShapeDtypeStruct((T, D_MODEL), F32),
        compiler_params=_cparams("arbitrary"),
        name="moe_combine",
    )(d0, d1, ys, x1, slab, g, b)


def _t5_bucket(dist):
    max_exact = N_BUCKETS // 2
    large = max_exact + (jnp.log(jnp.maximum(dist, 1).astype(F32) / max_exact)
                         / math.log(MAX_DISTANCE / max_exact) * (N_BUCKETS - max_exact)).astype(jnp.int32)
    return jnp.where(dist < max_exact, dist, jnp.minimum(large, N_BUCKETS - 1))


def _bias_table(table_cols, dil, max_dist):
    dist = BLK + jnp.arange(BLK)[:, None] - jnp.arange(2 * BLK)[None, :]
    bucket = _t5_bucket(jnp.maximum(dist * dil, 0))
    hit = bucket[None, :, :, None] == jnp.arange(N_BUCKETS)[None, None, None, :]
    bias = jnp.sum(jnp.where(hit, table_cols.T.astype(F32)[:, None, None, :], 0.0), axis=-1)
    mask = (dist >= 0) & (dist <= max_dist)
    return jnp.where(mask[None], bias, NEG_INF)


def _rope_tables(T):
    inv_freq = ROPE_THETA ** (-jnp.arange(0, C_ROPE, 2, dtype=F32) / C_ROPE)
    ang = jnp.arange(T, dtype=F32)[:, None] * inv_freq[None, :]
    cos, sin = jnp.cos(ang), jnp.sin(ang)
    ones = jnp.ones((T, C_NOPE), F32)
    zeros_n = jnp.zeros((T, C_NOPE), F32)
    zeros_p = jnp.zeros((T, C_PAD - C_NOPE - C_ROPE), F32)
    cs = jnp.concatenate([ones, cos, cos, zeros_p], axis=1)
    sn = jnp.concatenate([zeros_n, -sin, sin, zeros_p], axis=1)
    return cs, sn


def _swap_halves(w):
    half = w.shape[-1] // 2
    return jnp.concatenate([w[..., half:], w[..., :half]], axis=-1)


def _pad_cols(w, width):
    return jnp.pad(w, ((0, 0), (0, width - w.shape[1])))


def _prep_layer_weights(w_in, w_out, w_uq, w_ukv, w_rg, b_rg, w_re, b_re):
    head_cols = lambda w, h: w[:, h * HEAD_DIM:(h + 1) * HEAD_DIM]
    wab = jnp.concatenate([head_cols(w_in, h) for h in A_HEAD_ORDER] + [w_in[:, A_Q_HEADS * HEAD_DIM:AB_COLS]],
                          axis=1).astype(BF16)
    c0 = AB_COLS
    w_cq = w_in[:, c0:c0 + Q_LORA]
    w_ckv = w_in[:, c0 + Q_LORA:c0 + Q_LORA + KV_LORA]
    w_kr = w_in[:, c0 + Q_LORA + KV_LORA:]
    lead = jnp.zeros((D_MODEL, C_NOPE), F32)
    wc = jnp.concatenate([w_cq, w_ckv,
                          _pad_cols(jnp.concatenate([lead, w_kr], 1), C_PAD),
                          _pad_cols(jnp.concatenate([lead, _swap_halves(w_kr)], 1), C_PAD)], axis=1).astype(BF16)
    uq = w_uq.reshape(Q_LORA, C_HEADS, C_NOPE + C_ROPE)
    wq = jnp.pad(uq, ((0, 0), (0, 0), (0, C_PAD - C_NOPE - C_ROPE))).reshape(Q_LORA, C_HEADS * C_PAD).astype(BF16)
    uqs = jnp.concatenate([jnp.zeros((Q_LORA, C_HEADS, C_NOPE), F32), _swap_halves(uq[..., C_NOPE:])], axis=-1)
    wqs = jnp.pad(uqs, ((0, 0), (0, 0), (0, C_PAD - C_NOPE - C_ROPE))).reshape(Q_LORA, C_HEADS * C_PAD).astype(BF16)
    ukv = w_ukv.reshape(KV_LORA, C_HEADS, C_NOPE + C_V)
    wk = jnp.pad(ukv[..., :C_NOPE], ((0, 0), (0, 0), (0, C_PAD - C_NOPE))).reshape(KV_LORA, C_HEADS * C_PAD)
    wv = jnp.pad(ukv[..., C_NOPE:], ((0, 0), (0, 0), (0, C_PAD - C_V))).reshape(KV_LORA, C_HEADS * C_PAD)
    woa = jnp.concatenate([w_out[h * HEAD_DIM:(h + 1) * HEAD_DIM] for h in A_HEAD_ORDER],
                          axis=0).reshape(2, LANES, D_MODEL).astype(BF16)
    wob = w_out[256:512].reshape(2, LANES, D_MODEL).astype(BF16)
    woc = w_out[512:].astype(BF16)
    wr = _pad_cols(jnp.concatenate([w_rg, w_re.transpose(1, 0, 2).reshape(D_MODEL, N_EXPERTS)], axis=1), LANES)
    wr_hi = wr.astype(BF16)
    wr_lo = (wr - wr_hi.astype(F32)).astype(BF16)
    wr = jnp.concatenate([wr_hi, wr_hi, wr_lo], axis=0)
    br = _pad_cols(jnp.concatenate([b_rg, b_re.reshape(N_EXPERTS)])[None, :], LANES)
    return dict(wab=wab, wc=wc, wq=wq, wqs=wqs, wk=wk.astype(BF16), wv=wv.astype(BF16),
                woa=woa, wob=wob, woc=woc, wr=wr, br=br)


def kernel(x, w_in, w_out, sinks, rel_bias, mla_q_norm, mla_kv_norm, w_uq, w_ukv, ln1_g, ln1_b,
           w_route_group, b_route_group, w_route_expert, b_route_expert,
           w_expert_gate, w_expert_up, w_expert_down, ln2_g, ln2_b):
    Bsz, T, D = x.shape
    assert Bsz == 1 and D == D_MODEL and T % CHUNK == 0
    xt = x.reshape(T, D)
    cs, sn = _rope_tables(T)
    bias_a = _bias_table(rel_bias[:, :A_Q_HEADS], 1, SWA_WINDOW - 1)
    bias_b = [_bias_table(rel_bias[:, A_Q_HEADS + g * B_HEADS_PER_GROUP:A_Q_HEADS + (g + 1) * B_HEADS_PER_GROUP],
                          dil, window // dil) for g, (window, dil) in enumerate(DILATED_PAIRS)]
    n_assign = T * TOP_K
    n_blocks = n_assign // ROW_BLK + N_EXPERTS
    n_rows = n_blocks * ROW_BLK
    no_sink = jnp.zeros((B_HEADS_PER_GROUP,), F32)

    for layer in range(w_in.shape[0]):
        w = _prep_layer_weights(w_in[layer], w_out[layer], w_uq[layer], w_ukv[layer],
                                w_route_group[layer], b_route_group[layer],
                                w_route_expert[layer], b_route_expert[layer])
        hab, qt, k, vt = _in_proj(xt, w["wab"], w["wc"], w["wq"], w["wqs"], w["wk"], w["wv"],
                                mla_q_norm[layer][None, :], mla_kv_norm[layer][None, :], cs, sn)
        oa, _ = _banded(hab, sinks[layer], bias_a, q_col=0, k_col=2, v_col=3, dil=1,
                        n_heads=A_Q_HEADS, group=A_Q_HEADS // A_KV_HEADS, has_sink=True, want_lse=False)
        ob, lse = [], []
        for g, (_, dil) in enumerate(DILATED_PAIRS):
            o_g, lse_g = _banded(hab, no_sink, bias_b[g], q_col=4 + 2 * g, k_col=10 + 2 * g, v_col=16 + 2 * g,
                                 dil=dil, n_heads=B_HEADS_PER_GROUP, group=1, has_sink=False, want_lse=True)
            ob += o_g
            lse += lse_g
        oc = _flash(qt, k, vt)
        x1, slab, cnt = _out_router(oa, ob, lse, oc, xt, w["woa"], w["wob"], w["woc"],
                                    ln1_g[layer][None, :], ln1_b[layer][None, :], w["wr"], w["br"])

        counts = cnt[0, N_GROUPS:N_GROUPS + N_EXPERTS].astype(jnp.int32)
        padded = ((counts + ROW_BLK - 1) // ROW_BLK) * ROW_BLK
        pad_end = jnp.cumsum(padded)
        pad_start = pad_end - padded
        eids = jnp.arange(N_EXPERTS, dtype=jnp.int32)
        pick = lambda e: jnp.sum(jnp.where(e.astype(jnp.int32)[:, None] == eids[None, :], pad_start[None, :], 0), axis=1)
        d0 = pick(slab[:, 0]) + slab[:, 2].astype(jnp.int32)
        d1 = pick(slab[:, 1]) + slab[:, 3].astype(jnp.int32)
        block_start = jnp.arange(n_blocks, dtype=jnp.int32) * ROW_BLK
        block_expert = jnp.minimum(jnp.sum((pad_end[None, :] <= block_start[:, None]).astype(jnp.int32), axis=1),
                                   N_EXPERTS - 1)

        n_used = (pad_end[-1:] // ROW_BLK).astype(jnp.int32)
        xs = _dispatch(d0, d1, pad_start + counts, padded - counts, n_used, x1, n_rows)
        ys = _ffn(block_expert, n_used, xs, w_expert_gate, w_expert_up, w_expert_down, layer)
        xt = _combine(d0, d1, ys, x1, slab, ln2_g[layer][None, :], ln2_b[layer][None, :])
    return xt.reshape(Bsz, T, D)
```

```python
import functools
import math

import jax
import jax.numpy as jnp
from jax import lax
from jax.experimental import pallas as pl
from jax.experimental.pallas import tpu as pltpu

F32 = jnp.float32
BF16 = jnp.bfloat16

D_MODEL = 1024
DEPTH = 4
HEAD_DIM = 64
BLK = 128
NEG_INF = -1e30

SWA_WINDOW = 128
A_Q_HEADS = 4
A_KV_HEADS = 2
A_COLS = (A_Q_HEADS + 2 * A_KV_HEADS) * HEAD_DIM
A_HEAD_ORDER = (0, 2, 1, 3)

DILATED_PAIRS = ((128, 1), (512, 4), (2048, 16))
B_HEADS_PER_GROUP = 4
B_HEADS = len(DILATED_PAIRS) * B_HEADS_PER_GROUP
B_COLS = 3 * B_HEADS * HEAD_DIM
AB_COLS = A_COLS + B_COLS

C_HEADS = 8
C_NOPE = 64
C_ROPE = 32
C_V = 64
Q_LORA = 256
KV_LORA = 128
ROPE_THETA = 10000.0
C_PAD = 128
C_VROWS = 80

N_BUCKETS = 32
MAX_DISTANCE = 2048

N_GROUPS = 4
EXPERTS_PER_GROUP = 8
N_EXPERTS = N_GROUPS * EXPERTS_PER_GROUP
TOP_K = 2
D_EXPERT = 256
ROW_BLK = 512

DEEPNORM_ALPHA = (2 * DEPTH) ** 0.25

VMEM_LIMIT = 56 * 1024 * 1024

CHUNK = 2048
TM_IN = 512
TM_OUT = 512
TQ = 1024
TK = 512
N_DIAG = TQ // TK
FLASH_UNROLL = 4
LANES = 128


def _dot(a, b):
    return jnp.dot(a, b, preferred_element_type=F32)


def _dot_nt(a, b):
    return lax.dot_general(a, b, (((1,), (1,)), ((), ())), preferred_element_type=F32)


def _bdot(a, b):
    return lax.dot_general(a, b, (((2,), (1,)), ((0,), (0,))), preferred_element_type=F32)


def _bdot_nt(a, b):
    return lax.dot_general(a, b, (((2,), (2,)), ((0,), (0,))), preferred_element_type=F32)


def _cparams(*sem):
    return pltpu.CompilerParams(dimension_semantics=sem, vmem_limit_bytes=VMEM_LIMIT)


def _in_proj_kernel(x_ref, wab_ref, wc_ref, wq_ref, wqs_ref, wk_ref, wv_ref, gq_ref, gkv_ref, cs_ref, sn_ref,
                    hab_ref, qt_ref, k_ref, vt_ref):
    xb = x_ref[...].astype(BF16)
    hab_ref[...] = _dot(xb, wab_ref[...])
    hc = _dot(xb, wc_ref[...])
    cq = hc[:, :Q_LORA]
    ckv = hc[:, Q_LORA:Q_LORA + KV_LORA]
    kr = hc[:, Q_LORA + KV_LORA:Q_LORA + KV_LORA + C_PAD]
    krs = hc[:, Q_LORA + KV_LORA + C_PAD:]
    cqn = (cq * lax.rsqrt(jnp.mean(cq * cq, -1, keepdims=True) + 1e-6) * gq_ref[...]).astype(BF16)
    ckvn = (ckv * lax.rsqrt(jnp.mean(ckv * ckv, -1, keepdims=True) + 1e-6) * gkv_ref[...]).astype(BF16)
    cs = cs_ref[...]
    sn = sn_ref[...]
    q = _dot(cqn, wq_ref[...])
    qs = _dot(cqn, wqs_ref[...])
    k = _dot(ckvn, wk_ref[...])
    v = _dot(ckvn, wv_ref[...])
    krot = kr * cs + krs * sn
    scale = (C_NOPE + C_ROPE) ** -0.5 * math.log2(math.e)
    for h in range(C_HEADS):
        sl = slice(h * C_PAD, (h + 1) * C_PAD)
        qh = (q[:, sl] * cs + qs[:, sl] * sn) * scale
        qt_ref[sl, :] = qh.T.astype(BF16)
        k_ref[h] = (k[:, sl] + krot).astype(BF16)
    one_row = (lax.broadcasted_iota(jnp.int32, (1, C_PAD), 1) == C_V).astype(F32)
    for h in range(C_HEADS):
        vh = v[:, h * C_PAD:(h + 1) * C_PAD] + one_row
        vt_ref[h * C_VROWS:(h + 1) * C_VROWS, :] = vh.T[:C_VROWS, :].astype(BF16)


def _layer_block(a, layer):
    return pl.BlockSpec((None,) + a.shape[1:], lambda i, *_: (layer,) + (0,) * (a.ndim - 1))


def _in_proj(x, wab, wc, wq, wqs, wk, wv, gq, gkv, cs, sn, layer):
    T = x.shape[0]
    tm = TM_IN
    full = lambda a: _layer_block(a, layer)
    row = lambda w: pl.BlockSpec((tm, w), lambda i: (i, 0))
    colblk = lambda r: pl.BlockSpec((r, tm), lambda i: (0, i))
    return pl.pallas_call(
        _in_proj_kernel,
        grid=(T // tm,),
        in_specs=[row(D_MODEL), full(wab), full(wc), full(wq), full(wqs), full(wk), full(wv), full(gq), full(gkv),
                  row(C_PAD), row(C_PAD)],
        out_specs=[row(AB_COLS), colblk(C_HEADS * C_PAD), pl.BlockSpec((C_HEADS, tm, C_PAD), lambda i: (0, i, 0)),
                   colblk(C_HEADS * C_VROWS)],
        out_shape=[jax.ShapeDtypeStruct((T, AB_COLS), F32), jax.ShapeDtypeStruct((C_HEADS * C_PAD, T), BF16),
                   jax.ShapeDtypeStruct((C_HEADS, T, C_PAD), BF16),
                   jax.ShapeDtypeStruct((C_HEADS * C_VROWS, T), BF16)],
        compiler_params=_cparams("arbitrary"),
        name="in_proj",
    )(x, wab, wc, wq, wqs, wk, wv, gq, gkv, cs, sn)


def _banded_kernel(sink_ref, *refs, dil, n_heads, group, has_sink, want_lse):
    nq = n_heads // 2
    nkv = n_heads // group // 2
    q_refs, refs = refs[:nq], refs[nq:]
    k_refs, v_refs, kp_refs, vp_refs = (refs[i * nkv:(i + 1) * nkv] for i in range(4))
    bias_ref = refs[4 * nkv]
    o_refs = refs[4 * nkv + 1:4 * nkv + 1 + nq]
    lse_refs = refs[4 * nkv + 1 + nq:]
    c = pl.program_id(0)
    span = BLK * dil
    nb = CHUNK // span
    n_combo = nb * dil
    starts = [b * span + r for b in range(nb) for r in range(dil)]
    scale = HEAD_DIM ** -0.5

    def rows(start):
        return pl.ds(start, BLK) if dil == 1 else pl.ds(start, BLK, stride=dil)

    def gather(ref):
        return jnp.stack([ref[rows(s), :] for s in starts], axis=0)

    def with_prev(ref, prev_ref):
        cur = gather(ref)
        first = jnp.stack([prev_ref[rows(r), :] for r in range(dil)], axis=0)
        prev = jnp.concatenate([first, cur[:n_combo - dil]], axis=0) if dil < n_combo else first
        return jnp.concatenate([prev, cur], axis=1).astype(BF16)

    q = [gather(ref).astype(BF16) for ref in q_refs]
    kcat = [with_prev(k_refs[j], kp_refs[j]) for j in range(nkv)]
    vcat = [with_prev(v_refs[j], vp_refs[j]) for j in range(nkv)]

    lane = lax.broadcasted_iota(jnp.int32, (1, 1, LANES), 2)
    low = lane < HEAD_DIM
    combo = lax.broadcasted_iota(jnp.int32, (n_combo, 1, 2 * BLK), 0)
    col = lax.broadcasted_iota(jnp.int32, (n_combo, 1, 2 * BLK), 2)
    no_prev = jnp.logical_and(c == 0, jnp.logical_and(combo < dil, col < BLK))

    for j in range(nq):
        o_par, lse_par = [], []
        for par in range(2):
            h = 2 * j + par if group == 1 else group * par + j
            kvb = j if group == 1 else 0
            keep = low if par == 0 else jnp.logical_not(low)
            kk = jnp.where(keep, kcat[kvb], jnp.zeros((), BF16))
            s = _bdot_nt(q[j], kk)
            s = s * scale + bias_ref[h][None]
            s = jnp.where(no_prev, NEG_INF, s)
            m = jnp.max(s, axis=-1, keepdims=True)
            if has_sink:
                snk = sink_ref[h]
                m = jnp.maximum(m, snk)
            e = jnp.exp(s - m)
            l = jnp.sum(e, axis=-1, keepdims=True)
            if has_sink:
                l = l + jnp.exp(snk - m)
            p = (e * (1.0 / l)).astype(BF16)
            o_par.append(_bdot(p, vcat[kvb]))
            if want_lse:
                lse_par.append(m + jnp.log(l))
        o_pair = jnp.where(low, o_par[0], o_par[1])
        if want_lse:
            lse_pair = jnp.where(low, lse_par[0], lse_par[1])
        for ci, s0 in enumerate(starts):
            o_refs[j][rows(s0), :] = o_pair[ci]
            if want_lse:
                lse_refs[j][rows(s0), :] = lse_pair[ci]


def _banded(hab, sink, bias, *, q_col, k_col, v_col, dil, n_heads, group, has_sink, want_lse):
    T = hab.shape[0]
    span = BLK * dil
    nb = CHUNK // span
    nq = n_heads // 2
    nkv = n_heads // group // 2
    kernel = functools.partial(_banded_kernel, dil=dil, n_heads=n_heads, group=group,
                               has_sink=has_sink, want_lse=want_lse)
    prev = lambda cb: pl.BlockSpec((span, LANES), lambda c, s: (jnp.maximum(c * nb - 1, 0), cb))
    cur = lambda cb: pl.BlockSpec((CHUNK, LANES), lambda c, s: (c, cb))
    in_specs = ([cur(q_col + j) for j in range(nq)] + [cur(k_col + j) for j in range(nkv)]
                + [cur(v_col + j) for j in range(nkv)] + [prev(k_col + j) for j in range(nkv)]
                + [prev(v_col + j) for j in range(nkv)] + [pl.BlockSpec(bias.shape, lambda c, s: (0, 0, 0))])
    n_out = nq * (2 if want_lse else 1)
    grid_spec = pltpu.PrefetchScalarGridSpec(
        num_scalar_prefetch=1,
        grid=(T // CHUNK,),
        in_specs=in_specs,
        out_specs=[pl.BlockSpec((CHUNK, LANES), lambda c, s: (c, 0))] * n_out,
    )
    outs = pl.pallas_call(
        kernel,
        grid_spec=grid_spec,
        out_shape=[jax.ShapeDtypeStruct((T, LANES), F32)] * n_out,
        compiler_params=_cparams("arbitrary"),
        name=f"banded_d{dil}" + ("_sink" if has_sink else ""),
    )(sink, *([hab] * (nq + 4 * nkv)), bias)
    return outs[:nq], outs[nq:]


def _flash_kernel(qt_ref, k_ref, vt_ref, ot_ref, m_ref, acc_ref, s_ref):
    qi = pl.program_id(1)
    qt = qt_ref[...]
    m_ref[...] = jnp.full(m_ref.shape, NEG_INF, F32)
    acc_ref[...] = jnp.zeros(acc_ref.shape, F32)

    def scores(j):
        off = pl.multiple_of(j * TK, TK)
        return _dot(k_ref[pl.ds(off, TK), :], qt)

    def accumulate(j, s):
        off = pl.multiple_of(j * TK, TK)
        m_old = m_ref[...]
        m_new = jnp.maximum(m_old, jnp.max(s, axis=0, keepdims=True))
        alpha = jnp.exp2(m_old - m_new)
        p = jnp.exp2(s - m_new).astype(BF16)
        acc_ref[...] = alpha * acc_ref[...] + _dot(vt_ref[:, pl.ds(off, TK)], p)
        m_ref[...] = m_new

    def diagonal(s, d):
        key = lax.broadcasted_iota(jnp.int32, (TK, TQ), 0)
        qry = lax.broadcasted_iota(jnp.int32, (TK, TQ), 1)
        return jnp.where(key + d * TK <= qry, s, NEG_INF)

    n_full = qi * N_DIAG
    s_ref[0] = scores(0)
    n_main = n_full // FLASH_UNROLL

    def main_body(i, carry):
        j = FLASH_UNROLL * i
        for u in range(FLASH_UNROLL):
            s_ref[(u + 1) % 2] = scores(j + u + 1)
            accumulate(j + u, s_ref[u % 2])
        return carry

    lax.fori_loop(0, n_main, main_body, 0)

    base = FLASH_UNROLL * n_main
    for rem in range(0, FLASH_UNROLL, math.gcd(FLASH_UNROLL, N_DIAG)):
        @pl.when(n_full - base == rem)
        def _():
            n_tail = rem + N_DIAG
            for u in range(n_tail):
                if u + 1 < n_tail:
                    s_ref[(u + 1) % 2] = scores(base + u + 1)
                s = s_ref[u % 2]
                accumulate(base + u, s if u < rem else diagonal(s, u - rem))

    ot_ref[...] = (acc_ref[:C_V, :] / acc_ref[C_V:C_V + 1, :]).astype(ot_ref.dtype)


def _flash(qt, k, vt):
    H, T, _ = k.shape
    return pl.pallas_call(
        _flash_kernel,
        grid=(H, T // TQ),
        in_specs=[pl.BlockSpec((C_PAD, TQ), lambda h, i: (h, i)),
                  pl.BlockSpec((None, T, C_PAD), lambda h, i: (h, 0, 0)),
                  pl.BlockSpec((C_VROWS, T), lambda h, i: (h, 0))],
        out_specs=pl.BlockSpec((C_V, TQ), lambda h, i: (h, i)),
        out_shape=jax.ShapeDtypeStruct((H * C_V, T), BF16),
        scratch_shapes=[pltpu.VMEM((1, TQ), F32), pltpu.VMEM((C_VROWS, TQ), F32), pltpu.VMEM((2, TK, TQ), F32)],
        compiler_params=_cparams("arbitrary", "arbitrary"),
        name="mla_flash",
    )(qt, k, vt)


def _layer_norm(z, g, b):
    mu = jnp.mean(z, -1, keepdims=True)
    zc = z - mu
    var = jnp.mean(zc * zc, -1, keepdims=True)
    return zc * lax.rsqrt(var + 1e-5) * g + b


def _out_router_kernel(*refs):
    n_ab = 2 + 6 + 6
    oa_refs, ob_refs, lse_refs = refs[0:2], refs[2:8], refs[8:14]
    (oc_ref, x_ref, woa_ref, wob_ref, woc_ref, g_ref, b_ref, wr_ref, br_ref,
     x1_ref, slab_ref, cnt_ref, carry_ref, prev_ref) = refs[n_ab:]
    i = pl.program_id(0)
    tm = x_ref.shape[0]

    @pl.when(i == 0)
    def _():
        carry_ref[...] = jnp.zeros(carry_ref.shape, F32)
        prev_ref[...] = jnp.zeros(prev_ref.shape, F32)

    xr = prev_ref[...]

    y = None
    for j in range(2):
        l0, l1, l2 = (lse_refs[2 * g + j][...] for g in range(3))
        mx = jnp.maximum(jnp.maximum(l0, l1), l2)
        e0, e1, e2 = jnp.exp(l0 - mx), jnp.exp(l1 - mx), jnp.exp(l2 - mx)
        den = e0 + e1 + e2
        ob = ((e0 / den) * ob_refs[j][...] + (e1 / den) * ob_refs[2 + j][...] + (e2 / den) * ob_refs[4 + j][...])
        part = _dot(oa_refs[j][...].astype(BF16), woa_ref[j]) + _dot(ob.astype(BF16), wob_ref[j])
        y = part if y is None else y + part
    y = y + _dot(oc_ref[...].astype(F32).T.astype(BF16), woc_ref[...])
    x1 = _layer_norm(DEEPNORM_ALPHA * x_ref[...] + y, g_ref[...], b_ref[...])
    x1_ref[...] = x1
    prev_ref[...] = x1

    xr_hi = xr.astype(BF16)
    xr_lo = (xr - xr_hi.astype(F32)).astype(BF16)
    logits = _dot(jnp.concatenate([xr_hi, xr_lo, xr_hi], axis=1), wr_ref[...]) + br_ref[...]
    lane = lax.broadcasted_iota(jnp.int32, (tm, LANES), 1)
    ninf = jnp.float32(-jnp.inf)
    gmask = lane < N_GROUPS
    gl = jnp.where(gmask, logits, ninf)
    gmax = jnp.max(gl, axis=-1, keepdims=True)
    grp = jnp.min(jnp.where(gl == gmax, lane, LANES), axis=-1, keepdims=True)
    p_grp = 1.0 / jnp.sum(jnp.where(gmask, jnp.exp(logits - gmax), 0.0), axis=-1, keepdims=True)
    lo = N_GROUPS + EXPERTS_PER_GROUP * grp
    emask = jnp.logical_and(lane >= lo, lane < lo + EXPERTS_PER_GROUP)
    el = jnp.where(emask, logits, ninf)
    v0 = jnp.max(el, axis=-1, keepdims=True)
    i0 = jnp.min(jnp.where(el == v0, lane, LANES), axis=-1, keepdims=True)
    el2 = jnp.where(lane == i0, ninf, el)
    v1 = jnp.max(el2, axis=-1, keepdims=True)
    i1 = jnp.min(jnp.where(el2 == v1, lane, LANES), axis=-1, keepdims=True)
    t = jnp.exp(v1 - v0)
    gate0 = p_grp / (1.0 + t)
    gate1 = p_grp * t / (1.0 + t)

    hit0 = lane == i0
    hit1 = lane == i1
    cmat = jnp.logical_or(hit0, hit1).astype(BF16)
    tr = lax.broadcasted_iota(jnp.int32, (tm, tm), 0)
    tc = lax.broadcasted_iota(jnp.int32, (tm, tm), 1)
    tril = (tc < tr).astype(BF16)
    prefix = _dot(tril, cmat) + carry_ref[...]
    r0 = jnp.sum(jnp.where(hit0, prefix, 0.0), axis=-1, keepdims=True)
    r1 = jnp.sum(jnp.where(hit1, prefix, 0.0), axis=-1, keepdims=True)
    counted = (i > 0).astype(F32)
    carry_ref[...] = carry_ref[...] + counted * jnp.sum(cmat.astype(F32), axis=0, keepdims=True)
    cnt_ref[...] = carry_ref[...]

    fields = [(i0 - N_GROUPS).astype(F32), (i1 - N_GROUPS).astype(F32), r0, r1, gate0, gate1]
    slab = jnp.zeros((tm, LANES), F32)
    for n, f in enumerate(fields):
        slab = jnp.where(lane == n, f, slab)
    slab_ref[...] = slab


def _out_router(oa, ob, lse, oc, x, woa, wob, woc, g, b, wr, br, layer):
    T = x.shape[0]
    tm = TM_OUT
    n = T // tm
    full = lambda a: _layer_block(a, layer)
    row = lambda w: pl.BlockSpec((tm, w), lambda i: (jnp.minimum(i, n - 1), 0))
    lag = pl.BlockSpec((tm, LANES), lambda i: (jnp.maximum(i - 1, 0), 0))
    return pl.pallas_call(
        _out_router_kernel,
        grid=(n + 1,),
        in_specs=[row(LANES)] * 14 + [pl.BlockSpec((C_HEADS * C_V, tm), lambda i: (0, jnp.minimum(i, n - 1))),
                                      row(D_MODEL),
                                      full(woa), full(wob), full(woc), full(g), full(b), full(wr), full(br)],
        out_specs=[row(D_MODEL), lag, pl.BlockSpec((1, LANES), lambda i: (0, 0))],
        out_shape=[jax.ShapeDtypeStruct((T, D_MODEL), F32), jax.ShapeDtypeStruct((T, LANES), F32),
                   jax.ShapeDtypeStruct((1, LANES), F32)],
        scratch_shapes=[pltpu.VMEM((1, LANES), F32), pltpu.VMEM((tm, D_MODEL), F32)],
        compiler_params=_cparams("arbitrary"),
        name="out_router",
    )(*oa, *ob, *lse, oc, x, woa, wob, woc, g, b, wr, br)


SUBLANES = 8
PAD_BITS = tuple(1 << k for k in reversed(range(SUBLANES.bit_length() - 1, ROW_BLK.bit_length() - 1)))


def _dispatch_kernel(d0_ref, d1_ref, pad_at_ref, pad_len_ref, nu_ref, x_ref, xs_ref, zero_ref, sem, zsem):
    i = pl.program_id(0)
    tm = x_ref.shape[0]
    base = i * tm

    @pl.when(i == 0)
    def _():
        zero_ref[...] = jnp.zeros(zero_ref.shape, F32)

        def pad_copies(e):
            n = pad_len_ref[e]
            at = pad_at_ref[e]
            head = n & (SUBLANES - 1)
            out = [(j < head, pltpu.make_async_copy(zero_ref.at[pl.ds(0, 1), :], xs_ref.at[pl.ds(at + j, 1), :], zsem))
                   for j in range(SUBLANES - 1)]
            body = n - head
            for bit in PAD_BITS:
                start = pl.multiple_of(at + head + (body & ~(2 * bit - 1)), SUBLANES)
                out.append(((body & bit) != 0,
                            pltpu.make_async_copy(zero_ref.at[pl.ds(0, bit), :], xs_ref.at[pl.ds(start, bit), :], zsem)))
            return out

        def spare_copies():
            n_blocks = xs_ref.shape[0] // ROW_BLK
            first = d0_ref.shape[0] * TOP_K // ROW_BLK
            return [(b >= nu_ref[0], pltpu.make_async_copy(zero_ref, xs_ref.at[pl.ds(b * ROW_BLK, ROW_BLK), :], zsem))
                    for b in range(first, n_blocks)]

        for e in range(N_EXPERTS):
            for on, cp in pad_copies(e):
                pl.when(on)(cp.start)
        for on, cp in spare_copies():
            pl.when(on)(cp.start)
        for e in range(N_EXPERTS):
            for on, cp in pad_copies(e):
                pl.when(on)(cp.wait)
        for on, cp in spare_copies():
            pl.when(on)(cp.wait)

    def copy(t, d_ref):
        return pltpu.make_async_copy(x_ref.at[pl.ds(t, 1), :], xs_ref.at[pl.ds(d_ref[base + t], 1), :], sem)

    def start(t, carry):
        copy(t, d0_ref).start()
        copy(t, d1_ref).start()
        return carry

    lax.fori_loop(0, tm, start, 0, unroll=8)
    for _ in range(2):
        pltpu.make_async_copy(x_ref, xs_ref.at[pl.ds(0, tm), :], sem).wait()


def _dispatch(d0, d1, pad_at, pad_len, n_used, x1, n_rows):
    T = x1.shape[0]
    tm = ROW_BLK
    grid_spec = pltpu.PrefetchScalarGridSpec(
        num_scalar_prefetch=5,
        grid=(T // tm,),
        in_specs=[pl.BlockSpec((tm, D_MODEL), lambda i, *_: (i, 0))],
        out_specs=pl.BlockSpec(memory_space=pl.ANY),
        scratch_shapes=[pltpu.VMEM((ROW_BLK, D_MODEL), F32), pltpu.SemaphoreType.DMA, pltpu.SemaphoreType.DMA],
    )
    return pl.pallas_call(
        _dispatch_kernel,
        grid_spec=grid_spec,
        out_shape=jax.ShapeDtypeStruct((n_rows, D_MODEL), F32),
        compiler_params=_cparams("arbitrary"),
        name="moe_dispatch",
    )(d0, d1, pad_at, pad_len, n_used, x1)


FFN_BLOCKS = 1


def _ffn_kernel(be_ref, nu_ref, xs_ref, *refs):
    del be_ref
    w_refs, ys_ref = refs[:-1], refs[-1]
    i = pl.program_id(0)
    for k in range(FFN_BLOCKS):
        wg_ref, wu_ref, wd_ref = w_refs[3 * k:3 * k + 3]
        rows = pl.ds(k * ROW_BLK, ROW_BLK)
        used = FFN_BLOCKS * i + k < nu_ref[0]

        @pl.when(used)
        def _():
            xb = xs_ref[rows, :].astype(BF16)
            a = _dot(xb, wg_ref[...].astype(BF16))
            u = _dot(xb, wu_ref[...].astype(BF16))
            hmid = (a * jax.nn.sigmoid(a) * u).astype(BF16)
            ys_ref[rows, :] = _dot(hmid, wd_ref[...].astype(BF16))

        @pl.when(jnp.logical_not(used))
        def _():
            ys_ref[rows, :] = jnp.zeros((ROW_BLK, D_MODEL), F32)


def _ffn(block_expert, n_used, xs, wg, wu, wd, layer):
    n_rows = xs.shape[0]
    step_rows = FFN_BLOCKS * ROW_BLK
    assert n_rows % step_rows == 0
    w_spec = lambda shape, k: pl.BlockSpec((None, None) + shape,
                                           lambda i, be, nu: (layer, be[FFN_BLOCKS * i + k], 0, 0))
    w_specs = [w_spec(s, k) for k in range(FFN_BLOCKS)
               for s in ((D_MODEL, D_EXPERT), (D_MODEL, D_EXPERT), (D_EXPERT, D_MODEL))]
    grid_spec = pltpu.PrefetchScalarGridSpec(
        num_scalar_prefetch=2,
        grid=(n_rows // step_rows,),
        in_specs=[pl.BlockSpec((step_rows, D_MODEL),
                               lambda i, be, nu: (jnp.minimum(i, (nu[0] - 1) // FFN_BLOCKS), 0))] + w_specs,
        out_specs=pl.BlockSpec((step_rows, D_MODEL), lambda i, be, nu: (i, 0)),
    )
    return pl.pallas_call(
        _ffn_kernel,
        grid_spec=grid_spec,
        out_shape=jax.ShapeDtypeStruct((n_rows, D_MODEL), F32),
        compiler_params=_cparams("arbitrary"),
        name="moe_ffn",
    )(block_expert, n_used, xs, *([wg, wu, wd] * FFN_BLOCKS))


def _combine_kernel(d0_ref, d1_ref, ys_ref, x1_ref, slab_ref, g_ref, b_ref, o_ref, buf_ref, sem):
    i = pl.program_id(0)
    n = pl.num_programs(0)
    tm = x1_ref.shape[0]

    def gather(tile):
        slot = tile % 2
        base = tile * tm

        def start(t, carry):
            for k, d_ref in enumerate((d0_ref, d1_ref)):
                pltpu.make_async_copy(ys_ref.at[pl.ds(d_ref[base + t], 1), :],
                                      buf_ref.at[slot, k, pl.ds(t, 1), :], sem.at[slot]).start()
            return carry

        lax.fori_loop(0, tm, start, 0, unroll=8)

    @pl.when(i == 0)
    def _():
        gather(0)

    @pl.when(i + 1 < n)
    def _():
        gather(i + 1)

    slot = i % 2
    for k in range(2):
        pltpu.make_async_copy(ys_ref.at[pl.ds(0, tm), :], buf_ref.at[slot, k], sem.at[slot]).wait()
    slab = slab_ref[...]
    y = slab[:, 4:5] * buf_ref[slot, 0] + slab[:, 5:6] * buf_ref[slot, 1]
    o_ref[...] = _layer_norm(DEEPNORM_ALPHA * x1_ref[...] + y, g_ref[...], b_ref[...])


def _combine(d0, d1, ys, x1, slab, g, b, layer):
    T = x1.shape[0]
    tm = ROW_BLK
    grid_spec = pltpu.PrefetchScalarGridSpec(
        num_scalar_prefetch=2,
        grid=(T // tm,),
        in_specs=[pl.BlockSpec(memory_space=pl.ANY),
                  pl.BlockSpec((tm, D_MODEL), lambda i, a, c: (i, 0)),
                  pl.BlockSpec((tm, LANES), lambda i, a, c: (i, 0)),
                  _layer_block(g, layer), _layer_block(b, layer)],
        out_specs=pl.BlockSpec((tm, D_MODEL), lambda i, a, c: (i, 0)),
        scratch_shapes=[pltpu.VMEM((2, 2, tm, D_MODEL), F32), pltpu.SemaphoreType.DMA((2,))],
    )
    return pl.pallas_call(
        _combine_kernel,
        grid_spec=grid_spec,
        out_shape=jax.ShapeDtypeStruct((T, D_MODEL), F32),
        compiler_params=_cparams("arbitrary"),
        name="moe_combine",
    )(d0, d1, ys, x1, slab, g, b)


def _t5_bucket(dist):
    max_exact = N_BUCKETS // 2
    large = max_exact + (jnp.log(jnp.maximum(dist, 1).astype(F32) / max_exact)
                         / math.log(MAX_DISTANCE / max_exact) * (N_BUCKETS - max_exact)).astype(jnp.int32)
    return jnp.where(dist < max_exact, dist, jnp.minimum(large, N_BUCKETS - 1))


def _bias_table(table_cols, dil, max_dist):
    dist = BLK + jnp.arange(BLK)[:, None] - jnp.arange(2 * BLK)[None, :]
    bucket = _t5_bucket(jnp.maximum(dist * dil, 0))
    hit = bucket[None, :, :, None] == jnp.arange(N_BUCKETS)[None, None, None, :]
    bias = jnp.sum(jnp.where(hit, table_cols.T.astype(F32)[:, None, None, :], 0.0), axis=-1)
    mask = (dist >= 0) & (dist <= max_dist)
    return jnp.where(mask[None], bias, NEG_INF)


def _rope_tables(T):
    inv_freq = ROPE_THETA ** (-jnp.arange(0, C_ROPE, 2, dtype=F32) / C_ROPE)
    ang = jnp.arange(T, dtype=F32)[:, None] * inv_freq[None, :]
    cos, sin = jnp.cos(ang), jnp.sin(ang)
    ones = jnp.ones((T, C_NOPE), F32)
    zeros_n = jnp.zeros((T, C_NOPE), F32)
    zeros_p = jnp.zeros((T, C_PAD - C_NOPE - C_ROPE), F32)
    cs = jnp.concatenate([ones, cos, cos, zeros_p], axis=1)
    sn = jnp.concatenate([zeros_n, -sin, sin, zeros_p], axis=1)
    return cs, sn


def _swap_halves(w):
    half = w.shape[-1] // 2
    return jnp.concatenate([w[..., half:], w[..., :half]], axis=-1)


def _pad_cols(w, width):
    return jnp.pad(w, ((0, 0), (0, width - w.shape[1])))


def _prep_layer_weights(w_in, w_out, w_uq, w_ukv, w_rg, b_rg, w_re, b_re):
    head_cols = lambda w, h: w[:, h * HEAD_DIM:(h + 1) * HEAD_DIM]
    wab = jnp.concatenate([head_cols(w_in, h) for h in A_HEAD_ORDER] + [w_in[:, A_Q_HEADS * HEAD_DIM:AB_COLS]],
                          axis=1).astype(BF16)
    c0 = AB_COLS
    w_cq = w_in[:, c0:c0 + Q_LORA]
    w_ckv = w_in[:, c0 + Q_LORA:c0 + Q_LORA + KV_LORA]
    w_kr = w_in[:, c0 + Q_LORA + KV_LORA:]
    lead = jnp.zeros((D_MODEL, C_NOPE), F32)
    wc = jnp.concatenate([w_cq, w_ckv,
                          _pad_cols(jnp.concatenate([lead, w_kr], 1), C_PAD),
                          _pad_cols(jnp.concatenate([lead, _swap_halves(w_kr)], 1), C_PAD)], axis=1).astype(BF16)
    uq = w_uq.reshape(Q_LORA, C_HEADS, C_NOPE + C_ROPE)
    wq = jnp.pad(uq, ((0, 0), (0, 0), (0, C_PAD - C_NOPE - C_ROPE))).reshape(Q_LORA, C_HEADS * C_PAD).astype(BF16)
    uqs = jnp.concatenate([jnp.zeros((Q_LORA, C_HEADS, C_NOPE), F32), _swap_halves(uq[..., C_NOPE:])], axis=-1)
    wqs = jnp.pad(uqs, ((0, 0), (0, 0), (0, C_PAD - C_NOPE - C_ROPE))).reshape(Q_LORA, C_HEADS * C_PAD).astype(BF16)
    ukv = w_ukv.reshape(KV_LORA, C_HEADS, C_NOPE + C_V)
    wk = jnp.pad(ukv[..., :C_NOPE], ((0, 0), (0, 0), (0, C_PAD - C_NOPE))).reshape(KV_LORA, C_HEADS * C_PAD)
    wv = jnp.pad(ukv[..., C_NOPE:], ((0, 0), (0, 0), (0, C_PAD - C_V))).reshape(KV_LORA, C_HEADS * C_PAD)
    woa = jnp.concatenate([w_out[h * HEAD_DIM:(h + 1) * HEAD_DIM] for h in A_HEAD_ORDER],
                          axis=0).reshape(2, LANES, D_MODEL).astype(BF16)
    wob = w_out[256:512].reshape(2, LANES, D_MODEL).astype(BF16)
    woc = w_out[512:].astype(BF16)
    wr = _pad_cols(jnp.concatenate([w_rg, w_re.transpose(1, 0, 2).reshape(D_MODEL, N_EXPERTS)], axis=1), LANES)
    wr_hi = wr.astype(BF16)
    wr_lo = (wr - wr_hi.astype(F32)).astype(BF16)
    wr = jnp.concatenate([wr_hi, wr_hi, wr_lo], axis=0)
    br = _pad_cols(jnp.concatenate([b_rg, b_re.reshape(N_EXPERTS)])[None, :], LANES)
    return dict(wab=wab, wc=wc, wq=wq, wqs=wqs, wk=wk.astype(BF16), wv=wv.astype(BF16),
                woa=woa, wob=wob, woc=woc, wr=wr, br=br)


def kernel(x, w_in, w_out, sinks, rel_bias, mla_q_norm, mla_kv_norm, w_uq, w_ukv, ln1_g, ln1_b,
           w_route_group, b_route_group, w_route_expert, b_route_expert,
           w_expert_gate, w_expert_up, w_expert_down, ln2_g, ln2_b):
    Bsz, T, D = x.shape
    assert Bsz == 1 and D == D_MODEL and T % CHUNK == 0
    xt = x.reshape(T, D)
    cs, sn = _rope_tables(T)
    bias_a = _bias_table(rel_bias[:, :A_Q_HEADS], 1, SWA_WINDOW - 1)
    bias_b = [_bias_table(rel_bias[:, A_Q_HEADS + g * B_HEADS_PER_GROUP:A_Q_HEADS + (g + 1) * B_HEADS_PER_GROUP],
                          dil, window // dil) for g, (window, dil) in enumerate(DILATED_PAIRS)]
    n_assign = T * TOP_K
    n_blocks = n_assign // ROW_BLK + N_EXPERTS
    n_rows = n_blocks * ROW_BLK
    no_sink = jnp.zeros((B_HEADS_PER_GROUP,), F32)

    w = jax.vmap(_prep_layer_weights)(w_in, w_out, w_uq, w_ukv, w_route_group, b_route_group,
                                      w_route_expert, b_route_expert)
    per_layer_row = lambda a: a[:, None, :]
    gq, gkv = per_layer_row(mla_q_norm), per_layer_row(mla_kv_norm)
    g1, b1, g2, b2 = (per_layer_row(a) for a in (ln1_g, ln1_b, ln2_g, ln2_b))

    for layer in range(w_in.shape[0]):
        hab, qt, k, vt = _in_proj(xt, w["wab"], w["wc"], w["wq"], w["wqs"], w["wk"], w["wv"], gq, gkv, cs, sn,
                                  layer)
        oa, _ = _banded(hab, sinks[layer], bias_a, q_col=0, k_col=2, v_col=3, dil=1,
                        n_heads=A_Q_HEADS, group=A_Q_HEADS // A_KV_HEADS, has_sink=True, want_lse=False)
        ob, lse = [], []
        for g, (_, dil) in enumerate(DILATED_PAIRS):
            o_g, lse_g = _banded(hab, no_sink, bias_b[g], q_col=4 + 2 * g, k_col=10 + 2 * g, v_col=16 + 2 * g,
                                 dil=dil, n_heads=B_HEADS_PER_GROUP, group=1, has_sink=False, want_lse=True)
            ob += o_g
            lse += lse_g
        oc = _flash(qt, k, vt)
        x1, slab, cnt = _out_router(oa, ob, lse, oc, xt, w["woa"], w["wob"], w["woc"], g1, b1, w["wr"], w["br"],
                                    layer)

        counts = cnt[0, N_GROUPS:N_GROUPS + N_EXPERTS].astype(jnp.int32)
        padded = ((counts + ROW_BLK - 1) // ROW_BLK) * ROW_BLK
        pad_end = jnp.cumsum(padded)
        pad_start = pad_end - padded
        eids = jnp.arange(N_EXPERTS, dtype=jnp.int32)
        pick = lambda e: jnp.sum(jnp.where(e.astype(jnp.int32)[:, None] == eids[None, :], pad_start[None, :], 0), axis=1)
        d0 = pick(slab[:, 0]) + slab[:, 2].astype(jnp.int32)
        d1 = pick(slab[:, 1]) + slab[:, 3].astype(jnp.int32)
        block_start = jnp.arange(n_blocks, dtype=jnp.int32) * ROW_BLK
        block_expert = jnp.minimum(jnp.sum((pad_end[None, :] <= block_start[:, None]).astype(jnp.int32), axis=1),
                                   N_EXPERTS - 1)

        n_used = (pad_end[-1:] // ROW_BLK).astype(jnp.int32)
        xs = _dispatch(d0, d1, pad_start + counts, padded - counts, n_used, x1, n_rows)
        ys = _ffn(block_expert, n_used, xs, w_expert_gate, w_expert_up, w_expert_down, layer)
        xt = _combine(d0, d1, ys, x1, slab, g2, b2, layer)
    return xt.reshape(Bsz, T, D)
```

```python
import functools
import math

import jax
import jax.numpy as jnp
from jax import lax
from jax.experimental import pallas as pl
from jax.experimental.pallas import tpu as pltpu

F32 = jnp.float32
BF16 = jnp.bfloat16

D_MODEL = 1024
DEPTH = 4
HEAD_DIM = 64
BLK = 128
NEG_INF = -1e30

SWA_WINDOW = 128
A_Q_HEADS = 4
A_KV_HEADS = 2
A_COLS = (A_Q_HEADS + 2 * A_KV_HEADS) * HEAD_DIM
A_HEAD_ORDER = (0, 2, 1, 3)

DILATED_PAIRS = ((128, 1), (512, 4), (2048, 16))
B_HEADS_PER_GROUP = 4
B_HEADS = len(DILATED_PAIRS) * B_HEADS_PER_GROUP
B_COLS = 3 * B_HEADS * HEAD_DIM
AB_COLS = A_COLS + B_COLS

C_HEADS = 8
C_NOPE = 64
C_ROPE = 32
C_V = 64
Q_LORA = 256
KV_LORA = 128
ROPE_THETA = 10000.0
C_PAD = 128
C_VROWS = 80

N_BUCKETS = 32
MAX_DISTANCE = 2048

N_GROUPS = 4
EXPERTS_PER_GROUP = 8
N_EXPERTS = N_GROUPS * EXPERTS_PER_GROUP
TOP_K = 2
D_EXPERT = 256
ROW_BLK = 512

DEEPNORM_ALPHA = (2 * DEPTH) ** 0.25

VMEM_LIMIT = 56 * 1024 * 1024

CHUNK = 2048
TM_IN = 512
TM_OUT = 512
TQ = 1024
TK = 512
N_DIAG = TQ // TK
FLASH_UNROLL = 4
LANES = 128


def _dot(a, b):
    return jnp.dot(a, b, preferred_element_type=F32)


def _dot_nt(a, b):
    return lax.dot_general(a, b, (((1,), (1,)), ((), ())), preferred_element_type=F32)


def _bdot(a, b):
    return lax.dot_general(a, b, (((2,), (1,)), ((0,), (0,))), preferred_element_type=F32)


def _bdot_nt(a, b):
    return lax.dot_general(a, b, (((2,), (2,)), ((0,), (0,))), preferred_element_type=F32)


def _cparams(*sem):
    return pltpu.CompilerParams(dimension_semantics=sem, vmem_limit_bytes=VMEM_LIMIT)


def _in_proj_kernel(x_ref, wab_ref, wc_ref, wq_ref, wqs_ref, wk_ref, wv_ref, gq_ref, gkv_ref, cs_ref, sn_ref,
                    hab_ref, qt_ref, k_ref, vt_ref):
    xb = x_ref[...].astype(BF16)
    hab_ref[...] = _dot(xb, wab_ref[...])
    hc = _dot(xb, wc_ref[...])
    cq = hc[:, :Q_LORA]
    ckv = hc[:, Q_LORA:Q_LORA + KV_LORA]
    kr = hc[:, Q_LORA + KV_LORA:Q_LORA + KV_LORA + C_PAD]
    krs = hc[:, Q_LORA + KV_LORA + C_PAD:]
    cqn = (cq * lax.rsqrt(jnp.mean(cq * cq, -1, keepdims=True) + 1e-6) * gq_ref[...]).astype(BF16)
    ckvn = (ckv * lax.rsqrt(jnp.mean(ckv * ckv, -1, keepdims=True) + 1e-6) * gkv_ref[...]).astype(BF16)
    cs = cs_ref[...]
    sn = sn_ref[...]
    q = _dot(cqn, wq_ref[...])
    qs = _dot(cqn, wqs_ref[...])
    k = _dot(ckvn, wk_ref[...])
    v = _dot(ckvn, wv_ref[...])
    krot = kr * cs + krs * sn
    scale = (C_NOPE + C_ROPE) ** -0.5 * math.log2(math.e)
    for h in range(C_HEADS):
        sl = slice(h * C_PAD, (h + 1) * C_PAD)
        qh = (q[:, sl] * cs + qs[:, sl] * sn) * scale
        qt_ref[sl, :] = qh.T.astype(BF16)
        k_ref[h] = (k[:, sl] + krot).astype(BF16)
    one_row = (lax.broadcasted_iota(jnp.int32, (1, C_PAD), 1) == C_V).astype(F32)
    for h in range(C_HEADS):
        vh = v[:, h * C_PAD:(h + 1) * C_PAD] + one_row
        vt_ref[h * C_VROWS:(h + 1) * C_VROWS, :] = vh.T[:C_VROWS, :].astype(BF16)


def _layer_block(a, layer):
    return pl.BlockSpec((None,) + a.shape[1:], lambda i, *_: (layer,) + (0,) * (a.ndim - 1))


def _in_proj(x, wab, wc, wq, wqs, wk, wv, gq, gkv, cs, sn, layer):
    T = x.shape[0]
    tm = TM_IN
    full = lambda a: _layer_block(a, layer)
    row = lambda w: pl.BlockSpec((tm, w), lambda i: (i, 0))
    colblk = lambda r: pl.BlockSpec((r, tm), lambda i: (0, i))
    return pl.pallas_call(
        _in_proj_kernel,
        grid=(T // tm,),
        in_specs=[row(D_MODEL), full(wab), full(wc), full(wq), full(wqs), full(wk), full(wv), full(gq), full(gkv),
                  row(C_PAD), row(C_PAD)],
        out_specs=[row(AB_COLS), colblk(C_HEADS * C_PAD), pl.BlockSpec((C_HEADS, tm, C_PAD), lambda i: (0, i, 0)),
                   colblk(C_HEADS * C_VROWS)],
        out_shape=[jax.ShapeDtypeStruct((T, AB_COLS), F32), jax.ShapeDtypeStruct((C_HEADS * C_PAD, T), BF16),
                   jax.ShapeDtypeStruct((C_HEADS, T, C_PAD), BF16),
                   jax.ShapeDtypeStruct((C_HEADS * C_VROWS, T), BF16)],
        compiler_params=_cparams("arbitrary"),
        name="in_proj",
    )(x, wab, wc, wq, wqs, wk, wv, gq, gkv, cs, sn)


def _banded_kernel(sink_ref, *refs, dil, n_heads, group, has_sink, want_lse):
    nq = n_heads // 2
    nkv = n_heads // group // 2
    q_refs, refs = refs[:nq], refs[nq:]
    k_refs, v_refs, kp_refs, vp_refs = (refs[i * nkv:(i + 1) * nkv] for i in range(4))
    bias_ref = refs[4 * nkv]
    o_refs = refs[4 * nkv + 1:4 * nkv + 1 + nq]
    lse_refs = refs[4 * nkv + 1 + nq:]
    c = pl.program_id(0)
    span = BLK * dil
    nb = CHUNK // span
    n_combo = nb * dil
    starts = [b * span + r for b in range(nb) for r in range(dil)]
    scale = HEAD_DIM ** -0.5

    def rows(start):
        return pl.ds(start, BLK) if dil == 1 else pl.ds(start, BLK, stride=dil)

    def gather(ref):
        return jnp.stack([ref[rows(s), :] for s in starts], axis=0)

    def with_prev(ref, prev_ref):
        cur = gather(ref)
        first = jnp.stack([prev_ref[rows(r), :] for r in range(dil)], axis=0)
        prev = jnp.concatenate([first, cur[:n_combo - dil]], axis=0) if dil < n_combo else first
        return jnp.concatenate([prev, cur], axis=1).astype(BF16)

    q = [gather(ref).astype(BF16) for ref in q_refs]
    kcat = [with_prev(k_refs[j], kp_refs[j]) for j in range(nkv)]
    vcat = [with_prev(v_refs[j], vp_refs[j]) for j in range(nkv)]

    lane = lax.broadcasted_iota(jnp.int32, (1, 1, LANES), 2)
    low = lane < HEAD_DIM
    combo = lax.broadcasted_iota(jnp.int32, (n_combo, 1, 2 * BLK), 0)
    col = lax.broadcasted_iota(jnp.int32, (n_combo, 1, 2 * BLK), 2)
    no_prev = jnp.logical_and(c == 0, jnp.logical_and(combo < dil, col < BLK))

    for j in range(nq):
        o_par, lse_par = [], []
        for par in range(2):
            h = 2 * j + par if group == 1 else group * par + j
            kvb = j if group == 1 else 0
            keep = low if par == 0 else jnp.logical_not(low)
            kk = jnp.where(keep, kcat[kvb], jnp.zeros((), BF16))
            s = _bdot_nt(q[j], kk)
            s = s * scale + bias_ref[h][None]
            s = jnp.where(no_prev, NEG_INF, s)
            m = jnp.max(s, axis=-1, keepdims=True)
            if has_sink:
                snk = sink_ref[h]
                m = jnp.maximum(m, snk)
            e = jnp.exp(s - m)
            l = jnp.sum(e, axis=-1, keepdims=True)
            if has_sink:
                l = l + jnp.exp(snk - m)
            p = (e * (1.0 / l)).astype(BF16)
            o_par.append(_bdot(p, vcat[kvb]))
            if want_lse:
                lse_par.append(m + jnp.log(l))
        o_pair = jnp.where(low, o_par[0], o_par[1])
        if want_lse:
            lse_pair = jnp.where(low, lse_par[0], lse_par[1])
        for ci, s0 in enumerate(starts):
            o_refs[j][rows(s0), :] = o_pair[ci]
            if want_lse:
                lse_refs[j][rows(s0), :] = lse_pair[ci]


def _banded(hab, sink, bias, *, q_col, k_col, v_col, dil, n_heads, group, has_sink, want_lse):
    T = hab.shape[0]
    span = BLK * dil
    nb = CHUNK // span
    nq = n_heads // 2
    nkv = n_heads // group // 2
    kernel = functools.partial(_banded_kernel, dil=dil, n_heads=n_heads, group=group,
                               has_sink=has_sink, want_lse=want_lse)
    prev = lambda cb: pl.BlockSpec((span, LANES), lambda c, s: (jnp.maximum(c * nb - 1, 0), cb))
    cur = lambda cb: pl.BlockSpec((CHUNK, LANES), lambda c, s: (c, cb))
    in_specs = ([cur(q_col + j) for j in range(nq)] + [cur(k_col + j) for j in range(nkv)]
                + [cur(v_col + j) for j in range(nkv)] + [prev(k_col + j) for j in range(nkv)]
                + [prev(v_col + j) for j in range(nkv)] + [pl.BlockSpec(bias.shape, lambda c, s: (0, 0, 0))])
    n_out = nq * (2 if want_lse else 1)
    grid_spec = pltpu.PrefetchScalarGridSpec(
        num_scalar_prefetch=1,
        grid=(T // CHUNK,),
        in_specs=in_specs,
        out_specs=[pl.BlockSpec((CHUNK, LANES), lambda c, s: (c, 0))] * n_out,
    )
    outs = pl.pallas_call(
        kernel,
        grid_spec=grid_spec,
        out_shape=[jax.ShapeDtypeStruct((T, LANES), F32)] * n_out,
        compiler_params=_cparams("arbitrary"),
        name=f"banded_d{dil}" + ("_sink" if has_sink else ""),
    )(sink, *([hab] * (nq + 4 * nkv)), bias)
    return outs[:nq], outs[nq:]


def _flash_kernel(qt_ref, k_ref, vt_ref, ot_ref, m_ref, acc_ref, s_ref):
    qi = pl.program_id(1)
    qt = qt_ref[...]
    m_ref[...] = jnp.full(m_ref.shape, NEG_INF, F32)
    acc_ref[...] = jnp.zeros(acc_ref.shape, F32)

    def scores(j):
        off = pl.multiple_of(j * TK, TK)
        return _dot(k_ref[pl.ds(off, TK), :], qt)

    def accumulate(j, s):
        off = pl.multiple_of(j * TK, TK)
        m_old = m_ref[...]
        m_new = jnp.maximum(m_old, jnp.max(s, axis=0, keepdims=True))
        alpha = jnp.exp2(m_old - m_new)
        p = jnp.exp2(s - m_new).astype(BF16)
        acc_ref[...] = alpha * acc_ref[...] + _dot(vt_ref[:, pl.ds(off, TK)], p)
        m_ref[...] = m_new

    def diagonal(s, d):
        key = lax.broadcasted_iota(jnp.int32, (TK, TQ), 0)
        qry = lax.broadcasted_iota(jnp.int32, (TK, TQ), 1)
        return jnp.where(key + d * TK <= qry, s, NEG_INF)

    n_full = qi * N_DIAG
    s_ref[0] = scores(0)
    n_main = n_full // FLASH_UNROLL

    def main_body(i, carry):
        j = FLASH_UNROLL * i
        for u in range(FLASH_UNROLL):
            s_ref[(u + 1) % 2] = scores(j + u + 1)
            accumulate(j + u, s_ref[u % 2])
        return carry

    lax.fori_loop(0, n_main, main_body, 0)

    base = FLASH_UNROLL * n_main
    for rem in range(0, FLASH_UNROLL, math.gcd(FLASH_UNROLL, N_DIAG)):
        @pl.when(n_full - base == rem)
        def _():
            n_tail = rem + N_DIAG
            for u in range(n_tail):
                if u + 1 < n_tail:
                    s_ref[(u + 1) % 2] = scores(base + u + 1)
                s = s_ref[u % 2]
                accumulate(base + u, s if u < rem else diagonal(s, u - rem))

    ot_ref[...] = (acc_ref[:C_V, :] / acc_ref[C_V:C_V + 1, :]).astype(ot_ref.dtype)


def _flash(qt, k, vt):
    H, T, _ = k.shape
    return pl.pallas_call(
        _flash_kernel,
        grid=(H, T // TQ),
        in_specs=[pl.BlockSpec((C_PAD, TQ), lambda h, i: (h, i)),
                  pl.BlockSpec((None, T, C_PAD), lambda h, i: (h, 0, 0)),
                  pl.BlockSpec((C_VROWS, T), lambda h, i: (h, 0))],
        out_specs=pl.BlockSpec((C_V, TQ), lambda h, i: (h, i)),
        out_shape=jax.ShapeDtypeStruct((H * C_V, T), BF16),
        scratch_shapes=[pltpu.VMEM((1, TQ), F32), pltpu.VMEM((C_VROWS, TQ), F32), pltpu.VMEM((2, TK, TQ), F32)],
        compiler_params=_cparams("arbitrary", "arbitrary"),
        name="mla_flash",
    )(qt, k, vt)


def _layer_norm(z, g, b):
    mu = jnp.mean(z, -1, keepdims=True)
    zc = z - mu
    var = jnp.mean(zc * zc, -1, keepdims=True)
    return zc * lax.rsqrt(var + 1e-5) * g + b


def _out_router_kernel(*refs):
    n_ab = 2 + 6 + 6
    oa_refs, ob_refs, lse_refs = refs[0:2], refs[2:8], refs[8:14]
    (oc_ref, x_ref, woa_ref, wob_ref, woc_ref, g_ref, b_ref, wr_ref, br_ref,
     x1_ref, slab_ref, cnt_ref, carry_ref, prev_ref) = refs[n_ab:]
    i = pl.program_id(0)
    tm = x_ref.shape[0]

    @pl.when(i == 0)
    def _():
        carry_ref[...] = jnp.zeros(carry_ref.shape, F32)
        prev_ref[...] = jnp.zeros(prev_ref.shape, F32)

    xr = prev_ref[...]

    y = None
    for j in range(2):
        l0, l1, l2 = (lse_refs[2 * g + j][...] for g in range(3))
        mx = jnp.maximum(jnp.maximum(l0, l1), l2)
        e0, e1, e2 = jnp.exp(l0 - mx), jnp.exp(l1 - mx), jnp.exp(l2 - mx)
        den = e0 + e1 + e2
        ob = (e0 * ob_refs[j][...] + e1 * ob_refs[2 + j][...] + e2 * ob_refs[4 + j][...]) * (1.0 / den)
        part = _dot(oa_refs[j][...].astype(BF16), woa_ref[j]) + _dot(ob.astype(BF16), wob_ref[j])
        y = part if y is None else y + part
    y = y + _dot(oc_ref[...].astype(F32).T.astype(BF16), woc_ref[...])
    x1 = _layer_norm(DEEPNORM_ALPHA * x_ref[...] + y, g_ref[...], b_ref[...])
    x1_ref[...] = x1
    prev_ref[...] = x1

    xr_hi = xr.astype(BF16)
    xr_lo = (xr - xr_hi.astype(F32)).astype(BF16)
    logits = _dot(jnp.concatenate([xr_hi, xr_lo, xr_hi], axis=1), wr_ref[...]) + br_ref[...]
    lane = lax.broadcasted_iota(jnp.int32, (tm, LANES), 1)
    ninf = jnp.float32(-jnp.inf)
    gmask = lane < N_GROUPS
    gl = jnp.where(gmask, logits, ninf)
    gmax = jnp.max(gl, axis=-1, keepdims=True)
    grp = jnp.min(jnp.where(gl == gmax, lane, LANES), axis=-1, keepdims=True)
    p_grp = 1.0 / jnp.sum(jnp.where(gmask, jnp.exp(logits - gmax), 0.0), axis=-1, keepdims=True)
    lo = N_GROUPS + EXPERTS_PER_GROUP * grp
    emask = jnp.logical_and(lane >= lo, lane < lo + EXPERTS_PER_GROUP)
    el = jnp.where(emask, logits, ninf)
    v0 = jnp.max(el, axis=-1, keepdims=True)
    i0 = jnp.min(jnp.where(el == v0, lane, LANES), axis=-1, keepdims=True)
    el2 = jnp.where(lane == i0, ninf, el)
    v1 = jnp.max(el2, axis=-1, keepdims=True)
    i1 = jnp.min(jnp.where(el2 == v1, lane, LANES), axis=-1, keepdims=True)
    t = jnp.exp(v1 - v0)
    gate0 = p_grp / (1.0 + t)
    gate1 = p_grp * t / (1.0 + t)

    hit0 = lane == i0
    hit1 = lane == i1
    cmat = jnp.logical_or(hit0, hit1).astype(BF16)
    tr = lax.broadcasted_iota(jnp.int32, (tm, tm), 0)
    tc = lax.broadcasted_iota(jnp.int32, (tm, tm), 1)
    tril = (tc < tr).astype(BF16)
    prefix = _dot(tril, cmat) + carry_ref[...]
    r0 = jnp.sum(jnp.where(hit0, prefix, 0.0), axis=-1, keepdims=True)
    r1 = jnp.sum(jnp.where(hit1, prefix, 0.0), axis=-1, keepdims=True)
    counted = (i > 0).astype(F32)
    carry_ref[...] = carry_ref[...] + counted * jnp.sum(cmat.astype(F32), axis=0, keepdims=True)
    cnt_ref[...] = carry_ref[...]

    fields = [(i0 - N_GROUPS).astype(F32), (i1 - N_GROUPS).astype(F32), r0, r1, gate0, gate1]
    slab = jnp.zeros((tm, LANES), F32)
    for n, f in enumerate(fields):
        slab = jnp.where(lane == n, f, slab)
    slab_ref[...] = slab


def _out_router(oa, ob, lse, oc, x, woa, wob, woc, g, b, wr, br, layer):
    T = x.shape[0]
    tm = TM_OUT
    n = T // tm
    full = lambda a: _layer_block(a, layer)
    row = lambda w: pl.BlockSpec((tm, w), lambda i: (jnp.minimum(i, n - 1), 0))
    lag = pl.BlockSpec((tm, LANES), lambda i: (jnp.maximum(i - 1, 0), 0))
    return pl.pallas_call(
        _out_router_kernel,
        grid=(n + 1,),
        in_specs=[row(LANES)] * 14 + [pl.BlockSpec((C_HEADS * C_V, tm), lambda i: (0, jnp.minimum(i, n - 1))),
                                      row(D_MODEL),
                                      full(woa), full(wob), full(woc), full(g), full(b), full(wr), full(br)],
        out_specs=[row(D_MODEL), lag, pl.BlockSpec((1, LANES), lambda i: (0, 0))],
        out_shape=[jax.ShapeDtypeStruct((T, D_MODEL), F32), jax.ShapeDtypeStruct((T, LANES), F32),
                   jax.ShapeDtypeStruct((1, LANES), F32)],
        scratch_shapes=[pltpu.VMEM((1, LANES), F32), pltpu.VMEM((tm, D_MODEL), F32)],
        compiler_params=_cparams("arbitrary"),
        name="out_router",
    )(*oa, *ob, *lse, oc, x, woa, wob, woc, g, b, wr, br)


SUBLANES = 8
PAD_BITS = tuple(1 << k for k in reversed(range(SUBLANES.bit_length() - 1, ROW_BLK.bit_length() - 1)))


def _dispatch_kernel(d0_ref, d1_ref, pad_at_ref, pad_len_ref, nu_ref, x_ref, xs_ref, zero_ref, stage_ref, sem, zsem):
    i = pl.program_id(0)
    tm = x_ref.shape[0]
    base = i * tm

    @pl.when(i == 0)
    def _():
        zero_ref[...] = jnp.zeros(zero_ref.shape, F32)

        def pad_copies(e):
            n = pad_len_ref[e]
            at = pad_at_ref[e]
            head = n & (SUBLANES - 1)
            out = [(j < head, pltpu.make_async_copy(zero_ref.at[pl.ds(0, 1), :], xs_ref.at[pl.ds(at + j, 1), :], zsem))
                   for j in range(SUBLANES - 1)]
            body = n - head
            for bit in PAD_BITS:
                start = pl.multiple_of(at + head + (body & ~(2 * bit - 1)), SUBLANES)
                out.append(((body & bit) != 0,
                            pltpu.make_async_copy(zero_ref.at[pl.ds(0, bit), :], xs_ref.at[pl.ds(start, bit), :], zsem)))
            return out

        def spare_copies():
            n_blocks = xs_ref.shape[0] // ROW_BLK
            first = d0_ref.shape[0] * TOP_K // ROW_BLK
            return [(b >= nu_ref[0], pltpu.make_async_copy(zero_ref, xs_ref.at[pl.ds(b * ROW_BLK, ROW_BLK), :], zsem))
                    for b in range(first, n_blocks)]

        for e in range(N_EXPERTS):
            for on, cp in pad_copies(e):
                pl.when(on)(cp.start)
        for on, cp in spare_copies():
            pl.when(on)(cp.start)
        for e in range(N_EXPERTS):
            for on, cp in pad_copies(e):
                pl.when(on)(cp.wait)
        for on, cp in spare_copies():
            pl.when(on)(cp.wait)

    slot = i % 2
    stage_ref[slot] = x_ref[...]

    def start(t, carry):
        for d_ref in (d0_ref, d1_ref):
            pltpu.make_async_copy(stage_ref.at[slot, pl.ds(t, 1), :], xs_ref.at[pl.ds(d_ref[base + t], 1), :],
                                  sem.at[slot]).start()
        return carry

    lax.fori_loop(0, tm, start, 0, unroll=8)

    def wait_tile(s):
        for _ in range(2):
            pltpu.make_async_copy(stage_ref.at[s], xs_ref.at[pl.ds(0, tm), :], sem.at[s]).wait()

    @pl.when(i > 0)
    def _():
        wait_tile(1 - slot)

    @pl.when(i == pl.num_programs(0) - 1)
    def _():
        wait_tile(slot)


def _dispatch(d0, d1, pad_at, pad_len, n_used, x1, n_rows):
    T = x1.shape[0]
    tm = ROW_BLK
    grid_spec = pltpu.PrefetchScalarGridSpec(
        num_scalar_prefetch=5,
        grid=(T // tm,),
        in_specs=[pl.BlockSpec((tm, D_MODEL), lambda i, *_: (i, 0))],
        out_specs=pl.BlockSpec(memory_space=pl.ANY),
        scratch_shapes=[pltpu.VMEM((ROW_BLK, D_MODEL), F32), pltpu.VMEM((2, tm, D_MODEL), F32),
                        pltpu.SemaphoreType.DMA((2,)), pltpu.SemaphoreType.DMA],
    )
    return pl.pallas_call(
        _dispatch_kernel,
        grid_spec=grid_spec,
        out_shape=jax.ShapeDtypeStruct((n_rows, D_MODEL), F32),
        compiler_params=_cparams("arbitrary"),
        name="moe_dispatch",
    )(d0, d1, pad_at, pad_len, n_used, x1)


FFN_BLOCKS = 1


def _ffn_kernel(be_ref, nu_ref, xs_ref, *refs):
    del be_ref
    w_refs, ys_ref = refs[:-1], refs[-1]
    i = pl.program_id(0)
    for k in range(FFN_BLOCKS):
        wg_ref, wu_ref, wd_ref = w_refs[3 * k:3 * k + 3]
        rows = pl.ds(k * ROW_BLK, ROW_BLK)
        used = FFN_BLOCKS * i + k < nu_ref[0]

        @pl.when(used)
        def _():
            xb = xs_ref[rows, :].astype(BF16)
            a = _dot(xb, wg_ref[...].astype(BF16))
            u = _dot(xb, wu_ref[...].astype(BF16))
            hmid = (a * jax.nn.sigmoid(a) * u).astype(BF16)
            ys_ref[rows, :] = _dot(hmid, wd_ref[...].astype(BF16))

        @pl.when(jnp.logical_not(used))
        def _():
            ys_ref[rows, :] = jnp.zeros((ROW_BLK, D_MODEL), F32)


def _ffn(block_expert, n_used, xs, wg, wu, wd, layer):
    n_rows = xs.shape[0]
    step_rows = FFN_BLOCKS * ROW_BLK
    assert n_rows % step_rows == 0
    w_spec = lambda shape, k: pl.BlockSpec((None, None) + shape,
                                           lambda i, be, nu: (layer, be[FFN_BLOCKS * i + k], 0, 0))
    w_specs = [w_spec(s, k) for k in range(FFN_BLOCKS)
               for s in ((D_MODEL, D_EXPERT), (D_MODEL, D_EXPERT), (D_EXPERT, D_MODEL))]
    grid_spec = pltpu.PrefetchScalarGridSpec(
        num_scalar_prefetch=2,
        grid=(n_rows // step_rows,),
        in_specs=[pl.BlockSpec((step_rows, D_MODEL),
                               lambda i, be, nu: (jnp.minimum(i, (nu[0] - 1) // FFN_BLOCKS), 0))] + w_specs,
        out_specs=pl.BlockSpec((step_rows, D_MODEL), lambda i, be, nu: (i, 0)),
    )
    return pl.pallas_call(
        _ffn_kernel,
        grid_spec=grid_spec,
        out_shape=jax.ShapeDtypeStruct((n_rows, D_MODEL), F32),
        compiler_params=_cparams("arbitrary"),
        name="moe_ffn",
    )(block_expert, n_used, xs, *([wg, wu, wd] * FFN_BLOCKS))


def _combine_kernel(d0_ref, d1_ref, ys_ref, x1_ref, slab_ref, g_ref, b_ref, o_ref, buf_ref, sem):
    i = pl.program_id(0)
    n = pl.num_programs(0)
    tm = x1_ref.shape[0]

    def gather(tile):
        slot = tile % 2
        base = tile * tm

        def start(t, carry):
            for k, d_ref in enumerate((d0_ref, d1_ref)):
                pltpu.make_async_copy(ys_ref.at[pl.ds(d_ref[base + t], 1), :],
                                      buf_ref.at[slot, k, pl.ds(t, 1), :], sem.at[slot]).start()
            return carry

        lax.fori_loop(0, tm, start, 0, unroll=8)

    @pl.when(i == 0)
    def _():
        gather(0)

    @pl.when(i + 1 < n)
    def _():
        gather(i + 1)

    slot = i % 2
    for k in range(2):
        pltpu.make_async_copy(ys_ref.at[pl.ds(0, tm), :], buf_ref.at[slot, k], sem.at[slot]).wait()
    slab = slab_ref[...]
    y = slab[:, 4:5] * buf_ref[slot, 0] + slab[:, 5:6] * buf_ref[slot, 1]
    o_ref[...] = _layer_norm(DEEPNORM_ALPHA * x1_ref[...] + y, g_ref[...], b_ref[...])


def _combine(d0, d1, ys, x1, slab, g, b, layer):
    T = x1.shape[0]
    tm = ROW_BLK
    grid_spec = pltpu.PrefetchScalarGridSpec(
        num_scalar_prefetch=2,
        grid=(T // tm,),
        in_specs=[pl.BlockSpec(memory_space=pl.ANY),
                  pl.BlockSpec((tm, D_MODEL), lambda i, a, c: (i, 0)),
                  pl.BlockSpec((tm, LANES), lambda i, a, c: (i, 0)),
                  _layer_block(g, layer), _layer_block(b, layer)],
        out_specs=pl.BlockSpec((tm, D_MODEL), lambda i, a, c: (i, 0)),
        scratch_shapes=[pltpu.VMEM((2, 2, tm, D_MODEL), F32), pltpu.SemaphoreType.DMA((2,))],
    )
    return pl.pallas_call(
        _combine_kernel,
        grid_spec=grid_spec,
        out_shape=jax.ShapeDtypeStruct((T, D_MODEL), F32),
        compiler_params=_cparams("arbitrary"),
        name="moe_combine",
    )(d0, d1, ys, x1, slab, g, b)


def _t5_bucket(dist):
    max_exact = N_BUCKETS // 2
    large = max_exact + (jnp.log(jnp.maximum(dist, 1).astype(F32) / max_exact)
                         / math.log(MAX_DISTANCE / max_exact) * (N_BUCKETS - max_exact)).astype(jnp.int32)
    return jnp.where(dist < max_exact, dist, jnp.minimum(large, N_BUCKETS - 1))


def _bias_table(table_cols, dil, max_dist):
    dist = BLK + jnp.arange(BLK)[:, None] - jnp.arange(2 * BLK)[None, :]
    bucket = _t5_bucket(jnp.maximum(dist * dil, 0))
    hit = bucket[None, :, :, None] == jnp.arange(N_BUCKETS)[None, None, None, :]
    bias = jnp.sum(jnp.where(hit, table_cols.T.astype(F32)[:, None, None, :], 0.0), axis=-1)
    mask = (dist >= 0) & (dist <= max_dist)
    return jnp.where(mask[None], bias, NEG_INF)


def _rope_tables(T):
    inv_freq = ROPE_THETA ** (-jnp.arange(0, C_ROPE, 2, dtype=F32) / C_ROPE)
    ang = jnp.arange(T, dtype=F32)[:, None] * inv_freq[None, :]
    cos, sin = jnp.cos(ang), jnp.sin(ang)
    ones = jnp.ones((T, C_NOPE), F32)
    zeros_n = jnp.zeros((T, C_NOPE), F32)
    zeros_p = jnp.zeros((T, C_PAD - C_NOPE - C_ROPE), F32)
    cs = jnp.concatenate([ones, cos, cos, zeros_p], axis=1)
    sn = jnp.concatenate([zeros_n, -sin, sin, zeros_p], axis=1)
    return cs, sn


def _swap_halves(w):
    half = w.shape[-1] // 2
    return jnp.concatenate([w[..., half:], w[..., :half]], axis=-1)


def _pad_cols(w, width):
    return jnp.pad(w, ((0, 0), (0, width - w.shape[1])))


def _prep_layer_weights(w_in, w_out, w_uq, w_ukv, w_rg, b_rg, w_re, b_re):
    head_cols = lambda w, h: w[:, h * HEAD_DIM:(h + 1) * HEAD_DIM]
    wab = jnp.concatenate([head_cols(w_in, h) for h in A_HEAD_ORDER] + [w_in[:, A_Q_HEADS * HEAD_DIM:AB_COLS]],
                          axis=1).astype(BF16)
    c0 = AB_COLS
    w_cq = w_in[:, c0:c0 + Q_LORA]
    w_ckv = w_in[:, c0 + Q_LORA:c0 + Q_LORA + KV_LORA]
    w_kr = w_in[:, c0 + Q_LORA + KV_LORA:]
    lead = jnp.zeros((D_MODEL, C_NOPE), F32)
    wc = jnp.concatenate([w_cq, w_ckv,
                          _pad_cols(jnp.concatenate([lead, w_kr], 1), C_PAD),
                          _pad_cols(jnp.concatenate([lead, _swap_halves(w_kr)], 1), C_PAD)], axis=1).astype(BF16)
    uq = w_uq.reshape(Q_LORA, C_HEADS, C_NOPE + C_ROPE)
    wq = jnp.pad(uq, ((0, 0), (0, 0), (0, C_PAD - C_NOPE - C_ROPE))).reshape(Q_LORA, C_HEADS * C_PAD).astype(BF16)
    uqs = jnp.concatenate([jnp.zeros((Q_LORA, C_HEADS, C_NOPE), F32), _swap_halves(uq[..., C_NOPE:])], axis=-1)
    wqs = jnp.pad(uqs, ((0, 0), (0, 0), (0, C_PAD - C_NOPE - C_ROPE))).reshape(Q_LORA, C_HEADS * C_PAD).astype(BF16)
    ukv = w_ukv.reshape(KV_LORA, C_HEADS, C_NOPE + C_V)
    wk = jnp.pad(ukv[..., :C_NOPE], ((0, 0), (0, 0), (0, C_PAD - C_NOPE))).reshape(KV_LORA, C_HEADS * C_PAD)
    wv = jnp.pad(ukv[..., C_NOPE:], ((0, 0), (0, 0), (0, C_PAD - C_V))).reshape(KV_LORA, C_HEADS * C_PAD)
    woa = jnp.concatenate([w_out[h * HEAD_DIM:(h + 1) * HEAD_DIM] for h in A_HEAD_ORDER],
                          axis=0).reshape(2, LANES, D_MODEL).astype(BF16)
    wob = w_out[256:512].reshape(2, LANES, D_MODEL).astype(BF16)
    woc = w_out[512:].astype(BF16)
    wr = _pad_cols(jnp.concatenate([w_rg, w_re.transpose(1, 0, 2).reshape(D_MODEL, N_EXPERTS)], axis=1), LANES)
    wr_hi = wr.astype(BF16)
    wr_lo = (wr - wr_hi.astype(F32)).astype(BF16)
    wr = jnp.concatenate([wr_hi, wr_hi, wr_lo], axis=0)
    br = _pad_cols(jnp.concatenate([b_rg, b_re.reshape(N_EXPERTS)])[None, :], LANES)
    return dict(wab=wab, wc=wc, wq=wq, wqs=wqs, wk=wk.astype(BF16), wv=wv.astype(BF16),
                woa=woa, wob=wob, woc=woc, wr=wr, br=br)


def kernel(x, w_in, w_out, sinks, rel_bias, mla_q_norm, mla_kv_norm, w_uq, w_ukv, ln1_g, ln1_b,
           w_route_group, b_route_group, w_route_expert, b_route_expert,
           w_expert_gate, w_expert_up, w_expert_down, ln2_g, ln2_b):
    Bsz, T, D = x.shape
    assert Bsz == 1 and D == D_MODEL and T % CHUNK == 0
    xt = x.reshape(T, D)
    cs, sn = _rope_tables(T)
    bias_a = _bias_table(rel_bias[:, :A_Q_HEADS], 1, SWA_WINDOW - 1)
    bias_b = [_bias_table(rel_bias[:, A_Q_HEADS + g * B_HEADS_PER_GROUP:A_Q_HEADS + (g + 1) * B_HEADS_PER_GROUP],
                          dil, window // dil) for g, (window, dil) in enumerate(DILATED_PAIRS)]
    n_assign = T * TOP_K
    n_blocks = n_assign // ROW_BLK + N_EXPERTS
    n_rows = n_blocks * ROW_BLK
    no_sink = jnp.zeros((B_HEADS_PER_GROUP,), F32)

    w = jax.vmap(_prep_layer_weights)(w_in, w_out, w_uq, w_ukv, w_route_group, b_route_group,
                                      w_route_expert, b_route_expert)
    per_layer_row = lambda a: a[:, None, :]
    gq, gkv = per_layer_row(mla_q_norm), per_layer_row(mla_kv_norm)
    g1, b1, g2, b2 = (per_layer_row(a) for a in (ln1_g, ln1_b, ln2_g, ln2_b))

    for layer in range(w_in.shape[0]):
        hab, qt, k, vt = _in_proj(xt, w["wab"], w["wc"], w["wq"], w["wqs"], w["wk"], w["wv"], gq, gkv, cs, sn,
                                  layer)
        oa, _ = _banded(hab, sinks[layer], bias_a, q_col=0, k_col=2, v_col=3, dil=1,
                        n_heads=A_Q_HEADS, group=A_Q_HEADS // A_KV_HEADS, has_sink=True, want_lse=False)
        ob, lse = [], []
        for g, (_, dil) in enumerate(DILATED_PAIRS):
            o_g, lse_g = _banded(hab, no_sink, bias_b[g], q_col=4 + 2 * g, k_col=10 + 2 * g, v_col=16 + 2 * g,
                                 dil=dil, n_heads=B_HEADS_PER_GROUP, group=1, has_sink=False, want_lse=True)
            ob += o_g
            lse += lse_g
        oc = _flash(qt, k, vt)
        x1, slab, cnt = _out_router(oa, ob, lse, oc, xt, w["woa"], w["wob"], w["woc"], g1, b1, w["wr"], w["br"],
                                    layer)

        counts = cnt[0, N_GROUPS:N_GROUPS + N_EXPERTS].astype(jnp.int32)
        padded = ((counts + ROW_BLK - 1) // ROW_BLK) * ROW_BLK
        pad_end = jnp.cumsum(padded)
        pad_start = pad_end - padded
        eids = jnp.arange(N_EXPERTS, dtype=jnp.int32)
        pick = lambda e: jnp.sum(jnp.where(e.astype(jnp.int32)[:, None] == eids[None, :], pad_start[None, :], 0), axis=1)
        d0 = pick(slab[:, 0]) + slab[:, 2].astype(jnp.int32)
        d1 = pick(slab[:, 1]) + slab[:, 3].astype(jnp.int32)
        block_start = jnp.arange(n_blocks, dtype=jnp.int32) * ROW_BLK
        block_expert = jnp.minimum(jnp.sum((pad_end[None, :] <= block_start[:, None]).astype(jnp.int32), axis=1),
                                   N_EXPERTS - 1)

        n_used = (pad_end[-1:] // ROW_BLK).astype(jnp.int32)
        xs = _dispatch(d0, d1, pad_start + counts, padded - counts, n_used, x1, n_rows)
        ys = _ffn(block_expert, n_used, xs, w_expert_gate, w_expert_up, w_expert_down, layer)
        xt = _combine(d0, d1, ys, x1, slab, g2, b2, layer)
    return xt.reshape(Bsz, T, D)
```

```python
import functools
import math

import jax
import jax.numpy as jnp
from jax import lax
from jax.experimental import pallas as pl
from jax.experimental.pallas import tpu as pltpu

F32 = jnp.float32
BF16 = jnp.bfloat16

D_MODEL = 1024
DEPTH = 4
HEAD_DIM = 64
BLK = 128
NEG_INF = -1e30

SWA_WINDOW = 128
A_Q_HEADS = 4
A_KV_HEADS = 2
A_COLS = (A_Q_HEADS + 2 * A_KV_HEADS) * HEAD_DIM
A_HEAD_ORDER = (0, 2, 1, 3)

DILATED_PAIRS = ((128, 1), (512, 4), (2048, 16))
B_HEADS_PER_GROUP = 4
B_HEADS = len(DILATED_PAIRS) * B_HEADS_PER_GROUP
B_COLS = 3 * B_HEADS * HEAD_DIM
AB_COLS = A_COLS + B_COLS

C_HEADS = 8
C_NOPE = 64
C_ROPE = 32
C_V = 64
Q_LORA = 256
KV_LORA = 128
ROPE_THETA = 10000.0
C_PAD = 128
C_VROWS = 80

N_BUCKETS = 32
MAX_DISTANCE = 2048

N_GROUPS = 4
EXPERTS_PER_GROUP = 8
N_EXPERTS = N_GROUPS * EXPERTS_PER_GROUP
TOP_K = 2
D_EXPERT = 256
ROW_BLK = 512

DEEPNORM_ALPHA = (2 * DEPTH) ** 0.25

VMEM_LIMIT = 56 * 1024 * 1024

CHUNK = 2048
TM_IN = 512
TM_OUT = 512
TQ = 1024
TK = 512
N_DIAG = TQ // TK
FLASH_UNROLL = 4
LANES = 128


def _dot(a, b):
    return jnp.dot(a, b, preferred_element_type=F32)


def _dot_nt(a, b):
    return lax.dot_general(a, b, (((1,), (1,)), ((), ())), preferred_element_type=F32)


def _bdot(a, b):
    return lax.dot_general(a, b, (((2,), (1,)), ((0,), (0,))), preferred_element_type=F32)


def _bdot_nt(a, b):
    return lax.dot_general(a, b, (((2,), (2,)), ((0,), (0,))), preferred_element_type=F32)


def _cparams(*sem):
    return pltpu.CompilerParams(dimension_semantics=sem, vmem_limit_bytes=VMEM_LIMIT)


def _in_proj_kernel(x_ref, wab_ref, wc_ref, wq_ref, wqs_ref, wk_ref, wv_ref, gq_ref, gkv_ref, cs_ref, sn_ref,
                    hab_ref, qt_ref, k_ref, vt_ref):
    xb = x_ref[...].astype(BF16)
    hab_ref[...] = _dot(xb, wab_ref[...])
    hc = _dot(xb, wc_ref[...])
    cq = hc[:, :Q_LORA]
    ckv = hc[:, Q_LORA:Q_LORA + KV_LORA]
    kr = hc[:, Q_LORA + KV_LORA:Q_LORA + KV_LORA + C_PAD]
    krs = hc[:, Q_LORA + KV_LORA + C_PAD:]
    cqn = (cq * lax.rsqrt(jnp.mean(cq * cq, -1, keepdims=True) + 1e-6) * gq_ref[...]).astype(BF16)
    ckvn = (ckv * lax.rsqrt(jnp.mean(ckv * ckv, -1, keepdims=True) + 1e-6) * gkv_ref[...]).astype(BF16)
    cs = cs_ref[...]
    sn = sn_ref[...]
    q = _dot(cqn, wq_ref[...])
    qs = _dot(cqn, wqs_ref[...])
    k = _dot(ckvn, wk_ref[...])
    v = _dot(ckvn, wv_ref[...])
    krot = kr * cs + krs * sn
    scale = (C_NOPE + C_ROPE) ** -0.5 * math.log2(math.e)
    for h in range(C_HEADS):
        sl = slice(h * C_PAD, (h + 1) * C_PAD)
        qh = (q[:, sl] * cs + qs[:, sl] * sn) * scale
        qt_ref[sl, :] = qh.T.astype(BF16)
        k_ref[h] = (k[:, sl] + krot).astype(BF16)
    one_row = (lax.broadcasted_iota(jnp.int32, (1, C_PAD), 1) == C_V).astype(F32)
    for h in range(C_HEADS):
        vh = v[:, h * C_PAD:(h + 1) * C_PAD] + one_row
        vt_ref[h * C_VROWS:(h + 1) * C_VROWS, :] = vh.T[:C_VROWS, :].astype(BF16)


def _layer_block(a, layer):
    return pl.BlockSpec((None,) + a.shape[1:], lambda i, *_: (layer,) + (0,) * (a.ndim - 1))


def _in_proj(x, wab, wc, wq, wqs, wk, wv, gq, gkv, cs, sn, layer):
    T = x.shape[0]
    tm = TM_IN
    full = lambda a: _layer_block(a, layer)
    row = lambda w: pl.BlockSpec((tm, w), lambda i: (i, 0))
    colblk = lambda r: pl.BlockSpec((r, tm), lambda i: (0, i))
    return pl.pallas_call(
        _in_proj_kernel,
        grid=(T // tm,),
        in_specs=[row(D_MODEL), full(wab), full(wc), full(wq), full(wqs), full(wk), full(wv), full(gq), full(gkv),
                  row(C_PAD), row(C_PAD)],
        out_specs=[row(AB_COLS), colblk(C_HEADS * C_PAD), pl.BlockSpec((C_HEADS, tm, C_PAD), lambda i: (0, i, 0)),
                   colblk(C_HEADS * C_VROWS)],
        out_shape=[jax.ShapeDtypeStruct((T, AB_COLS), F32), jax.ShapeDtypeStruct((C_HEADS * C_PAD, T), BF16),
                   jax.ShapeDtypeStruct((C_HEADS, T, C_PAD), BF16),
                   jax.ShapeDtypeStruct((C_HEADS * C_VROWS, T), BF16)],
        compiler_params=_cparams("arbitrary"),
        name="in_proj",
    )(x, wab, wc, wq, wqs, wk, wv, gq, gkv, cs, sn)


def _banded_kernel(sink_ref, *refs, dil, n_heads, group, has_sink, want_lse):
    nq = n_heads // 2
    nkv = n_heads // group // 2
    q_refs, refs = refs[:nq], refs[nq:]
    k_refs, v_refs, kp_refs, vp_refs = (refs[i * nkv:(i + 1) * nkv] for i in range(4))
    bias_ref = refs[4 * nkv]
    o_refs = refs[4 * nkv + 1:4 * nkv + 1 + nq]
    lse_refs = refs[4 * nkv + 1 + nq:]
    c = pl.program_id(0)
    span = BLK * dil
    nb = CHUNK // span
    n_combo = nb * dil
    starts = [b * span + r for b in range(nb) for r in range(dil)]
    scale = HEAD_DIM ** -0.5

    def rows(start):
        return pl.ds(start, BLK) if dil == 1 else pl.ds(start, BLK, stride=dil)

    def gather(ref):
        return jnp.stack([ref[rows(s), :] for s in starts], axis=0)

    def with_prev(ref, prev_ref):
        cur = gather(ref)
        first = jnp.stack([prev_ref[rows(r), :] for r in range(dil)], axis=0)
        prev = jnp.concatenate([first, cur[:n_combo - dil]], axis=0) if dil < n_combo else first
        return jnp.concatenate([prev, cur], axis=1).astype(BF16)

    q = [gather(ref).astype(BF16) for ref in q_refs]
    kcat = [with_prev(k_refs[j], kp_refs[j]) for j in range(nkv)]
    vcat = [with_prev(v_refs[j], vp_refs[j]) for j in range(nkv)]

    lane = lax.broadcasted_iota(jnp.int32, (1, 1, LANES), 2)
    low = lane < HEAD_DIM
    combo = lax.broadcasted_iota(jnp.int32, (n_combo, 1, 2 * BLK), 0)
    col = lax.broadcasted_iota(jnp.int32, (n_combo, 1, 2 * BLK), 2)
    no_prev = jnp.logical_and(c == 0, jnp.logical_and(combo < dil, col < BLK))

    for j in range(nq):
        o_par, lse_par = [], []
        for par in range(2):
            h = 2 * j + par if group == 1 else group * par + j
            kvb = j if group == 1 else 0
            keep = low if par == 0 else jnp.logical_not(low)
            kk = jnp.where(keep, kcat[kvb], jnp.zeros((), BF16))
            s = _bdot_nt(q[j], kk)
            s = s * scale + bias_ref[h][None]
            s = jnp.where(no_prev, NEG_INF, s)
            m = jnp.max(s, axis=-1, keepdims=True)
            if has_sink:
                snk = sink_ref[h]
                m = jnp.maximum(m, snk)
            e = jnp.exp(s - m)
            l = jnp.sum(e, axis=-1, keepdims=True)
            if has_sink:
                l = l + jnp.exp(snk - m)
            p = (e * (1.0 / l)).astype(BF16)
            o_par.append(_bdot(p, vcat[kvb]))
            if want_lse:
                lse_par.append(m + jnp.log(l))
        o_pair = jnp.where(low, o_par[0], o_par[1])
        if want_lse:
            lse_pair = jnp.where(low, lse_par[0], lse_par[1])
        for ci, s0 in enumerate(starts):
            o_refs[j][rows(s0), :] = o_pair[ci]
            if want_lse:
                lse_refs[j][rows(s0), :] = lse_pair[ci]


def _banded(hab, sink, bias, *, q_col, k_col, v_col, dil, n_heads, group, has_sink, want_lse):
    T = hab.shape[0]
    span = BLK * dil
    nb = CHUNK // span
    nq = n_heads // 2
    nkv = n_heads // group // 2
    kernel = functools.partial(_banded_kernel, dil=dil, n_heads=n_heads, group=group,
                               has_sink=has_sink, want_lse=want_lse)
    prev = lambda cb: pl.BlockSpec((span, LANES), lambda c, s: (jnp.maximum(c * nb - 1, 0), cb))
    cur = lambda cb: pl.BlockSpec((CHUNK, LANES), lambda c, s: (c, cb))
    in_specs = ([cur(q_col + j) for j in range(nq)] + [cur(k_col + j) for j in range(nkv)]
                + [cur(v_col + j) for j in range(nkv)] + [prev(k_col + j) for j in range(nkv)]
                + [prev(v_col + j) for j in range(nkv)] + [pl.BlockSpec(bias.shape, lambda c, s: (0, 0, 0))])
    n_out = nq * (2 if want_lse else 1)
    grid_spec = pltpu.PrefetchScalarGridSpec(
        num_scalar_prefetch=1,
        grid=(T // CHUNK,),
        in_specs=in_specs,
        out_specs=[pl.BlockSpec((CHUNK, LANES), lambda c, s: (c, 0))] * n_out,
    )
    outs = pl.pallas_call(
        kernel,
        grid_spec=grid_spec,
        out_shape=[jax.ShapeDtypeStruct((T, LANES), F32)] * n_out,
        compiler_params=_cparams("arbitrary"),
        name=f"banded_d{dil}" + ("_sink" if has_sink else ""),
    )(sink, *([hab] * (nq + 4 * nkv)), bias)
    return outs[:nq], outs[nq:]


def _flash_kernel(qt_ref, k_ref, vt_ref, ot_ref, m_ref, acc_ref, s_ref):
    qi = pl.program_id(1)
    qt = qt_ref[...]
    m_ref[...] = jnp.full(m_ref.shape, NEG_INF, F32)
    acc_ref[...] = jnp.zeros(acc_ref.shape, F32)

    def scores(j):
        off = pl.multiple_of(j * TK, TK)
        return _dot(k_ref[pl.ds(off, TK), :], qt)

    def accumulate(j, s):
        off = pl.multiple_of(j * TK, TK)
        m_old = m_ref[...]
        m_new = jnp.maximum(m_old, jnp.max(s, axis=0, keepdims=True))
        alpha = jnp.exp2(m_old - m_new)
        p = jnp.exp2(s - m_new).astype(BF16)
        acc_ref[...] = alpha * acc_ref[...] + _dot(vt_ref[:, pl.ds(off, TK)], p)
        m_ref[...] = m_new

    def diagonal(s, d):
        key = lax.broadcasted_iota(jnp.int32, (TK, TQ), 0)
        qry = lax.broadcasted_iota(jnp.int32, (TK, TQ), 1)
        return jnp.where(key + d * TK <= qry, s, NEG_INF)

    n_full = qi * N_DIAG
    s_ref[0] = scores(0)
    n_main = n_full // FLASH_UNROLL

    def main_body(i, carry):
        j = FLASH_UNROLL * i
        for u in range(FLASH_UNROLL):
            s_ref[(u + 1) % 2] = scores(j + u + 1)
            accumulate(j + u, s_ref[u % 2])
        return carry

    lax.fori_loop(0, n_main, main_body, 0)

    base = FLASH_UNROLL * n_main
    for rem in range(0, FLASH_UNROLL, math.gcd(FLASH_UNROLL, N_DIAG)):
        @pl.when(n_full - base == rem)
        def _():
            n_tail = rem + N_DIAG
            for u in range(n_tail):
                if u + 1 < n_tail:
                    s_ref[(u + 1) % 2] = scores(base + u + 1)
                s = s_ref[u % 2]
                accumulate(base + u, s if u < rem else diagonal(s, u - rem))

    ot_ref[...] = (acc_ref[:C_V, :] / acc_ref[C_V:C_V + 1, :]).astype(ot_ref.dtype)


def _flash(qt, k, vt):
    H, T, _ = k.shape
    return pl.pallas_call(
        _flash_kernel,
        grid=(H, T // TQ),
        in_specs=[pl.BlockSpec((C_PAD, TQ), lambda h, i: (h, i)),
                  pl.BlockSpec((None, T, C_PAD), lambda h, i: (h, 0, 0)),
                  pl.BlockSpec((C_VROWS, T), lambda h, i: (h, 0))],
        out_specs=pl.BlockSpec((C_V, TQ), lambda h, i: (h, i)),
        out_shape=jax.ShapeDtypeStruct((H * C_V, T), BF16),
        scratch_shapes=[pltpu.VMEM((1, TQ), F32), pltpu.VMEM((C_VROWS, TQ), F32), pltpu.VMEM((2, TK, TQ), F32)],
        compiler_params=_cparams("arbitrary", "arbitrary"),
        name="mla_flash",
    )(qt, k, vt)


def _layer_norm(z, g, b):
    mu = jnp.mean(z, -1, keepdims=True)
    zc = z - mu
    var = jnp.mean(zc * zc, -1, keepdims=True)
    return zc * lax.rsqrt(var + 1e-5) * g + b


def _out_router_kernel(*refs):
    n_ab = 2 + 6 + 6
    oa_refs, ob_refs, lse_refs = refs[0:2], refs[2:8], refs[8:14]
    (oc_ref, x_ref, woa_ref, wob_ref, woc_ref, g_ref, b_ref, wr_ref, br_ref,
     x1_ref, slab_ref, cnt_ref, carry_ref, prev_ref) = refs[n_ab:]
    i = pl.program_id(0)
    tm = x_ref.shape[0]

    @pl.when(i == 0)
    def _():
        carry_ref[...] = jnp.zeros(carry_ref.shape, F32)
        prev_ref[...] = jnp.zeros(prev_ref.shape, F32)

    xr = prev_ref[...]

    y = None
    for j in range(2):
        l0, l1, l2 = (lse_refs[2 * g + j][...] for g in range(3))
        mx = jnp.maximum(jnp.maximum(l0, l1), l2)
        e0, e1, e2 = jnp.exp(l0 - mx), jnp.exp(l1 - mx), jnp.exp(l2 - mx)
        den = e0 + e1 + e2
        ob = (e0 * ob_refs[j][...] + e1 * ob_refs[2 + j][...] + e2 * ob_refs[4 + j][...]) * (1.0 / den)
        part = _dot(oa_refs[j][...].astype(BF16), woa_ref[j]) + _dot(ob.astype(BF16), wob_ref[j])
        y = part if y is None else y + part
    y = y + _dot(oc_ref[...].astype(F32).T.astype(BF16), woc_ref[...])
    x1 = _layer_norm(DEEPNORM_ALPHA * x_ref[...] + y, g_ref[...], b_ref[...])
    x1_ref[...] = x1
    prev_ref[...] = x1

    xr_hi = xr.astype(BF16)
    xr_lo = (xr - xr_hi.astype(F32)).astype(BF16)
    logits = _dot(jnp.concatenate([xr_hi, xr_lo, xr_hi], axis=1), wr_ref[...]) + br_ref[...]
    lane = lax.broadcasted_iota(jnp.int32, (tm, LANES), 1)
    ninf = jnp.float32(-jnp.inf)
    gmask = lane < N_GROUPS
    gl = jnp.where(gmask, logits, ninf)
    gmax = jnp.max(gl, axis=-1, keepdims=True)
    grp = jnp.min(jnp.where(gl == gmax, lane, LANES), axis=-1, keepdims=True)
    p_grp = 1.0 / jnp.sum(jnp.where(gmask, jnp.exp(logits - gmax), 0.0), axis=-1, keepdims=True)
    lo = N_GROUPS + EXPERTS_PER_GROUP * grp
    emask = jnp.logical_and(lane >= lo, lane < lo + EXPERTS_PER_GROUP)
    el = jnp.where(emask, logits, ninf)
    v0 = jnp.max(el, axis=-1, keepdims=True)
    i0 = jnp.min(jnp.where(el == v0, lane, LANES), axis=-1, keepdims=True)
    el2 = jnp.where(lane == i0, ninf, el)
    v1 = jnp.max(el2, axis=-1, keepdims=True)
    i1 = jnp.min(jnp.where(el2 == v1, lane, LANES), axis=-1, keepdims=True)
    t = jnp.exp(v1 - v0)
    gate0 = p_grp / (1.0 + t)
    gate1 = p_grp * t / (1.0 + t)

    hit0 = lane == i0
    hit1 = lane == i1
    cmat = jnp.logical_or(hit0, hit1).astype(BF16)
    tr = lax.broadcasted_iota(jnp.int32, (tm, tm), 0)
    tc = lax.broadcasted_iota(jnp.int32, (tm, tm), 1)
    tril = (tc < tr).astype(BF16)
    prefix = _dot(tril, cmat) + carry_ref[...]
    r0 = jnp.sum(jnp.where(hit0, prefix, 0.0), axis=-1, keepdims=True)
    r1 = jnp.sum(jnp.where(hit1, prefix, 0.0), axis=-1, keepdims=True)
    counted = (i > 0).astype(F32)
    carry_ref[...] = carry_ref[...] + counted * jnp.sum(cmat.astype(F32), axis=0, keepdims=True)
    cnt_ref[...] = carry_ref[...]

    fields = [(i0 - N_GROUPS).astype(F32), (i1 - N_GROUPS).astype(F32), r0, r1, gate0, gate1]
    slab = jnp.zeros((tm, LANES), F32)
    for n, f in enumerate(fields):
        slab = jnp.where(lane == n, f, slab)
    slab_ref[...] = slab


def _out_router(oa, ob, lse, oc, x, woa, wob, woc, g, b, wr, br, layer):
    T = x.shape[0]
    tm = TM_OUT
    n = T // tm
    full = lambda a: _layer_block(a, layer)
    row = lambda w: pl.BlockSpec((tm, w), lambda i: (jnp.minimum(i, n - 1), 0))
    lag = pl.BlockSpec((tm, LANES), lambda i: (jnp.maximum(i - 1, 0), 0))
    return pl.pallas_call(
        _out_router_kernel,
        grid=(n + 1,),
        in_specs=[row(LANES)] * 14 + [pl.BlockSpec((C_HEADS * C_V, tm), lambda i: (0, jnp.minimum(i, n - 1))),
                                      row(D_MODEL),
                                      full(woa), full(wob), full(woc), full(g), full(b), full(wr), full(br)],
        out_specs=[row(D_MODEL), lag, pl.BlockSpec((1, LANES), lambda i: (0, 0))],
        out_shape=[jax.ShapeDtypeStruct((T, D_MODEL), F32), jax.ShapeDtypeStruct((T, LANES), F32),
                   jax.ShapeDtypeStruct((1, LANES), F32)],
        scratch_shapes=[pltpu.VMEM((1, LANES), F32), pltpu.VMEM((tm, D_MODEL), F32)],
        compiler_params=_cparams("arbitrary"),
        name="out_router",
    )(*oa, *ob, *lse, oc, x, woa, wob, woc, g, b, wr, br)


SUBLANES = 8
PAD_BITS = tuple(1 << k for k in reversed(range(SUBLANES.bit_length() - 1, ROW_BLK.bit_length() - 1)))


def _dispatch_kernel(d0_ref, d1_ref, pad_at_ref, pad_len_ref, nu_ref, x_ref, xs_ref, zero_ref, sem, zsem):
    i = pl.program_id(0)
    tm = x_ref.shape[0] * SUBLANES
    base = i * tm

    @pl.when(i == 0)
    def _():
        zero_ref[...] = jnp.zeros(zero_ref.shape, F32)

        def pad_copies(e):
            n = pad_len_ref[e]
            at = pad_at_ref[e]
            head = n & (SUBLANES - 1)
            out = [(j < head, pltpu.make_async_copy(zero_ref.at[pl.ds(0, 1), :], xs_ref.at[pl.ds(at + j, 1), :], zsem))
                   for j in range(SUBLANES - 1)]
            body = n - head
            for bit in PAD_BITS:
                start = pl.multiple_of(at + head + (body & ~(2 * bit - 1)), SUBLANES)
                out.append(((body & bit) != 0,
                            pltpu.make_async_copy(zero_ref.at[pl.ds(0, bit), :], xs_ref.at[pl.ds(start, bit), :], zsem)))
            return out

        def spare_copies():
            n_blocks = xs_ref.shape[0] // ROW_BLK
            first = d0_ref.shape[0] * TOP_K // ROW_BLK
            return [(b >= nu_ref[0], pltpu.make_async_copy(zero_ref, xs_ref.at[pl.ds(b * ROW_BLK, ROW_BLK), :], zsem))
                    for b in range(first, n_blocks)]

        for e in range(N_EXPERTS):
            for on, cp in pad_copies(e):
                pl.when(on)(cp.start)
        for on, cp in spare_copies():
            pl.when(on)(cp.start)
        for e in range(N_EXPERTS):
            for on, cp in pad_copies(e):
                pl.when(on)(cp.wait)
        for on, cp in spare_copies():
            pl.when(on)(cp.wait)

    def start(g, carry):
        for u in range(SUBLANES):
            t = g * SUBLANES + u
            for d_ref in (d0_ref, d1_ref):
                pltpu.make_async_copy(x_ref.at[g, pl.ds(u, 1), :], xs_ref.at[pl.ds(d_ref[base + t], 1), :], sem).start()
        return carry

    lax.fori_loop(0, tm // SUBLANES, start, 0)
    for _ in range(2):
        pltpu.make_async_copy(xs_ref.at[pl.ds(0, tm), :], zero_ref, sem).wait()


def _dispatch(d0, d1, pad_at, pad_len, n_used, x1, n_rows):
    T = x1.shape[0]
    tm = ROW_BLK
    grid_spec = pltpu.PrefetchScalarGridSpec(
        num_scalar_prefetch=5,
        grid=(T // tm,),
        in_specs=[pl.BlockSpec((tm // SUBLANES, SUBLANES, D_MODEL), lambda i, *_: (i, 0, 0))],
        out_specs=pl.BlockSpec(memory_space=pl.ANY),
        scratch_shapes=[pltpu.VMEM((ROW_BLK, D_MODEL), F32), pltpu.SemaphoreType.DMA, pltpu.SemaphoreType.DMA],
    )
    return pl.pallas_call(
        _dispatch_kernel,
        grid_spec=grid_spec,
        out_shape=jax.ShapeDtypeStruct((n_rows, D_MODEL), F32),
        compiler_params=_cparams("arbitrary"),
        name="moe_dispatch",
    )(d0, d1, pad_at, pad_len, n_used, x1.reshape(T // SUBLANES, SUBLANES, D_MODEL))


FFN_BLOCKS = 1


def _ffn_kernel(be_ref, nu_ref, xs_ref, *refs):
    del be_ref
    w_refs, ys_ref = refs[:-1], refs[-1]
    i = pl.program_id(0)
    for k in range(FFN_BLOCKS):
        wg_ref, wu_ref, wd_ref = w_refs[3 * k:3 * k + 3]
        rows = pl.ds(k * ROW_BLK, ROW_BLK)
        used = FFN_BLOCKS * i + k < nu_ref[0]

        @pl.when(used)
        def _():
            xb = xs_ref[rows, :].astype(BF16)
            a = _dot(xb, wg_ref[...].astype(BF16))
            u = _dot(xb, wu_ref[...].astype(BF16))
            hmid = (a * jax.nn.sigmoid(a) * u).astype(BF16)
            ys_ref[rows, :] = _dot(hmid, wd_ref[...].astype(BF16))

        @pl.when(jnp.logical_not(used))
        def _():
            ys_ref[rows, :] = jnp.zeros((ROW_BLK, D_MODEL), F32)


def _ffn(block_expert, n_used, xs, wg, wu, wd, layer):
    n_rows = xs.shape[0]
    step_rows = FFN_BLOCKS * ROW_BLK
    assert n_rows % step_rows == 0
    w_spec = lambda shape, k: pl.BlockSpec((None, None) + shape,
                                           lambda i, be, nu: (layer, be[FFN_BLOCKS * i + k], 0, 0))
    w_specs = [w_spec(s, k) for k in range(FFN_BLOCKS)
               for s in ((D_MODEL, D_EXPERT), (D_MODEL, D_EXPERT), (D_EXPERT, D_MODEL))]
    grid_spec = pltpu.PrefetchScalarGridSpec(
        num_scalar_prefetch=2,
        grid=(n_rows // step_rows,),
        in_specs=[pl.BlockSpec((step_rows, D_MODEL),
                               lambda i, be, nu: (jnp.minimum(i, (nu[0] - 1) // FFN_BLOCKS), 0))] + w_specs,
        out_specs=pl.BlockSpec((step_rows, D_MODEL), lambda i, be, nu: (i, 0)),
    )
    return pl.pallas_call(
        _ffn_kernel,
        grid_spec=grid_spec,
        out_shape=jax.ShapeDtypeStruct((n_rows, D_MODEL), F32),
        compiler_params=_cparams("arbitrary"),
        name="moe_ffn",
    )(block_expert, n_used, xs, *([wg, wu, wd] * FFN_BLOCKS))


def _combine_kernel(d0_ref, d1_ref, ys_ref, x1_ref, slab_ref, g_ref, b_ref, o_ref, buf_ref, sem):
    i = pl.program_id(0)
    n = pl.num_programs(0)
    tm = x1_ref.shape[0]

    def gather(tile):
        slot = tile % 2
        base = tile * tm

        def start(g, carry):
            for u in range(SUBLANES):
                t = g * SUBLANES + u
                for k, d_ref in enumerate((d0_ref, d1_ref)):
                    pltpu.make_async_copy(ys_ref.at[pl.ds(d_ref[base + t], 1), :],
                                          buf_ref.at[slot, k, g, pl.ds(u, 1), :], sem.at[slot]).start()
            return carry

        lax.fori_loop(0, tm // SUBLANES, start, 0)

    @pl.when(i == 0)
    def _():
        gather(0)

    @pl.when(i + 1 < n)
    def _():
        gather(i + 1)

    slot = i % 2
    for k in range(2):
        pltpu.make_async_copy(ys_ref.at[pl.ds(0, tm), :], o_ref, sem.at[slot]).wait()
    slab = slab_ref[...]
    rows = lambda k: buf_ref[slot, k].reshape(tm, D_MODEL)
    y = slab[:, 4:5] * rows(0) + slab[:, 5:6] * rows(1)
    o_ref[...] = _layer_norm(DEEPNORM_ALPHA * x1_ref[...] + y, g_ref[...], b_ref[...])


def _combine(d0, d1, ys, x1, slab, g, b, layer):
    T = x1.shape[0]
    tm = ROW_BLK
    grid_spec = pltpu.PrefetchScalarGridSpec(
        num_scalar_prefetch=2,
        grid=(T // tm,),
        in_specs=[pl.BlockSpec(memory_space=pl.ANY),
                  pl.BlockSpec((tm, D_MODEL), lambda i, a, c: (i, 0)),
                  pl.BlockSpec((tm, LANES), lambda i, a, c: (i, 0)),
                  _layer_block(g, layer), _layer_block(b, layer)],
        out_specs=pl.BlockSpec((tm, D_MODEL), lambda i, a, c: (i, 0)),
        scratch_shapes=[pltpu.VMEM((2, 2, tm // SUBLANES, SUBLANES, D_MODEL), F32), pltpu.SemaphoreType.DMA((2,))],
    )
    return pl.pallas_call(
        _combine_kernel,
        grid_spec=grid_spec,
        out_shape=jax.ShapeDtypeStruct((T, D_MODEL), F32),
        compiler_params=_cparams("arbitrary"),
        name="moe_combine",
    )(d0, d1, ys, x1, slab, g, b)


def _t5_bucket(dist):
    max_exact = N_BUCKETS // 2
    large = max_exact + (jnp.log(jnp.maximum(dist, 1).astype(F32) / max_exact)
                         / math.log(MAX_DISTANCE / max_exact) * (N_BUCKETS - max_exact)).astype(jnp.int32)
    return jnp.where(dist < max_exact, dist, jnp.minimum(large, N_BUCKETS - 1))


def _bias_table(table_cols, dil, max_dist):
    dist = BLK + jnp.arange(BLK)[:, None] - jnp.arange(2 * BLK)[None, :]
    bucket = _t5_bucket(jnp.maximum(dist * dil, 0))
    hit = bucket[None, :, :, None] == jnp.arange(N_BUCKETS)[None, None, None, :]
    bias = jnp.sum(jnp.where(hit, table_cols.T.astype(F32)[:, None, None, :], 0.0), axis=-1)
    mask = (dist >= 0) & (dist <= max_dist)
    return jnp.where(mask[None], bias, NEG_INF)


def _rope_tables(T):
    inv_freq = ROPE_THETA ** (-jnp.arange(0, C_ROPE, 2, dtype=F32) / C_ROPE)
    ang = jnp.arange(T, dtype=F32)[:, None] * inv_freq[None, :]
    cos, sin = jnp.cos(ang), jnp.sin(ang)
    ones = jnp.ones((T, C_NOPE), F32)
    zeros_n = jnp.zeros((T, C_NOPE), F32)
    zeros_p = jnp.zeros((T, C_PAD - C_NOPE - C_ROPE), F32)
    cs = jnp.concatenate([ones, cos, cos, zeros_p], axis=1)
    sn = jnp.concatenate([zeros_n, -sin, sin, zeros_p], axis=1)
    return cs, sn


def _swap_halves(w):
    half = w.shape[-1] // 2
    return jnp.concatenate([w[..., half:], w[..., :half]], axis=-1)


def _pad_cols(w, width):
    return jnp.pad(w, ((0, 0), (0, width - w.shape[1])))


def _prep_layer_weights(w_in, w_out, w_uq, w_ukv, w_rg, b_rg, w_re, b_re):
    head_cols = lambda w, h: w[:, h * HEAD_DIM:(h + 1) * HEAD_DIM]
    wab = jnp.concatenate([head_cols(w_in, h) for h in A_HEAD_ORDER] + [w_in[:, A_Q_HEADS * HEAD_DIM:AB_COLS]],
                          axis=1).astype(BF16)
    c0 = AB_COLS
    w_cq = w_in[:, c0:c0 + Q_LORA]
    w_ckv = w_in[:, c0 + Q_LORA:c0 + Q_LORA + KV_LORA]
    w_kr = w_in[:, c0 + Q_LORA + KV_LORA:]
    lead = jnp.zeros((D_MODEL, C_NOPE), F32)
    wc = jnp.concatenate([w_cq, w_ckv,
                          _pad_cols(jnp.concatenate([lead, w_kr], 1), C_PAD),
                          _pad_cols(jnp.concatenate([lead, _swap_halves(w_kr)], 1), C_PAD)], axis=1).astype(BF16)
    uq = w_uq.reshape(Q_LORA, C_HEADS, C_NOPE + C_ROPE)
    wq = jnp.pad(uq, ((0, 0), (0, 0), (0, C_PAD - C_NOPE - C_ROPE))).reshape(Q_LORA, C_HEADS * C_PAD).astype(BF16)
    uqs = jnp.concatenate([jnp.zeros((Q_LORA, C_HEADS, C_NOPE), F32), _swap_halves(uq[..., C_NOPE:])], axis=-1)
    wqs = jnp.pad(uqs, ((0, 0), (0, 0), (0, C_PAD - C_NOPE - C_ROPE))).reshape(Q_LORA, C_HEADS * C_PAD).astype(BF16)
    ukv = w_ukv.reshape(KV_LORA, C_HEADS, C_NOPE + C_V)
    wk = jnp.pad(ukv[..., :C_NOPE], ((0, 0), (0, 0), (0, C_PAD - C_NOPE))).reshape(KV_LORA, C_HEADS * C_PAD)
    wv = jnp.pad(ukv[..., C_NOPE:], ((0, 0), (0, 0), (0, C_PAD - C_V))).reshape(KV_LORA, C_HEADS * C_PAD)
    woa = jnp.concatenate([w_out[h * HEAD_DIM:(h + 1) * HEAD_DIM] for h in A_HEAD_ORDER],
                          axis=0).reshape(2, LANES, D_MODEL).astype(BF16)
    wob = w_out[256:512].reshape(2, LANES, D_MODEL).astype(BF16)
    woc = w_out[512:].astype(BF16)
    wr = _pad_cols(jnp.concatenate([w_rg, w_re.transpose(1, 0, 2).reshape(D_MODEL, N_EXPERTS)], axis=1), LANES)
    wr_hi = wr.astype(BF16)
    wr_lo = (wr - wr_hi.astype(F32)).astype(BF16)
    wr = jnp.concatenate([wr_hi, wr_hi, wr_lo], axis=0)
    br = _pad_cols(jnp.concatenate([b_rg, b_re.reshape(N_EXPERTS)])[None, :], LANES)
    return dict(wab=wab, wc=wc, wq=wq, wqs=wqs, wk=wk.astype(BF16), wv=wv.astype(BF16),
                woa=woa, wob=wob, woc=woc, wr=wr, br=br)


def kernel(x, w_in, w_out, sinks, rel_bias, mla_q_norm, mla_kv_norm, w_uq, w_ukv, ln1_g, ln1_b,
           w_route_group, b_route_group, w_route_expert, b_route_expert,
           w_expert_gate, w_expert_up, w_expert_down, ln2_g, ln2_b):
    Bsz, T, D = x.shape
    assert Bsz == 1 and D == D_MODEL and T % CHUNK == 0
    xt = x.reshape(T, D)
    cs, sn = _rope_tables(T)
    bias_a = _bias_table(rel_bias[:, :A_Q_HEADS], 1, SWA_WINDOW - 1)
    bias_b = [_bias_table(rel_bias[:, A_Q_HEADS + g * B_HEADS_PER_GROUP:A_Q_HEADS + (g + 1) * B_HEADS_PER_GROUP],
                          dil, window // dil) for g, (window, dil) in enumerate(DILATED_PAIRS)]
    n_assign = T * TOP_K
    n_blocks = n_assign // ROW_BLK + N_EXPERTS
    n_rows = n_blocks * ROW_BLK
    no_sink = jnp.zeros((B_HEADS_PER_GROUP,), F32)

    w = jax.vmap(_prep_layer_weights)(w_in, w_out, w_uq, w_ukv, w_route_group, b_route_group,
                                      w_route_expert, b_route_expert)
    per_layer_row = lambda a: a[:, None, :]
    gq, gkv = per_layer_row(mla_q_norm), per_layer_row(mla_kv_norm)
    g1, b1, g2, b2 = (per_layer_row(a) for a in (ln1_g, ln1_b, ln2_g, ln2_b))

    for layer in range(w_in.shape[0]):
        hab, qt, k, vt = _in_proj(xt, w["wab"], w["wc"], w["wq"], w["wqs"], w["wk"], w["wv"], gq, gkv, cs, sn,
                                  layer)
        oa, _ = _banded(hab, sinks[layer], bias_a, q_col=0, k_col=2, v_col=3, dil=1,
                        n_heads=A_Q_HEADS, group=A_Q_HEADS // A_KV_HEADS, has_sink=True, want_lse=False)
        ob, lse = [], []
        for g, (_, dil) in enumerate(DILATED_PAIRS):
            o_g, lse_g = _banded(hab, no_sink, bias_b[g], q_col=4 + 2 * g, k_col=10 + 2 * g, v_col=16 + 2 * g,
                                 dil=dil, n_heads=B_HEADS_PER_GROUP, group=1, has_sink=False, want_lse=True)
            ob += o_g
            lse += lse_g
        oc = _flash(qt, k, vt)
        x1, slab, cnt = _out_router(oa, ob, lse, oc, xt, w["woa"], w["wob"], w["woc"], g1, b1, w["wr"], w["br"],
                                    layer)

        counts = cnt[0, N_GROUPS:N_GROUPS + N_EXPERTS].astype(jnp.int32)
        padded = ((counts + ROW_BLK - 1) // ROW_BLK) * ROW_BLK
        pad_end = jnp.cumsum(padded)
        pad_start = pad_end - padded
        eids = jnp.arange(N_EXPERTS, dtype=jnp.int32)
        pick = lambda e: jnp.sum(jnp.where(e.astype(jnp.int32)[:, None] == eids[None, :], pad_start[None, :], 0), axis=1)
        d0 = pick(slab[:, 0]) + slab[:, 2].astype(jnp.int32)
        d1 = pick(slab[:, 1]) + slab[:, 3].astype(jnp.int32)
        block_start = jnp.arange(n_blocks, dtype=jnp.int32) * ROW_BLK
        block_expert = jnp.minimum(jnp.sum((pad_end[None, :] <= block_start[:, None]).astype(jnp.int32), axis=1),
                                   N_EXPERTS - 1)

        n_used = (pad_end[-1:] // ROW_BLK).astype(jnp.int32)
        xs = _dispatch(d0, d1, pad_start + counts, padded - counts, n_used, x1, n_rows)
        ys = _ffn(block_expert, n_used, xs, w_expert_gate, w_expert_up, w_expert_down, layer)
        xt = _combine(d0, d1, ys, x1, slab, g2, b2, layer)
    return xt.reshape(Bsz, T, D)
```

```python
import functools
import math

import jax
import jax.numpy as jnp
from jax import lax
from jax.experimental import pallas as pl
from jax.experimental.pallas import tpu as pltpu

F32 = jnp.float32
BF16 = jnp.bfloat16

D_MODEL = 1024
DEPTH = 4
HEAD_DIM = 64
BLK = 128
NEG_INF = -1e30

SWA_WINDOW = 128
A_Q_HEADS = 4
A_KV_HEADS = 2
A_COLS = (A_Q_HEADS + 2 * A_KV_HEADS) * HEAD_DIM
A_HEAD_ORDER = (0, 2, 1, 3)

DILATED_PAIRS = ((128, 1), (512, 4), (2048, 16))
B_HEADS_PER_GROUP = 4
B_HEADS = len(DILATED_PAIRS) * B_HEADS_PER_GROUP
B_COLS = 3 * B_HEADS * HEAD_DIM
AB_COLS = A_COLS + B_COLS

C_HEADS = 8
C_NOPE = 64
C_ROPE = 32
C_V = 64
Q_LORA = 256
KV_LORA = 128
ROPE_THETA = 10000.0
C_PAD = 128
C_VROWS = 80

N_BUCKETS = 32
MAX_DISTANCE = 2048

N_GROUPS = 4
EXPERTS_PER_GROUP = 8
N_EXPERTS = N_GROUPS * EXPERTS_PER_GROUP
TOP_K = 2
D_EXPERT = 256
ROW_BLK = 512

DEEPNORM_ALPHA = (2 * DEPTH) ** 0.25

VMEM_LIMIT = 56 * 1024 * 1024

CHUNK = 2048
TM_IN = 512
TM_OUT = 512
TQ = 1024
TK = 512
N_DIAG = TQ // TK
FLASH_UNROLL = 4
LANES = 128


def _dot(a, b):
    return jnp.dot(a, b, preferred_element_type=F32)


def _dot_nt(a, b):
    return lax.dot_general(a, b, (((1,), (1,)), ((), ())), preferred_element_type=F32)


def _bdot(a, b):
    return lax.dot_general(a, b, (((2,), (1,)), ((0,), (0,))), preferred_element_type=F32)


def _bdot_nt(a, b):
    return lax.dot_general(a, b, (((2,), (2,)), ((0,), (0,))), preferred_element_type=F32)


def _cparams(*sem):
    return pltpu.CompilerParams(dimension_semantics=sem, vmem_limit_bytes=VMEM_LIMIT)


def _in_proj_kernel(x_ref, waq_ref, wab_ref, wc_ref, wq_ref, wqs_ref, wk_ref, wv_ref, gq_ref, gkv_ref, cs_ref, sn_ref,
                    hab_ref, qt_ref, k_ref, vt_ref):
    xb = x_ref[...].astype(BF16)
    n_aq = waq_ref.shape[1]
    hab_ref[:, :n_aq] = _dot(xb, waq_ref[...])
    hab_ref[:, n_aq:] = _dot(xb, wab_ref[...])
    hc = _dot(xb, wc_ref[...])
    cq = hc[:, :Q_LORA]
    ckv = hc[:, Q_LORA:Q_LORA + KV_LORA]
    kr = hc[:, Q_LORA + KV_LORA:Q_LORA + KV_LORA + C_PAD]
    krs = hc[:, Q_LORA + KV_LORA + C_PAD:]
    cqn = (cq * lax.rsqrt(jnp.mean(cq * cq, -1, keepdims=True) + 1e-6) * gq_ref[...]).astype(BF16)
    ckvn = (ckv * lax.rsqrt(jnp.mean(ckv * ckv, -1, keepdims=True) + 1e-6) * gkv_ref[...]).astype(BF16)
    cs = cs_ref[...]
    sn = sn_ref[...]
    q = _dot(cqn, wq_ref[...])
    qs = _dot(cqn, wqs_ref[...])
    k = _dot(ckvn, wk_ref[...])
    v = _dot(ckvn, wv_ref[...])
    krot = kr * cs + krs * sn
    scale = (C_NOPE + C_ROPE) ** -0.5 * math.log2(math.e)
    for h in range(C_HEADS):
        sl = slice(h * C_PAD, (h + 1) * C_PAD)
        qh = (q[:, sl] * cs + qs[:, sl] * sn) * scale
        qt_ref[sl, :] = qh.T.astype(BF16)
        k_ref[h] = (k[:, sl] + krot).astype(BF16)
    one_row = (lax.broadcasted_iota(jnp.int32, (1, C_PAD), 1) == C_V).astype(F32)
    for h in range(C_HEADS):
        vh = v[:, h * C_PAD:(h + 1) * C_PAD] + one_row
        vt_ref[h * C_VROWS:(h + 1) * C_VROWS, :] = vh.T[:C_VROWS, :].astype(BF16)


def _layer_block(a, layer):
    return pl.BlockSpec((None,) + a.shape[1:], lambda i, *_: (layer,) + (0,) * (a.ndim - 1))


def _in_proj(x, waq, wab, wc, wq, wqs, wk, wv, gq, gkv, cs, sn, layer):
    T = x.shape[0]
    tm = TM_IN
    full = lambda a: _layer_block(a, layer)
    row = lambda w: pl.BlockSpec((tm, w), lambda i: (i, 0))
    colblk = lambda r: pl.BlockSpec((r, tm), lambda i: (0, i))
    return pl.pallas_call(
        _in_proj_kernel,
        grid=(T // tm,),
        in_specs=[row(D_MODEL), full(waq), full(wab), full(wc), full(wq), full(wqs), full(wk), full(wv), full(gq),
                  full(gkv),
                  row(C_PAD), row(C_PAD)],
        out_specs=[row(AB_COLS), colblk(C_HEADS * C_PAD), pl.BlockSpec((C_HEADS, tm, C_PAD), lambda i: (0, i, 0)),
                   colblk(C_HEADS * C_VROWS)],
        out_shape=[jax.ShapeDtypeStruct((T, AB_COLS), F32), jax.ShapeDtypeStruct((C_HEADS * C_PAD, T), BF16),
                   jax.ShapeDtypeStruct((C_HEADS, T, C_PAD), BF16),
                   jax.ShapeDtypeStruct((C_HEADS * C_VROWS, T), BF16)],
        compiler_params=_cparams("arbitrary"),
        name="in_proj",
    )(x, waq, wab, wc, wq, wqs, wk, wv, gq, gkv, cs, sn)


def _banded_kernel(sink_ref, *refs, dil, n_heads, group, has_sink, want_lse):
    nq = n_heads // 2
    nkv = n_heads // group // 2
    q_refs, refs = refs[:nq], refs[nq:]
    k_refs, v_refs, kp_refs, vp_refs = (refs[i * nkv:(i + 1) * nkv] for i in range(4))
    bias_ref = refs[4 * nkv]
    o_refs = refs[4 * nkv + 1:4 * nkv + 1 + nq]
    lse_refs = refs[4 * nkv + 1 + nq:]
    c = pl.program_id(0)
    span = BLK * dil
    nb = CHUNK // span
    n_combo = nb * dil
    starts = [b * span + r for b in range(nb) for r in range(dil)]
    scale = HEAD_DIM ** -0.5

    def rows(start):
        return pl.ds(start, BLK) if dil == 1 else pl.ds(start, BLK, stride=dil)

    def gather(ref):
        return jnp.stack([ref[rows(s), :] for s in starts], axis=0)

    def with_prev(ref, prev_ref):
        cur = gather(ref)
        first = jnp.stack([prev_ref[rows(r), :] for r in range(dil)], axis=0)
        prev = jnp.concatenate([first, cur[:n_combo - dil]], axis=0) if dil < n_combo else first
        return jnp.concatenate([prev, cur], axis=1).astype(BF16)

    q = [gather(ref).astype(BF16) for ref in q_refs]
    kcat = [with_prev(k_refs[j], kp_refs[j]) for j in range(nkv)]
    vcat = [with_prev(v_refs[j], vp_refs[j]) for j in range(nkv)]

    lane = lax.broadcasted_iota(jnp.int32, (1, 1, LANES), 2)
    low = lane < HEAD_DIM
    combo = lax.broadcasted_iota(jnp.int32, (n_combo, 1, 2 * BLK), 0)
    col = lax.broadcasted_iota(jnp.int32, (n_combo, 1, 2 * BLK), 2)
    no_prev = jnp.logical_and(c == 0, jnp.logical_and(combo < dil, col < BLK))

    for j in range(nq):
        o_par, lse_par = [], []
        for par in range(2):
            h = 2 * j + par if group == 1 else group * par + j
            kvb = j if group == 1 else 0
            keep = low if par == 0 else jnp.logical_not(low)
            kk = jnp.where(keep, kcat[kvb], jnp.zeros((), BF16))
            s = _bdot_nt(q[j], kk)
            s = s * scale + bias_ref[h][None]
            s = jnp.where(no_prev, NEG_INF, s)
            m = jnp.max(s, axis=-1, keepdims=True)
            if has_sink:
                snk = sink_ref[h]
                m = jnp.maximum(m, snk)
            e = jnp.exp(s - m)
            l = jnp.sum(e, axis=-1, keepdims=True)
            if has_sink:
                l = l + jnp.exp(snk - m)
            p = (e * (1.0 / l)).astype(BF16)
            o_par.append(_bdot(p, vcat[kvb]))
            if want_lse:
                lse_par.append(m + jnp.log(l))
        o_pair = jnp.where(low, o_par[0], o_par[1])
        if want_lse:
            lse_pair = jnp.where(low, lse_par[0], lse_par[1])
        for ci, s0 in enumerate(starts):
            o_refs[j][rows(s0), :] = o_pair[ci]
            if want_lse:
                lse_refs[j][rows(s0), :] = lse_pair[ci]


def _banded(hab, sink, bias, *, q_col, k_col, v_col, dil, n_heads, group, has_sink, want_lse):
    T = hab.shape[0]
    span = BLK * dil
    nb = CHUNK // span
    nq = n_heads // 2
    nkv = n_heads // group // 2
    kernel = functools.partial(_banded_kernel, dil=dil, n_heads=n_heads, group=group,
                               has_sink=has_sink, want_lse=want_lse)
    prev = lambda cb: pl.BlockSpec((span, LANES), lambda c, s: (jnp.maximum(c * nb - 1, 0), cb))
    cur = lambda cb: pl.BlockSpec((CHUNK, LANES), lambda c, s: (c, cb))
    in_specs = ([cur(q_col + j) for j in range(nq)] + [cur(k_col + j) for j in range(nkv)]
                + [cur(v_col + j) for j in range(nkv)] + [prev(k_col + j) for j in range(nkv)]
                + [prev(v_col + j) for j in range(nkv)] + [pl.BlockSpec(bias.shape, lambda c, s: (0, 0, 0))])
    n_out = nq * (2 if want_lse else 1)
    grid_spec = pltpu.PrefetchScalarGridSpec(
        num_scalar_prefetch=1,
        grid=(T // CHUNK,),
        in_specs=in_specs,
        out_specs=[pl.BlockSpec((CHUNK, LANES), lambda c, s: (c, 0))] * n_out,
    )
    outs = pl.pallas_call(
        kernel,
        grid_spec=grid_spec,
        out_shape=[jax.ShapeDtypeStruct((T, LANES), F32)] * n_out,
        compiler_params=_cparams("arbitrary"),
        name=f"banded_d{dil}" + ("_sink" if has_sink else ""),
    )(sink, *([hab] * (nq + 4 * nkv)), bias)
    return outs[:nq], outs[nq:]


def _flash_kernel(qt_ref, k_ref, vt_ref, ot_ref, m_ref, acc_ref, s_ref):
    qi = pl.program_id(1)
    qt = qt_ref[...]
    m_ref[...] = jnp.full(m_ref.shape, NEG_INF, F32)
    acc_ref[...] = jnp.zeros(acc_ref.shape, F32)

    def scores(j):
        off = pl.multiple_of(j * TK, TK)
        return _dot(k_ref[pl.ds(off, TK), :], qt)

    def accumulate(j, s):
        off = pl.multiple_of(j * TK, TK)
        m_old = m_ref[...]
        m_new = jnp.maximum(m_old, jnp.max(s, axis=0, keepdims=True))
        alpha = jnp.exp2(m_old - m_new)
        p = jnp.exp2(s - m_new).astype(BF16)
        acc_ref[...] = alpha * acc_ref[...] + _dot(vt_ref[:, pl.ds(off, TK)], p)
        m_ref[...] = m_new

    def diagonal(s, d):
        key = lax.broadcasted_iota(jnp.int32, (TK, TQ), 0)
        qry = lax.broadcasted_iota(jnp.int32, (TK, TQ), 1)
        return jnp.where(key + d * TK <= qry, s, NEG_INF)

    n_full = qi * N_DIAG
    s_ref[0] = scores(0)
    n_main = n_full // FLASH_UNROLL

    def main_body(i, carry):
        j = FLASH_UNROLL * i
        for u in range(FLASH_UNROLL):
            s_ref[(u + 1) % 2] = scores(j + u + 1)
            accumulate(j + u, s_ref[u % 2])
        return carry

    lax.fori_loop(0, n_main, main_body, 0)

    base = FLASH_UNROLL * n_main
    for rem in range(0, FLASH_UNROLL, math.gcd(FLASH_UNROLL, N_DIAG)):
        @pl.when(n_full - base == rem)
        def _():
            n_tail = rem + N_DIAG
            for u in range(n_tail):
                if u + 1 < n_tail:
                    s_ref[(u + 1) % 2] = scores(base + u + 1)
                s = s_ref[u % 2]
                accumulate(base + u, s if u < rem else diagonal(s, u - rem))

    ot_ref[...] = (acc_ref[:C_V, :] / acc_ref[C_V:C_V + 1, :]).astype(ot_ref.dtype)


def _flash(qt, k, vt):
    H, T, _ = k.shape
    return pl.pallas_call(
        _flash_kernel,
        grid=(H, T // TQ),
        in_specs=[pl.BlockSpec((C_PAD, TQ), lambda h, i: (h, i)),
                  pl.BlockSpec((None, T, C_PAD), lambda h, i: (h, 0, 0)),
                  pl.BlockSpec((C_VROWS, T), lambda h, i: (h, 0))],
        out_specs=pl.BlockSpec((C_V, TQ), lambda h, i: (h, i)),
        out_shape=jax.ShapeDtypeStruct((H * C_V, T), BF16),
        scratch_shapes=[pltpu.VMEM((1, TQ), F32), pltpu.VMEM((C_VROWS, TQ), F32), pltpu.VMEM((2, TK, TQ), F32)],
        compiler_params=_cparams("arbitrary", "arbitrary"),
        name="mla_flash",
    )(qt, k, vt)


def _layer_norm(z, g, b):
    mu = jnp.mean(z, -1, keepdims=True)
    zc = z - mu
    var = jnp.mean(zc * zc, -1, keepdims=True)
    return zc * lax.rsqrt(var + 1e-5) * g + b


def _out_router_kernel(*refs):
    n_ab = 2 + 6 + 6
    oa_refs, ob_refs, lse_refs = refs[0:2], refs[2:8], refs[8:14]
    (oc_ref, x_ref, woa_ref, wob_ref, woc_ref, g_ref, b_ref, wr_ref, br_ref,
     x1_ref, slab_ref, cnt_ref, carry_ref, prev_ref) = refs[n_ab:]
    i = pl.program_id(0)
    tm = x_ref.shape[0]

    @pl.when(i == 0)
    def _():
        carry_ref[...] = jnp.zeros(carry_ref.shape, F32)
        prev_ref[...] = jnp.zeros(prev_ref.shape, F32)

    xr = prev_ref[...]

    y = None
    for j in range(2):
        l0, l1, l2 = (lse_refs[2 * g + j][...] for g in range(3))
        mx = jnp.maximum(jnp.maximum(l0, l1), l2)
        e0, e1, e2 = jnp.exp(l0 - mx), jnp.exp(l1 - mx), jnp.exp(l2 - mx)
        den = e0 + e1 + e2
        ob = (e0 * ob_refs[j][...] + e1 * ob_refs[2 + j][...] + e2 * ob_refs[4 + j][...]) * (1.0 / den)
        part = _dot(oa_refs[j][...].astype(BF16), woa_ref[j]) + _dot(ob.astype(BF16), wob_ref[j])
        y = part if y is None else y + part
    y = y + _dot(oc_ref[...].astype(F32).T.astype(BF16), woc_ref[...])
    x1 = _layer_norm(DEEPNORM_ALPHA * x_ref[...] + y, g_ref[...], b_ref[...])
    x1_ref[...] = x1
    prev_ref[...] = x1

    xr_hi = xr.astype(BF16)
    xr_lo = (xr - xr_hi.astype(F32)).astype(BF16)
    logits = _dot(jnp.concatenate([xr_hi, xr_lo, xr_hi], axis=1), wr_ref[...]) + br_ref[...]
    lane = lax.broadcasted_iota(jnp.int32, (tm, LANES), 1)
    ninf = jnp.float32(-jnp.inf)
    gmask = lane < N_GROUPS
    gl = jnp.where(gmask, logits, ninf)
    gmax = jnp.max(gl, axis=-1, keepdims=True)
    grp = jnp.min(jnp.where(gl == gmax, lane, LANES), axis=-1, keepdims=True)
    p_grp = 1.0 / jnp.sum(jnp.where(gmask, jnp.exp(logits - gmax), 0.0), axis=-1, keepdims=True)
    lo = N_GROUPS + EXPERTS_PER_GROUP * grp
    emask = jnp.logical_and(lane >= lo, lane < lo + EXPERTS_PER_GROUP)
    el = jnp.where(emask, logits, ninf)
    v0 = jnp.max(el, axis=-1, keepdims=True)
    i0 = jnp.min(jnp.where(el == v0, lane, LANES), axis=-1, keepdims=True)
    el2 = jnp.where(lane == i0, ninf, el)
    v1 = jnp.max(el2, axis=-1, keepdims=True)
    i1 = jnp.min(jnp.where(el2 == v1, lane, LANES), axis=-1, keepdims=True)
    t = jnp.exp(v1 - v0)
    gate0 = p_grp / (1.0 + t)
    gate1 = p_grp * t / (1.0 + t)

    hit0 = lane == i0
    hit1 = lane == i1
    cmat = jnp.logical_or(hit0, hit1).astype(BF16)
    tr = lax.broadcasted_iota(jnp.int32, (tm, tm), 0)
    tc = lax.broadcasted_iota(jnp.int32, (tm, tm), 1)
    tril = (tc < tr).astype(BF16)
    prefix = _dot(tril, cmat) + carry_ref[...]
    r0 = jnp.sum(jnp.where(hit0, prefix, 0.0), axis=-1, keepdims=True)
    r1 = jnp.sum(jnp.where(hit1, prefix, 0.0), axis=-1, keepdims=True)
    counted = (i > 0).astype(F32)
    carry_ref[...] = carry_ref[...] + counted * jnp.sum(cmat.astype(F32), axis=0, keepdims=True)
    cnt_ref[...] = carry_ref[...]

    fields = [(i0 - N_GROUPS).astype(F32), (i1 - N_GROUPS).astype(F32), r0, r1, gate0, gate1]
    slab = jnp.zeros((tm, LANES), F32)
    for n, f in enumerate(fields):
        slab = jnp.where(lane == n, f, slab)
    slab_ref[...] = slab


def _out_router(oa, ob, lse, oc, x, woa, wob, woc, g, b, wr, br, layer):
    T = x.shape[0]
    tm = TM_OUT
    n = T // tm
    full = lambda a: _layer_block(a, layer)
    row = lambda w: pl.BlockSpec((tm, w), lambda i: (jnp.minimum(i, n - 1), 0))
    lag = pl.BlockSpec((tm, LANES), lambda i: (jnp.maximum(i - 1, 0), 0))
    return pl.pallas_call(
        _out_router_kernel,
        grid=(n + 1,),
        in_specs=[row(LANES)] * 14 + [pl.BlockSpec((C_HEADS * C_V, tm), lambda i: (0, jnp.minimum(i, n - 1))),
                                      row(D_MODEL),
                                      full(woa), full(wob), full(woc), full(g), full(b), full(wr), full(br)],
        out_specs=[row(D_MODEL), lag, pl.BlockSpec((1, LANES), lambda i: (0, 0))],
        out_shape=[jax.ShapeDtypeStruct((T, D_MODEL), F32), jax.ShapeDtypeStruct((T, LANES), F32),
                   jax.ShapeDtypeStruct((1, LANES), F32)],
        scratch_shapes=[pltpu.VMEM((1, LANES), F32), pltpu.VMEM((tm, D_MODEL), F32)],
        compiler_params=_cparams("arbitrary"),
        name="out_router",
    )(*oa, *ob, *lse, oc, x, woa, wob, woc, g, b, wr, br)


SUBLANES = 8
PAD_BITS = tuple(1 << k for k in reversed(range(SUBLANES.bit_length() - 1, ROW_BLK.bit_length() - 1)))


def _dispatch_kernel(d0_ref, d1_ref, pad_at_ref, pad_len_ref, nu_ref, x_ref, xs_ref, zero_ref, sem, zsem):
    i = pl.program_id(0)
    tm = x_ref.shape[0] * SUBLANES
    base = i * tm

    @pl.when(i == 0)
    def _():
        zero_ref[...] = jnp.zeros(zero_ref.shape, F32)

        def pad_copies(e):
            n = pad_len_ref[e]
            at = pad_at_ref[e]
            head = n & (SUBLANES - 1)
            out = [(j < head, pltpu.make_async_copy(zero_ref.at[pl.ds(0, 1), :], xs_ref.at[pl.ds(at + j, 1), :], zsem))
                   for j in range(SUBLANES - 1)]
            body = n - head
            for bit in PAD_BITS:
                start = pl.multiple_of(at + head + (body & ~(2 * bit - 1)), SUBLANES)
                out.append(((body & bit) != 0,
                            pltpu.make_async_copy(zero_ref.at[pl.ds(0, bit), :], xs_ref.at[pl.ds(start, bit), :], zsem)))
            return out

        def spare_copies():
            n_blocks = xs_ref.shape[0] // ROW_BLK
            first = d0_ref.shape[0] * TOP_K // ROW_BLK
            return [(b >= nu_ref[0], pltpu.make_async_copy(zero_ref, xs_ref.at[pl.ds(b * ROW_BLK, ROW_BLK), :], zsem))
                    for b in range(first, n_blocks)]

        for e in range(N_EXPERTS):
            for on, cp in pad_copies(e):
                pl.when(on)(cp.start)
        for on, cp in spare_copies():
            pl.when(on)(cp.start)
        for e in range(N_EXPERTS):
            for on, cp in pad_copies(e):
                pl.when(on)(cp.wait)
        for on, cp in spare_copies():
            pl.when(on)(cp.wait)

    def start(g, carry):
        for u in range(SUBLANES):
            t = g * SUBLANES + u
            for d_ref in (d0_ref, d1_ref):
                pltpu.make_async_copy(x_ref.at[g, pl.ds(u, 1), :], xs_ref.at[pl.ds(d_ref[base + t], 1), :], sem).start()
        return carry

    lax.fori_loop(0, tm // SUBLANES, start, 0)
    for _ in range(2):
        pltpu.make_async_copy(xs_ref.at[pl.ds(0, tm), :], zero_ref, sem).wait()


def _dispatch(d0, d1, pad_at, pad_len, n_used, x1, n_rows):
    T = x1.shape[0]
    tm = ROW_BLK
    grid_spec = pltpu.PrefetchScalarGridSpec(
        num_scalar_prefetch=5,
        grid=(T // tm,),
        in_specs=[pl.BlockSpec((tm // SUBLANES, SUBLANES, D_MODEL), lambda i, *_: (i, 0, 0))],
        out_specs=pl.BlockSpec(memory_space=pl.ANY),
        scratch_shapes=[pltpu.VMEM((ROW_BLK, D_MODEL), F32), pltpu.SemaphoreType.DMA, pltpu.SemaphoreType.DMA],
    )
    return pl.pallas_call(
        _dispatch_kernel,
        grid_spec=grid_spec,
        out_shape=jax.ShapeDtypeStruct((n_rows, D_MODEL), F32),
        compiler_params=_cparams("arbitrary"),
        name="moe_dispatch",
    )(d0, d1, pad_at, pad_len, n_used, x1.reshape(T // SUBLANES, SUBLANES, D_MODEL))


FFN_BLOCKS = 1


def _ffn_kernel(be_ref, nu_ref, xs_ref, *refs):
    del be_ref
    w_refs, ys_ref = refs[:-1], refs[-1]
    i = pl.program_id(0)
    for k in range(FFN_BLOCKS):
        wg_ref, wu_ref, wd_ref = w_refs[3 * k:3 * k + 3]
        rows = pl.ds(k * ROW_BLK, ROW_BLK)
        used = FFN_BLOCKS * i + k < nu_ref[0]

        @pl.when(used)
        def _():
            xb = xs_ref[rows, :].astype(BF16)
            a = _dot(xb, wg_ref[...].astype(BF16))
            u = _dot(xb, wu_ref[...].astype(BF16))
            hmid = (a * jax.nn.sigmoid(a) * u).astype(BF16)
            ys_ref[rows, :] = _dot(hmid, wd_ref[...].astype(BF16))

        @pl.when(jnp.logical_not(used))
        def _():
            ys_ref[rows, :] = jnp.zeros((ROW_BLK, D_MODEL), F32)


def _ffn(block_expert, n_used, xs, wg, wu, wd, layer):
    n_rows = xs.shape[0]
    step_rows = FFN_BLOCKS * ROW_BLK
    assert n_rows % step_rows == 0
    w_spec = lambda shape, k: pl.BlockSpec((None, None) + shape,
                                           lambda i, be, nu: (layer, be[FFN_BLOCKS * i + k], 0, 0))
    w_specs = [w_spec(s, k) for k in range(FFN_BLOCKS)
               for s in ((D_MODEL, D_EXPERT), (D_MODEL, D_EXPERT), (D_EXPERT, D_MODEL))]
    grid_spec = pltpu.PrefetchScalarGridSpec(
        num_scalar_prefetch=2,
        grid=(n_rows // step_rows,),
        in_specs=[pl.BlockSpec((step_rows, D_MODEL),
                               lambda i, be, nu: (jnp.minimum(i, (nu[0] - 1) // FFN_BLOCKS), 0))] + w_specs,
        out_specs=pl.BlockSpec((step_rows, D_MODEL), lambda i, be, nu: (i, 0)),
    )
    return pl.pallas_call(
        _ffn_kernel,
        grid_spec=grid_spec,
        out_shape=jax.ShapeDtypeStruct((n_rows, D_MODEL), F32),
        compiler_params=_cparams("arbitrary"),
        name="moe_ffn",
    )(block_expert, n_used, xs, *([wg, wu, wd] * FFN_BLOCKS))


def _combine_kernel(d0_ref, d1_ref, ys_ref, x1_ref, slab_ref, g_ref, b_ref, o_ref, buf_ref, sem):
    i = pl.program_id(0)
    n = pl.num_programs(0)
    tm = x1_ref.shape[0]

    def gather(tile):
        slot = tile % 2
        base = tile * tm

        def start(g, carry):
            for u in range(SUBLANES):
                t = g * SUBLANES + u
                for k, d_ref in enumerate((d0_ref, d1_ref)):
                    pltpu.make_async_copy(ys_ref.at[pl.ds(d_ref[base + t], 1), :],
                                          buf_ref.at[slot, k, g, pl.ds(u, 1), :], sem.at[slot]).start()
            return carry

        lax.fori_loop(0, tm // SUBLANES, start, 0)

    @pl.when(i == 0)
    def _():
        gather(0)

    @pl.when(i + 1 < n)
    def _():
        gather(i + 1)

    slot = i % 2
    for k in range(2):
        pltpu.make_async_copy(ys_ref.at[pl.ds(0, tm), :], o_ref, sem.at[slot]).wait()
    slab = slab_ref[...]
    rows = lambda k: buf_ref[slot, k].reshape(tm, D_MODEL)
    y = slab[:, 4:5] * rows(0) + slab[:, 5:6] * rows(1)
    o_ref[...] = _layer_norm(DEEPNORM_ALPHA * x1_ref[...] + y, g_ref[...], b_ref[...])


def _combine(d0, d1, ys, x1, slab, g, b, layer):
    T = x1.shape[0]
    tm = ROW_BLK
    grid_spec = pltpu.PrefetchScalarGridSpec(
        num_scalar_prefetch=2,
        grid=(T // tm,),
        in_specs=[pl.BlockSpec(memory_space=pl.ANY),
                  pl.BlockSpec((tm, D_MODEL), lambda i, a, c: (i, 0)),
                  pl.BlockSpec((tm, LANES), lambda i, a, c: (i, 0)),
                  _layer_block(g, layer), _layer_block(b, layer)],
        out_specs=pl.BlockSpec((tm, D_MODEL), lambda i, a, c: (i, 0)),
        scratch_shapes=[pltpu.VMEM((2, 2, tm // SUBLANES, SUBLANES, D_MODEL), F32), pltpu.SemaphoreType.DMA((2,))],
    )
    return pl.pallas_call(
        _combine_kernel,
        grid_spec=grid_spec,
        out_shape=jax.ShapeDtypeStruct((T, D_MODEL), F32),
        compiler_params=_cparams("arbitrary"),
        name="moe_combine",
    )(d0, d1, ys, x1, slab, g, b)


def _t5_bucket(dist):
    max_exact = N_BUCKETS // 2
    large = max_exact + (jnp.log(jnp.maximum(dist, 1).astype(F32) / max_exact)
                         / math.log(MAX_DISTANCE / max_exact) * (N_BUCKETS - max_exact)).astype(jnp.int32)
    return jnp.where(dist < max_exact, dist, jnp.minimum(large, N_BUCKETS - 1))


def _bias_table(table_cols, dil, max_dist):
    dist = BLK + jnp.arange(BLK)[:, None] - jnp.arange(2 * BLK)[None, :]
    bucket = _t5_bucket(jnp.maximum(dist * dil, 0))
    hit = bucket[None, :, :, None] == jnp.arange(N_BUCKETS)[None, None, None, :]
    bias = jnp.sum(jnp.where(hit, table_cols.T.astype(F32)[:, None, None, :], 0.0), axis=-1)
    mask = (dist >= 0) & (dist <= max_dist)
    return jnp.where(mask[None], bias, NEG_INF)


def _rope_tables(T):
    inv_freq = ROPE_THETA ** (-jnp.arange(0, C_ROPE, 2, dtype=F32) / C_ROPE)
    ang = jnp.arange(T, dtype=F32)[:, None] * inv_freq[None, :]
    cos, sin = jnp.cos(ang), jnp.sin(ang)
    ones = jnp.ones((T, C_NOPE), F32)
    zeros_n = jnp.zeros((T, C_NOPE), F32)
    zeros_p = jnp.zeros((T, C_PAD - C_NOPE - C_ROPE), F32)
    cs = jnp.concatenate([ones, cos, cos, zeros_p], axis=1)
    sn = jnp.concatenate([zeros_n, -sin, sin, zeros_p], axis=1)
    return cs, sn


def _swap_halves(w):
    half = w.shape[-1] // 2
    return jnp.concatenate([w[..., half:], w[..., :half]], axis=-1)


def _pad_cols(w, width):
    return jnp.pad(w, ((0, 0), (0, width - w.shape[1])))


def _prep_layer_weights(w_in, w_out, w_uq, w_ukv, w_rg, b_rg, w_re, b_re):
    head_cols = lambda w, h: w[:, h * HEAD_DIM:(h + 1) * HEAD_DIM]
    waq = jnp.concatenate([head_cols(w_in, h) for h in A_HEAD_ORDER], axis=1).astype(BF16)
    wab = w_in[:, A_Q_HEADS * HEAD_DIM:AB_COLS].astype(BF16)
    c0 = AB_COLS
    w_cq = w_in[:, c0:c0 + Q_LORA]
    w_ckv = w_in[:, c0 + Q_LORA:c0 + Q_LORA + KV_LORA]
    w_kr = w_in[:, c0 + Q_LORA + KV_LORA:]
    lead = jnp.zeros((D_MODEL, C_NOPE), F32)
    wc = jnp.concatenate([w_cq, w_ckv,
                          _pad_cols(jnp.concatenate([lead, w_kr], 1), C_PAD),
                          _pad_cols(jnp.concatenate([lead, _swap_halves(w_kr)], 1), C_PAD)], axis=1).astype(BF16)
    uq = w_uq.reshape(Q_LORA, C_HEADS, C_NOPE + C_ROPE)
    wq = jnp.pad(uq, ((0, 0), (0, 0), (0, C_PAD - C_NOPE - C_ROPE))).reshape(Q_LORA, C_HEADS * C_PAD).astype(BF16)
    uqs = jnp.concatenate([jnp.zeros((Q_LORA, C_HEADS, C_NOPE), F32), _swap_halves(uq[..., C_NOPE:])], axis=-1)
    wqs = jnp.pad(uqs, ((0, 0), (0, 0), (0, C_PAD - C_NOPE - C_ROPE))).reshape(Q_LORA, C_HEADS * C_PAD).astype(BF16)
    ukv = w_ukv.reshape(KV_LORA, C_HEADS, C_NOPE + C_V)
    wk = jnp.pad(ukv[..., :C_NOPE], ((0, 0), (0, 0), (0, C_PAD - C_NOPE))).reshape(KV_LORA, C_HEADS * C_PAD)
    wv = jnp.pad(ukv[..., C_NOPE:], ((0, 0), (0, 0), (0, C_PAD - C_V))).reshape(KV_LORA, C_HEADS * C_PAD)
    woa = jnp.concatenate([w_out[h * HEAD_DIM:(h + 1) * HEAD_DIM] for h in A_HEAD_ORDER],
                          axis=0).reshape(2, LANES, D_MODEL).astype(BF16)
    wob = w_out[256:512].reshape(2, LANES, D_MODEL).astype(BF16)
    woc = w_out[512:].astype(BF16)
    wr = _pad_cols(jnp.concatenate([w_rg, w_re.transpose(1, 0, 2).reshape(D_MODEL, N_EXPERTS)], axis=1), LANES)
    wr_hi = wr.astype(BF16)
    wr_lo = (wr - wr_hi.astype(F32)).astype(BF16)
    wr = jnp.concatenate([wr_hi, wr_hi, wr_lo], axis=0)
    br = _pad_cols(jnp.concatenate([b_rg, b_re.reshape(N_EXPERTS)])[None, :], LANES)
    return dict(waq=waq, wab=wab, wc=wc, wq=wq, wqs=wqs, wk=wk.astype(BF16), wv=wv.astype(BF16),
                woa=woa, wob=wob, woc=woc, wr=wr, br=br)


def kernel(x, w_in, w_out, sinks, rel_bias, mla_q_norm, mla_kv_norm, w_uq, w_ukv, ln1_g, ln1_b,
           w_route_group, b_route_group, w_route_expert, b_route_expert,
           w_expert_gate, w_expert_up, w_expert_down, ln2_g, ln2_b):
    Bsz, T, D = x.shape
    assert Bsz == 1 and D == D_MODEL and T % CHUNK == 0
    xt = x.reshape(T, D)
    cs, sn = _rope_tables(T)
    bias_a = _bias_table(rel_bias[:, :A_Q_HEADS], 1, SWA_WINDOW - 1)
    bias_b = [_bias_table(rel_bias[:, A_Q_HEADS + g * B_HEADS_PER_GROUP:A_Q_HEADS + (g + 1) * B_HEADS_PER_GROUP],
                          dil, window // dil) for g, (window, dil) in enumerate(DILATED_PAIRS)]
    n_assign = T * TOP_K
    n_blocks = n_assign // ROW_BLK + N_EXPERTS
    n_rows = n_blocks * ROW_BLK
    no_sink = jnp.zeros((B_HEADS_PER_GROUP,), F32)

    w = jax.vmap(_prep_layer_weights)(w_in, w_out, w_uq, w_ukv, w_route_group, b_route_group,
                                      w_route_expert, b_route_expert)
    per_layer_row = lambda a: a[:, None, :]
    gq, gkv = per_layer_row(mla_q_norm), per_layer_row(mla_kv_norm)
    g1, b1, g2, b2 = (per_layer_row(a) for a in (ln1_g, ln1_b, ln2_g, ln2_b))

    for layer in range(w_in.shape[0]):
        hab, qt, k, vt = _in_proj(xt, w["waq"], w["wab"], w["wc"], w["wq"], w["wqs"], w["wk"], w["wv"], gq, gkv, cs, sn,
                                  layer)
        oa, _ = _banded(hab, sinks[layer], bias_a, q_col=0, k_col=2, v_col=3, dil=1,
                        n_heads=A_Q_HEADS, group=A_Q_HEADS // A_KV_HEADS, has_sink=True, want_lse=False)
        ob, lse = [], []
        for g, (_, dil) in enumerate(DILATED_PAIRS):
            o_g, lse_g = _banded(hab, no_sink, bias_b[g], q_col=4 + 2 * g, k_col=10 + 2 * g, v_col=16 + 2 * g,
                                 dil=dil, n_heads=B_HEADS_PER_GROUP, group=1, has_sink=False, want_lse=True)
            ob += o_g
            lse += lse_g
        oc = _flash(qt, k, vt)
        x1, slab, cnt = _out_router(oa, ob, lse, oc, xt, w["woa"], w["wob"], w["woc"], g1, b1, w["wr"], w["br"],
                                    layer)

        counts = cnt[0, N_GROUPS:N_GROUPS + N_EXPERTS].astype(jnp.int32)
        padded = ((counts + ROW_BLK - 1) // ROW_BLK) * ROW_BLK
        pad_end = jnp.cumsum(padded)
        pad_start = pad_end - padded
        eids = jnp.arange(N_EXPERTS, dtype=F32)
        onehot = (slab[:, 0:TOP_K, None] == eids[None, None, :]).astype(F32)
        seg = jnp.einsum('tke,e->tk', onehot, pad_start.astype(F32), precision=lax.Precision.HIGHEST)
        d0 = (seg[:, 0] + slab[:, 2]).astype(jnp.int32)
        d1 = (seg[:, 1] + slab[:, 3]).astype(jnp.int32)
        block_start = jnp.arange(n_blocks, dtype=jnp.int32) * ROW_BLK
        block_expert = jnp.minimum(jnp.sum((pad_end[None, :] <= block_start[:, None]).astype(jnp.int32), axis=1),
                                   N_EXPERTS - 1)

        n_used = (pad_end[-1:] // ROW_BLK).astype(jnp.int32)
        xs = _dispatch(d0, d1, pad_start + counts, padded - counts, n_used, x1, n_rows)
        ys = _ffn(block_expert, n_used, xs, w_expert_gate, w_expert_up, w_expert_down, layer)
        xt = _combine(d0, d1, ys, x1, slab, g2, b2, layer)
    return xt.reshape(Bsz, T, D)
```

```python
import functools
import math

import jax
import jax.numpy as jnp
from jax import lax
from jax.experimental import pallas as pl
from jax.experimental.pallas import tpu as pltpu

F32 = jnp.float32
BF16 = jnp.bfloat16

D_MODEL = 1024
DEPTH = 4
HEAD_DIM = 64
BLK = 128
NEG_INF = -1e30

SWA_WINDOW = 128
A_Q_HEADS = 4
A_KV_HEADS = 2
A_COLS = (A_Q_HEADS + 2 * A_KV_HEADS) * HEAD_DIM
A_HEAD_ORDER = (0, 2, 1, 3)

DILATED_PAIRS = ((128, 1), (512, 4), (2048, 16))
B_HEADS_PER_GROUP = 4
B_HEADS = len(DILATED_PAIRS) * B_HEADS_PER_GROUP
B_COLS = 3 * B_HEADS * HEAD_DIM
AB_COLS = A_COLS + B_COLS

C_HEADS = 8
C_NOPE = 64
C_ROPE = 32
C_V = 64
Q_LORA = 256
KV_LORA = 128
ROPE_THETA = 10000.0
C_PAD = 128
C_VROWS = 80

N_BUCKETS = 32
MAX_DISTANCE = 2048

N_GROUPS = 4
EXPERTS_PER_GROUP = 8
N_EXPERTS = N_GROUPS * EXPERTS_PER_GROUP
TOP_K = 2
D_EXPERT = 256
ROW_BLK = 512

DEEPNORM_ALPHA = (2 * DEPTH) ** 0.25

VMEM_LIMIT = 56 * 1024 * 1024

CHUNK = 2048
TM_IN = 512
TM_OUT = 512
TQ = 1024
TK = 512
N_DIAG = TQ // TK
FLASH_UNROLL = 4
LANES = 128


def _dot(a, b):
    return jnp.dot(a, b, preferred_element_type=F32)


def _dot_nt(a, b):
    return lax.dot_general(a, b, (((1,), (1,)), ((), ())), preferred_element_type=F32)


def _bdot(a, b):
    return lax.dot_general(a, b, (((2,), (1,)), ((0,), (0,))), preferred_element_type=F32)


def _bdot_nt(a, b):
    return lax.dot_general(a, b, (((2,), (2,)), ((0,), (0,))), preferred_element_type=F32)


def _cparams(*sem):
    return pltpu.CompilerParams(dimension_semantics=sem, vmem_limit_bytes=VMEM_LIMIT)


def _in_proj_kernel(x_ref, waq_ref, wab_ref, wc_ref, wq_ref, wqs_ref, wk_ref, wv_ref, gq_ref, gkv_ref, cs_ref, sn_ref,
                    hab_ref, qt_ref, k_ref, vt_ref):
    xb = x_ref[...].astype(BF16)
    n_aq = waq_ref.shape[1]
    hab_ref[:, :n_aq] = _dot(xb, waq_ref[...])
    hab_ref[:, n_aq:] = _dot(xb, wab_ref[...])
    hc = _dot(xb, wc_ref[...])
    cq = hc[:, :Q_LORA]
    ckv = hc[:, Q_LORA:Q_LORA + KV_LORA]
    kr = hc[:, Q_LORA + KV_LORA:Q_LORA + KV_LORA + C_PAD]
    krs = hc[:, Q_LORA + KV_LORA + C_PAD:]
    cqn = (cq * lax.rsqrt(jnp.mean(cq * cq, -1, keepdims=True) + 1e-6) * gq_ref[...]).astype(BF16)
    ckvn = (ckv * lax.rsqrt(jnp.mean(ckv * ckv, -1, keepdims=True) + 1e-6) * gkv_ref[...]).astype(BF16)
    cs = cs_ref[...]
    sn = sn_ref[...]
    q = _dot(cqn, wq_ref[...])
    qs = _dot(cqn, wqs_ref[...])
    k = _dot(ckvn, wk_ref[...])
    v = _dot(ckvn, wv_ref[...])
    krot = kr * cs + krs * sn
    scale = (C_NOPE + C_ROPE) ** -0.5 * math.log2(math.e)
    for h in range(C_HEADS):
        sl = slice(h * C_PAD, (h + 1) * C_PAD)
        qh = (q[:, sl] * cs + qs[:, sl] * sn) * scale
        qt_ref[sl, :] = qh.T.astype(BF16)
        k_ref[h] = (k[:, sl] + krot).astype(BF16)
    one_row = (lax.broadcasted_iota(jnp.int32, (1, C_PAD), 1) == C_V).astype(F32)
    for h in range(C_HEADS):
        vh = v[:, h * C_PAD:(h + 1) * C_PAD] + one_row
        vt_ref[h * C_VROWS:(h + 1) * C_VROWS, :] = vh.T[:C_VROWS, :].astype(BF16)


def _layer_block(a, layer):
    return pl.BlockSpec((None,) + a.shape[1:], lambda i, *_: (layer,) + (0,) * (a.ndim - 1))


def _in_proj(x, waq, wab, wc, wq, wqs, wk, wv, gq, gkv, cs, sn, layer):
    T = x.shape[0]
    tm = TM_IN
    full = lambda a: _layer_block(a, layer)
    row = lambda w: pl.BlockSpec((tm, w), lambda i: (i, 0))
    colblk = lambda r: pl.BlockSpec((r, tm), lambda i: (0, i))
    return pl.pallas_call(
        _in_proj_kernel,
        grid=(T // tm,),
        in_specs=[row(D_MODEL), full(waq), full(wab), full(wc), full(wq), full(wqs), full(wk), full(wv), full(gq),
                  full(gkv),
                  row(C_PAD), row(C_PAD)],
        out_specs=[row(AB_COLS), colblk(C_HEADS * C_PAD), pl.BlockSpec((C_HEADS, tm, C_PAD), lambda i: (0, i, 0)),
                   colblk(C_HEADS * C_VROWS)],
        out_shape=[jax.ShapeDtypeStruct((T, AB_COLS), F32), jax.ShapeDtypeStruct((C_HEADS * C_PAD, T), BF16),
                   jax.ShapeDtypeStruct((C_HEADS, T, C_PAD), BF16),
                   jax.ShapeDtypeStruct((C_HEADS * C_VROWS, T), BF16)],
        compiler_params=_cparams("arbitrary"),
        name="in_proj",
    )(x, waq, wab, wc, wq, wqs, wk, wv, gq, gkv, cs, sn)


def _banded_kernel(sink_ref, *refs, dil, n_heads, group, has_sink, want_lse):
    nq = n_heads // 2
    nkv = n_heads // group // 2
    q_refs, refs = refs[:nq], refs[nq:]
    k_refs, v_refs, kp_refs, vp_refs = (refs[i * nkv:(i + 1) * nkv] for i in range(4))
    bias_ref = refs[4 * nkv]
    o_refs = refs[4 * nkv + 1:4 * nkv + 1 + nq]
    lse_refs = refs[4 * nkv + 1 + nq:]
    c = pl.program_id(0)
    span = BLK * dil
    nb = CHUNK // span
    n_combo = nb * dil
    starts = [b * span + r for b in range(nb) for r in range(dil)]
    scale = HEAD_DIM ** -0.5

    def rows(start):
        return pl.ds(start, BLK) if dil == 1 else pl.ds(start, BLK, stride=dil)

    def gather(ref):
        return jnp.stack([ref[rows(s), :] for s in starts], axis=0)

    def with_prev(ref, prev_ref):
        cur = gather(ref)
        first = jnp.stack([prev_ref[rows(r), :] for r in range(dil)], axis=0)
        prev = jnp.concatenate([first, cur[:n_combo - dil]], axis=0) if dil < n_combo else first
        return jnp.concatenate([prev, cur], axis=1).astype(BF16)

    q = [gather(ref).astype(BF16) for ref in q_refs]
    kcat = [with_prev(k_refs[j], kp_refs[j]) for j in range(nkv)]
    vcat = [with_prev(v_refs[j], vp_refs[j]) for j in range(nkv)]

    lane = lax.broadcasted_iota(jnp.int32, (1, 1, LANES), 2)
    low = lane < HEAD_DIM
    combo = lax.broadcasted_iota(jnp.int32, (n_combo, 1, 2 * BLK), 0)
    col = lax.broadcasted_iota(jnp.int32, (n_combo, 1, 2 * BLK), 2)
    no_prev = jnp.logical_and(c == 0, jnp.logical_and(combo < dil, col < BLK))

    for j in range(nq):
        o_par, lse_par = [], []
        for par in range(2):
            h = 2 * j + par if group == 1 else group * par + j
            kvb = j if group == 1 else 0
            keep = low if par == 0 else jnp.logical_not(low)
            kk = jnp.where(keep, kcat[kvb], jnp.zeros((), BF16))
            s = _bdot_nt(q[j], kk)
            s = s * scale + bias_ref[h][None]
            s = jnp.where(no_prev, NEG_INF, s)
            m = jnp.max(s, axis=-1, keepdims=True)
            if has_sink:
                snk = sink_ref[h]
                m = jnp.maximum(m, snk)
            e = jnp.exp(s - m)
            l = jnp.sum(e, axis=-1, keepdims=True)
            if has_sink:
                l = l + jnp.exp(snk - m)
            p = (e * (1.0 / l)).astype(BF16)
            o_par.append(_bdot(p, vcat[kvb]))
            if want_lse:
                lse_par.append(m + jnp.log(l))
        o_pair = jnp.where(low, o_par[0], o_par[1])
        if want_lse:
            lse_pair = jnp.where(low, lse_par[0], lse_par[1])
        for ci, s0 in enumerate(starts):
            o_refs[j][rows(s0), :] = o_pair[ci]
            if want_lse:
                lse_refs[j][rows(s0), :] = lse_pair[ci]


def _banded(hab, sink, bias, *, q_col, k_col, v_col, dil, n_heads, group, has_sink, want_lse):
    T = hab.shape[0]
    span = BLK * dil
    nb = CHUNK // span
    nq = n_heads // 2
    nkv = n_heads // group // 2
    kernel = functools.partial(_banded_kernel, dil=dil, n_heads=n_heads, group=group,
                               has_sink=has_sink, want_lse=want_lse)
    prev = lambda cb: pl.BlockSpec((span, LANES), lambda c, s: (jnp.maximum(c * nb - 1, 0), cb))
    cur = lambda cb: pl.BlockSpec((CHUNK, LANES), lambda c, s: (c, cb))
    in_specs = ([cur(q_col + j) for j in range(nq)] + [cur(k_col + j) for j in range(nkv)]
                + [cur(v_col + j) for j in range(nkv)] + [prev(k_col + j) for j in range(nkv)]
                + [prev(v_col + j) for j in range(nkv)] + [pl.BlockSpec(bias.shape, lambda c, s: (0, 0, 0))])
    n_out = nq * (2 if want_lse else 1)
    grid_spec = pltpu.PrefetchScalarGridSpec(
        num_scalar_prefetch=1,
        grid=(T // CHUNK,),
        in_specs=in_specs,
        out_specs=[pl.BlockSpec((CHUNK, LANES), lambda c, s: (c, 0))] * n_out,
    )
    outs = pl.pallas_call(
        kernel,
        grid_spec=grid_spec,
        out_shape=[jax.ShapeDtypeStruct((T, LANES), F32)] * n_out,
        compiler_params=_cparams("arbitrary"),
        name=f"banded_d{dil}" + ("_sink" if has_sink else ""),
    )(sink, *([hab] * (nq + 4 * nkv)), bias)
    return outs[:nq], outs[nq:]


def _flash_kernel(qt_ref, k_ref, vt_ref, ot_ref, m_ref, acc_ref, s_ref):
    qi = pl.program_id(1)
    qt = qt_ref[...]
    m_ref[...] = jnp.full(m_ref.shape, NEG_INF, F32)
    acc_ref[...] = jnp.zeros(acc_ref.shape, F32)

    def scores(j, lo=0):
        off = pl.multiple_of(j * TK, TK)
        return _dot(k_ref[pl.ds(off, TK), :], qt[:, lo:])

    def accumulate(j, s, lo=0):
        off = pl.multiple_of(j * TK, TK)
        m_old = m_ref[:, lo:]
        m_new = jnp.maximum(m_old, jnp.max(s, axis=0, keepdims=True))
        alpha = jnp.exp2(m_old - m_new)
        p = jnp.exp2(s - m_new).astype(BF16)
        acc_ref[:, lo:] = alpha * acc_ref[:, lo:] + _dot(vt_ref[:, pl.ds(off, TK)], p)
        m_ref[:, lo:] = m_new

    def diagonal(s):
        key = lax.broadcasted_iota(jnp.int32, s.shape, 0)
        qry = lax.broadcasted_iota(jnp.int32, s.shape, 1)
        return jnp.where(key <= qry, s, NEG_INF)

    n_full = qi * N_DIAG
    s_ref[0] = scores(0)
    n_main = n_full // FLASH_UNROLL

    def main_body(i, carry):
        j = FLASH_UNROLL * i
        for u in range(FLASH_UNROLL):
            s_ref[(u + 1) % 2] = scores(j + u + 1)
            accumulate(j + u, s_ref[u % 2])
        return carry

    lax.fori_loop(0, n_main, main_body, 0)

    base = FLASH_UNROLL * n_main
    for rem in range(0, FLASH_UNROLL, math.gcd(FLASH_UNROLL, N_DIAG)):
        @pl.when(n_full - base == rem)
        def _():
            n_tail = rem + N_DIAG
            first_lane = lambda u: max(u - rem, 0) * TK
            for u in range(n_tail):
                if u + 1 < n_tail:
                    lo = first_lane(u + 1)
                    s_ref[(u + 1) % 2, :, lo:] = scores(base + u + 1, lo)
                lo = first_lane(u)
                s = s_ref[u % 2, :, lo:]
                accumulate(base + u, s if u < rem else diagonal(s), lo)

    ot_ref[...] = (acc_ref[:C_V, :] / acc_ref[C_V:C_V + 1, :]).astype(ot_ref.dtype)


def _flash(qt, k, vt):
    H, T, _ = k.shape
    return pl.pallas_call(
        _flash_kernel,
        grid=(H, T // TQ),
        in_specs=[pl.BlockSpec((C_PAD, TQ), lambda h, i: (h, i)),
                  pl.BlockSpec((None, T, C_PAD), lambda h, i: (h, 0, 0)),
                  pl.BlockSpec((C_VROWS, T), lambda h, i: (h, 0))],
        out_specs=pl.BlockSpec((C_V, TQ), lambda h, i: (h, i)),
        out_shape=jax.ShapeDtypeStruct((H * C_V, T), BF16),
        scratch_shapes=[pltpu.VMEM((1, TQ), F32), pltpu.VMEM((C_VROWS, TQ), F32), pltpu.VMEM((2, TK, TQ), F32)],
        compiler_params=_cparams("arbitrary", "arbitrary"),
        name="mla_flash",
    )(qt, k, vt)


def _layer_norm(z, g, b):
    mu = jnp.mean(z, -1, keepdims=True)
    zc = z - mu
    var = jnp.mean(zc * zc, -1, keepdims=True)
    return zc * lax.rsqrt(var + 1e-5) * g + b


def _out_router_kernel(*refs):
    n_ab = 2 + 6 + 6
    oa_refs, ob_refs, lse_refs = refs[0:2], refs[2:8], refs[8:14]
    (oc_ref, x_ref, woa_ref, wob_ref, woc_ref, g_ref, b_ref, wr_ref, br_ref,
     x1_ref, slab_ref, cnt_ref, carry_ref, prev_ref) = refs[n_ab:]
    i = pl.program_id(0)
    tm = x_ref.shape[0]

    @pl.when(i == 0)
    def _():
        carry_ref[...] = jnp.zeros(carry_ref.shape, F32)
        prev_ref[...] = jnp.zeros(prev_ref.shape, F32)

    xr = prev_ref[...]

    y = None
    for j in range(2):
        l0, l1, l2 = (lse_refs[2 * g + j][...] for g in range(3))
        mx = jnp.maximum(jnp.maximum(l0, l1), l2)
        e0, e1, e2 = jnp.exp(l0 - mx), jnp.exp(l1 - mx), jnp.exp(l2 - mx)
        den = e0 + e1 + e2
        ob = (e0 * ob_refs[j][...] + e1 * ob_refs[2 + j][...] + e2 * ob_refs[4 + j][...]) * (1.0 / den)
        part = _dot(oa_refs[j][...].astype(BF16), woa_ref[j]) + _dot(ob.astype(BF16), wob_ref[j])
        y = part if y is None else y + part
    y = y + _dot(oc_ref[...].astype(F32).T.astype(BF16), woc_ref[...])
    x1 = _layer_norm(DEEPNORM_ALPHA * x_ref[...] + y, g_ref[...], b_ref[...])
    x1_ref[...] = x1
    prev_ref[...] = x1

    xr_hi = xr.astype(BF16)
    xr_lo = (xr - xr_hi.astype(F32)).astype(BF16)
    logits = _dot(jnp.concatenate([xr_hi, xr_lo, xr_hi], axis=1), wr_ref[...]) + br_ref[...]
    lane = lax.broadcasted_iota(jnp.int32, (tm, LANES), 1)
    ninf = jnp.float32(-jnp.inf)
    gmask = lane < N_GROUPS
    gl = jnp.where(gmask, logits, ninf)
    gmax = jnp.max(gl, axis=-1, keepdims=True)
    grp = jnp.min(jnp.where(gl == gmax, lane, LANES), axis=-1, keepdims=True)
    p_grp = 1.0 / jnp.sum(jnp.where(gmask, jnp.exp(logits - gmax), 0.0), axis=-1, keepdims=True)
    lo = N_GROUPS + EXPERTS_PER_GROUP * grp
    emask = jnp.logical_and(lane >= lo, lane < lo + EXPERTS_PER_GROUP)
    el = jnp.where(emask, logits, ninf)
    v0 = jnp.max(el, axis=-1, keepdims=True)
    i0 = jnp.min(jnp.where(el == v0, lane, LANES), axis=-1, keepdims=True)
    el2 = jnp.where(lane == i0, ninf, el)
    v1 = jnp.max(el2, axis=-1, keepdims=True)
    i1 = jnp.min(jnp.where(el2 == v1, lane, LANES), axis=-1, keepdims=True)
    t = jnp.exp(v1 - v0)
    gate0 = p_grp / (1.0 + t)
    gate1 = p_grp * t / (1.0 + t)

    hit0 = lane == i0
    hit1 = lane == i1
    cmat = jnp.logical_or(hit0, hit1).astype(BF16)
    tr = lax.broadcasted_iota(jnp.int32, (tm, tm), 0)
    tc = lax.broadcasted_iota(jnp.int32, (tm, tm), 1)
    tril = (tc < tr).astype(BF16)
    prefix = _dot(tril, cmat) + carry_ref[...]
    r0 = jnp.sum(jnp.where(hit0, prefix, 0.0), axis=-1, keepdims=True)
    r1 = jnp.sum(jnp.where(hit1, prefix, 0.0), axis=-1, keepdims=True)
    counted = (i > 0).astype(F32)
    carry_ref[...] = carry_ref[...] + counted * jnp.sum(cmat.astype(F32), axis=0, keepdims=True)
    cnt_ref[...] = carry_ref[...]

    fields = [(i0 - N_GROUPS).astype(F32), (i1 - N_GROUPS).astype(F32), r0, r1, gate0, gate1]
    slab = jnp.zeros((tm, LANES), F32)
    for n, f in enumerate(fields):
        slab = jnp.where(lane == n, f, slab)
    slab_ref[...] = slab


def _out_router(oa, ob, lse, oc, x, woa, wob, woc, g, b, wr, br, layer):
    T = x.shape[0]
    tm = TM_OUT
    n = T // tm
    full = lambda a: _layer_block(a, layer)
    row = lambda w: pl.BlockSpec((tm, w), lambda i: (jnp.minimum(i, n - 1), 0))
    lag = pl.BlockSpec((tm, LANES), lambda i: (jnp.maximum(i - 1, 0), 0))
    return pl.pallas_call(
        _out_router_kernel,
        grid=(n + 1,),
        in_specs=[row(LANES)] * 14 + [pl.BlockSpec((C_HEADS * C_V, tm), lambda i: (0, jnp.minimum(i, n - 1))),
                                      row(D_MODEL),
                                      full(woa), full(wob), full(woc), full(g), full(b), full(wr), full(br)],
        out_specs=[row(D_MODEL), lag, pl.BlockSpec((1, LANES), lambda i: (0, 0))],
        out_shape=[jax.ShapeDtypeStruct((T, D_MODEL), F32), jax.ShapeDtypeStruct((T, LANES), F32),
                   jax.ShapeDtypeStruct((1, LANES), F32)],
        scratch_shapes=[pltpu.VMEM((1, LANES), F32), pltpu.VMEM((tm, D_MODEL), F32)],
        compiler_params=_cparams("arbitrary"),
        name="out_router",
    )(*oa, *ob, *lse, oc, x, woa, wob, woc, g, b, wr, br)


SUBLANES = 8
PAD_BITS = tuple(1 << k for k in reversed(range(SUBLANES.bit_length() - 1, ROW_BLK.bit_length() - 1)))


def _dispatch_kernel(d0_ref, d1_ref, pad_at_ref, pad_len_ref, nu_ref, x_ref, xs_ref, zero_ref, sem, zsem):
    i = pl.program_id(0)
    tm = x_ref.shape[0] * SUBLANES
    base = i * tm

    @pl.when(i == 0)
    def _():
        zero_ref[...] = jnp.zeros(zero_ref.shape, F32)

        def pad_copies(e):
            n = pad_len_ref[e]
            at = pad_at_ref[e]
            head = n & (SUBLANES - 1)
            out = [(j < head, pltpu.make_async_copy(zero_ref.at[pl.ds(0, 1), :], xs_ref.at[pl.ds(at + j, 1), :], zsem))
                   for j in range(SUBLANES - 1)]
            body = n - head
            for bit in PAD_BITS:
                start = pl.multiple_of(at + head + (body & ~(2 * bit - 1)), SUBLANES)
                out.append(((body & bit) != 0,
                            pltpu.make_async_copy(zero_ref.at[pl.ds(0, bit), :], xs_ref.at[pl.ds(start, bit), :], zsem)))
            return out

        def spare_copies():
            n_blocks = xs_ref.shape[0] // ROW_BLK
            first = d0_ref.shape[0] * TOP_K // ROW_BLK
            return [(b >= nu_ref[0], pltpu.make_async_copy(zero_ref, xs_ref.at[pl.ds(b * ROW_BLK, ROW_BLK), :], zsem))
                    for b in range(first, n_blocks)]

        for e in range(N_EXPERTS):
            for on, cp in pad_copies(e):
                pl.when(on)(cp.start)
        for on, cp in spare_copies():
            pl.when(on)(cp.start)
        for e in range(N_EXPERTS):
            for on, cp in pad_copies(e):
                pl.when(on)(cp.wait)
        for on, cp in spare_copies():
            pl.when(on)(cp.wait)

    def start(g, carry):
        for u in range(SUBLANES):
            t = g * SUBLANES + u
            for d_ref in (d0_ref, d1_ref):
                pltpu.make_async_copy(x_ref.at[g, pl.ds(u, 1), :], xs_ref.at[pl.ds(d_ref[base + t], 1), :], sem).start()
        return carry

    lax.fori_loop(0, tm // SUBLANES, start, 0)
    for _ in range(2):
        pltpu.make_async_copy(xs_ref.at[pl.ds(0, tm), :], zero_ref, sem).wait()


def _dispatch(d0, d1, pad_at, pad_len, n_used, x1, n_rows):
    T = x1.shape[0]
    tm = ROW_BLK
    grid_spec = pltpu.PrefetchScalarGridSpec(
        num_scalar_prefetch=5,
        grid=(T // tm,),
        in_specs=[pl.BlockSpec((tm // SUBLANES, SUBLANES, D_MODEL), lambda i, *_: (i, 0, 0))],
        out_specs=pl.BlockSpec(memory_space=pl.ANY),
        scratch_shapes=[pltpu.VMEM((ROW_BLK, D_MODEL), F32), pltpu.SemaphoreType.DMA, pltpu.SemaphoreType.DMA],
    )
    return pl.pallas_call(
        _dispatch_kernel,
        grid_spec=grid_spec,
        out_shape=jax.ShapeDtypeStruct((n_rows, D_MODEL), F32),
        compiler_params=_cparams("arbitrary"),
        name="moe_dispatch",
    )(d0, d1, pad_at, pad_len, n_used, x1.reshape(T // SUBLANES, SUBLANES, D_MODEL))


FFN_BLOCKS = 1


def _ffn_kernel(be_ref, nu_ref, xs_ref, *refs):
    del be_ref
    w_refs, ys_ref = refs[:-1], refs[-1]
    i = pl.program_id(0)
    for k in range(FFN_BLOCKS):
        wg_ref, wu_ref, wd_ref = w_refs[3 * k:3 * k + 3]
        rows = pl.ds(k * ROW_BLK, ROW_BLK)
        used = FFN_BLOCKS * i + k < nu_ref[0]

        @pl.when(used)
        def _():
            xb = xs_ref[rows, :].astype(BF16)
            a = _dot(xb, wg_ref[...].astype(BF16))
            u = _dot(xb, wu_ref[...].astype(BF16))
            hmid = (a * jax.nn.sigmoid(a) * u).astype(BF16)
            ys_ref[rows, :] = _dot(hmid, wd_ref[...].astype(BF16))

        @pl.when(jnp.logical_not(used))
        def _():
            ys_ref[rows, :] = jnp.zeros((ROW_BLK, D_MODEL), F32)


def _ffn(block_expert, n_used, xs, wg, wu, wd, layer):
    n_rows = xs.shape[0]
    step_rows = FFN_BLOCKS * ROW_BLK
    assert n_rows % step_rows == 0
    w_spec = lambda shape, k: pl.BlockSpec((None, None) + shape,
                                           lambda i, be, nu: (layer, be[FFN_BLOCKS * i + k], 0, 0))
    w_specs = [w_spec(s, k) for k in range(FFN_BLOCKS)
               for s in ((D_MODEL, D_EXPERT), (D_MODEL, D_EXPERT), (D_EXPERT, D_MODEL))]
    grid_spec = pltpu.PrefetchScalarGridSpec(
        num_scalar_prefetch=2,
        grid=(n_rows // step_rows,),
        in_specs=[pl.BlockSpec((step_rows, D_MODEL),
                               lambda i, be, nu: (jnp.minimum(i, (nu[0] - 1) // FFN_BLOCKS), 0))] + w_specs,
        out_specs=pl.BlockSpec((step_rows, D_MODEL), lambda i, be, nu: (i, 0)),
    )
    return pl.pallas_call(
        _ffn_kernel,
        grid_spec=grid_spec,
        out_shape=jax.ShapeDtypeStruct((n_rows, D_MODEL), F32),
        compiler_params=_cparams("arbitrary"),
        name="moe_ffn",
    )(block_expert, n_used, xs, *([wg, wu, wd] * FFN_BLOCKS))


def _combine_kernel(d0_ref, d1_ref, ys_ref, x1_ref, slab_ref, g_ref, b_ref, o_ref, buf_ref, sem):
    i = pl.program_id(0)
    n = pl.num_programs(0)
    tm = x1_ref.shape[0]

    def gather(tile):
        slot = tile % 2
        base = tile * tm

        def start(g, carry):
            for u in range(SUBLANES):
                t = g * SUBLANES + u
                for k, d_ref in enumerate((d0_ref, d1_ref)):
                    pltpu.make_async_copy(ys_ref.at[pl.ds(d_ref[base + t], 1), :],
                                          buf_ref.at[slot, k, g, pl.ds(u, 1), :], sem.at[slot]).start()
            return carry

        lax.fori_loop(0, tm // SUBLANES, start, 0)

    @pl.when(i == 0)
    def _():
        gather(0)

    @pl.when(i + 1 < n)
    def _():
        gather(i + 1)

    slot = i % 2
    for k in range(2):
        pltpu.make_async_copy(ys_ref.at[pl.ds(0, tm), :], o_ref, sem.at[slot]).wait()
    slab = slab_ref[...]
    rows = lambda k: buf_ref[slot, k].reshape(tm, D_MODEL)
    y = slab[:, 4:5] * rows(0) + slab[:, 5:6] * rows(1)
    o_ref[...] = _layer_norm(DEEPNORM_ALPHA * x1_ref[...] + y, g_ref[...], b_ref[...])


def _combine(d0, d1, ys, x1, slab, g, b, layer):
    T = x1.shape[0]
    tm = ROW_BLK
    grid_spec = pltpu.PrefetchScalarGridSpec(
        num_scalar_prefetch=2,
        grid=(T // tm,),
        in_specs=[pl.BlockSpec(memory_space=pl.ANY),
                  pl.BlockSpec((tm, D_MODEL), lambda i, a, c: (i, 0)),
                  pl.BlockSpec((tm, LANES), lambda i, a, c: (i, 0)),
                  _layer_block(g, layer), _layer_block(b, layer)],
        out_specs=pl.BlockSpec((tm, D_MODEL), lambda i, a, c: (i, 0)),
        scratch_shapes=[pltpu.VMEM((2, 2, tm // SUBLANES, SUBLANES, D_MODEL), F32), pltpu.SemaphoreType.DMA((2,))],
    )
    return pl.pallas_call(
        _combine_kernel,
        grid_spec=grid_spec,
        out_shape=jax.ShapeDtypeStruct((T, D_MODEL), F32),
        compiler_params=_cparams("arbitrary"),
        name="moe_combine",
    )(d0, d1, ys, x1, slab, g, b)


def _t5_bucket(dist):
    max_exact = N_BUCKETS // 2
    large = max_exact + (jnp.log(jnp.maximum(dist, 1).astype(F32) / max_exact)
                         / math.log(MAX_DISTANCE / max_exact) * (N_BUCKETS - max_exact)).astype(jnp.int32)
    return jnp.where(dist < max_exact, dist, jnp.minimum(large, N_BUCKETS - 1))


def _bias_table(table_cols, dil, max_dist):
    dist = BLK + jnp.arange(BLK)[:, None] - jnp.arange(2 * BLK)[None, :]
    bucket = _t5_bucket(jnp.maximum(dist * dil, 0))
    hit = bucket[None, :, :, None] == jnp.arange(N_BUCKETS)[None, None, None, :]
    bias = jnp.sum(jnp.where(hit, table_cols.T.astype(F32)[:, None, None, :], 0.0), axis=-1)
    mask = (dist >= 0) & (dist <= max_dist)
    return jnp.where(mask[None], bias, NEG_INF)


def _rope_tables(T):
    inv_freq = ROPE_THETA ** (-jnp.arange(0, C_ROPE, 2, dtype=F32) / C_ROPE)
    ang = jnp.arange(T, dtype=F32)[:, None] * inv_freq[None, :]
    cos, sin = jnp.cos(ang), jnp.sin(ang)
    ones = jnp.ones((T, C_NOPE), F32)
    zeros_n = jnp.zeros((T, C_NOPE), F32)
    zeros_p = jnp.zeros((T, C_PAD - C_NOPE - C_ROPE), F32)
    cs = jnp.concatenate([ones, cos, cos, zeros_p], axis=1)
    sn = jnp.concatenate([zeros_n, -sin, sin, zeros_p], axis=1)
    return cs, sn


def _swap_halves(w):
    half = w.shape[-1] // 2
    return jnp.concatenate([w[..., half:], w[..., :half]], axis=-1)


def _pad_cols(w, width):
    return jnp.pad(w, ((0, 0), (0, width - w.shape[1])))


def _prep_layer_weights(w_in, w_out, w_uq, w_ukv, w_rg, b_rg, w_re, b_re):
    head_cols = lambda w, h: w[:, h * HEAD_DIM:(h + 1) * HEAD_DIM]
    waq = jnp.concatenate([head_cols(w_in, h) for h in A_HEAD_ORDER], axis=1).astype(BF16)
    wab = w_in[:, A_Q_HEADS * HEAD_DIM:AB_COLS].astype(BF16)
    c0 = AB_COLS
    w_cq = w_in[:, c0:c0 + Q_LORA]
    w_ckv = w_in[:, c0 + Q_LORA:c0 + Q_LORA + KV_LORA]
    w_kr = w_in[:, c0 + Q_LORA + KV_LORA:]
    lead = jnp.zeros((D_MODEL, C_NOPE), F32)
    wc = jnp.concatenate([w_cq, w_ckv,
                          _pad_cols(jnp.concatenate([lead, w_kr], 1), C_PAD),
                          _pad_cols(jnp.concatenate([lead, _swap_halves(w_kr)], 1), C_PAD)], axis=1).astype(BF16)
    uq = w_uq.reshape(Q_LORA, C_HEADS, C_NOPE + C_ROPE)
    wq = jnp.pad(uq, ((0, 0), (0, 0), (0, C_PAD - C_NOPE - C_ROPE))).reshape(Q_LORA, C_HEADS * C_PAD).astype(BF16)
    uqs = jnp.concatenate([jnp.zeros((Q_LORA, C_HEADS, C_NOPE), F32), _swap_halves(uq[..., C_NOPE:])], axis=-1)
    wqs = jnp.pad(uqs, ((0, 0), (0, 0), (0, C_PAD - C_NOPE - C_ROPE))).reshape(Q_LORA, C_HEADS * C_PAD).astype(BF16)
    ukv = w_ukv.reshape(KV_LORA, C_HEADS, C_NOPE + C_V)
    wk = jnp.pad(ukv[..., :C_NOPE], ((0, 0), (0, 0), (0, C_PAD - C_NOPE))).reshape(KV_LORA, C_HEADS * C_PAD)
    wv = jnp.pad(ukv[..., C_NOPE:], ((0, 0), (0, 0), (0, C_PAD - C_V))).reshape(KV_LORA, C_HEADS * C_PAD)
    woa = jnp.concatenate([w_out[h * HEAD_DIM:(h + 1) * HEAD_DIM] for h in A_HEAD_ORDER],
                          axis=0).reshape(2, LANES, D_MODEL).astype(BF16)
    wob = w_out[256:512].reshape(2, LANES, D_MODEL).astype(BF16)
    woc = w_out[512:].astype(BF16)
    wr = _pad_cols(jnp.concatenate([w_rg, w_re.transpose(1, 0, 2).reshape(D_MODEL, N_EXPERTS)], axis=1), LANES)
    wr_hi = wr.astype(BF16)
    wr_lo = (wr - wr_hi.astype(F32)).astype(BF16)
    wr = jnp.concatenate([wr_hi, wr_hi, wr_lo], axis=0)
    br = _pad_cols(jnp.concatenate([b_rg, b_re.reshape(N_EXPERTS)])[None, :], LANES)
    return dict(waq=waq, wab=wab, wc=wc, wq=wq, wqs=wqs, wk=wk.astype(BF16), wv=wv.astype(BF16),
                woa=woa, wob=wob, woc=woc, wr=wr, br=br)


def kernel(x, w_in, w_out, sinks, rel_bias, mla_q_norm, mla_kv_norm, w_uq, w_ukv, ln1_g, ln1_b,
           w_route_group, b_route_group, w_route_expert, b_route_expert,
           w_expert_gate, w_expert_up, w_expert_down, ln2_g, ln2_b):
    Bsz, T, D = x.shape
    assert Bsz == 1 and D == D_MODEL and T % CHUNK == 0
    xt = x.reshape(T, D)
    cs, sn = _rope_tables(T)
    bias_a = _bias_table(rel_bias[:, :A_Q_HEADS], 1, SWA_WINDOW - 1)
    bias_b = [_bias_table(rel_bias[:, A_Q_HEADS + g * B_HEADS_PER_GROUP:A_Q_HEADS + (g + 1) * B_HEADS_PER_GROUP],
                          dil, window // dil) for g, (window, dil) in enumerate(DILATED_PAIRS)]
    n_assign = T * TOP_K
    n_blocks = n_assign // ROW_BLK + N_EXPERTS
    n_rows = n_blocks * ROW_BLK
    no_sink = jnp.zeros((B_HEADS_PER_GROUP,), F32)

    w = jax.vmap(_prep_layer_weights)(w_in, w_out, w_uq, w_ukv, w_route_group, b_route_group,
                                      w_route_expert, b_route_expert)
    per_layer_row = lambda a: a[:, None, :]
    gq, gkv = per_layer_row(mla_q_norm), per_layer_row(mla_kv_norm)
    g1, b1, g2, b2 = (per_layer_row(a) for a in (ln1_g, ln1_b, ln2_g, ln2_b))

    for layer in range(w_in.shape[0]):
        hab, qt, k, vt = _in_proj(xt, w["waq"], w["wab"], w["wc"], w["wq"], w["wqs"], w["wk"], w["wv"], gq, gkv, cs, sn,
                                  layer)
        oa, _ = _banded(hab, sinks[layer], bias_a, q_col=0, k_col=2, v_col=3, dil=1,
                        n_heads=A_Q_HEADS, group=A_Q_HEADS // A_KV_HEADS, has_sink=True, want_lse=False)
        ob, lse = [], []
        for g, (_, dil) in enumerate(DILATED_PAIRS):
            o_g, lse_g = _banded(hab, no_sink, bias_b[g], q_col=4 + 2 * g, k_col=10 + 2 * g, v_col=16 + 2 * g,
                                 dil=dil, n_heads=B_HEADS_PER_GROUP, group=1, has_sink=False, want_lse=True)
            ob += o_g
            lse += lse_g
        oc = _flash(qt, k, vt)
        x1, slab, cnt = _out_router(oa, ob, lse, oc, xt, w["woa"], w["wob"], w["woc"], g1, b1, w["wr"], w["br"],
                                    layer)

        counts = cnt[0, N_GROUPS:N_GROUPS + N_EXPERTS].astype(jnp.int32)
        padded = ((counts + ROW_BLK - 1) // ROW_BLK) * ROW_BLK
        pad_end = jnp.cumsum(padded)
        pad_start = pad_end - padded
        eids = jnp.arange(N_EXPERTS, dtype=F32)
        onehot = (slab[:, 0:TOP_K, None] == eids[None, None, :]).astype(F32)
        seg = jnp.einsum('tke,e->tk', onehot, pad_start.astype(F32), precision=lax.Precision.HIGHEST)
        d0 = (seg[:, 0] + slab[:, 2]).astype(jnp.int32)
        d1 = (seg[:, 1] + slab[:, 3]).astype(jnp.int32)
        block_start = jnp.arange(n_blocks, dtype=jnp.int32) * ROW_BLK
        block_expert = jnp.minimum(jnp.sum((pad_end[None, :] <= block_start[:, None]).astype(jnp.int32), axis=1),
                                   N_EXPERTS - 1)

        n_used = (pad_end[-1:] // ROW_BLK).astype(jnp.int32)
        xs = _dispatch(d0, d1, pad_start + counts, padded - counts, n_used, x1, n_rows)
        ys = _ffn(block_expert, n_used, xs, w_expert_gate, w_expert_up, w_expert_down, layer)
        xt = _combine(d0, d1, ys, x1, slab, g2, b2, layer)
    return xt.reshape(Bsz, T, D)
```

```python
import functools
import math

import jax
import jax.numpy as jnp
from jax import lax
from jax.experimental import pallas as pl
from jax.experimental.pallas import tpu as pltpu

F32 = jnp.float32
BF16 = jnp.bfloat16

D_MODEL = 1024
DEPTH = 4
HEAD_DIM = 64
BLK = 128
NEG_INF = -1e30

SWA_WINDOW = 128
A_Q_HEADS = 4
A_KV_HEADS = 2
A_COLS = (A_Q_HEADS + 2 * A_KV_HEADS) * HEAD_DIM
A_HEAD_ORDER = (0, 2, 1, 3)

DILATED_PAIRS = ((128, 1), (512, 4), (2048, 16))
B_HEADS_PER_GROUP = 4
B_HEADS = len(DILATED_PAIRS) * B_HEADS_PER_GROUP
B_COLS = 3 * B_HEADS * HEAD_DIM
AB_COLS = A_COLS + B_COLS

C_HEADS = 8
C_NOPE = 64
C_ROPE = 32
C_V = 64
Q_LORA = 256
KV_LORA = 128
ROPE_THETA = 10000.0
C_PAD = 128
C_VROWS = 80

N_BUCKETS = 32
MAX_DISTANCE = 2048

N_GROUPS = 4
EXPERTS_PER_GROUP = 8
N_EXPERTS = N_GROUPS * EXPERTS_PER_GROUP
TOP_K = 2
D_EXPERT = 256
ROW_BLK = 512

DEEPNORM_ALPHA = (2 * DEPTH) ** 0.25

VMEM_LIMIT = 56 * 1024 * 1024

CHUNK = 2048
TM_IN = 512
TM_OUT = 512
TQ = 1024
TK = 512
N_DIAG = TQ // TK
FLASH_UNROLL = 4
LANES = 128


def _dot(a, b):
    return jnp.dot(a, b, preferred_element_type=F32)


def _dot_nt(a, b):
    return lax.dot_general(a, b, (((1,), (1,)), ((), ())), preferred_element_type=F32)


def _bdot(a, b):
    return lax.dot_general(a, b, (((2,), (1,)), ((0,), (0,))), preferred_element_type=F32)


def _bdot_nt(a, b):
    return lax.dot_general(a, b, (((2,), (2,)), ((0,), (0,))), preferred_element_type=F32)


def _cparams(*sem):
    return pltpu.CompilerParams(dimension_semantics=sem, vmem_limit_bytes=VMEM_LIMIT)


def _in_proj_kernel(x_ref, waq_ref, wab_ref, wc_ref, wq_ref, wqs_ref, wk_ref, wv_ref, gq_ref, gkv_ref, cs_ref, sn_ref,
                    hab_ref, qt_ref, k_ref, vt_ref):
    xb = x_ref[...].astype(BF16)
    n_aq = waq_ref.shape[1]
    hab_ref[:, :n_aq] = _dot(xb, waq_ref[...])
    hab_ref[:, n_aq:] = _dot(xb, wab_ref[...])
    hc = _dot(xb, wc_ref[...])
    cq = hc[:, :Q_LORA]
    ckv = hc[:, Q_LORA:Q_LORA + KV_LORA]
    kr = hc[:, Q_LORA + KV_LORA:Q_LORA + KV_LORA + C_PAD]
    krs = hc[:, Q_LORA + KV_LORA + C_PAD:]
    cqn = (cq * lax.rsqrt(jnp.mean(cq * cq, -1, keepdims=True) + 1e-6) * gq_ref[...]).astype(BF16)
    ckvn = (ckv * lax.rsqrt(jnp.mean(ckv * ckv, -1, keepdims=True) + 1e-6) * gkv_ref[...]).astype(BF16)
    cs = cs_ref[...]
    sn = sn_ref[...]
    q = _dot(cqn, wq_ref[...])
    qs = _dot(cqn, wqs_ref[...])
    k = _dot(ckvn, wk_ref[...])
    v = _dot(ckvn, wv_ref[...])
    krot = kr * cs + krs * sn
    scale = (C_NOPE + C_ROPE) ** -0.5 * math.log2(math.e)
    for h in range(C_HEADS):
        sl = slice(h * C_PAD, (h + 1) * C_PAD)
        qh = (q[:, sl] * cs + qs[:, sl] * sn) * scale
        qt_ref[sl, :] = qh.T.astype(BF16)
        k_ref[h] = (k[:, sl] + krot).astype(BF16)
    one_row = (lax.broadcasted_iota(jnp.int32, (1, C_PAD), 1) == C_V).astype(F32)
    for h in range(C_HEADS):
        vh = v[:, h * C_PAD:(h + 1) * C_PAD] + one_row
        vt_ref[h * C_VROWS:(h + 1) * C_VROWS, :] = vh.T[:C_VROWS, :].astype(BF16)


def _layer_block(a, layer):
    return pl.BlockSpec((None,) + a.shape[1:], lambda i, *_: (layer,) + (0,) * (a.ndim - 1))


def _in_proj(x, waq, wab, wc, wq, wqs, wk, wv, gq, gkv, cs, sn, layer):
    T = x.shape[0]
    tm = TM_IN
    full = lambda a: _layer_block(a, layer)
    row = lambda w: pl.BlockSpec((tm, w), lambda i: (i, 0))
    colblk = lambda r: pl.BlockSpec((r, tm), lambda i: (0, i))
    return pl.pallas_call(
        _in_proj_kernel,
        grid=(T // tm,),
        in_specs=[row(D_MODEL), full(waq), full(wab), full(wc), full(wq), full(wqs), full(wk), full(wv), full(gq),
                  full(gkv),
                  row(C_PAD), row(C_PAD)],
        out_specs=[row(AB_COLS), colblk(C_HEADS * C_PAD), pl.BlockSpec((C_HEADS, tm, C_PAD), lambda i: (0, i, 0)),
                   colblk(C_HEADS * C_VROWS)],
        out_shape=[jax.ShapeDtypeStruct((T, AB_COLS), F32), jax.ShapeDtypeStruct((C_HEADS * C_PAD, T), BF16),
                   jax.ShapeDtypeStruct((C_HEADS, T, C_PAD), BF16),
                   jax.ShapeDtypeStruct((C_HEADS * C_VROWS, T), BF16)],
        compiler_params=_cparams("arbitrary"),
        name="in_proj",
    )(x, waq, wab, wc, wq, wqs, wk, wv, gq, gkv, cs, sn)


def _banded_kernel(sink_ref, *refs, dil, n_heads, group, has_sink, want_lse):
    nq = n_heads // 2
    nkv = n_heads // group // 2
    q_refs, refs = refs[:nq], refs[nq:]
    k_refs, v_refs, kp_refs, vp_refs = (refs[i * nkv:(i + 1) * nkv] for i in range(4))
    bias_ref = refs[4 * nkv]
    o_refs = refs[4 * nkv + 1:4 * nkv + 1 + nq]
    lse_refs = refs[4 * nkv + 1 + nq:]
    c = pl.program_id(0)
    span = BLK * dil
    nb = CHUNK // span
    n_combo = nb * dil
    starts = [b * span + r for b in range(nb) for r in range(dil)]
    scale = HEAD_DIM ** -0.5

    def rows(start):
        return pl.ds(start, BLK) if dil == 1 else pl.ds(start, BLK, stride=dil)

    def gather(ref):
        return jnp.stack([ref[rows(s), :] for s in starts], axis=0)

    def with_prev(ref, prev_ref):
        cur = gather(ref)
        first = jnp.stack([prev_ref[rows(r), :] for r in range(dil)], axis=0)
        prev = jnp.concatenate([first, cur[:n_combo - dil]], axis=0) if dil < n_combo else first
        return jnp.concatenate([prev, cur], axis=1).astype(BF16)

    q = [gather(ref).astype(BF16) for ref in q_refs]
    kcat = [with_prev(k_refs[j], kp_refs[j]) for j in range(nkv)]
    vcat = [with_prev(v_refs[j], vp_refs[j]) for j in range(nkv)]

    lane = lax.broadcasted_iota(jnp.int32, (1, 1, LANES), 2)
    low = lane < HEAD_DIM
    combo = lax.broadcasted_iota(jnp.int32, (n_combo, 1, 2 * BLK), 0)
    col = lax.broadcasted_iota(jnp.int32, (n_combo, 1, 2 * BLK), 2)
    no_prev = jnp.logical_and(c == 0, jnp.logical_and(combo < dil, col < BLK))

    for j in range(nq):
        o_par, lse_par = [], []
        for par in range(2):
            h = 2 * j + par if group == 1 else group * par + j
            kvb = j if group == 1 else 0
            keep = low if par == 0 else jnp.logical_not(low)
            kk = jnp.where(keep, kcat[kvb], jnp.zeros((), BF16))
            s = _bdot_nt(q[j], kk)
            s = s * scale + bias_ref[h][None]
            s = jnp.where(no_prev, NEG_INF, s)
            m = jnp.max(s, axis=-1, keepdims=True)
            if has_sink:
                snk = sink_ref[h]
                m = jnp.maximum(m, snk)
            e = jnp.exp(s - m)
            l = jnp.sum(e, axis=-1, keepdims=True)
            if has_sink:
                l = l + jnp.exp(snk - m)
            p = (e * (1.0 / l)).astype(BF16)
            o_par.append(_bdot(p, vcat[kvb]))
            if want_lse:
                lse_par.append(m + jnp.log(l))
        o_pair = jnp.where(low, o_par[0], o_par[1])
        if want_lse:
            lse_pair = jnp.where(low, lse_par[0], lse_par[1])
        for ci, s0 in enumerate(starts):
            o_refs[j][rows(s0), :] = o_pair[ci]
            if want_lse:
                lse_refs[j][rows(s0), :] = lse_pair[ci]


def _banded(hab, sink, bias, *, q_col, k_col, v_col, dil, n_heads, group, has_sink, want_lse):
    T = hab.shape[0]
    span = BLK * dil
    nb = CHUNK // span
    nq = n_heads // 2
    nkv = n_heads // group // 2
    kernel = functools.partial(_banded_kernel, dil=dil, n_heads=n_heads, group=group,
                               has_sink=has_sink, want_lse=want_lse)
    prev = lambda cb: pl.BlockSpec((span, LANES), lambda c, s: (jnp.maximum(c * nb - 1, 0), cb))
    cur = lambda cb: pl.BlockSpec((CHUNK, LANES), lambda c, s: (c, cb))
    in_specs = ([cur(q_col + j) for j in range(nq)] + [cur(k_col + j) for j in range(nkv)]
                + [cur(v_col + j) for j in range(nkv)] + [prev(k_col + j) for j in range(nkv)]
                + [prev(v_col + j) for j in range(nkv)] + [pl.BlockSpec(bias.shape, lambda c, s: (0, 0, 0))])
    n_out = nq * (2 if want_lse else 1)
    grid_spec = pltpu.PrefetchScalarGridSpec(
        num_scalar_prefetch=1,
        grid=(T // CHUNK,),
        in_specs=in_specs,
        out_specs=[pl.BlockSpec((CHUNK, LANES), lambda c, s: (c, 0))] * n_out,
    )
    outs = pl.pallas_call(
        kernel,
        grid_spec=grid_spec,
        out_shape=[jax.ShapeDtypeStruct((T, LANES), F32)] * n_out,
        compiler_params=_cparams("arbitrary"),
        name=f"banded_d{dil}" + ("_sink" if has_sink else ""),
    )(sink, *([hab] * (nq + 4 * nkv)), bias)
    return outs[:nq], outs[nq:]


def _flash_kernel(qt_ref, k_ref, vt_ref, ot_ref, m_ref, acc_ref, s_ref):
    qi = pl.program_id(1)
    qt = qt_ref[...]
    m_ref[...] = jnp.full(m_ref.shape, NEG_INF, F32)
    acc_ref[...] = jnp.zeros(acc_ref.shape, F32)

    def scores(j, lo=0):
        off = pl.multiple_of(j * TK, TK)
        return _dot(k_ref[pl.ds(off, TK), :], qt[:, lo:])

    def accumulate(j, s, lo=0):
        off = pl.multiple_of(j * TK, TK)
        m_old = m_ref[:, lo:]
        m_new = jnp.maximum(m_old, jnp.max(s, axis=0, keepdims=True))
        alpha = jnp.exp2(m_old - m_new)
        p = jnp.exp2(s - m_new).astype(BF16)
        acc_ref[:, lo:] = alpha * acc_ref[:, lo:] + _dot(vt_ref[:, pl.ds(off, TK)], p)
        m_ref[:, lo:] = m_new

    def diagonal(s):
        key = lax.broadcasted_iota(jnp.int32, s.shape, 0)
        qry = lax.broadcasted_iota(jnp.int32, s.shape, 1)
        return jnp.where(key <= qry, s, NEG_INF)

    n_full = qi * N_DIAG
    s_ref[0] = scores(0)
    n_main = n_full // FLASH_UNROLL

    def main_body(i, carry):
        j = FLASH_UNROLL * i
        for u in range(FLASH_UNROLL):
            s_ref[(u + 1) % 2] = scores(j + u + 1)
            accumulate(j + u, s_ref[u % 2])
        return carry

    lax.fori_loop(0, n_main, main_body, 0)

    base = FLASH_UNROLL * n_main
    for rem in range(0, FLASH_UNROLL, math.gcd(FLASH_UNROLL, N_DIAG)):
        @pl.when(n_full - base == rem)
        def _():
            n_tail = rem + N_DIAG
            first_lane = lambda u: max(u - rem, 0) * TK
            for u in range(n_tail):
                if u + 1 < n_tail:
                    lo = first_lane(u + 1)
                    s_ref[(u + 1) % 2, :, lo:] = scores(base + u + 1, lo)
                lo = first_lane(u)
                s = s_ref[u % 2, :, lo:]
                accumulate(base + u, s if u < rem else diagonal(s), lo)

    ot_ref[...] = (acc_ref[:C_V, :] / acc_ref[C_V:C_V + 1, :]).astype(ot_ref.dtype)


def _flash(qt, k, vt):
    H, T, _ = k.shape
    return pl.pallas_call(
        _flash_kernel,
        grid=(H, T // TQ),
        in_specs=[pl.BlockSpec((C_PAD, TQ), lambda h, i: (h, i)),
                  pl.BlockSpec((None, T, C_PAD), lambda h, i: (h, 0, 0)),
                  pl.BlockSpec((C_VROWS, T), lambda h, i: (h, 0))],
        out_specs=pl.BlockSpec((C_V, TQ), lambda h, i: (h, i)),
        out_shape=jax.ShapeDtypeStruct((H * C_V, T), BF16),
        scratch_shapes=[pltpu.VMEM((1, TQ), F32), pltpu.VMEM((C_VROWS, TQ), F32), pltpu.VMEM((2, TK, TQ), F32)],
        compiler_params=_cparams("arbitrary", "arbitrary"),
        name="mla_flash",
    )(qt, k, vt)


def _layer_norm(z, g, b):
    mu = jnp.mean(z, -1, keepdims=True)
    zc = z - mu
    var = jnp.mean(zc * zc, -1, keepdims=True)
    return zc * lax.rsqrt(var + 1e-5) * g + b


def _out_router_kernel(*refs):
    n_ab = 2 + 6 + 6
    oa_refs, ob_refs, lse_refs = refs[0:2], refs[2:8], refs[8:14]
    (oc_ref, x_ref, woa_ref, wob_ref, woc_ref, g_ref, b_ref, wr_ref, br_ref,
     x1_ref, slab_ref, cnt_ref, carry_ref, prev_ref) = refs[n_ab:]
    i = pl.program_id(0)
    tm = x_ref.shape[0]

    @pl.when(i == 0)
    def _():
        carry_ref[...] = jnp.zeros(carry_ref.shape, F32)
        prev_ref[...] = jnp.zeros(prev_ref.shape, F32)

    xr = prev_ref[...]

    y = None
    for j in range(2):
        l0, l1, l2 = (lse_refs[2 * g + j][...] for g in range(3))
        mx = jnp.maximum(jnp.maximum(l0, l1), l2)
        e0, e1, e2 = jnp.exp(l0 - mx), jnp.exp(l1 - mx), jnp.exp(l2 - mx)
        den = e0 + e1 + e2
        ob = (e0 * ob_refs[j][...] + e1 * ob_refs[2 + j][...] + e2 * ob_refs[4 + j][...]) * (1.0 / den)
        part = _dot(oa_refs[j][...].astype(BF16), woa_ref[j]) + _dot(ob.astype(BF16), wob_ref[j])
        y = part if y is None else y + part
    y = y + _dot(oc_ref[...].astype(F32).T.astype(BF16), woc_ref[...])
    x1 = _layer_norm(DEEPNORM_ALPHA * x_ref[...] + y, g_ref[...], b_ref[...])
    x1_ref[...] = x1
    prev_ref[...] = x1

    xr_hi = xr.astype(BF16)
    xr_lo = (xr - xr_hi.astype(F32)).astype(BF16)
    logits = _dot(jnp.concatenate([xr_hi, xr_lo, xr_hi], axis=1), wr_ref[...]) + br_ref[...]
    lane = lax.broadcasted_iota(jnp.int32, (tm, LANES), 1)
    ninf = jnp.float32(-jnp.inf)
    gmask = lane < N_GROUPS
    gl = jnp.where(gmask, logits, ninf)
    gmax = jnp.max(gl, axis=-1, keepdims=True)
    grp = jnp.min(jnp.where(gl == gmax, lane, LANES), axis=-1, keepdims=True)
    p_grp = 1.0 / jnp.sum(jnp.where(gmask, jnp.exp(logits - gmax), 0.0), axis=-1, keepdims=True)
    lo = N_GROUPS + EXPERTS_PER_GROUP * grp
    emask = jnp.logical_and(lane >= lo, lane < lo + EXPERTS_PER_GROUP)
    el = jnp.where(emask, logits, ninf)
    v0 = jnp.max(el, axis=-1, keepdims=True)
    i0 = jnp.min(jnp.where(el == v0, lane, LANES), axis=-1, keepdims=True)
    el2 = jnp.where(lane == i0, ninf, el)
    v1 = jnp.max(el2, axis=-1, keepdims=True)
    i1 = jnp.min(jnp.where(el2 == v1, lane, LANES), axis=-1, keepdims=True)
    t = jnp.exp(v1 - v0)
    gate0 = p_grp / (1.0 + t)
    gate1 = p_grp * t / (1.0 + t)

    hit0 = lane == i0
    hit1 = lane == i1
    cmat = jnp.logical_or(hit0, hit1).astype(BF16)
    tr = lax.broadcasted_iota(jnp.int32, (tm, tm), 0)
    tc = lax.broadcasted_iota(jnp.int32, (tm, tm), 1)
    tril = (tc < tr).astype(BF16)
    prefix = _dot(tril, cmat) + carry_ref[...]
    r0 = jnp.sum(jnp.where(hit0, prefix, 0.0), axis=-1, keepdims=True)
    r1 = jnp.sum(jnp.where(hit1, prefix, 0.0), axis=-1, keepdims=True)
    counted = (i > 0).astype(F32)
    carry_ref[...] = carry_ref[...] + counted * jnp.sum(cmat.astype(F32), axis=0, keepdims=True)
    cnt_ref[...] = carry_ref[...]

    fields = [(i0 - N_GROUPS).astype(F32), (i1 - N_GROUPS).astype(F32), r0, r1, gate0, gate1]
    slab = jnp.zeros((tm, LANES), F32)
    for n, f in enumerate(fields):
        slab = jnp.where(lane == n, f, slab)
    slab_ref[...] = slab


def _out_router(oa, ob, lse, oc, x, woa, wob, woc, g, b, wr, br, layer):
    T = x.shape[0]
    tm = TM_OUT
    n = T // tm
    full = lambda a: _layer_block(a, layer)
    row = lambda w: pl.BlockSpec((tm, w), lambda i: (jnp.minimum(i, n - 1), 0))
    lag = pl.BlockSpec((tm, LANES), lambda i: (jnp.maximum(i - 1, 0), 0))
    return pl.pallas_call(
        _out_router_kernel,
        grid=(n + 1,),
        in_specs=[row(LANES)] * 14 + [pl.BlockSpec((C_HEADS * C_V, tm), lambda i: (0, jnp.minimum(i, n - 1))),
                                      row(D_MODEL),
                                      full(woa), full(wob), full(woc), full(g), full(b), full(wr), full(br)],
        out_specs=[row(D_MODEL), lag, pl.BlockSpec((1, LANES), lambda i: (0, 0))],
        out_shape=[jax.ShapeDtypeStruct((T, D_MODEL), F32), jax.ShapeDtypeStruct((T, LANES), F32),
                   jax.ShapeDtypeStruct((1, LANES), F32)],
        scratch_shapes=[pltpu.VMEM((1, LANES), F32), pltpu.VMEM((tm, D_MODEL), F32)],
        compiler_params=_cparams("arbitrary"),
        name="out_router",
    )(*oa, *ob, *lse, oc, x, woa, wob, woc, g, b, wr, br)


SUBLANES = 8
PAD_BITS = tuple(1 << k for k in reversed(range(SUBLANES.bit_length() - 1, ROW_BLK.bit_length() - 1)))


def _dispatch_kernel(d0_ref, d1_ref, pad_at_ref, pad_len_ref, nu_ref, x_ref, xs_ref, zero_ref, size_ref, sem, zsem):
    i = pl.program_id(0)
    tm = x_ref.shape[0] * SUBLANES
    base = i * tm

    def pad_copies(e):
        n = pad_len_ref[e]
        at = pad_at_ref[e]
        head = n & (SUBLANES - 1)
        out = [(j < head, pltpu.make_async_copy(zero_ref.at[pl.ds(0, 1), :], xs_ref.at[pl.ds(at + j, 1), :], zsem))
               for j in range(SUBLANES - 1)]
        body = n - head
        for bit in PAD_BITS:
            start = pl.multiple_of(at + head + (body & ~(2 * bit - 1)), SUBLANES)
            out.append(((body & bit) != 0,
                        pltpu.make_async_copy(zero_ref.at[pl.ds(0, bit), :], xs_ref.at[pl.ds(start, bit), :], zsem)))
        return out

    def spare_copies():
        n_blocks = xs_ref.shape[0] // ROW_BLK
        first = d0_ref.shape[0] * TOP_K // ROW_BLK
        return [(b >= nu_ref[0], pltpu.make_async_copy(zero_ref, xs_ref.at[pl.ds(b * ROW_BLK, ROW_BLK), :], zsem))
                for b in range(first, n_blocks)]

    def all_zero_copies():
        return [pair for e in range(N_EXPERTS) for pair in pad_copies(e)] + spare_copies()

    @pl.when(i == 0)
    def _():
        zero_ref[...] = jnp.zeros(zero_ref.shape, F32)
        for on, cp in all_zero_copies():
            pl.when(on)(cp.start)

    def start(g, carry):
        for u in range(SUBLANES):
            t = g * SUBLANES + u
            for d_ref in (d0_ref, d1_ref):
                pltpu.make_async_copy(x_ref.at[g, pl.ds(u, 1), :], xs_ref.at[pl.ds(d_ref[base + t], 1), :], sem).start()
        return carry

    lax.fori_loop(0, tm // SUBLANES, start, 0)
    for _ in range(2):
        pltpu.make_async_copy(xs_ref.at[pl.ds(0, tm), :], size_ref, sem).wait()

    @pl.when(i == pl.num_programs(0) - 1)
    def _():
        for on, cp in all_zero_copies():
            pl.when(on)(cp.wait)


def _dispatch(d0, d1, pad_at, pad_len, n_used, x1, n_rows):
    T = x1.shape[0]
    tm = ROW_BLK
    grid_spec = pltpu.PrefetchScalarGridSpec(
        num_scalar_prefetch=5,
        grid=(T // tm,),
        in_specs=[pl.BlockSpec((tm // SUBLANES, SUBLANES, D_MODEL), lambda i, *_: (i, 0, 0))],
        out_specs=pl.BlockSpec(memory_space=pl.ANY),
        scratch_shapes=[pltpu.VMEM((ROW_BLK, D_MODEL), F32), pltpu.VMEM((tm, D_MODEL), F32),
                        pltpu.SemaphoreType.DMA, pltpu.SemaphoreType.DMA],
    )
    return pl.pallas_call(
        _dispatch_kernel,
        grid_spec=grid_spec,
        out_shape=jax.ShapeDtypeStruct((n_rows, D_MODEL), F32),
        compiler_params=_cparams("arbitrary"),
        name="moe_dispatch",
    )(d0, d1, pad_at, pad_len, n_used, x1.reshape(T // SUBLANES, SUBLANES, D_MODEL))


FFN_BLOCKS = 1


def _ffn_kernel(be_ref, nu_ref, xs_ref, *refs):
    del be_ref
    w_refs, ys_ref = refs[:-1], refs[-1]
    i = pl.program_id(0)
    for k in range(FFN_BLOCKS):
        wg_ref, wu_ref, wd_ref = w_refs[3 * k:3 * k + 3]
        rows = pl.ds(k * ROW_BLK, ROW_BLK)
        used = FFN_BLOCKS * i + k < nu_ref[0]

        @pl.when(used)
        def _():
            xb = xs_ref[rows, :].astype(BF16)
            a = _dot(xb, wg_ref[...].astype(BF16))
            u = _dot(xb, wu_ref[...].astype(BF16))
            hmid = (a * jax.nn.sigmoid(a) * u).astype(BF16)
            ys_ref[rows, :] = _dot(hmid, wd_ref[...].astype(BF16))

        @pl.when(jnp.logical_not(used))
        def _():
            ys_ref[rows, :] = jnp.zeros((ROW_BLK, D_MODEL), F32)


def _ffn(block_expert, n_used, xs, wg, wu, wd, layer):
    n_rows = xs.shape[0]
    step_rows = FFN_BLOCKS * ROW_BLK
    assert n_rows % step_rows == 0
    w_spec = lambda shape, k: pl.BlockSpec((None, None) + shape,
                                           lambda i, be, nu: (layer, be[FFN_BLOCKS * i + k], 0, 0))
    w_specs = [w_spec(s, k) for k in range(FFN_BLOCKS)
               for s in ((D_MODEL, D_EXPERT), (D_MODEL, D_EXPERT), (D_EXPERT, D_MODEL))]
    grid_spec = pltpu.PrefetchScalarGridSpec(
        num_scalar_prefetch=2,
        grid=(n_rows // step_rows,),
        in_specs=[pl.BlockSpec((step_rows, D_MODEL),
                               lambda i, be, nu: (jnp.minimum(i, (nu[0] - 1) // FFN_BLOCKS), 0))] + w_specs,
        out_specs=pl.BlockSpec((step_rows, D_MODEL), lambda i, be, nu: (i, 0)),
    )
    return pl.pallas_call(
        _ffn_kernel,
        grid_spec=grid_spec,
        out_shape=jax.ShapeDtypeStruct((n_rows, D_MODEL), F32),
        compiler_params=_cparams("arbitrary"),
        name="moe_ffn",
    )(block_expert, n_used, xs, *([wg, wu, wd] * FFN_BLOCKS))


def _combine_kernel(d0_ref, d1_ref, ys_ref, x1_ref, slab_ref, g_ref, b_ref, o_ref, buf_ref, sem):
    i = pl.program_id(0)
    n = pl.num_programs(0)
    tm = x1_ref.shape[0]

    def gather(tile):
        slot = tile % 2
        base = tile * tm

        def start(g, carry):
            for u in range(SUBLANES):
                t = g * SUBLANES + u
                for k, d_ref in enumerate((d0_ref, d1_ref)):
                    pltpu.make_async_copy(ys_ref.at[pl.ds(d_ref[base + t], 1), :],
                                          buf_ref.at[slot, k, g, pl.ds(u, 1), :], sem.at[slot]).start()
            return carry

        lax.fori_loop(0, tm // SUBLANES, start, 0)

    @pl.when(i == 0)
    def _():
        gather(0)

    @pl.when(i + 1 < n)
    def _():
        gather(i + 1)

    slot = i % 2
    for k in range(2):
        pltpu.make_async_copy(ys_ref.at[pl.ds(0, tm), :], o_ref, sem.at[slot]).wait()
    slab = slab_ref[...]
    rows = lambda k: buf_ref[slot, k].reshape(tm, D_MODEL)
    y = slab[:, 4:5] * rows(0) + slab[:, 5:6] * rows(1)
    o_ref[...] = _layer_norm(DEEPNORM_ALPHA * x1_ref[...] + y, g_ref[...], b_ref[...])


def _combine(d0, d1, ys, x1, slab, g, b, layer):
    T = x1.shape[0]
    tm = ROW_BLK
    grid_spec = pltpu.PrefetchScalarGridSpec(
        num_scalar_prefetch=2,
        grid=(T // tm,),
        in_specs=[pl.BlockSpec(memory_space=pl.ANY),
                  pl.BlockSpec((tm, D_MODEL), lambda i, a, c: (i, 0)),
                  pl.BlockSpec((tm, LANES), lambda i, a, c: (i, 0)),
                  _layer_block(g, layer), _layer_block(b, layer)],
        out_specs=pl.BlockSpec((tm, D_MODEL), lambda i, a, c: (i, 0)),
        scratch_shapes=[pltpu.VMEM((2, 2, tm // SUBLANES, SUBLANES, D_MODEL), F32), pltpu.SemaphoreType.DMA((2,))],
    )
    return pl.pallas_call(
        _combine_kernel,
        grid_spec=grid_spec,
        out_shape=jax.ShapeDtypeStruct((T, D_MODEL), F32),
        compiler_params=_cparams("arbitrary"),
        name="moe_combine",
    )(d0, d1, ys, x1, slab, g, b)


def _t5_bucket(dist):
    max_exact = N_BUCKETS // 2
    large = max_exact + (jnp.log(jnp.maximum(dist, 1).astype(F32) / max_exact)
                         / math.log(MAX_DISTANCE / max_exact) * (N_BUCKETS - max_exact)).astype(jnp.int32)
    return jnp.where(dist < max_exact, dist, jnp.minimum(large, N_BUCKETS - 1))


def _bias_table(table_cols, dil, max_dist):
    dist = BLK + jnp.arange(BLK)[:, None] - jnp.arange(2 * BLK)[None, :]
    bucket = _t5_bucket(jnp.maximum(dist * dil, 0))
    hit = bucket[None, :, :, None] == jnp.arange(N_BUCKETS)[None, None, None, :]
    bias = jnp.sum(jnp.where(hit, table_cols.T.astype(F32)[:, None, None, :], 0.0), axis=-1)
    mask = (dist >= 0) & (dist <= max_dist)
    return jnp.where(mask[None], bias, NEG_INF)


def _rope_tables(T):
    inv_freq = ROPE_THETA ** (-jnp.arange(0, C_ROPE, 2, dtype=F32) / C_ROPE)
    ang = jnp.arange(T, dtype=F32)[:, None] * inv_freq[None, :]
    cos, sin = jnp.cos(ang), jnp.sin(ang)
    ones = jnp.ones((T, C_NOPE), F32)
    zeros_n = jnp.zeros((T, C_NOPE), F32)
    zeros_p = jnp.zeros((T, C_PAD - C_NOPE - C_ROPE), F32)
    cs = jnp.concatenate([ones, cos, cos, zeros_p], axis=1)
    sn = jnp.concatenate([zeros_n, -sin, sin, zeros_p], axis=1)
    return cs, sn


def _swap_halves(w):
    half = w.shape[-1] // 2
    return jnp.concatenate([w[..., half:], w[..., :half]], axis=-1)


def _pad_cols(w, width):
    return jnp.pad(w, ((0, 0), (0, width - w.shape[1])))


def _prep_layer_weights(w_in, w_out, w_uq, w_ukv, w_rg, b_rg, w_re, b_re):
    head_cols = lambda w, h: w[:, h * HEAD_DIM:(h + 1) * HEAD_DIM]
    waq = jnp.concatenate([head_cols(w_in, h) for h in A_HEAD_ORDER], axis=1).astype(BF16)
    wab = w_in[:, A_Q_HEADS * HEAD_DIM:AB_COLS].astype(BF16)
    c0 = AB_COLS
    w_cq = w_in[:, c0:c0 + Q_LORA]
    w_ckv = w_in[:, c0 + Q_LORA:c0 + Q_LORA + KV_LORA]
    w_kr = w_in[:, c0 + Q_LORA + KV_LORA:]
    lead = jnp.zeros((D_MODEL, C_NOPE), F32)
    wc = jnp.concatenate([w_cq, w_ckv,
                          _pad_cols(jnp.concatenate([lead, w_kr], 1), C_PAD),
                          _pad_cols(jnp.concatenate([lead, _swap_halves(w_kr)], 1), C_PAD)], axis=1).astype(BF16)
    uq = w_uq.reshape(Q_LORA, C_HEADS, C_NOPE + C_ROPE)
    wq = jnp.pad(uq, ((0, 0), (0, 0), (0, C_PAD - C_NOPE - C_ROPE))).reshape(Q_LORA, C_HEADS * C_PAD).astype(BF16)
    uqs = jnp.concatenate([jnp.zeros((Q_LORA, C_HEADS, C_NOPE), F32), _swap_halves(uq[..., C_NOPE:])], axis=-1)
    wqs = jnp.pad(uqs, ((0, 0), (0, 0), (0, C_PAD - C_NOPE - C_ROPE))).reshape(Q_LORA, C_HEADS * C_PAD).astype(BF16)
    ukv = w_ukv.reshape(KV_LORA, C_HEADS, C_NOPE + C_V)
    wk = jnp.pad(ukv[..., :C_NOPE], ((0, 0), (0, 0), (0, C_PAD - C_NOPE))).reshape(KV_LORA, C_HEADS * C_PAD)
    wv = jnp.pad(ukv[..., C_NOPE:], ((0, 0), (0, 0), (0, C_PAD - C_V))).reshape(KV_LORA, C_HEADS * C_PAD)
    woa = jnp.concatenate([w_out[h * HEAD_DIM:(h + 1) * HEAD_DIM] for h in A_HEAD_ORDER],
                          axis=0).reshape(2, LANES, D_MODEL).astype(BF16)
    wob = w_out[256:512].reshape(2, LANES, D_MODEL).astype(BF16)
    woc = w_out[512:].astype(BF16)
    wr = _pad_cols(jnp.concatenate([w_rg, w_re.transpose(1, 0, 2).reshape(D_MODEL, N_EXPERTS)], axis=1), LANES)
    wr_hi = wr.astype(BF16)
    wr_lo = (wr - wr_hi.astype(F32)).astype(BF16)
    wr = jnp.concatenate([wr_hi, wr_hi, wr_lo], axis=0)
    br = _pad_cols(jnp.concatenate([b_rg, b_re.reshape(N_EXPERTS)])[None, :], LANES)
    return dict(waq=waq, wab=wab, wc=wc, wq=wq, wqs=wqs, wk=wk.astype(BF16), wv=wv.astype(BF16),
                woa=woa, wob=wob, woc=woc, wr=wr, br=br)


def kernel(x, w_in, w_out, sinks, rel_bias, mla_q_norm, mla_kv_norm, w_uq, w_ukv, ln1_g, ln1_b,
           w_route_group, b_route_group, w_route_expert, b_route_expert,
           w_expert_gate, w_expert_up, w_expert_down, ln2_g, ln2_b):
    Bsz, T, D = x.shape
    assert Bsz == 1 and D == D_MODEL and T % CHUNK == 0
    xt = x.reshape(T, D)
    cs, sn = _rope_tables(T)
    bias_a = _bias_table(rel_bias[:, :A_Q_HEADS], 1, SWA_WINDOW - 1)
    bias_b = [_bias_table(rel_bias[:, A_Q_HEADS + g * B_HEADS_PER_GROUP:A_Q_HEADS + (g + 1) * B_HEADS_PER_GROUP],
                          dil, window // dil) for g, (window, dil) in enumerate(DILATED_PAIRS)]
    n_assign = T * TOP_K
    n_blocks = n_assign // ROW_BLK + N_EXPERTS
    n_rows = n_blocks * ROW_BLK
    no_sink = jnp.zeros((B_HEADS_PER_GROUP,), F32)

    w = jax.vmap(_prep_layer_weights)(w_in, w_out, w_uq, w_ukv, w_route_group, b_route_group,
                                      w_route_expert, b_route_expert)
    per_layer_row = lambda a: a[:, None, :]
    gq, gkv = per_layer_row(mla_q_norm), per_layer_row(mla_kv_norm)
    g1, b1, g2, b2 = (per_layer_row(a) for a in (ln1_g, ln1_b, ln2_g, ln2_b))

    for layer in range(w_in.shape[0]):
        hab, qt, k, vt = _in_proj(xt, w["waq"], w["wab"], w["wc"], w["wq"], w["wqs"], w["wk"], w["wv"], gq, gkv, cs, sn,
                                  layer)
        oa, _ = _banded(hab, sinks[layer], bias_a, q_col=0, k_col=2, v_col=3, dil=1,
                        n_heads=A_Q_HEADS, group=A_Q_HEADS // A_KV_HEADS, has_sink=True, want_lse=False)
        ob, lse = [], []
        for g, (_, dil) in enumerate(DILATED_PAIRS):
            o_g, lse_g = _banded(hab, no_sink, bias_b[g], q_col=4 + 2 * g, k_col=10 + 2 * g, v_col=16 + 2 * g,
                                 dil=dil, n_heads=B_HEADS_PER_GROUP, group=1, has_sink=False, want_lse=True)
            ob += o_g
            lse += lse_g
        oc = _flash(qt, k, vt)
        x1, slab, cnt = _out_router(oa, ob, lse, oc, xt, w["woa"], w["wob"], w["woc"], g1, b1, w["wr"], w["br"],
                                    layer)

        counts = cnt[0, N_GROUPS:N_GROUPS + N_EXPERTS].astype(jnp.int32)
        padded = ((counts + ROW_BLK - 1) // ROW_BLK) * ROW_BLK
        pad_end = jnp.cumsum(padded)
        pad_start = pad_end - padded
        eids = jnp.arange(N_EXPERTS, dtype=F32)
        onehot = (slab[:, 0:TOP_K, None] == eids[None, None, :]).astype(F32)
        seg = jnp.einsum('tke,e->tk', onehot, pad_start.astype(F32), precision=lax.Precision.HIGHEST)
        d0 = (seg[:, 0] + slab[:, 2]).astype(jnp.int32)
        d1 = (seg[:, 1] + slab[:, 3]).astype(jnp.int32)
        block_start = jnp.arange(n_blocks, dtype=jnp.int32) * ROW_BLK
        block_expert = jnp.minimum(jnp.sum((pad_end[None, :] <= block_start[:, None]).astype(jnp.int32), axis=1),
                                   N_EXPERTS - 1)

        n_used = (pad_end[-1:] // ROW_BLK).astype(jnp.int32)
        xs = _dispatch(d0, d1, pad_start + counts, padded - counts, n_used, x1, n_rows)
        ys = _ffn(block_expert, n_used, xs, w_expert_gate, w_expert_up, w_expert_down, layer)
        xt = _combine(d0, d1, ys, x1, slab, g2, b2, layer)
    return xt.reshape(Bsz, T, D)
```

```python
import functools
import math

import jax
import jax.numpy as jnp
from jax import lax
from jax.experimental import pallas as pl
from jax.experimental.pallas import tpu as pltpu

F32 = jnp.float32
BF16 = jnp.bfloat16

D_MODEL = 1024
DEPTH = 4
HEAD_DIM = 64
BLK = 128
NEG_INF = -1e30

SWA_WINDOW = 128
A_Q_HEADS = 4
A_KV_HEADS = 2
A_COLS = (A_Q_HEADS + 2 * A_KV_HEADS) * HEAD_DIM
A_HEAD_ORDER = (0, 2, 1, 3)

DILATED_PAIRS = ((128, 1), (512, 4), (2048, 16))
B_HEADS_PER_GROUP = 4
B_HEADS = len(DILATED_PAIRS) * B_HEADS_PER_GROUP
B_COLS = 3 * B_HEADS * HEAD_DIM
AB_COLS = A_COLS + B_COLS

C_HEADS = 8
C_NOPE = 64
C_ROPE = 32
C_V = 64
Q_LORA = 256
KV_LORA = 128
ROPE_THETA = 10000.0
C_PAD = 128
C_VROWS = 80

N_BUCKETS = 32
MAX_DISTANCE = 2048

N_GROUPS = 4
EXPERTS_PER_GROUP = 8
N_EXPERTS = N_GROUPS * EXPERTS_PER_GROUP
TOP_K = 2
D_EXPERT = 256
ROW_BLK = 512

DEEPNORM_ALPHA = (2 * DEPTH) ** 0.25

VMEM_LIMIT = 56 * 1024 * 1024

CHUNK = 2048
TM_IN = 512
TM_OUT = 512
TQ = 1024
TK = 512
N_DIAG = TQ // TK
FLASH_UNROLL = 4
LANES = 128


def _dot(a, b):
    return jnp.dot(a, b, preferred_element_type=F32)


def _bdot(a, b):
    return lax.dot_general(a, b, (((2,), (1,)), ((0,), (0,))), preferred_element_type=F32)


def _bdot_nt(a, b):
    return lax.dot_general(a, b, (((2,), (2,)), ((0,), (0,))), preferred_element_type=F32)


def _cparams(*sem):
    return pltpu.CompilerParams(dimension_semantics=sem, vmem_limit_bytes=VMEM_LIMIT)


def _in_proj_kernel(x_ref, waq_ref, wab_ref, wc_ref, wq_ref, wqs_ref, wk_ref, wv_ref, gq_ref, gkv_ref, cs_ref, sn_ref,
                    hab_ref, qt_ref, k_ref, vt_ref):
    xb = x_ref[...].astype(BF16)
    n_aq = waq_ref.shape[1]
    hab_ref[:, :n_aq] = _dot(xb, waq_ref[...])
    hab_ref[:, n_aq:] = _dot(xb, wab_ref[...])
    hc = _dot(xb, wc_ref[...])
    cq = hc[:, :Q_LORA]
    ckv = hc[:, Q_LORA:Q_LORA + KV_LORA]
    kr = hc[:, Q_LORA + KV_LORA:Q_LORA + KV_LORA + C_PAD]
    krs = hc[:, Q_LORA + KV_LORA + C_PAD:]
    cqn = (cq * lax.rsqrt(jnp.mean(cq * cq, -1, keepdims=True) + 1e-6) * gq_ref[...]).astype(BF16)
    ckvn = (ckv * lax.rsqrt(jnp.mean(ckv * ckv, -1, keepdims=True) + 1e-6) * gkv_ref[...]).astype(BF16)
    cs = cs_ref[...]
    sn = sn_ref[...]
    q = _dot(cqn, wq_ref[...])
    qs = _dot(cqn, wqs_ref[...])
    k = _dot(ckvn, wk_ref[...])
    v = _dot(ckvn, wv_ref[...])
    krot = kr * cs + krs * sn
    scale = (C_NOPE + C_ROPE) ** -0.5 * math.log2(math.e)
    for h in range(C_HEADS):
        sl = slice(h * C_PAD, (h + 1) * C_PAD)
        qh = (q[:, sl] * cs + qs[:, sl] * sn) * scale
        qt_ref[sl, :] = qh.T.astype(BF16)
        k_ref[h] = (k[:, sl] + krot).astype(BF16)
    one_row = (lax.broadcasted_iota(jnp.int32, (1, C_PAD), 1) == C_V).astype(F32)
    for h in range(C_HEADS):
        vh = v[:, h * C_PAD:(h + 1) * C_PAD] + one_row
        vt_ref[h * C_VROWS:(h + 1) * C_VROWS, :] = vh.T[:C_VROWS, :].astype(BF16)


def _layer_block(a, layer):
    return pl.BlockSpec((None,) + a.shape[1:], lambda i, *_: (layer,) + (0,) * (a.ndim - 1))


def _in_proj(x, waq, wab, wc, wq, wqs, wk, wv, gq, gkv, cs, sn, layer):
    T = x.shape[0]
    tm = TM_IN
    full = lambda a: _layer_block(a, layer)
    row = lambda w: pl.BlockSpec((tm, w), lambda i: (i, 0))
    colblk = lambda r: pl.BlockSpec((r, tm), lambda i: (0, i))
    return pl.pallas_call(
        _in_proj_kernel,
        grid=(T // tm,),
        in_specs=[row(D_MODEL), full(waq), full(wab), full(wc), full(wq), full(wqs), full(wk), full(wv), full(gq),
                  full(gkv),
                  row(C_PAD), row(C_PAD)],
        out_specs=[row(AB_COLS), colblk(C_HEADS * C_PAD), pl.BlockSpec((C_HEADS, tm, C_PAD), lambda i: (0, i, 0)),
                   colblk(C_HEADS * C_VROWS)],
        out_shape=[jax.ShapeDtypeStruct((T, AB_COLS), F32), jax.ShapeDtypeStruct((C_HEADS * C_PAD, T), BF16),
                   jax.ShapeDtypeStruct((C_HEADS, T, C_PAD), BF16),
                   jax.ShapeDtypeStruct((C_HEADS * C_VROWS, T), BF16)],
        compiler_params=_cparams("arbitrary"),
        name="in_proj",
    )(x, waq, wab, wc, wq, wqs, wk, wv, gq, gkv, cs, sn)


def _banded_kernel(sink_ref, *refs, dil, n_heads, group, has_sink, want_lse):
    nq = n_heads // 2
    nkv = n_heads // group // 2
    q_refs, refs = refs[:nq], refs[nq:]
    k_refs, v_refs, kp_refs, vp_refs = (refs[i * nkv:(i + 1) * nkv] for i in range(4))
    bias_ref = refs[4 * nkv]
    o_refs = refs[4 * nkv + 1:4 * nkv + 1 + nq]
    lse_refs = refs[4 * nkv + 1 + nq:]
    c = pl.program_id(0)
    span = BLK * dil
    nb = CHUNK // span
    n_combo = nb * dil
    starts = [b * span + r for b in range(nb) for r in range(dil)]
    scale = HEAD_DIM ** -0.5

    def rows(start):
        return pl.ds(start, BLK) if dil == 1 else pl.ds(start, BLK, stride=dil)

    def gather(ref):
        return jnp.stack([ref[rows(s), :] for s in starts], axis=0)

    def with_prev(ref, prev_ref):
        cur = gather(ref)
        first = jnp.stack([prev_ref[rows(r), :] for r in range(dil)], axis=0)
        prev = jnp.concatenate([first, cur[:n_combo - dil]], axis=0) if dil < n_combo else first
        return jnp.concatenate([prev, cur], axis=1).astype(BF16)

    q = [gather(ref).astype(BF16) for ref in q_refs]
    kcat = [with_prev(k_refs[j], kp_refs[j]) for j in range(nkv)]
    vcat = [with_prev(v_refs[j], vp_refs[j]) for j in range(nkv)]

    lane = lax.broadcasted_iota(jnp.int32, (1, 1, LANES), 2)
    low = lane < HEAD_DIM
    combo = lax.broadcasted_iota(jnp.int32, (n_combo, 1, 2 * BLK), 0)
    col = lax.broadcasted_iota(jnp.int32, (n_combo, 1, 2 * BLK), 2)
    no_prev = jnp.logical_and(c == 0, jnp.logical_and(combo < dil, col < BLK))

    for j in range(nq):
        o_par, lse_par = [], []
        for par in range(2):
            h = 2 * j + par if group == 1 else group * par + j
            kvb = j if group == 1 else 0
            keep = low if par == 0 else jnp.logical_not(low)
            kk = jnp.where(keep, kcat[kvb], jnp.zeros((), BF16))
            s = _bdot_nt(q[j], kk)
            s = s * scale + bias_ref[h][None]
            s = jnp.where(no_prev, NEG_INF, s)
            m = jnp.max(s, axis=-1, keepdims=True)
            if has_sink:
                snk = sink_ref[h]
                m = jnp.maximum(m, snk)
            e = jnp.exp(s - m)
            l = jnp.sum(e, axis=-1, keepdims=True)
            if has_sink:
                l = l + jnp.exp(snk - m)
            p = (e * (1.0 / l)).astype(BF16)
            o_par.append(_bdot(p, vcat[kvb]))
            if want_lse:
                lse_par.append(m + jnp.log(l))
        o_pair = jnp.where(low, o_par[0], o_par[1])
        if want_lse:
            lse_pair = jnp.where(low, lse_par[0], lse_par[1])
        for ci, s0 in enumerate(starts):
            o_refs[j][rows(s0), :] = o_pair[ci]
            if want_lse:
                lse_refs[j][rows(s0), :] = lse_pair[ci]


def _banded(hab, sink, bias, *, q_col, k_col, v_col, dil, n_heads, group, has_sink, want_lse):
    T = hab.shape[0]
    span = BLK * dil
    nb = CHUNK // span
    nq = n_heads // 2
    nkv = n_heads // group // 2
    kernel = functools.partial(_banded_kernel, dil=dil, n_heads=n_heads, group=group,
                               has_sink=has_sink, want_lse=want_lse)
    prev = lambda cb: pl.BlockSpec((span, LANES), lambda c, s: (jnp.maximum(c * nb - 1, 0), cb))
    cur = lambda cb: pl.BlockSpec((CHUNK, LANES), lambda c, s: (c, cb))
    in_specs = ([cur(q_col + j) for j in range(nq)] + [cur(k_col + j) for j in range(nkv)]
                + [cur(v_col + j) for j in range(nkv)] + [prev(k_col + j) for j in range(nkv)]
                + [prev(v_col + j) for j in range(nkv)] + [pl.BlockSpec(bias.shape, lambda c, s: (0, 0, 0))])
    n_out = nq * (2 if want_lse else 1)
    grid_spec = pltpu.PrefetchScalarGridSpec(
        num_scalar_prefetch=1,
        grid=(T // CHUNK,),
        in_specs=in_specs,
        out_specs=[pl.BlockSpec((CHUNK, LANES), lambda c, s: (c, 0))] * n_out,
    )
    outs = pl.pallas_call(
        kernel,
        grid_spec=grid_spec,
        out_shape=[jax.ShapeDtypeStruct((T, LANES), F32)] * n_out,
        compiler_params=_cparams("arbitrary"),
        name=f"banded_d{dil}" + ("_sink" if has_sink else ""),
    )(sink, *([hab] * (nq + 4 * nkv)), bias)
    return outs[:nq], outs[nq:]


def _flash_kernel(qt_ref, k_ref, vt_ref, ot_ref, m_ref, acc_ref, s_ref):
    qi = pl.program_id(1)
    qt = qt_ref[...]
    m_ref[...] = jnp.full(m_ref.shape, NEG_INF, F32)
    acc_ref[...] = jnp.zeros(acc_ref.shape, F32)

    def scores(j, lo=0):
        off = pl.multiple_of(j * TK, TK)
        return _dot(k_ref[pl.ds(off, TK), :], qt[:, lo:])

    def accumulate(j, s, lo=0):
        off = pl.multiple_of(j * TK, TK)
        m_old = m_ref[:, lo:]
        m_new = jnp.maximum(m_old, jnp.max(s, axis=0, keepdims=True))
        alpha = jnp.exp2(m_old - m_new)
        p = jnp.exp2(s - m_new).astype(BF16)
        acc_ref[:, lo:] = alpha * acc_ref[:, lo:] + _dot(vt_ref[:, pl.ds(off, TK)], p)
        m_ref[:, lo:] = m_new

    def diagonal(s):
        key = lax.broadcasted_iota(jnp.int32, s.shape, 0)
        qry = lax.broadcasted_iota(jnp.int32, s.shape, 1)
        return jnp.where(key <= qry, s, NEG_INF)

    n_full = qi * N_DIAG
    s_ref[0] = scores(0)
    n_main = n_full // FLASH_UNROLL

    def main_body(i, carry):
        j = FLASH_UNROLL * i
        for u in range(FLASH_UNROLL):
            s_ref[(u + 1) % 2] = scores(j + u + 1)
            accumulate(j + u, s_ref[u % 2])
        return carry

    lax.fori_loop(0, n_main, main_body, 0)

    base = FLASH_UNROLL * n_main
    for rem in range(0, FLASH_UNROLL, math.gcd(FLASH_UNROLL, N_DIAG)):
        @pl.when(n_full - base == rem)
        def _():
            n_tail = rem + N_DIAG
            first_lane = lambda u: max(u - rem, 0) * TK
            for u in range(n_tail):
                if u + 1 < n_tail:
                    lo = first_lane(u + 1)
                    s_ref[(u + 1) % 2, :, lo:] = scores(base + u + 1, lo)
                lo = first_lane(u)
                s = s_ref[u % 2, :, lo:]
                accumulate(base + u, s if u < rem else diagonal(s), lo)

    ot_ref[...] = (acc_ref[:C_V, :] / acc_ref[C_V:C_V + 1, :]).astype(ot_ref.dtype)


def _flash(qt, k, vt):
    H, T, _ = k.shape
    return pl.pallas_call(
        _flash_kernel,
        grid=(H, T // TQ),
        in_specs=[pl.BlockSpec((C_PAD, TQ), lambda h, i: (h, i)),
                  pl.BlockSpec((None, T, C_PAD), lambda h, i: (h, 0, 0)),
                  pl.BlockSpec((C_VROWS, T), lambda h, i: (h, 0))],
        out_specs=pl.BlockSpec((C_V, TQ), lambda h, i: (h, i)),
        out_shape=jax.ShapeDtypeStruct((H * C_V, T), BF16),
        scratch_shapes=[pltpu.VMEM((1, TQ), F32), pltpu.VMEM((C_VROWS, TQ), F32), pltpu.VMEM((2, TK, TQ), F32)],
        compiler_params=_cparams("arbitrary", "arbitrary"),
        name="mla_flash",
    )(qt, k, vt)


def _layer_norm(z, g, b):
    mu = jnp.mean(z, -1, keepdims=True)
    zc = z - mu
    var = jnp.mean(zc * zc, -1, keepdims=True)
    return zc * lax.rsqrt(var + 1e-5) * g + b


def _out_router_kernel(*refs):
    n_ab = 2 + 6 + 6
    oa_refs, ob_refs, lse_refs = refs[0:2], refs[2:8], refs[8:14]
    (oc_ref, x_ref, woa_ref, wob_ref, woc_ref, g_ref, b_ref, wr_ref, br_ref,
     x1_ref, slab_ref, cnt_ref, carry_ref, prev_ref) = refs[n_ab:]
    i = pl.program_id(0)
    tm = x_ref.shape[0]

    @pl.when(i == 0)
    def _():
        carry_ref[...] = jnp.zeros(carry_ref.shape, F32)
        prev_ref[...] = jnp.zeros(prev_ref.shape, F32)

    xr = prev_ref[...]

    y = None
    for j in range(2):
        l0, l1, l2 = (lse_refs[2 * g + j][...] for g in range(3))
        mx = jnp.maximum(jnp.maximum(l0, l1), l2)
        e0, e1, e2 = jnp.exp(l0 - mx), jnp.exp(l1 - mx), jnp.exp(l2 - mx)
        den = e0 + e1 + e2
        ob = (e0 * ob_refs[j][...] + e1 * ob_refs[2 + j][...] + e2 * ob_refs[4 + j][...]) * (1.0 / den)
        part = _dot(oa_refs[j][...].astype(BF16), woa_ref[j]) + _dot(ob.astype(BF16), wob_ref[j])
        y = part if y is None else y + part
    y = y + _dot(oc_ref[...].astype(F32).T.astype(BF16), woc_ref[...])
    x1 = _layer_norm(DEEPNORM_ALPHA * x_ref[...] + y, g_ref[...], b_ref[...])
    x1_ref[...] = x1
    prev_ref[...] = x1

    xr_hi = xr.astype(BF16)
    xr_lo = (xr - xr_hi.astype(F32)).astype(BF16)
    logits = _dot(jnp.concatenate([xr_hi, xr_lo, xr_hi], axis=1), wr_ref[...]) + br_ref[...]
    lane = lax.broadcasted_iota(jnp.int32, (tm, LANES), 1)
    ninf = jnp.float32(-jnp.inf)
    gmask = lane < N_GROUPS
    gl = jnp.where(gmask, logits, ninf)
    gmax = jnp.max(gl, axis=-1, keepdims=True)
    grp = jnp.min(jnp.where(gl == gmax, lane, LANES), axis=-1, keepdims=True)
    p_grp = 1.0 / jnp.sum(jnp.where(gmask, jnp.exp(logits - gmax), 0.0), axis=-1, keepdims=True)
    lo = N_GROUPS + EXPERTS_PER_GROUP * grp
    emask = jnp.logical_and(lane >= lo, lane < lo + EXPERTS_PER_GROUP)
    el = jnp.where(emask, logits, ninf)
    v0 = jnp.max(el, axis=-1, keepdims=True)
    i0 = jnp.min(jnp.where(el == v0, lane, LANES), axis=-1, keepdims=True)
    el2 = jnp.where(lane == i0, ninf, el)
    v1 = jnp.max(el2, axis=-1, keepdims=True)
    i1 = jnp.min(jnp.where(el2 == v1, lane, LANES), axis=-1, keepdims=True)
    t = jnp.exp(v1 - v0)
    gate0 = p_grp / (1.0 + t)
    gate1 = p_grp * t / (1.0 + t)

    hit0 = lane == i0
    hit1 = lane == i1
    cmat = jnp.logical_or(hit0, hit1).astype(BF16)
    tr = lax.broadcasted_iota(jnp.int32, (tm, tm), 0)
    tc = lax.broadcasted_iota(jnp.int32, (tm, tm), 1)
    tril = (tc < tr).astype(BF16)
    prefix = _dot(tril, cmat) + carry_ref[...]
    r0 = jnp.sum(jnp.where(hit0, prefix, 0.0), axis=-1, keepdims=True)
    r1 = jnp.sum(jnp.where(hit1, prefix, 0.0), axis=-1, keepdims=True)
    counted = (i > 0).astype(F32)
    carry_ref[...] = carry_ref[...] + counted * jnp.sum(cmat.astype(F32), axis=0, keepdims=True)
    cnt_ref[...] = carry_ref[...]

    fields = [(i0 - N_GROUPS).astype(F32), (i1 - N_GROUPS).astype(F32), r0, r1, gate0, gate1]
    slab = jnp.zeros((tm, LANES), F32)
    for n, f in enumerate(fields):
        slab = jnp.where(lane == n, f, slab)
    slab_ref[...] = slab


def _out_router(oa, ob, lse, oc, x, woa, wob, woc, g, b, wr, br, layer):
    T = x.shape[0]
    tm = TM_OUT
    n = T // tm
    full = lambda a: _layer_block(a, layer)
    row = lambda w: pl.BlockSpec((tm, w), lambda i: (jnp.minimum(i, n - 1), 0))
    lag = pl.BlockSpec((tm, LANES), lambda i: (jnp.maximum(i - 1, 0), 0))
    return pl.pallas_call(
        _out_router_kernel,
        grid=(n + 1,),
        in_specs=[row(LANES)] * 14 + [pl.BlockSpec((C_HEADS * C_V, tm), lambda i: (0, jnp.minimum(i, n - 1))),
                                      row(D_MODEL),
                                      full(woa), full(wob), full(woc), full(g), full(b), full(wr), full(br)],
        out_specs=[row(D_MODEL), lag, pl.BlockSpec((1, LANES), lambda i: (0, 0))],
        out_shape=[jax.ShapeDtypeStruct((T, D_MODEL), F32), jax.ShapeDtypeStruct((T, LANES), F32),
                   jax.ShapeDtypeStruct((1, LANES), F32)],
        scratch_shapes=[pltpu.VMEM((1, LANES), F32), pltpu.VMEM((tm, D_MODEL), F32)],
        compiler_params=_cparams("arbitrary"),
        name="out_router",
    )(*oa, *ob, *lse, oc, x, woa, wob, woc, g, b, wr, br)


SUBLANES = 8
PAD_BITS = tuple(1 << k for k in reversed(range(SUBLANES.bit_length() - 1, ROW_BLK.bit_length() - 1)))


def _dispatch_kernel(d0_ref, d1_ref, pad_at_ref, pad_len_ref, nu_ref, x_ref, xs_ref, zero_ref, size_ref, sem, zsem):
    i = pl.program_id(0)
    tm = x_ref.shape[0] * SUBLANES
    base = i * tm

    def pad_copies(e):
        n = pad_len_ref[e]
        at = pad_at_ref[e]
        head = n & (SUBLANES - 1)
        out = [(j < head, pltpu.make_async_copy(zero_ref.at[pl.ds(0, 1), :], xs_ref.at[pl.ds(at + j, 1), :], zsem))
               for j in range(SUBLANES - 1)]
        body = n - head
        for bit in PAD_BITS:
            start = pl.multiple_of(at + head + (body & ~(2 * bit - 1)), SUBLANES)
            out.append(((body & bit) != 0,
                        pltpu.make_async_copy(zero_ref.at[pl.ds(0, bit), :], xs_ref.at[pl.ds(start, bit), :], zsem)))
        return out

    def spare_copies():
        n_blocks = xs_ref.shape[0] // ROW_BLK
        first = d0_ref.shape[0] * TOP_K // ROW_BLK
        return [(b >= nu_ref[0], pltpu.make_async_copy(zero_ref, xs_ref.at[pl.ds(b * ROW_BLK, ROW_BLK), :], zsem))
                for b in range(first, n_blocks)]

    def all_zero_copies():
        return [pair for e in range(N_EXPERTS) for pair in pad_copies(e)] + spare_copies()

    @pl.when(i == 0)
    def _():
        zero_ref[...] = jnp.zeros(zero_ref.shape, F32)
        for on, cp in all_zero_copies():
            pl.when(on)(cp.start)

    def start(g, carry):
        for u in range(SUBLANES):
            t = g * SUBLANES + u
            for d_ref in (d0_ref, d1_ref):
                pltpu.make_async_copy(x_ref.at[g, pl.ds(u, 1), :], xs_ref.at[pl.ds(d_ref[base + t], 1), :], sem).start()
        return carry

    lax.fori_loop(0, tm // SUBLANES, start, 0)
    for _ in range(2):
        pltpu.make_async_copy(xs_ref.at[pl.ds(0, tm), :], size_ref, sem).wait()

    @pl.when(i == pl.num_programs(0) - 1)
    def _():
        for on, cp in all_zero_copies():
            pl.when(on)(cp.wait)


def _dispatch(d0, d1, pad_at, pad_len, n_used, x1, n_rows):
    T = x1.shape[0]
    tm = ROW_BLK
    grid_spec = pltpu.PrefetchScalarGridSpec(
        num_scalar_prefetch=5,
        grid=(T // tm,),
        in_specs=[pl.BlockSpec((tm // SUBLANES, SUBLANES, D_MODEL), lambda i, *_: (i, 0, 0))],
        out_specs=pl.BlockSpec(memory_space=pl.ANY),
        scratch_shapes=[pltpu.VMEM((ROW_BLK, D_MODEL), F32), pltpu.VMEM((tm, D_MODEL), F32),
                        pltpu.SemaphoreType.DMA, pltpu.SemaphoreType.DMA],
    )
    return pl.pallas_call(
        _dispatch_kernel,
        grid_spec=grid_spec,
        out_shape=jax.ShapeDtypeStruct((n_rows, D_MODEL), F32),
        compiler_params=_cparams("arbitrary"),
        name="moe_dispatch",
    )(d0, d1, pad_at, pad_len, n_used, x1.reshape(T // SUBLANES, SUBLANES, D_MODEL))


FFN_BLOCKS = 1


def _ffn_kernel(be_ref, nu_ref, xs_ref, *refs):
    del be_ref
    w_refs, ys_ref = refs[:-1], refs[-1]
    i = pl.program_id(0)
    for k in range(FFN_BLOCKS):
        wg_ref, wu_ref, wd_ref = w_refs[3 * k:3 * k + 3]
        rows = pl.ds(k * ROW_BLK, ROW_BLK)
        used = FFN_BLOCKS * i + k < nu_ref[0]

        @pl.when(used)
        def _():
            xb = xs_ref[rows, :].astype(BF16)
            a = _dot(xb, wg_ref[...].astype(BF16))
            u = _dot(xb, wu_ref[...].astype(BF16))
            hmid = (a * jax.nn.sigmoid(a) * u).astype(BF16)
            ys_ref[rows, :] = _dot(hmid, wd_ref[...].astype(BF16))

        @pl.when(jnp.logical_not(used))
        def _():
            ys_ref[rows, :] = jnp.zeros((ROW_BLK, D_MODEL), F32)


def _ffn(block_expert, n_used, xs, wg, wu, wd, layer):
    n_rows = xs.shape[0]
    step_rows = FFN_BLOCKS * ROW_BLK
    assert n_rows % step_rows == 0
    w_spec = lambda shape, k: pl.BlockSpec((None, None) + shape,
                                           lambda i, be, nu: (layer, be[FFN_BLOCKS * i + k], 0, 0))
    w_specs = [w_spec(s, k) for k in range(FFN_BLOCKS)
               for s in ((D_MODEL, D_EXPERT), (D_MODEL, D_EXPERT), (D_EXPERT, D_MODEL))]
    grid_spec = pltpu.PrefetchScalarGridSpec(
        num_scalar_prefetch=2,
        grid=(n_rows // step_rows,),
        in_specs=[pl.BlockSpec((step_rows, D_MODEL),
                               lambda i, be, nu: (jnp.minimum(i, (nu[0] - 1) // FFN_BLOCKS), 0))] + w_specs,
        out_specs=pl.BlockSpec((step_rows, D_MODEL), lambda i, be, nu: (i, 0)),
    )
    return pl.pallas_call(
        _ffn_kernel,
        grid_spec=grid_spec,
        out_shape=jax.ShapeDtypeStruct((n_rows, D_MODEL), F32),
        compiler_params=_cparams("arbitrary"),
        name="moe_ffn",
    )(block_expert, n_used, xs, *([wg, wu, wd] * FFN_BLOCKS))


def _combine_kernel(d0_ref, d1_ref, ys_ref, x1_ref, slab_ref, g_ref, b_ref, o_ref, buf_ref, sem):
    i = pl.program_id(0)
    n = pl.num_programs(0)
    tm = x1_ref.shape[0]

    def gather(tile):
        slot = tile % 2
        base = tile * tm

        def start(g, carry):
            for u in range(SUBLANES):
                t = g * SUBLANES + u
                for k, d_ref in enumerate((d0_ref, d1_ref)):
                    pltpu.make_async_copy(ys_ref.at[pl.ds(d_ref[base + t], 1), :],
                                          buf_ref.at[slot, k, g, pl.ds(u, 1), :], sem.at[slot]).start()
            return carry

        lax.fori_loop(0, tm // SUBLANES, start, 0)

    @pl.when(i == 0)
    def _():
        gather(0)

    @pl.when(i + 1 < n)
    def _():
        gather(i + 1)

    slot = i % 2
    for k in range(2):
        pltpu.make_async_copy(ys_ref.at[pl.ds(0, tm), :], o_ref, sem.at[slot]).wait()
    slab = slab_ref[...]
    rows = lambda k: buf_ref[slot, k].reshape(tm, D_MODEL)
    y = slab[:, 4:5] * rows(0) + slab[:, 5:6] * rows(1)
    o_ref[...] = _layer_norm(DEEPNORM_ALPHA * x1_ref[...] + y, g_ref[...], b_ref[...])


def _combine(d0, d1, ys, x1, slab, g, b, layer):
    T = x1.shape[0]
    tm = ROW_BLK
    grid_spec = pltpu.PrefetchScalarGridSpec(
        num_scalar_prefetch=2,
        grid=(T // tm,),
        in_specs=[pl.BlockSpec(memory_space=pl.ANY),
                  pl.BlockSpec((tm, D_MODEL), lambda i, a, c: (i, 0)),
                  pl.BlockSpec((tm, LANES), lambda i, a, c: (i, 0)),
                  _layer_block(g, layer), _layer_block(b, layer)],
        out_specs=pl.BlockSpec((tm, D_MODEL), lambda i, a, c: (i, 0)),
        scratch_shapes=[pltpu.VMEM((2, 2, tm // SUBLANES, SUBLANES, D_MODEL), F32), pltpu.SemaphoreType.DMA((2,))],
    )
    return pl.pallas_call(
        _combine_kernel,
        grid_spec=grid_spec,
        out_shape=jax.ShapeDtypeStruct((T, D_MODEL), F32),
        compiler_params=_cparams("arbitrary"),
        name="moe_combine",
    )(d0, d1, ys, x1, slab, g, b)


def _t5_bucket(dist):
    max_exact = N_BUCKETS // 2
    large = max_exact + (jnp.log(jnp.maximum(dist, 1).astype(F32) / max_exact)
                         / math.log(MAX_DISTANCE / max_exact) * (N_BUCKETS - max_exact)).astype(jnp.int32)
    return jnp.where(dist < max_exact, dist, jnp.minimum(large, N_BUCKETS - 1))


def _bias_table(table_cols, dil, max_dist):
    dist = BLK + jnp.arange(BLK)[:, None] - jnp.arange(2 * BLK)[None, :]
    bucket = _t5_bucket(jnp.maximum(dist * dil, 0))
    hit = bucket[None, :, :, None] == jnp.arange(N_BUCKETS)[None, None, None, :]
    bias = jnp.sum(jnp.where(hit, table_cols.T.astype(F32)[:, None, None, :], 0.0), axis=-1)
    mask = (dist >= 0) & (dist <= max_dist)
    return jnp.where(mask[None], bias, NEG_INF)


def _rope_tables(T):
    inv_freq = ROPE_THETA ** (-jnp.arange(0, C_ROPE, 2, dtype=F32) / C_ROPE)
    ang = jnp.arange(T, dtype=F32)[:, None] * inv_freq[None, :]
    cos, sin = jnp.cos(ang), jnp.sin(ang)
    ones = jnp.ones((T, C_NOPE), F32)
    zeros_n = jnp.zeros((T, C_NOPE), F32)
    zeros_p = jnp.zeros((T, C_PAD - C_NOPE - C_ROPE), F32)
    cs = jnp.concatenate([ones, cos, cos, zeros_p], axis=1)
    sn = jnp.concatenate([zeros_n, -sin, sin, zeros_p], axis=1)
    return cs, sn


def _swap_halves(w):
    half = w.shape[-1] // 2
    return jnp.concatenate([w[..., half:], w[..., :half]], axis=-1)


def _pad_cols(w, width):
    return jnp.pad(w, ((0, 0), (0, width - w.shape[1])))


def _prep_layer_weights(w_in, w_out, w_uq, w_ukv, w_rg, b_rg, w_re, b_re):
    head_cols = lambda w, h: w[:, h * HEAD_DIM:(h + 1) * HEAD_DIM]
    waq = jnp.concatenate([head_cols(w_in, h) for h in A_HEAD_ORDER], axis=1).astype(BF16)
    wab = w_in[:, A_Q_HEADS * HEAD_DIM:AB_COLS].astype(BF16)
    c0 = AB_COLS
    w_cq = w_in[:, c0:c0 + Q_LORA]
    w_ckv = w_in[:, c0 + Q_LORA:c0 + Q_LORA + KV_LORA]
    w_kr = w_in[:, c0 + Q_LORA + KV_LORA:]
    lead = jnp.zeros((D_MODEL, C_NOPE), F32)
    wc = jnp.concatenate([w_cq, w_ckv,
                          _pad_cols(jnp.concatenate([lead, w_kr], 1), C_PAD),
                          _pad_cols(jnp.concatenate([lead, _swap_halves(w_kr)], 1), C_PAD)], axis=1).astype(BF16)
    uq = w_uq.reshape(Q_LORA, C_HEADS, C_NOPE + C_ROPE)
    wq = jnp.pad(uq, ((0, 0), (0, 0), (0, C_PAD - C_NOPE - C_ROPE))).reshape(Q_LORA, C_HEADS * C_PAD).astype(BF16)
    uqs = jnp.concatenate([jnp.zeros((Q_LORA, C_HEADS, C_NOPE), F32), _swap_halves(uq[..., C_NOPE:])], axis=-1)
    wqs = jnp.pad(uqs, ((0, 0), (0, 0), (0, C_PAD - C_NOPE - C_ROPE))).reshape(Q_LORA, C_HEADS * C_PAD).astype(BF16)
    ukv = w_ukv.reshape(KV_LORA, C_HEADS, C_NOPE + C_V)
    wk = jnp.pad(ukv[..., :C_NOPE], ((0, 0), (0, 0), (0, C_PAD - C_NOPE))).reshape(KV_LORA, C_HEADS * C_PAD)
    wv = jnp.pad(ukv[..., C_NOPE:], ((0, 0), (0, 0), (0, C_PAD - C_V))).reshape(KV_LORA, C_HEADS * C_PAD)
    woa = jnp.concatenate([w_out[h * HEAD_DIM:(h + 1) * HEAD_DIM] for h in A_HEAD_ORDER],
                          axis=0).reshape(2, LANES, D_MODEL).astype(BF16)
    wob = w_out[256:512].reshape(2, LANES, D_MODEL).astype(BF16)
    woc = w_out[512:].astype(BF16)
    wr = _pad_cols(jnp.concatenate([w_rg, w_re.transpose(1, 0, 2).reshape(D_MODEL, N_EXPERTS)], axis=1), LANES)
    wr_hi = wr.astype(BF16)
    wr_lo = (wr - wr_hi.astype(F32)).astype(BF16)
    wr = jnp.concatenate([wr_hi, wr_hi, wr_lo], axis=0)
    br = _pad_cols(jnp.concatenate([b_rg, b_re.reshape(N_EXPERTS)])[None, :], LANES)
    return dict(waq=waq, wab=wab, wc=wc, wq=wq, wqs=wqs, wk=wk.astype(BF16), wv=wv.astype(BF16),
                woa=woa, wob=wob, woc=woc, wr=wr, br=br)


def kernel(x, w_in, w_out, sinks, rel_bias, mla_q_norm, mla_kv_norm, w_uq, w_ukv, ln1_g, ln1_b,
           w_route_group, b_route_group, w_route_expert, b_route_expert,
           w_expert_gate, w_expert_up, w_expert_down, ln2_g, ln2_b):
    Bsz, T, D = x.shape
    assert Bsz == 1 and D == D_MODEL and T % CHUNK == 0
    xt = x.reshape(T, D)
    cs, sn = _rope_tables(T)
    bias_a = _bias_table(rel_bias[:, :A_Q_HEADS], 1, SWA_WINDOW - 1)
    bias_b = [_bias_table(rel_bias[:, A_Q_HEADS + g * B_HEADS_PER_GROUP:A_Q_HEADS + (g + 1) * B_HEADS_PER_GROUP],
                          dil, window // dil) for g, (window, dil) in enumerate(DILATED_PAIRS)]
    n_assign = T * TOP_K
    n_blocks = n_assign // ROW_BLK + N_EXPERTS
    n_rows = n_blocks * ROW_BLK
    no_sink = jnp.zeros((B_HEADS_PER_GROUP,), F32)

    w = jax.vmap(_prep_layer_weights)(w_in, w_out, w_uq, w_ukv, w_route_group, b_route_group,
                                      w_route_expert, b_route_expert)
    per_layer_row = lambda a: a[:, None, :]
    gq, gkv = per_layer_row(mla_q_norm), per_layer_row(mla_kv_norm)
    g1, b1, g2, b2 = (per_layer_row(a) for a in (ln1_g, ln1_b, ln2_g, ln2_b))

    for layer in range(w_in.shape[0]):
        hab, qt, k, vt = _in_proj(xt, w["waq"], w["wab"], w["wc"], w["wq"], w["wqs"], w["wk"], w["wv"], gq, gkv, cs, sn,
                                  layer)
        oa, _ = _banded(hab, sinks[layer], bias_a, q_col=0, k_col=2, v_col=3, dil=1,
                        n_heads=A_Q_HEADS, group=A_Q_HEADS // A_KV_HEADS, has_sink=True, want_lse=False)
        ob, lse = [], []
        for g, (_, dil) in enumerate(DILATED_PAIRS):
            o_g, lse_g = _banded(hab, no_sink, bias_b[g], q_col=4 + 2 * g, k_col=10 + 2 * g, v_col=16 + 2 * g,
                                 dil=dil, n_heads=B_HEADS_PER_GROUP, group=1, has_sink=False, want_lse=True)
            ob += o_g
            lse += lse_g
        oc = _flash(qt, k, vt)
        x1, slab, cnt = _out_router(oa, ob, lse, oc, xt, w["woa"], w["wob"], w["woc"], g1, b1, w["wr"], w["br"],
                                    layer)

        counts = cnt[0, N_GROUPS:N_GROUPS + N_EXPERTS].astype(jnp.int32)
        padded = ((counts + ROW_BLK - 1) // ROW_BLK) * ROW_BLK
        pad_end = jnp.cumsum(padded)
        pad_start = pad_end - padded
        eids = jnp.arange(N_EXPERTS, dtype=F32)
        onehot = (slab[:, 0:TOP_K, None] == eids[None, None, :]).astype(F32)
        seg = jnp.einsum('tke,e->tk', onehot, pad_start.astype(F32), precision=lax.Precision.HIGHEST)
        d0 = (seg[:, 0] + slab[:, 2]).astype(jnp.int32)
        d1 = (seg[:, 1] + slab[:, 3]).astype(jnp.int32)
        block_start = jnp.arange(n_blocks, dtype=jnp.int32) * ROW_BLK
        block_expert = jnp.minimum(jnp.sum((pad_end[None, :] <= block_start[:, None]).astype(jnp.int32), axis=1),
                                   N_EXPERTS - 1)

        n_used = (pad_end[-1:] // ROW_BLK).astype(jnp.int32)
        xs = _dispatch(d0, d1, pad_start + counts, padded - counts, n_used, x1, n_rows)
        ys = _ffn(block_expert, n_used, xs, w_expert_gate, w_expert_up, w_expert_down, layer)
        xt = _combine(d0, d1, ys, x1, slab, g2, b2, layer)
    return xt.reshape(Bsz, T, D)
```

```python
import functools
import math

import jax
import jax.numpy as jnp
from jax import lax
from jax.experimental import pallas as pl
from jax.experimental.pallas import tpu as pltpu

F32 = jnp.float32
BF16 = jnp.bfloat16

D_MODEL = 1024
DEPTH = 4
HEAD_DIM = 64
BLK = 128
NEG_INF = -1e30

SWA_WINDOW = 128
A_Q_HEADS = 4
A_KV_HEADS = 2
A_COLS = (A_Q_HEADS + 2 * A_KV_HEADS) * HEAD_DIM
A_HEAD_ORDER = (0, 2, 1, 3)

DILATED_PAIRS = ((128, 1), (512, 4), (2048, 16))
B_HEADS_PER_GROUP = 4
B_HEADS = len(DILATED_PAIRS) * B_HEADS_PER_GROUP
B_COLS = 3 * B_HEADS * HEAD_DIM
AB_COLS = A_COLS + B_COLS

C_HEADS = 8
C_NOPE = 64
C_ROPE = 32
C_V = 64
Q_LORA = 256
KV_LORA = 128
ROPE_THETA = 10000.0
C_PAD = 128
C_VROWS = 80

N_BUCKETS = 32
MAX_DISTANCE = 2048

N_GROUPS = 4
EXPERTS_PER_GROUP = 8
N_EXPERTS = N_GROUPS * EXPERTS_PER_GROUP
TOP_K = 2
D_EXPERT = 256
ROW_BLK = 512

DEEPNORM_ALPHA = (2 * DEPTH) ** 0.25

VMEM_LIMIT = 56 * 1024 * 1024

CHUNK = 2048
TM_IN = 512
TM_OUT = 512
TQ = 1024
TK = 512
N_DIAG = TQ // TK
FLASH_UNROLL = 4
LANES = 128


def _dot(a, b):
    return jnp.dot(a, b, preferred_element_type=F32)


def _bdot(a, b):
    return lax.dot_general(a, b, (((2,), (1,)), ((0,), (0,))), preferred_element_type=F32)


def _bdot_nt(a, b):
    return lax.dot_general(a, b, (((2,), (2,)), ((0,), (0,))), preferred_element_type=F32)


def _cparams(*sem):
    return pltpu.CompilerParams(dimension_semantics=sem, vmem_limit_bytes=VMEM_LIMIT)


def _in_proj_kernel(x_ref, waq_ref, wab_ref, wc_ref, wq_ref, wqs_ref, wk_ref, wv_ref, gq_ref, gkv_ref, cs_ref, sn_ref,
                    hab_ref, qt_ref, k_ref, vt_ref):
    xb = x_ref[...].astype(BF16)
    n_aq = waq_ref.shape[1]
    hab_ref[:, :n_aq] = _dot(xb, waq_ref[...])
    hab_ref[:, n_aq:] = _dot(xb, wab_ref[...])
    hc = _dot(xb, wc_ref[...])
    cq = hc[:, :Q_LORA]
    ckv = hc[:, Q_LORA:Q_LORA + KV_LORA]
    kr = hc[:, Q_LORA + KV_LORA:Q_LORA + KV_LORA + C_PAD]
    krs = hc[:, Q_LORA + KV_LORA + C_PAD:]
    cqn = (cq * lax.rsqrt(jnp.mean(cq * cq, -1, keepdims=True) + 1e-6) * gq_ref[...]).astype(BF16)
    ckvn = (ckv * lax.rsqrt(jnp.mean(ckv * ckv, -1, keepdims=True) + 1e-6) * gkv_ref[...]).astype(BF16)
    cs = cs_ref[...]
    sn = sn_ref[...]
    q = _dot(cqn, wq_ref[...])
    qs = _dot(cqn, wqs_ref[...])
    k = _dot(ckvn, wk_ref[...])
    v = _dot(ckvn, wv_ref[...])
    krot = kr * cs + krs * sn
    scale = (C_NOPE + C_ROPE) ** -0.5 * math.log2(math.e)
    for h in range(C_HEADS):
        sl = slice(h * C_PAD, (h + 1) * C_PAD)
        qh = (q[:, sl] * cs + qs[:, sl] * sn) * scale
        qt_ref[sl, :] = qh.T.astype(BF16)
        k_ref[h] = (k[:, sl] + krot).astype(BF16)
    one_row = (lax.broadcasted_iota(jnp.int32, (1, C_PAD), 1) == C_V).astype(F32)
    for h in range(C_HEADS):
        vh = v[:, h * C_PAD:(h + 1) * C_PAD] + one_row
        vt_ref[h * C_VROWS:(h + 1) * C_VROWS, :] = vh.T[:C_VROWS, :].astype(BF16)


def _layer_block(a, layer):
    return pl.BlockSpec((None,) + a.shape[1:], lambda i, *_: (layer,) + (0,) * (a.ndim - 1))


def _in_proj(x, waq, wab, wc, wq, wqs, wk, wv, gq, gkv, cs, sn, layer):
    T = x.shape[0]
    tm = TM_IN
    full = lambda a: _layer_block(a, layer)
    row = lambda w: pl.BlockSpec((tm, w), lambda i: (i, 0))
    colblk = lambda r: pl.BlockSpec((r, tm), lambda i: (0, i))
    return pl.pallas_call(
        _in_proj_kernel,
        grid=(T // tm,),
        in_specs=[row(D_MODEL), full(waq), full(wab), full(wc), full(wq), full(wqs), full(wk), full(wv), full(gq),
                  full(gkv),
                  row(C_PAD), row(C_PAD)],
        out_specs=[row(AB_COLS), colblk(C_HEADS * C_PAD), pl.BlockSpec((C_HEADS, tm, C_PAD), lambda i: (0, i, 0)),
                   colblk(C_HEADS * C_VROWS)],
        out_shape=[jax.ShapeDtypeStruct((T, AB_COLS), F32), jax.ShapeDtypeStruct((C_HEADS * C_PAD, T), BF16),
                   jax.ShapeDtypeStruct((C_HEADS, T, C_PAD), BF16),
                   jax.ShapeDtypeStruct((C_HEADS * C_VROWS, T), BF16)],
        compiler_params=_cparams("arbitrary"),
        name="in_proj",
    )(x, waq, wab, wc, wq, wqs, wk, wv, gq, gkv, cs, sn)


def _banded_kernel(sink_ref, *refs, dil, n_heads, group, has_sink, want_lse):
    nq = n_heads // 2
    nkv = n_heads // group // 2
    q_refs, refs = refs[:nq], refs[nq:]
    k_refs, v_refs, kp_refs, vp_refs = (refs[i * nkv:(i + 1) * nkv] for i in range(4))
    bias_ref = refs[4 * nkv]
    o_refs = refs[4 * nkv + 1:4 * nkv + 1 + nq]
    lse_refs = refs[4 * nkv + 1 + nq:]
    c = pl.program_id(0)
    span = BLK * dil
    nb = CHUNK // span
    n_combo = nb * dil
    starts = [b * span + r for b in range(nb) for r in range(dil)]
    scale = HEAD_DIM ** -0.5

    def rows(start):
        return pl.ds(start, BLK) if dil == 1 else pl.ds(start, BLK, stride=dil)

    def gather(ref):
        return jnp.stack([ref[rows(s), :] for s in starts], axis=0)

    def with_prev(ref, prev_ref):
        cur = gather(ref)
        first = jnp.stack([prev_ref[rows(r), :] for r in range(dil)], axis=0)
        prev = jnp.concatenate([first, cur[:n_combo - dil]], axis=0) if dil < n_combo else first
        return jnp.concatenate([prev, cur], axis=1).astype(BF16)

    q = [gather(ref).astype(BF16) for ref in q_refs]
    kcat = [with_prev(k_refs[j], kp_refs[j]) for j in range(nkv)]
    vcat = [with_prev(v_refs[j], vp_refs[j]) for j in range(nkv)]

    lane = lax.broadcasted_iota(jnp.int32, (1, 1, LANES), 2)
    low = lane < HEAD_DIM
    combo = lax.broadcasted_iota(jnp.int32, (n_combo, 1, 2 * BLK), 0)
    col = lax.broadcasted_iota(jnp.int32, (n_combo, 1, 2 * BLK), 2)
    no_prev = jnp.logical_and(c == 0, jnp.logical_and(combo < dil, col < BLK))

    for j in range(nq):
        o_par, lse_par = [], []
        for par in range(2):
            h = 2 * j + par if group == 1 else group * par + j
            kvb = j if group == 1 else 0
            keep = low if par == 0 else jnp.logical_not(low)
            kk = jnp.where(keep, kcat[kvb], jnp.zeros((), BF16))
            s = _bdot_nt(q[j], kk)
            s = s * scale + bias_ref[h][None]
            s = jnp.where(no_prev, NEG_INF, s)
            m = jnp.max(s, axis=-1, keepdims=True)
            if has_sink:
                snk = sink_ref[h]
                m = jnp.maximum(m, snk)
            e = jnp.exp(s - m)
            l = jnp.sum(e, axis=-1, keepdims=True)
            if has_sink:
                l = l + jnp.exp(snk - m)
            p = (e * (1.0 / l)).astype(BF16)
            o_par.append(_bdot(p, vcat[kvb]))
            if want_lse:
                lse_par.append(m + jnp.log(l))
        o_pair = jnp.where(low, o_par[0], o_par[1])
        if want_lse:
            lse_pair = jnp.where(low, lse_par[0], lse_par[1])
        for ci, s0 in enumerate(starts):
            o_refs[j][rows(s0), :] = o_pair[ci]
            if want_lse:
                lse_refs[j][rows(s0), :] = lse_pair[ci]


def _banded(hab, sink, bias, *, q_col, k_col, v_col, dil, n_heads, group, has_sink, want_lse):
    T = hab.shape[0]
    span = BLK * dil
    nb = CHUNK // span
    nq = n_heads // 2
    nkv = n_heads // group // 2
    kernel = functools.partial(_banded_kernel, dil=dil, n_heads=n_heads, group=group,
                               has_sink=has_sink, want_lse=want_lse)
    prev = lambda cb: pl.BlockSpec((span, LANES), lambda c, s: (jnp.maximum(c * nb - 1, 0), cb))
    cur = lambda cb: pl.BlockSpec((CHUNK, LANES), lambda c, s: (c, cb))
    in_specs = ([cur(q_col + j) for j in range(nq)] + [cur(k_col + j) for j in range(nkv)]
                + [cur(v_col + j) for j in range(nkv)] + [prev(k_col + j) for j in range(nkv)]
                + [prev(v_col + j) for j in range(nkv)] + [pl.BlockSpec(bias.shape, lambda c, s: (0, 0, 0))])
    n_out = nq * (2 if want_lse else 1)
    grid_spec = pltpu.PrefetchScalarGridSpec(
        num_scalar_prefetch=1,
        grid=(T // CHUNK,),
        in_specs=in_specs,
        out_specs=[pl.BlockSpec((CHUNK, LANES), lambda c, s: (c, 0))] * n_out,
    )
    outs = pl.pallas_call(
        kernel,
        grid_spec=grid_spec,
        out_shape=[jax.ShapeDtypeStruct((T, LANES), F32)] * n_out,
        compiler_params=_cparams("arbitrary"),
        name=f"banded_d{dil}" + ("_sink" if has_sink else ""),
    )(sink, *([hab] * (nq + 4 * nkv)), bias)
    return outs[:nq], outs[nq:]


def _flash_kernel(qt_ref, k_ref, vt_ref, ot_ref, m_ref, acc_ref, s_ref):
    qi = pl.program_id(1)
    qt = qt_ref[...]
    m_ref[...] = jnp.full(m_ref.shape, NEG_INF, F32)
    acc_ref[...] = jnp.zeros(acc_ref.shape, F32)

    def scores(j, lo=0):
        off = pl.multiple_of(j * TK, TK)
        return _dot(k_ref[pl.ds(off, TK), :], qt[:, lo:])

    def accumulate(j, s, lo=0):
        off = pl.multiple_of(j * TK, TK)
        m_old = m_ref[:, lo:]
        m_new = jnp.maximum(m_old, jnp.max(s, axis=0, keepdims=True))
        alpha = jnp.exp2(m_old - m_new)
        p = jnp.exp2(s - m_new).astype(BF16)
        acc_ref[:, lo:] = alpha * acc_ref[:, lo:] + _dot(vt_ref[:, pl.ds(off, TK)], p)
        m_ref[:, lo:] = m_new

    def diagonal(s):
        key = lax.broadcasted_iota(jnp.int32, s.shape, 0)
        qry = lax.broadcasted_iota(jnp.int32, s.shape, 1)
        return jnp.where(key <= qry, s, NEG_INF)

    n_full = qi * N_DIAG
    s_ref[0] = scores(0)
    n_main = n_full // FLASH_UNROLL

    def main_body(i, carry):
        j = FLASH_UNROLL * i
        for u in range(FLASH_UNROLL):
            s_ref[(u + 1) % 2] = scores(j + u + 1)
            accumulate(j + u, s_ref[u % 2])
        return carry

    lax.fori_loop(0, n_main, main_body, 0)

    base = FLASH_UNROLL * n_main
    for rem in range(0, FLASH_UNROLL, math.gcd(FLASH_UNROLL, N_DIAG)):
        @pl.when(n_full - base == rem)
        def _():
            n_tail = rem + N_DIAG
            first_lane = lambda u: max(u - rem, 0) * TK
            for u in range(n_tail):
                if u + 1 < n_tail:
                    lo = first_lane(u + 1)
                    s_ref[(u + 1) % 2, :, lo:] = scores(base + u + 1, lo)
                lo = first_lane(u)
                s = s_ref[u % 2, :, lo:]
                accumulate(base + u, s if u < rem else diagonal(s), lo)

    ot_ref[...] = (acc_ref[:C_V, :] / acc_ref[C_V:C_V + 1, :]).astype(ot_ref.dtype)


def _flash(qt, k, vt):
    H, T, _ = k.shape
    return pl.pallas_call(
        _flash_kernel,
        grid=(H, T // TQ),
        in_specs=[pl.BlockSpec((C_PAD, TQ), lambda h, i: (h, i)),
                  pl.BlockSpec((None, T, C_PAD), lambda h, i: (h, 0, 0)),
                  pl.BlockSpec((C_VROWS, T), lambda h, i: (h, 0))],
        out_specs=pl.BlockSpec((C_V, TQ), lambda h, i: (h, i)),
        out_shape=jax.ShapeDtypeStruct((H * C_V, T), BF16),
        scratch_shapes=[pltpu.VMEM((1, TQ), F32), pltpu.VMEM((C_VROWS, TQ), F32), pltpu.VMEM((2, TK, TQ), F32)],
        compiler_params=_cparams("arbitrary", "arbitrary"),
        name="mla_flash",
    )(qt, k, vt)


def _layer_norm(z, g, b):
    mu = jnp.mean(z, -1, keepdims=True)
    zc = z - mu
    var = jnp.mean(zc * zc, -1, keepdims=True)
    return zc * lax.rsqrt(var + 1e-5) * g + b


def _out_router_kernel(*refs):
    n_ab = 2 + 6 + 6
    oa_refs, ob_refs, lse_refs = refs[0:2], refs[2:8], refs[8:14]
    (oc_ref, x_ref, woa_ref, wob_ref, woc_ref, g_ref, b_ref, wr_ref, br_ref,
     x1_ref, slab_ref, cnt_ref, carry_ref, prev_ref) = refs[n_ab:]
    i = pl.program_id(0)
    tm = x_ref.shape[0]

    @pl.when(i == 0)
    def _():
        carry_ref[...] = jnp.zeros(carry_ref.shape, F32)
        prev_ref[...] = jnp.zeros(prev_ref.shape, F32)

    xr = prev_ref[...]

    y = None
    for j in range(2):
        l0, l1, l2 = (lse_refs[2 * g + j][...] for g in range(3))
        mx = jnp.maximum(jnp.maximum(l0, l1), l2)
        e0, e1, e2 = jnp.exp(l0 - mx), jnp.exp(l1 - mx), jnp.exp(l2 - mx)
        den = e0 + e1 + e2
        ob = (e0 * ob_refs[j][...] + e1 * ob_refs[2 + j][...] + e2 * ob_refs[4 + j][...]) * (1.0 / den)
        part = _dot(oa_refs[j][...].astype(BF16), woa_ref[j]) + _dot(ob.astype(BF16), wob_ref[j])
        y = part if y is None else y + part
    y = y + _dot(oc_ref[...].astype(F32).T.astype(BF16), woc_ref[...])
    x1 = _layer_norm(DEEPNORM_ALPHA * x_ref[...] + y, g_ref[...], b_ref[...])
    x1_ref[...] = x1
    prev_ref[...] = x1

    xr_hi = xr.astype(BF16)
    xr_lo = (xr - xr_hi.astype(F32)).astype(BF16)
    logits = _dot(jnp.concatenate([xr_hi, xr_lo, xr_hi], axis=1), wr_ref[...]) + br_ref[...]
    lane = lax.broadcasted_iota(jnp.int32, (tm, LANES), 1)
    ninf = jnp.float32(-jnp.inf)
    gmask = lane < N_GROUPS
    gl = jnp.where(gmask, logits, ninf)
    gmax = jnp.max(gl, axis=-1, keepdims=True)
    grp = jnp.min(jnp.where(gl == gmax, lane, LANES), axis=-1, keepdims=True)
    p_grp = 1.0 / jnp.sum(jnp.where(gmask, jnp.exp(logits - gmax), 0.0), axis=-1, keepdims=True)
    lo = N_GROUPS + EXPERTS_PER_GROUP * grp
    emask = jnp.logical_and(lane >= lo, lane < lo + EXPERTS_PER_GROUP)
    el = jnp.where(emask, logits, ninf)
    v0 = jnp.max(el, axis=-1, keepdims=True)
    i0 = jnp.min(jnp.where(el == v0, lane, LANES), axis=-1, keepdims=True)
    el2 = jnp.where(lane == i0, ninf, el)
    v1 = jnp.max(el2, axis=-1, keepdims=True)
    i1 = jnp.min(jnp.where(el2 == v1, lane, LANES), axis=-1, keepdims=True)
    t = jnp.exp(v1 - v0)
    gate0 = p_grp / (1.0 + t)
    gate1 = p_grp * t / (1.0 + t)

    hit0 = lane == i0
    hit1 = lane == i1
    cmat = jnp.logical_or(hit0, hit1).astype(BF16)
    tr = lax.broadcasted_iota(jnp.int32, (tm, tm), 0)
    tc = lax.broadcasted_iota(jnp.int32, (tm, tm), 1)
    tril = (tc < tr).astype(BF16)
    prefix = _dot(tril, cmat) + carry_ref[...]
    r0 = jnp.sum(jnp.where(hit0, prefix, 0.0), axis=-1, keepdims=True)
    r1 = jnp.sum(jnp.where(hit1, prefix, 0.0), axis=-1, keepdims=True)
    counted = (i > 0).astype(F32)
    carry_ref[...] = carry_ref[...] + counted * jnp.sum(cmat.astype(F32), axis=0, keepdims=True)
    cnt_ref[...] = carry_ref[...]

    fields = [(i0 - N_GROUPS).astype(F32), (i1 - N_GROUPS).astype(F32), r0, r1, gate0, gate1]
    slab = jnp.zeros((tm, LANES), F32)
    for n, f in enumerate(fields):
        slab = jnp.where(lane == n, f, slab)
    slab_ref[...] = slab


def _out_router(oa, ob, lse, oc, x, woa, wob, woc, g, b, wr, br, layer):
    T = x.shape[0]
    tm = TM_OUT
    n = T // tm
    full = lambda a: _layer_block(a, layer)
    row = lambda w: pl.BlockSpec((tm, w), lambda i: (jnp.minimum(i, n - 1), 0))
    lag = pl.BlockSpec((tm, LANES), lambda i: (jnp.maximum(i - 1, 0), 0))
    return pl.pallas_call(
        _out_router_kernel,
        grid=(n + 1,),
        in_specs=[row(LANES)] * 14 + [pl.BlockSpec((C_HEADS * C_V, tm), lambda i: (0, jnp.minimum(i, n - 1))),
                                      row(D_MODEL),
                                      full(woa), full(wob), full(woc), full(g), full(b), full(wr), full(br)],
        out_specs=[row(D_MODEL), lag, pl.BlockSpec((1, LANES), lambda i: (0, 0))],
        out_shape=[jax.ShapeDtypeStruct((T, D_MODEL), F32), jax.ShapeDtypeStruct((T, LANES), F32),
                   jax.ShapeDtypeStruct((1, LANES), F32)],
        scratch_shapes=[pltpu.VMEM((1, LANES), F32), pltpu.VMEM((tm, D_MODEL), F32)],
        compiler_params=_cparams("arbitrary"),
        name="out_router",
    )(*oa, *ob, *lse, oc, x, woa, wob, woc, g, b, wr, br)


SUBLANES = 8
PAD_BITS = tuple(1 << k for k in reversed(range(SUBLANES.bit_length() - 1, ROW_BLK.bit_length() - 1)))


def _dispatch_kernel(d0_ref, d1_ref, pad_at_ref, pad_len_ref, nu_ref, x_ref, xs_ref, zero_ref, size_ref, sem, zsem):
    i = pl.program_id(0)
    tm = x_ref.shape[0] * SUBLANES
    base = i * tm

    def pad_copies(e):
        n = pad_len_ref[e]
        at = pad_at_ref[e]
        head = n & (SUBLANES - 1)
        out = [(j < head, pltpu.make_async_copy(zero_ref.at[pl.ds(0, 1), :], xs_ref.at[pl.ds(at + j, 1), :], zsem))
               for j in range(SUBLANES - 1)]
        body = n - head
        for bit in PAD_BITS:
            start = pl.multiple_of(at + head + (body & ~(2 * bit - 1)), SUBLANES)
            out.append(((body & bit) != 0,
                        pltpu.make_async_copy(zero_ref.at[pl.ds(0, bit), :], xs_ref.at[pl.ds(start, bit), :], zsem)))
        return out

    def spare_copies():
        n_blocks = xs_ref.shape[0] // ROW_BLK
        first = d0_ref.shape[0] * TOP_K // ROW_BLK
        return [(b >= nu_ref[0], pltpu.make_async_copy(zero_ref, xs_ref.at[pl.ds(b * ROW_BLK, ROW_BLK), :], zsem))
                for b in range(first, n_blocks)]

    def all_zero_copies():
        return [pair for e in range(N_EXPERTS) for pair in pad_copies(e)] + spare_copies()

    @pl.when(i == 0)
    def _():
        zero_ref[...] = jnp.zeros(zero_ref.shape, F32)
        for on, cp in all_zero_copies():
            pl.when(on)(cp.start)

    def start(g, carry):
        for u in range(SUBLANES):
            t = g * SUBLANES + u
            for d_ref in (d0_ref, d1_ref):
                pltpu.make_async_copy(x_ref.at[g, pl.ds(u, 1), :], xs_ref.at[pl.ds(d_ref[base + t], 1), :], sem).start()
        return carry

    lax.fori_loop(0, tm // SUBLANES, start, 0)
    for _ in range(2):
        pltpu.make_async_copy(xs_ref.at[pl.ds(0, tm), :], size_ref, sem).wait()

    @pl.when(i == pl.num_programs(0) - 1)
    def _():
        for on, cp in all_zero_copies():
            pl.when(on)(cp.wait)


def _dispatch(d0, d1, pad_at, pad_len, n_used, x1, n_rows):
    T = x1.shape[0]
    tm = ROW_BLK
    grid_spec = pltpu.PrefetchScalarGridSpec(
        num_scalar_prefetch=5,
        grid=(T // tm,),
        in_specs=[pl.BlockSpec((tm // SUBLANES, SUBLANES, D_MODEL), lambda i, *_: (i, 0, 0))],
        out_specs=pl.BlockSpec(memory_space=pl.ANY),
        scratch_shapes=[pltpu.VMEM((ROW_BLK, D_MODEL), F32), pltpu.VMEM((tm, D_MODEL), F32),
                        pltpu.SemaphoreType.DMA, pltpu.SemaphoreType.DMA],
    )
    return pl.pallas_call(
        _dispatch_kernel,
        grid_spec=grid_spec,
        out_shape=jax.ShapeDtypeStruct((n_rows, D_MODEL), F32),
        compiler_params=_cparams("arbitrary"),
        name="moe_dispatch",
    )(d0, d1, pad_at, pad_len, n_used, x1.reshape(T // SUBLANES, SUBLANES, D_MODEL))


FFN_RING = 3


def _ffn_kernel(be_ref, nu_ref, xs_ref, wg_ref, wu_ref, wd_ref, ys_ref, xbuf, sem):
    del be_ref
    i = pl.program_id(0)
    n_fetch = jnp.minimum(nu_ref[0], pl.num_programs(0))

    def fetch(j):
        start = pl.multiple_of(j * ROW_BLK, ROW_BLK)
        slot = j % FFN_RING
        return pltpu.make_async_copy(xs_ref.at[pl.ds(start, ROW_BLK), :], xbuf.at[slot], sem.at[slot])

    @pl.when(i == 0)
    def _():
        for j in range(FFN_RING - 1):
            pl.when(j < n_fetch)(fetch(j).start)

    @pl.when(i + FFN_RING - 1 < n_fetch)
    def _():
        fetch(i + FFN_RING - 1).start()

    used = i < nu_ref[0]

    @pl.when(used)
    def _():
        fetch(i).wait()
        xb = xbuf[i % FFN_RING].astype(BF16)
        a = _dot(xb, wg_ref[...].astype(BF16))
        u = _dot(xb, wu_ref[...].astype(BF16))
        hmid = (a * jax.nn.sigmoid(a) * u).astype(BF16)
        ys_ref[...] = _dot(hmid, wd_ref[...].astype(BF16))

    @pl.when(jnp.logical_not(used))
    def _():
        ys_ref[...] = jnp.zeros(ys_ref.shape, F32)


def _ffn(block_expert, n_used, xs, wg, wu, wd, layer):
    n_rows = xs.shape[0]
    assert n_rows % ROW_BLK == 0
    w_spec = lambda shape: pl.BlockSpec((None, None) + shape, lambda i, be, nu: (layer, be[i], 0, 0))
    grid_spec = pltpu.PrefetchScalarGridSpec(
        num_scalar_prefetch=2,
        grid=(n_rows // ROW_BLK,),
        in_specs=[pl.BlockSpec(memory_space=pl.ANY), w_spec((D_MODEL, D_EXPERT)), w_spec((D_MODEL, D_EXPERT)),
                  w_spec((D_EXPERT, D_MODEL))],
        out_specs=pl.BlockSpec((ROW_BLK, D_MODEL), lambda i, be, nu: (i, 0)),
        scratch_shapes=[pltpu.VMEM((FFN_RING, ROW_BLK, D_MODEL), F32), pltpu.SemaphoreType.DMA((FFN_RING,))],
    )
    return pl.pallas_call(
        _ffn_kernel,
        grid_spec=grid_spec,
        out_shape=jax.ShapeDtypeStruct((n_rows, D_MODEL), F32),
        compiler_params=_cparams("arbitrary"),
        name="moe_ffn",
    )(block_expert, n_used, xs, wg, wu, wd)


def _combine_kernel(d0_ref, d1_ref, ys_ref, x1_ref, slab_ref, g_ref, b_ref, o_ref, buf_ref, sem):
    i = pl.program_id(0)
    n = pl.num_programs(0)
    tm = x1_ref.shape[0]

    def gather(tile):
        slot = tile % 2
        base = tile * tm

        def start(g, carry):
            for u in range(SUBLANES):
                t = g * SUBLANES + u
                for k, d_ref in enumerate((d0_ref, d1_ref)):
                    pltpu.make_async_copy(ys_ref.at[pl.ds(d_ref[base + t], 1), :],
                                          buf_ref.at[slot, k, g, pl.ds(u, 1), :], sem.at[slot]).start()
            return carry

        lax.fori_loop(0, tm // SUBLANES, start, 0)

    @pl.when(i == 0)
    def _():
        gather(0)

    @pl.when(i + 1 < n)
    def _():
        gather(i + 1)

    slot = i % 2
    for k in range(2):
        pltpu.make_async_copy(ys_ref.at[pl.ds(0, tm), :], o_ref, sem.at[slot]).wait()
    slab = slab_ref[...]
    rows = lambda k: buf_ref[slot, k].reshape(tm, D_MODEL)
    y = slab[:, 4:5] * rows(0) + slab[:, 5:6] * rows(1)
    o_ref[...] = _layer_norm(DEEPNORM_ALPHA * x1_ref[...] + y, g_ref[...], b_ref[...])


def _combine(d0, d1, ys, x1, slab, g, b, layer):
    T = x1.shape[0]
    tm = ROW_BLK
    grid_spec = pltpu.PrefetchScalarGridSpec(
        num_scalar_prefetch=2,
        grid=(T // tm,),
        in_specs=[pl.BlockSpec(memory_space=pl.ANY),
                  pl.BlockSpec((tm, D_MODEL), lambda i, a, c: (i, 0)),
                  pl.BlockSpec((tm, LANES), lambda i, a, c: (i, 0)),
                  _layer_block(g, layer), _layer_block(b, layer)],
        out_specs=pl.BlockSpec((tm, D_MODEL), lambda i, a, c: (i, 0)),
        scratch_shapes=[pltpu.VMEM((2, 2, tm // SUBLANES, SUBLANES, D_MODEL), F32), pltpu.SemaphoreType.DMA((2,))],
    )
    return pl.pallas_call(
        _combine_kernel,
        grid_spec=grid_spec,
        out_shape=jax.ShapeDtypeStruct((T, D_MODEL), F32),
        compiler_params=_cparams("arbitrary"),
        name="moe_combine",
    )(d0, d1, ys, x1, slab, g, b)


def _t5_bucket(dist):
    max_exact = N_BUCKETS // 2
    large = max_exact + (jnp.log(jnp.maximum(dist, 1).astype(F32) / max_exact)
                         / math.log(MAX_DISTANCE / max_exact) * (N_BUCKETS - max_exact)).astype(jnp.int32)
    return jnp.where(dist < max_exact, dist, jnp.minimum(large, N_BUCKETS - 1))


def _bias_table(table_cols, dil, max_dist):
    dist = BLK + jnp.arange(BLK)[:, None] - jnp.arange(2 * BLK)[None, :]
    bucket = _t5_bucket(jnp.maximum(dist * dil, 0))
    hit = bucket[None, :, :, None] == jnp.arange(N_BUCKETS)[None, None, None, :]
    bias = jnp.sum(jnp.where(hit, table_cols.T.astype(F32)[:, None, None, :], 0.0), axis=-1)
    mask = (dist >= 0) & (dist <= max_dist)
    return jnp.where(mask[None], bias, NEG_INF)


def _rope_tables(T):
    inv_freq = ROPE_THETA ** (-jnp.arange(0, C_ROPE, 2, dtype=F32) / C_ROPE)
    ang = jnp.arange(T, dtype=F32)[:, None] * inv_freq[None, :]
    cos, sin = jnp.cos(ang), jnp.sin(ang)
    ones = jnp.ones((T, C_NOPE), F32)
    zeros_n = jnp.zeros((T, C_NOPE), F32)
    zeros_p = jnp.zeros((T, C_PAD - C_NOPE - C_ROPE), F32)
    cs = jnp.concatenate([ones, cos, cos, zeros_p], axis=1)
    sn = jnp.concatenate([zeros_n, -sin, sin, zeros_p], axis=1)
    return cs, sn


def _swap_halves(w):
    half = w.shape[-1] // 2
    return jnp.concatenate([w[..., half:], w[..., :half]], axis=-1)


def _pad_cols(w, width):
    return jnp.pad(w, ((0, 0), (0, width - w.shape[1])))


def _prep_layer_weights(w_in, w_out, w_uq, w_ukv, w_rg, b_rg, w_re, b_re):
    head_cols = lambda w, h: w[:, h * HEAD_DIM:(h + 1) * HEAD_DIM]
    waq = jnp.concatenate([head_cols(w_in, h) for h in A_HEAD_ORDER], axis=1).astype(BF16)
    wab = w_in[:, A_Q_HEADS * HEAD_DIM:AB_COLS].astype(BF16)
    c0 = AB_COLS
    w_cq = w_in[:, c0:c0 + Q_LORA]
    w_ckv = w_in[:, c0 + Q_LORA:c0 + Q_LORA + KV_LORA]
    w_kr = w_in[:, c0 + Q_LORA + KV_LORA:]
    lead = jnp.zeros((D_MODEL, C_NOPE), F32)
    wc = jnp.concatenate([w_cq, w_ckv,
                          _pad_cols(jnp.concatenate([lead, w_kr], 1), C_PAD),
                          _pad_cols(jnp.concatenate([lead, _swap_halves(w_kr)], 1), C_PAD)], axis=1).astype(BF16)
    uq = w_uq.reshape(Q_LORA, C_HEADS, C_NOPE + C_ROPE)
    wq = jnp.pad(uq, ((0, 0), (0, 0), (0, C_PAD - C_NOPE - C_ROPE))).reshape(Q_LORA, C_HEADS * C_PAD).astype(BF16)
    uqs = jnp.concatenate([jnp.zeros((Q_LORA, C_HEADS, C_NOPE), F32), _swap_halves(uq[..., C_NOPE:])], axis=-1)
    wqs = jnp.pad(uqs, ((0, 0), (0, 0), (0, C_PAD - C_NOPE - C_ROPE))).reshape(Q_LORA, C_HEADS * C_PAD).astype(BF16)
    ukv = w_ukv.reshape(KV_LORA, C_HEADS, C_NOPE + C_V)
    wk = jnp.pad(ukv[..., :C_NOPE], ((0, 0), (0, 0), (0, C_PAD - C_NOPE))).reshape(KV_LORA, C_HEADS * C_PAD)
    wv = jnp.pad(ukv[..., C_NOPE:], ((0, 0), (0, 0), (0, C_PAD - C_V))).reshape(KV_LORA, C_HEADS * C_PAD)
    woa = jnp.concatenate([w_out[h * HEAD_DIM:(h + 1) * HEAD_DIM] for h in A_HEAD_ORDER],
                          axis=0).reshape(2, LANES, D_MODEL).astype(BF16)
    wob = w_out[256:512].reshape(2, LANES, D_MODEL).astype(BF16)
    woc = w_out[512:].astype(BF16)
    wr = _pad_cols(jnp.concatenate([w_rg, w_re.transpose(1, 0, 2).reshape(D_MODEL, N_EXPERTS)], axis=1), LANES)
    wr_hi = wr.astype(BF16)
    wr_lo = (wr - wr_hi.astype(F32)).astype(BF16)
    wr = jnp.concatenate([wr_hi, wr_hi, wr_lo], axis=0)
    br = _pad_cols(jnp.concatenate([b_rg, b_re.reshape(N_EXPERTS)])[None, :], LANES)
    return dict(waq=waq, wab=wab, wc=wc, wq=wq, wqs=wqs, wk=wk.astype(BF16), wv=wv.astype(BF16),
                woa=woa, wob=wob, woc=woc, wr=wr, br=br)


def kernel(x, w_in, w_out, sinks, rel_bias, mla_q_norm, mla_kv_norm, w_uq, w_ukv, ln1_g, ln1_b,
           w_route_group, b_route_group, w_route_expert, b_route_expert,
           w_expert_gate, w_expert_up, w_expert_down, ln2_g, ln2_b):
    Bsz, T, D = x.shape
    assert Bsz == 1 and D == D_MODEL and T % CHUNK == 0
    xt = x.reshape(T, D)
    cs, sn = _rope_tables(T)
    bias_a = _bias_table(rel_bias[:, :A_Q_HEADS], 1, SWA_WINDOW - 1)
    bias_b = [_bias_table(rel_bias[:, A_Q_HEADS + g * B_HEADS_PER_GROUP:A_Q_HEADS + (g + 1) * B_HEADS_PER_GROUP],
                          dil, window // dil) for g, (window, dil) in enumerate(DILATED_PAIRS)]
    n_assign = T * TOP_K
    n_blocks = n_assign // ROW_BLK + N_EXPERTS
    n_rows = n_blocks * ROW_BLK
    no_sink = jnp.zeros((B_HEADS_PER_GROUP,), F32)

    w = jax.vmap(_prep_layer_weights)(w_in, w_out, w_uq, w_ukv, w_route_group, b_route_group,
                                      w_route_expert, b_route_expert)
    per_layer_row = lambda a: a[:, None, :]
    gq, gkv = per_layer_row(mla_q_norm), per_layer_row(mla_kv_norm)
    g1, b1, g2, b2 = (per_layer_row(a) for a in (ln1_g, ln1_b, ln2_g, ln2_b))

    for layer in range(w_in.shape[0]):
        hab, qt, k, vt = _in_proj(xt, w["waq"], w["wab"], w["wc"], w["wq"], w["wqs"], w["wk"], w["wv"], gq, gkv, cs, sn,
                                  layer)
        oa, _ = _banded(hab, sinks[layer], bias_a, q_col=0, k_col=2, v_col=3, dil=1,
                        n_heads=A_Q_HEADS, group=A_Q_HEADS // A_KV_HEADS, has_sink=True, want_lse=False)
        ob, lse = [], []
        for g, (_, dil) in enumerate(DILATED_PAIRS):
            o_g, lse_g = _banded(hab, no_sink, bias_b[g], q_col=4 + 2 * g, k_col=10 + 2 * g, v_col=16 + 2 * g,
                                 dil=dil, n_heads=B_HEADS_PER_GROUP, group=1, has_sink=False, want_lse=True)
            ob += o_g
            lse += lse_g
        oc = _flash(qt, k, vt)
        x1, slab, cnt = _out_router(oa, ob, lse, oc, xt, w["woa"], w["wob"], w["woc"], g1, b1, w["wr"], w["br"],
                                    layer)

        counts = cnt[0, N_GROUPS:N_GROUPS + N_EXPERTS].astype(jnp.int32)
        padded = ((counts + ROW_BLK - 1) // ROW_BLK) * ROW_BLK
        pad_end = jnp.cumsum(padded)
        pad_start = pad_end - padded
        eids = jnp.arange(N_EXPERTS, dtype=F32)
        onehot = (slab[:, 0:TOP_K, None] == eids[None, None, :]).astype(F32)
        seg = jnp.einsum('tke,e->tk', onehot, pad_start.astype(F32), precision=lax.Precision.HIGHEST)
        d0 = (seg[:, 0] + slab[:, 2]).astype(jnp.int32)
        d1 = (seg[:, 1] + slab[:, 3]).astype(jnp.int32)
        block_start = jnp.arange(n_blocks, dtype=jnp.int32) * ROW_BLK
        block_expert = jnp.minimum(jnp.sum((pad_end[None, :] <= block_start[:, None]).astype(jnp.int32), axis=1),
                                   N_EXPERTS - 1)

        n_used = (pad_end[-1:] // ROW_BLK).astype(jnp.int32)
        xs = _dispatch(d0, d1, pad_start + counts, padded - counts, n_used, x1, n_rows)
        ys = _ffn(block_expert, n_used, xs, w_expert_gate, w_expert_up, w_expert_down, layer)
        xt = _combine(d0, d1, ys, x1, slab, g2, b2, layer)
    return xt.reshape(Bsz, T, D)
```
